```python
import math
import jax
import jax.numpy as jnp
from jax import lax
import numpy as np

D_MODEL = 2048
BATCH = 4
SEQ = 2048
DEPTH = 2

GRID_W = 64
CTX_LEN = 256
N_EVEN = (DEPTH + 1) // 2
N_ODD = DEPTH // 2
EPS = 1e-6
MIX_W = D_MODEL
HALF_W = MIX_W // 2

A_HEADS = 4
A_DV = HALF_W // A_HEADS
A_DQK = A_DV // 2
A_CHUNK = 64
B_HEADS = 16
B_KV_HEADS = 4
B_HEAD_DIM = HALF_W // B_HEADS
WINDOW = 128
B_BLOCK = 128
ROPE_BASE = 10000.0
C_WIDTH = HALF_W
C_GROUP = 16
C_GROUPS = C_WIDTH // C_GROUP
C_STATE = 64
DT_MIN = 1e-3
DT_MAX = 1e-1
HY_WIDTH = HALF_W
HY_SHORT = 3
HY_BANDS = 16
HY_EMB = 2 * HY_BANDS + 1
HY_FFN = 64
HY_DECAY_TARGET = 1e-2
HY_FAST_PCT = 0.3
HY_SLOW_PCT = 1.5
N_EXPERTS = 32
TOP_K = 4
D_FF = D_MODEL
SWIGLU_LIMIT = 7.0
SWIGLU_ALPHA = 1.702
MOE_BLOCK = 128

EVEN_SPLITS = (A_HEADS * A_DQK, A_HEADS * A_DQK, HALF_W, HALF_W, 4 * A_HEADS,
               B_HEADS * B_HEAD_DIM, B_KV_HEADS * B_HEAD_DIM, B_KV_HEADS * B_HEAD_DIM)
EVEN_IN = sum(EVEN_SPLITS)
ODD_IN = C_WIDTH + 3 * HY_WIDTH

kernel_name = 'hybrid_mlstm_swa_s5_hyena_moe_dit'


def split_cols(t, sizes):
    return jnp.split(t, np.cumsum(sizes)[:-1].tolist(), axis=-1)


def rms_norm(x, g):
    xf = x.astype(jnp.float32)
    y = xf * lax.rsqrt(jnp.mean(xf * xf, axis=-1, keepdims=True) + EPS)
    return (y * g.astype(jnp.float32)).astype(x.dtype)


def ada_params(cond, w, b):
    m = jax.nn.silu(cond) @ w + b
    return jnp.split(m[..., None, :], 6, axis=-1)


def modulate(h, shift, scale):
    return h * (1 + scale) + shift


def to_heads(t, n):
    bsz, L, _ = t.shape
    return t.reshape(bsz, L, n, -1).transpose(0, 2, 1, 3)


def axial_rope(rows, dim):
    row = jnp.repeat(jnp.arange(rows), GRID_W).astype(jnp.float32)
    col = jnp.tile(jnp.arange(GRID_W), rows).astype(jnp.float32)
    n_freq = dim // 4
    inv = ROPE_BASE ** (-jnp.arange(n_freq, dtype=jnp.float32) / n_freq)
    ang = jnp.concatenate([row[:, None] * inv, col[:, None] * inv], axis=-1)
    return jnp.cos(ang), jnp.sin(ang)


def apply_rope(t, cos, sin):
    t1, t2 = jnp.split(t.astype(jnp.float32), 2, axis=-1)
    return jnp.concatenate([t1 * cos - t2 * sin, t1 * sin + t2 * cos], axis=-1).astype(t.dtype)


def softmax_with_sink(s, sink):
    sink = jnp.broadcast_to(sink, s.shape[:-1] + (1,))
    return jax.nn.softmax(jnp.concatenate([s, sink], axis=-1), axis=-1)[..., :-1]


def mlstm_scan(q, k, v, log_i, log_f, state):
    L = q.shape[2]
    nc = L // A_CHUNK

    def to_chunks(t):
        t = t.reshape(t.shape[:2] + (nc, A_CHUNK) + t.shape[3:])
        return jnp.moveaxis(t, 2, 0)

    lower = jnp.tril(jnp.ones((A_CHUNK, A_CHUNK), dtype=bool))

    def step(carry, blk):
        C, n, m = carry
        qc, kc, vc, ic, fc = blk
        b = jnp.cumsum(fc, axis=-1)
        log_d = jnp.where(lower, b[..., :, None] - b[..., None, :] + ic[..., None, :], -jnp.inf)
        log_carry = b + m[..., None]
        m_t = jnp.maximum(log_carry, jnp.max(log_d, axis=-1))
        w_intra = jnp.exp(log_d - m_t[..., None])
        w_carry = jnp.exp(log_carry - m_t)
        s = jnp.einsum('bhtd,bhsd->bhts', qc, kc) * w_intra
        num = jnp.einsum('bhts,bhsv->bhtv', s, vc) + w_carry[..., None] * jnp.einsum('bhvd,bhtd->bhtv', C, qc)
        den = jnp.sum(s, axis=-1) + w_carry * jnp.einsum('bhd,bhtd->bht', n, qc)
        h = num / jnp.maximum(jnp.abs(den), jnp.exp(-m_t))[..., None]
        m_new = m_t[..., -1]
        w_in = jnp.exp(b[..., -1:] - b + ic - m_new[..., None])
        decay = jnp.exp(b[..., -1] + m - m_new)
        C_new = decay[..., None, None] * C + jnp.einsum('bhs,bhsv,bhsd->bhvd', w_in, vc, kc)
        n_new = decay[..., None] * n + jnp.einsum('bhs,bhsd->bhd', w_in, kc)
        return (C_new, n_new, m_new), h

    state, h = lax.scan(step, state, tuple(to_chunks(t) for t in (q, k, v, log_i, log_f)))
    return jnp.moveaxis(h, 0, 2).reshape(v.shape), state


def mlstm_heads(q, k, v, gates, gate_b):
    f32 = jnp.float32
    q = to_heads(q, A_HEADS).astype(f32)
    k = to_heads(k, A_HEADS).astype(f32) * A_DQK ** -0.5
    v = to_heads(v, A_HEADS).astype(f32)
    g = jnp.swapaxes(gates.astype(f32) + gate_b.astype(f32), 1, 2)
    i_f, f_f, i_b, f_b = jnp.split(g, 4, axis=1)
    return (q, k, v), (i_f, jax.nn.log_sigmoid(f_f)), (i_b, jax.nn.log_sigmoid(f_b))


def mlstm_out(h, o_pre, norm_g):
    bsz, H, L, dv = h.shape
    h = rms_norm(h, norm_g.reshape(H, 1, dv))
    h = h.transpose(0, 2, 1, 3).reshape(bsz, L, H * dv)
    return (h * jax.nn.sigmoid(o_pre.astype(jnp.float32))).astype(o_pre.dtype)


def mlstm_mixer(a_ctx, a_lat, gate_b, norm_g, with_ctx):
    qkv_c, fwd_c, bwd_c = mlstm_heads(a_ctx[0], a_ctx[1], a_ctx[2], a_ctx[4], gate_b)
    qkv_l, fwd_l, bwd_l = mlstm_heads(a_lat[0], a_lat[1], a_lat[2], a_lat[4], gate_b)
    bsz = a_lat[0].shape[0]
    f32 = jnp.float32
    zero = (jnp.zeros((bsz, A_HEADS, A_DV, A_DQK), f32), jnp.zeros((bsz, A_HEADS, A_DQK), f32),
            jnp.zeros((bsz, A_HEADS), f32))

    def flip(ts):
        return tuple(jnp.flip(t, axis=2) for t in ts)

    h_cf, s_f = mlstm_scan(*qkv_c, *fwd_c, zero)
    h_cb, s_b = mlstm_scan(*flip(qkv_c + bwd_c), zero)
    h_lf, _ = mlstm_scan(*qkv_l, *fwd_l, s_f)
    h_lb, _ = mlstm_scan(*flip(qkv_l + bwd_l), s_b)
    y_lat = mlstm_out(h_lf + jnp.flip(h_lb, axis=2), a_lat[3], norm_g)
    y_ctx = mlstm_out(h_cf + jnp.flip(h_cb, axis=2), a_ctx[3], norm_g) if with_ctx else None
    return y_ctx, y_lat


def window_attention(b_ctx, b_lat, q_norm_g, k_norm_g, sink, rope, with_ctx):
    G = B_HEADS // B_KV_HEADS
    scale = B_HEAD_DIM ** -0.5

    def q_heads(t):
        bsz, L, _ = t.shape
        return rms_norm(t.reshape(bsz, L, B_KV_HEADS, G, B_HEAD_DIM).transpose(0, 2, 3, 1, 4), q_norm_g)

    qc, kc, vc = q_heads(b_ctx[0]), rms_norm(to_heads(b_ctx[1], B_KV_HEADS), k_norm_g), to_heads(b_ctx[2], B_KV_HEADS)
    ql, kl, vl = q_heads(b_lat[0]), rms_norm(to_heads(b_lat[1], B_KV_HEADS), k_norm_g), to_heads(b_lat[2], B_KV_HEADS)
    sink_kg = sink.astype(jnp.float32).reshape(B_KV_HEADS, G)

    bsz, L, dh = ql.shape[0], ql.shape[3], ql.shape[4]
    nb = L // B_BLOCK
    cos, sin = rope
    ql_rot, kl_rot = apply_rope(ql, cos, sin), apply_rope(kl, cos, sin)

    def band(t):
        tp = jnp.pad(t, ((0, 0), (0, 0), (B_BLOCK, B_BLOCK), (0, 0))).reshape(bsz, B_KV_HEADS, nb + 2, B_BLOCK, dh)
        return jnp.concatenate([tp[:, :, :-2], tp[:, :, 1:-1], tp[:, :, 2:]], axis=3)

    k_band, v_band = band(kl_rot), band(vl)
    qb = ql_rot.reshape(bsz, B_KV_HEADS, G, nb, B_BLOCK, dh)
    qb_plain = ql.reshape(bsz, B_KV_HEADS, G, nb, B_BLOCK, dh)
    s_band = jnp.einsum('bkgnqd,bknsd->bkgnqs', qb, k_band).astype(jnp.float32) * scale
    q_pos = jnp.arange(nb)[:, None, None] * B_BLOCK + jnp.arange(B_BLOCK)[None, :, None]
    k_pos = jnp.arange(nb)[:, None, None] * B_BLOCK + jnp.arange(3 * B_BLOCK)[None, None, :] - B_BLOCK
    valid = (jnp.abs(q_pos - k_pos) <= WINDOW) & (k_pos >= 0) & (k_pos < L)
    s_band = jnp.where(valid, s_band, -jnp.inf)
    s_ctx = jnp.einsum('bkgnqd,bkcd->bkgnqc', qb_plain, kc).astype(jnp.float32) * scale
    p = softmax_with_sink(jnp.concatenate([s_band, s_ctx], axis=-1), sink_kg[None, :, :, None, None, None])
    p_band, p_ctx = p[..., :3 * B_BLOCK], p[..., 3 * B_BLOCK:]
    o = (jnp.einsum('bkgnqs,bknsd->bkgnqd', p_band.astype(vl.dtype), v_band)
         + jnp.einsum('bkgnqc,bkcd->bkgnqd', p_ctx.astype(vc.dtype), vc))
    y_lat = o.reshape(bsz, B_KV_HEADS, G, L, dh).transpose(0, 3, 1, 2, 4).reshape(bsz, L, B_HEADS * dh)
    y_ctx = None
    if with_ctx:
        s = jnp.einsum('bkgqd,bkcd->bkgqc', qc, kc).astype(jnp.float32) * scale
        pc = softmax_with_sink(s, sink_kg[None, :, :, None, None])
        oc = jnp.einsum('bkgqc,bkcd->bkgqd', pc.astype(vc.dtype), vc)
        y_ctx = oc.transpose(0, 3, 1, 2, 4).reshape(bsz, qc.shape[3], B_HEADS * dh)
    return y_ctx, y_lat


def even_mixer(h_ctx, h_lat, rope, w_in, gate_b, h_norm_g, q_norm_g, k_norm_g, sink, w_out, with_ctx):
    pc = split_cols(h_ctx @ w_in, EVEN_SPLITS)
    pl = split_cols(h_lat @ w_in, EVEN_SPLITS)
    a_c, a_l = mlstm_mixer(pc[:5], pl[:5], gate_b, h_norm_g, with_ctx)
    b_c, b_l = window_attention(pc[5:], pl[5:], q_norm_g, k_norm_g, sink, rope, with_ctx)
    y_lat = jnp.concatenate([a_l, b_l], axis=-1) @ w_out
    y_ctx = jnp.concatenate([a_c, b_c], axis=-1) @ w_out if with_ctx else None
    return y_ctx, y_lat


def s5_scan(u, a_bar, b_bar, x0):
    bu = jnp.einsum('blgc,gpc->blgp', u.astype(jnp.complex64), b_bar)
    bu = bu.at[:, 0].add(a_bar * x0)
    a = jnp.broadcast_to(a_bar, (1, u.shape[1]) + a_bar.shape)

    def combine(e1, e2):
        a1, b1 = e1
        a2, b2 = e2
        return a1 * a2, a2 * b1 + b2

    return lax.associative_scan(combine, (a, bu), axis=1)[1]


def s5_readout(xs, u, c_re, c_im, d_skip, glu_w, glu_b):
    bsz, L, _ = u.shape
    f32 = jnp.float32
    y = (jnp.einsum('blgp,gcp->blgc', xs.real, c_re.astype(f32))
         - jnp.einsum('blgp,gcp->blgc', xs.imag, c_im.astype(f32)))
    y = y.reshape(bsz, L, C_WIDTH).astype(u.dtype) + d_skip * u
    y = jax.nn.gelu(y)
    return y * jax.nn.sigmoid(y @ glu_w + glu_b)


def s5_mixer(u_ctx, u_lat, a_re, a_im, log_dt, b_re, b_im, c_re, c_im, d_skip, glu_w, glu_b, with_ctx):
    f32 = jnp.float32
    lam = lax.complex(a_re.astype(f32), a_im.astype(f32))
    a_bar = jnp.exp(lam * jnp.exp(log_dt.astype(f32))[..., None])
    b_bar = ((a_bar - 1) / lam)[..., None] * lax.complex(b_re.astype(f32), b_im.astype(f32))

    def groups(u):
        return u.astype(f32).reshape(u.shape[0], u.shape[1], C_GROUPS, C_GROUP)

    uc, ul = groups(u_ctx), groups(u_lat)
    x0 = jnp.zeros((u_lat.shape[0], C_GROUPS, C_STATE), jnp.complex64)
    xc_f = s5_scan(uc, a_bar[0], b_bar[0], x0)
    xc_b = s5_scan(jnp.flip(uc, axis=1), a_bar[1], b_bar[1], x0)
    xl_f = s5_scan(ul, a_bar[0], b_bar[0], xc_f[:, -1])
    xl_b = s5_scan(jnp.flip(ul, axis=1), a_bar[1], b_bar[1], xc_b[:, -1])
    y_lat = s5_readout(xl_f + jnp.flip(xl_b, axis=1), u_lat, c_re, c_im, d_skip, glu_w, glu_b)
    y_ctx = s5_readout(xc_f + jnp.flip(xc_b, axis=1), u_ctx, c_re, c_im, d_skip, glu_w, glu_b) if with_ctx else None
    return y_ctx, y_lat


def hyena_filter(L, w1, b1, w2, b2, w3, freq):
    f32 = jnp.float32
    t = jnp.linspace(0.0, 1.0, L, dtype=f32)[:, None]
    w = 2 * math.pi * jnp.arange(L, dtype=f32)[:, None] / L
    bands = jnp.linspace(1e-4, HY_BANDS - 1, HY_BANDS, dtype=f32)
    feats = jnp.concatenate([t, jnp.cos(bands * w), -jnp.sin(bands * w)], axis=-1)
    freq = freq.astype(f32)
    h = jnp.sin(freq * (feats @ w1.astype(f32) + b1.astype(f32)))
    h = jnp.sin(freq * (h @ w2.astype(f32) + b2.astype(f32)))
    h = h @ w3.astype(f32)
    fast = abs(math.log(HY_DECAY_TARGET) / HY_FAST_PCT)
    slow = abs(math.log(HY_DECAY_TARGET) / HY_SLOW_PCT)
    deltas = jnp.tile(jnp.linspace(slow, fast, HY_WIDTH, dtype=f32), 2)
    h = h * jnp.exp(-t * deltas)
    h_fwd, h_bwd = jnp.split(h, 2, axis=-1)
    filt = jnp.concatenate([h_fwd, jnp.zeros((1, HY_WIDTH), f32), h_bwd[:0:-1]], axis=0)
    return filt / jnp.sum(jnp.abs(filt), axis=0, keepdims=True)


def short_conv(z, w, b):
    L = z.shape[1]
    pad = HY_SHORT // 2
    zp = jnp.pad(z, ((0, 0), (pad, pad), (0, 0)))
    return sum(zp[:, j:j + L] * w[j] for j in range(HY_SHORT)) + b


def hyena_mixer(z, conv_w, conv_b, f_w1, f_b1, f_w2, f_b2, f_w3, f_freq, hy_bias):
    L = z.shape[1]
    x0, x1, v = jnp.split(short_conv(z, conv_w, conv_b), 3, axis=-1)
    filt = hyena_filter(L, f_w1, f_b1, f_w2, f_b2, f_w3, f_freq)
    u = (x1 * v).astype(jnp.float32)
    y = jnp.fft.irfft(jnp.fft.rfft(u, n=2 * L, axis=1) * jnp.fft.rfft(filt, n=2 * L, axis=0), n=2 * L, axis=1)[:, :L]
    y = y + u * hy_bias.astype(jnp.float32)
    return x0 * y.astype(z.dtype)


def odd_mixer(h_ctx, h_lat, w_in, a_re, a_im, log_dt, b_re, b_im, c_re, c_im, d_skip, glu_w, glu_b,
              conv_w, conv_b, f_w1, f_b1, f_w2, f_b2, f_w3, f_freq, hy_bias, w_out, with_ctx):
    u_c, z_c = jnp.split(h_ctx @ w_in, [C_WIDTH], axis=-1)
    u_l, z_l = jnp.split(h_lat @ w_in, [C_WIDTH], axis=-1)
    s_c, s_l = s5_mixer(u_c, u_l, a_re, a_im, log_dt, b_re, b_im, c_re, c_im, d_skip, glu_w, glu_b, with_ctx)
    hy_l = hyena_mixer(z_l, conv_w, conv_b, f_w1, f_b1, f_w2, f_b2, f_w3, f_freq, hy_bias)
    y_lat = jnp.concatenate([s_l, hy_l], axis=-1) @ w_out
    y_ctx = None
    if with_ctx:
        hy_c = hyena_mixer(z_c, conv_w, conv_b, f_w1, f_b1, f_w2, f_b2, f_w3, f_freq, hy_bias)
        y_ctx = jnp.concatenate([s_c, hy_c], axis=-1) @ w_out
    return y_ctx, y_lat


def clamped_swiglu(h):
    gate, lin = h[..., ::2], h[..., 1::2]
    gate = jnp.minimum(gate, SWIGLU_LIMIT)
    lin = jnp.clip(lin, -SWIGLU_LIMIT, SWIGLU_LIMIT)
    return gate * jax.nn.sigmoid(SWIGLU_ALPHA * gate) * (lin + 1)


def moe_ffn(x, layer, router_w, router_b, w1, b1, w2, b2):
    T, D = x.shape
    logits = (x @ router_w[layer] + router_b[layer]).astype(jnp.float32)
    top_val, top_idx = lax.top_k(logits, TOP_K)
    gate = jax.nn.softmax(top_val, axis=-1)
    n_assign = T * TOP_K
    expert = top_idx.reshape(-1)
    order = jnp.argsort(expert)
    e_sorted = expert[order]
    tok_sorted = (order // TOP_K).astype(jnp.int32)
    g_sorted = gate.reshape(-1)[order]
    counts = jnp.bincount(expert, length=N_EXPERTS)
    padded = (counts + MOE_BLOCK - 1) // MOE_BLOCK * MOE_BLOCK
    pad_end = jnp.cumsum(padded)
    slot = (pad_end - padded)[e_sorted] + jnp.arange(n_assign) - (jnp.cumsum(counts) - counts)[e_sorted]
    n_blocks = -(-n_assign // MOE_BLOCK) + N_EXPERTS
    slot_tok = jnp.full((n_blocks * MOE_BLOCK,), T, jnp.int32).at[slot].set(tok_sorted)
    block_expert = jnp.minimum(jnp.searchsorted(pad_end, jnp.arange(n_blocks) * MOE_BLOCK, side='right'), N_EXPERTS - 1)
    xb = jnp.concatenate([x, jnp.zeros((1, D), x.dtype)], axis=0)[slot_tok].reshape(n_blocks, MOE_BLOCK, D)

    def expert_block(args):
        xe, e = args
        return clamped_swiglu(xe @ w1[layer, e] + b1[layer, e]) @ w2[layer, e] + b2[layer, e]

    yb = lax.map(expert_block, (xb, block_expert)).reshape(n_blocks * MOE_BLOCK, D)
    return jnp.zeros_like(x).at[tok_sorted].add(yb[slot] * g_sorted[:, None].astype(x.dtype))


def setup_inputs(seed: int = 0) -> dict:
    key = jax.random.key(seed)
    ks = iter(jax.random.split(key, 64))
    f32 = jnp.float32

    def nrm(shape, scale=1.0):
        return jax.random.normal(next(ks), shape, f32) * scale

    def gain(shape):
        return 1.0 + nrm(shape, 0.02)

    gate_bias = jnp.concatenate([jnp.zeros((A_HEADS,), f32), jnp.linspace(3.0, 6.0, A_HEADS, dtype=f32)] * 2)
    n_idx = jnp.arange(C_STATE, dtype=f32)
    return {
        'x': nrm((BATCH, SEQ, D_MODEL)),
        'c': nrm((BATCH, D_MODEL)),
        'ctx': nrm((BATCH, CTX_LEN, D_MODEL)),
        'c_ctx': nrm((D_MODEL,)),
        'ada_w': nrm((DEPTH, D_MODEL, 6 * D_MODEL), 0.5 * D_MODEL ** -0.5),
        'ada_b': nrm((DEPTH, 6 * D_MODEL), 0.01),
        'norm_mix_g': gain((DEPTH, D_MODEL)),
        'norm_ffn_g': gain((DEPTH, D_MODEL)),
        'ev_w_in': nrm((N_EVEN, D_MODEL, EVEN_IN), D_MODEL ** -0.5),
        'ev_gate_b': gate_bias + nrm((N_EVEN, 4 * A_HEADS), 0.01),
        'ev_h_norm_g': gain((N_EVEN, HALF_W)),
        'ev_q_norm_g': gain((N_EVEN, B_HEAD_DIM)),
        'ev_k_norm_g': gain((N_EVEN, B_HEAD_DIM)),
        'ev_sink': nrm((N_EVEN, B_HEADS), 0.5),
        'ev_w_out': nrm((N_EVEN, MIX_W, D_MODEL), MIX_W ** -0.5),
        'od_w_in': nrm((N_ODD, D_MODEL, ODD_IN), D_MODEL ** -0.5),
        'od_a_re': -0.5 + nrm((N_ODD, 2, C_GROUPS, C_STATE), 0.01),
        'od_a_im': math.pi * n_idx + nrm((N_ODD, 2, C_GROUPS, C_STATE), 0.01),
        'od_log_dt': jax.random.uniform(next(ks), (N_ODD, 2, C_GROUPS), f32, math.log(DT_MIN), math.log(DT_MAX)),
        'od_b_re': nrm((N_ODD, 2, C_GROUPS, C_STATE, C_GROUP), (2 * C_GROUP) ** -0.5),
        'od_b_im': nrm((N_ODD, 2, C_GROUPS, C_STATE, C_GROUP), (2 * C_GROUP) ** -0.5),
        'od_c_re': nrm((N_ODD, C_GROUPS, C_GROUP, C_STATE), C_STATE ** -0.5),
        'od_c_im': nrm((N_ODD, C_GROUPS, C_GROUP, C_STATE), C_STATE ** -0.5),
        'od_d': nrm((N_ODD, C_WIDTH)),
        'od_glu_w': nrm((N_ODD, C_WIDTH, C_WIDTH), C_WIDTH ** -0.5),
        'od_glu_b': nrm((N_ODD, C_WIDTH), 0.01),
        'od_conv_w': nrm((N_ODD, HY_SHORT, 3 * HY_WIDTH), HY_SHORT ** -0.5),
        'od_conv_b': nrm((N_ODD, 3 * HY_WIDTH), 0.01),
        'od_filt_w1': nrm((N_ODD, HY_EMB, HY_FFN), HY_EMB ** -0.5),
        'od_filt_b1': nrm((N_ODD, HY_FFN), 0.1),
        'od_filt_w2': nrm((N_ODD, HY_FFN, HY_FFN), HY_FFN ** -0.5),
        'od_filt_b2': nrm((N_ODD, HY_FFN), 0.1),
        'od_filt_w3': nrm((N_ODD, HY_FFN, 2 * HY_WIDTH), HY_FFN ** -0.5),
        'od_filt_freq': 1.0 + nrm((N_ODD, HY_FFN), 0.01),
        'od_hy_bias': nrm((N_ODD, HY_WIDTH)),
        'od_w_out': nrm((N_ODD, MIX_W, D_MODEL), MIX_W ** -0.5),
        'moe_router_w': nrm((DEPTH, D_MODEL, N_EXPERTS), D_MODEL ** -0.5),
        'moe_router_b': nrm((DEPTH, N_EXPERTS), 0.01),
        'moe_w1': nrm((DEPTH, N_EXPERTS, D_MODEL, 2 * D_FF), D_MODEL ** -0.5),
        'moe_b1': nrm((DEPTH, N_EXPERTS, 2 * D_FF), 0.01),
        'moe_w2': nrm((DEPTH, N_EXPERTS, D_FF, D_MODEL), D_FF ** -0.5),
        'moe_b2': nrm((DEPTH, N_EXPERTS, D_MODEL), 0.01),
    }


def reference(x, c, ctx, c_ctx, ada_w, ada_b, norm_mix_g, norm_ffn_g,
              ev_w_in, ev_gate_b, ev_h_norm_g, ev_q_norm_g, ev_k_norm_g, ev_sink, ev_w_out,
              od_w_in, od_a_re, od_a_im, od_log_dt, od_b_re, od_b_im, od_c_re, od_c_im, od_d,
              od_glu_w, od_glu_b, od_conv_w, od_conv_b, od_filt_w1, od_filt_b1, od_filt_w2,
              od_filt_b2, od_filt_w3, od_filt_freq, od_hy_bias, od_w_out,
              moe_router_w, moe_router_b, moe_w1, moe_b1, moe_w2, moe_b2):
    bsz, n_lat, D = x.shape
    n_ctx = ctx.shape[1]
    rows = n_lat // GRID_W
    rope = axial_rope(rows, B_HEAD_DIM)
    for layer in range(DEPTH):
        last = layer == DEPTH - 1
        i = layer // 2
        lat_mod = ada_params(c, ada_w[layer], ada_b[layer])
        ctx_mod = ada_params(c_ctx, ada_w[layer], ada_b[layer])
        h_lat = modulate(rms_norm(x, norm_mix_g[layer]), lat_mod[0], lat_mod[1])
        h_ctx = modulate(rms_norm(ctx, norm_mix_g[layer]), ctx_mod[0], ctx_mod[1])
        if layer % 2 == 0:
            y_ctx, y_lat = even_mixer(h_ctx, h_lat, rope, ev_w_in[i], ev_gate_b[i], ev_h_norm_g[i],
                                      ev_q_norm_g[i], ev_k_norm_g[i], ev_sink[i], ev_w_out[i], not last)
        else:
            y_ctx, y_lat = odd_mixer(h_ctx, h_lat, od_w_in[i], od_a_re[i], od_a_im[i], od_log_dt[i],
                                     od_b_re[i], od_b_im[i], od_c_re[i], od_c_im[i], od_d[i],
                                     od_glu_w[i], od_glu_b[i], od_conv_w[i], od_conv_b[i],
                                     od_filt_w1[i], od_filt_b1[i], od_filt_w2[i], od_filt_b2[i],
                                     od_filt_w3[i], od_filt_freq[i], od_hy_bias[i], od_w_out[i], not last)
        x = x + lat_mod[2] * y_lat
        h_lat = modulate(rms_norm(x, norm_ffn_g[layer]), lat_mod[3], lat_mod[4])
        if last:
            y = moe_ffn(h_lat.reshape(-1, D), layer, moe_router_w, moe_router_b, moe_w1, moe_b1, moe_w2, moe_b2)
            x = x + lat_mod[5] * y.reshape(x.shape)
        else:
            ctx = ctx + ctx_mod[2] * y_ctx
            h_ctx = modulate(rms_norm(ctx, norm_ffn_g[layer]), ctx_mod[3], ctx_mod[4])
            tokens = jnp.concatenate([h_ctx.reshape(-1, D), h_lat.reshape(-1, D)], axis=0)
            y = moe_ffn(tokens, layer, moe_router_w, moe_router_b, moe_w1, moe_b1, moe_w2, moe_b2)
            ctx = ctx + ctx_mod[5] * y[:bsz * n_ctx].reshape(ctx.shape)
            x = x + lat_mod[5] * y[bsz * n_ctx:].reshape(x.shape)
    return x
```

```python
import functools
import math

import jax
import jax.numpy as jnp
import numpy as np
from jax import lax
from jax.experimental import pallas as pl
from jax.experimental.pallas import tpu as pltpu

f32 = jnp.float32
bf16 = jnp.bfloat16

D = 2048
NB_ = 4
SEQ = 2048
CTX = 256
LA = SEQ + CTX
ROWS = NB_ * LA
RB = 256
BPB = LA // RB
LBPB = SEQ // RB
EPS = 1e-6
GRID_W = 64

HALF = D // 2
A_HEADS = 4
A_DV = HALF // A_HEADS
A_DQK = A_DV // 2
A_T = 256
B_HEADS = 16
B_KV = 4
B_G = B_HEADS // B_KV
B_DH = HALF // B_HEADS
WINDOW = 128
ROPE_BASE = 10000.0
C_GROUP = 16
C_GROUPS = HALF // C_GROUP
C_STATE = 64
S5_T = 16
HY_BANDS = 16
HY_DECAY_TARGET = 1e-2
HY_FAST_PCT = 0.3
HY_SLOW_PCT = 1.5
N_EXPERTS = 32
TOP_K = 4
D_FF = D
SWIGLU_LIMIT = 7.0
SWIGLU_ALPHA = 1.702
MOE_TM = 256

EVEN_SPLITS = (A_HEADS * A_DQK, A_HEADS * A_DQK, HALF, HALF, 4 * A_HEADS,
               B_HEADS * B_DH, B_KV * B_DH, B_KV * B_DH)

VMEM_MB = 1024 * 1024


def _cp(sem, vmem_mb=40):
    return pltpu.CompilerParams(dimension_semantics=sem, vmem_limit_bytes=vmem_mb * VMEM_MB)


def _lat_blk(i):
    return (i // LBPB) * BPB + 1 + i % LBPB


def _mod_row(u):
    return jnp.where(u % BPB == 0, 0, 1 + u // BPB)


def _lat_mod_row(i):
    return 1 + i // LBPB


def _ident(i):
    return i


def _ada_kernel(c_ref, w_ref, b_ref, o_ref):
    c = c_ref[...]
    s = c * jax.nn.sigmoid(c)
    o_ref[...] = jnp.dot(s.astype(bf16), w_ref[...].astype(bf16),
                         preferred_element_type=f32) + b_ref[...]


def _ada(cond8, ada_w, ada_b3, layer):
    tn = 1024
    n = 6 * D
    return pl.pallas_call(
        _ada_kernel,
        out_shape=jax.ShapeDtypeStruct((8, n), f32),
        grid=(n // tn,),
        in_specs=[pl.BlockSpec((8, D), lambda j: (0, 0)),
                  pl.BlockSpec((None, D, tn), lambda j: (layer, 0, j)),
                  pl.BlockSpec((None, 1, tn), lambda j: (layer, 0, j))],
        out_specs=pl.BlockSpec((8, tn), lambda j: (0, j)),
        compiler_params=_cp(("arbitrary",)),
        name="ada_params",
    )(cond8, ada_w, ada_b3)


def _norm_mod_kernel(x_ref, g_ref, sh_ref, sc_ref, o_ref):
    x = x_ref[...]
    y = x * lax.rsqrt(jnp.mean(x * x, axis=-1, keepdims=True) + EPS) * g_ref[...]
    o_ref[...] = (y * (1.0 + sc_ref[...]) + sh_ref[...]).astype(o_ref.dtype)


def _norm_mod(x, g3, mod3, layer, chunk, blk, mrow, nblk, out_dtype):
    return pl.pallas_call(
        _norm_mod_kernel,
        out_shape=jax.ShapeDtypeStruct((nblk * RB, D), out_dtype),
        grid=(nblk,),
        in_specs=[pl.BlockSpec((RB, D), lambda i: (blk(i), 0)),
                  pl.BlockSpec((None, 1, D), lambda i: (layer, 0, 0)),
                  pl.BlockSpec((None, 1, D), lambda i: (mrow(i), 0, chunk)),
                  pl.BlockSpec((None, 1, D), lambda i: (mrow(i), 0, chunk + 1))],
        out_specs=pl.BlockSpec((RB, D), lambda i: (i, 0)),
        compiler_params=_cp(("arbitrary",)),
        name="norm_mod",
    )(x, g3, mod3, mod3)


def _split_bf16(x):
    hi = x.astype(bf16)
    lo = (x - hi.astype(f32)).astype(bf16)
    return hi, lo


def _norm_router_kernel(x_ref, g_ref, sh_ref, sc_ref, rw_ref, rb_ref,
                        h_ref, idx_ref, gate_ref, rank_ref, cnt_ref, carry_ref):
    i = pl.program_id(0)

    @pl.when(i == 0)
    def _():
        carry_ref[...] = jnp.zeros_like(carry_ref)

    x = x_ref[...]
    y = x * lax.rsqrt(jnp.mean(x * x, axis=-1, keepdims=True) + EPS) * g_ref[...]
    h = y * (1.0 + sc_ref[...]) + sh_ref[...]
    h_ref[...] = h
    h_hi, h_lo = _split_bf16(h)
    w_hi, w_lo = _split_bf16(rw_ref[...])
    logits = (jnp.dot(h_hi, w_hi, preferred_element_type=f32)
              + jnp.dot(h_lo, w_hi, preferred_element_type=f32)
              + jnp.dot(h_hi, w_lo, preferred_element_type=f32)) + rb_ref[...]
    ne = logits.shape[-1]
    lane = lax.broadcasted_iota(jnp.int32, logits.shape, 1)
    l = logits
    vals, idxs = [], []
    for _ in range(TOP_K):
        m = jnp.max(l, axis=-1, keepdims=True)
        ix = jnp.min(jnp.where(l == m, lane, ne), axis=-1, keepdims=True)
        vals.append(m)
        idxs.append(ix)
        l = jnp.where(lane == ix, -jnp.inf, l)
    es = [jnp.exp(v - vals[0]) for v in vals]
    tot = es[0] + es[1] + es[2] + es[3]
    onehot = jnp.zeros(logits.shape, f32)
    for ix in idxs:
        onehot = onehot + jnp.where(lane == ix, 1.0, 0.0)
    r_i = lax.broadcasted_iota(jnp.int32, (RB, RB), 0)
    c_i = lax.broadcasted_iota(jnp.int32, (RB, RB), 1)
    tri = jnp.where(c_i < r_i, 1.0, 0.0).astype(bf16)
    prefix = jnp.dot(tri, onehot.astype(bf16), preferred_element_type=f32) + carry_ref[...]
    for k in range(TOP_K):
        idx_ref[:, k:k + 1] = idxs[k]
        gate_ref[:, k:k + 1] = es[k] / tot
        rk = jnp.sum(jnp.where(lane == idxs[k], prefix, 0.0), axis=-1, keepdims=True)
        rank_ref[:, k:k + 1] = rk.astype(jnp.int32)
    carry_ref[...] = carry_ref[...] + jnp.sum(onehot, axis=0, keepdims=True)
    cnt_ref[...] = carry_ref[...]


def _norm_router(x, g3, mod3, rw, rb3, layer, chunk, blk, mrow, nblk):
    t = nblk * RB
    return pl.pallas_call(
        _norm_router_kernel,
        out_shape=(jax.ShapeDtypeStruct((t, D), f32),
                   jax.ShapeDtypeStruct((t, TOP_K), jnp.int32),
                   jax.ShapeDtypeStruct((t, TOP_K), f32),
                   jax.ShapeDtypeStruct((t, TOP_K), jnp.int32),
                   jax.ShapeDtypeStruct((1, N_EXPERTS), f32)),
        grid=(nblk,),
        in_specs=[pl.BlockSpec((RB, D), lambda i: (blk(i), 0)),
                  pl.BlockSpec((None, 1, D), lambda i: (layer, 0, 0)),
                  pl.BlockSpec((None, 1, D), lambda i: (mrow(i), 0, chunk)),
                  pl.BlockSpec((None, 1, D), lambda i: (mrow(i), 0, chunk + 1)),
                  pl.BlockSpec((None, D, N_EXPERTS), lambda i: (layer, 0, 0)),
                  pl.BlockSpec((None, 1, N_EXPERTS), lambda i: (layer, 0, 0))],
        out_specs=(pl.BlockSpec((RB, D), lambda i: (i, 0)),
                   pl.BlockSpec((RB, TOP_K), lambda i: (i, 0)),
                   pl.BlockSpec((RB, TOP_K), lambda i: (i, 0)),
                   pl.BlockSpec((RB, TOP_K), lambda i: (i, 0)),
                   pl.BlockSpec((1, N_EXPERTS), lambda i: (0, 0))),
        scratch_shapes=[pltpu.VMEM((1, N_EXPERTS), f32)],
        compiler_params=_cp(("arbitrary",)),
        name="norm_router",
    )(x, g3, mod3, mod3, rw, rb3)


def _mm_kernel(*refs, n_a, k_split, epilogue):
    a_refs = refs[:n_a]
    w_ref = refs[n_a]
    pos = n_a + 1
    if epilogue:
        res_ref, gate_ref = refs[pos], refs[pos + 1]
        pos += 2
    o_ref, wbf_ref = refs[pos], refs[pos + 1]

    @pl.when(pl.program_id(1) == 0)
    def _():
        wbf_ref[...] = w_ref[...].astype(bf16)

    acc = None
    for t, a_ref in enumerate(a_refs):
        part = jnp.dot(a_ref[...], wbf_ref[t * k_split:(t + 1) * k_split, :],
                       preferred_element_type=f32)
        acc = part if acc is None else acc + part
    if epilogue:
        acc = res_ref[...] + gate_ref[...] * acc
    o_ref[...] = acc.astype(o_ref.dtype)


def _matmul(a_list, w, w_idx, n, tn, nblk, a_blk, out_dtype, *, a_col=None,
            res=None, res_blk=None, mrow=None, mod3=None, gate_chunk=None, name="matmul"):
    n_a = len(a_list)
    k = w.shape[-2]
    k_split = k // n_a
    lead = len(w_idx)
    a_col = a_col or [0] * n_a
    in_specs = [pl.BlockSpec((RB, k_split), functools.partial(lambda j, i, c: (a_blk(i), c), c=c))
                for c in a_col]
    in_specs.append(pl.BlockSpec((None,) * lead + (k, tn), lambda j, i: tuple(w_idx) + (0, j)))
    args = list(a_list) + [w]
    epilogue = res is not None
    if epilogue:
        per = D // tn
        in_specs.append(pl.BlockSpec((RB, tn), lambda j, i: (res_blk(i), j)))
        in_specs.append(pl.BlockSpec((None, 1, tn),
                                     lambda j, i: (mrow(i), 0, gate_chunk * per + j)))
        args += [res, mod3]
    return pl.pallas_call(
        functools.partial(_mm_kernel, n_a=n_a, k_split=k_split, epilogue=epilogue),
        out_shape=jax.ShapeDtypeStruct((nblk * RB, n), out_dtype),
        grid=(n // tn, nblk),
        in_specs=in_specs,
        out_specs=pl.BlockSpec((RB, tn), lambda j, i: (i, j)),
        scratch_shapes=[pltpu.VMEM((k, tn), bf16)],
        compiler_params=_cp(("arbitrary", "arbitrary")),
        name=name,
    )(*args)


def _gather_rows_kernel(tok_ref, h_hbm, o_hbm, sem, *, rows):
    base = pl.program_id(0) * rows

    def issue(r, c):
        t = tok_ref[base + r]
        pltpu.make_async_copy(h_hbm.at[pl.ds(t, 1), :], o_hbm.at[pl.ds(base + r, 1), :], sem).start()
        return c

    lax.fori_loop(0, rows, issue, 0)

    def wait(r, c):
        pltpu.make_async_copy(h_hbm.at[pl.ds(0, 1), :], o_hbm.at[pl.ds(base, 1), :], sem).wait()
        return c

    lax.fori_loop(0, rows, wait, 0)


def _gather_rows(slot_tok, h, nr):
    rows = 1024
    return pl.pallas_call(
        functools.partial(_gather_rows_kernel, rows=rows),
        out_shape=jax.ShapeDtypeStruct((nr, D), h.dtype),
        grid_spec=pltpu.PrefetchScalarGridSpec(
            num_scalar_prefetch=1,
            grid=(nr // rows,),
            in_specs=[pl.BlockSpec(memory_space=pl.ANY)],
            out_specs=pl.BlockSpec(memory_space=pl.ANY),
            scratch_shapes=[pltpu.SemaphoreType.DMA(())]),
        compiler_params=_cp(("arbitrary",)),
        name="moe_gather",
    )(slot_tok, h)


def _first_of_expert(be_ref, i):
    return jnp.logical_or(i == 0, be_ref[i] != be_ref[jnp.maximum(i - 1, 0)])


def _moe_up_kernel(be_ref, nv_ref, x_ref, w_ref, b_ref, p_ref, o_ref, wbf_ref, *, tn):
    i = pl.program_id(1)

    @pl.when(_first_of_expert(be_ref, i))
    def _():
        wbf_ref[...] = w_ref[...].astype(bf16)

    @pl.when(i < nv_ref[0])
    def _():
        h = jnp.dot(x_ref[...].astype(bf16), wbf_ref[...], preferred_element_type=f32) + b_ref[...]
        for c in range(tn // 256):
            d = jnp.dot(h[:, c * 256:(c + 1) * 256].astype(bf16), p_ref[...],
                        preferred_element_type=f32)
            gate = jnp.minimum(d[:, :128], SWIGLU_LIMIT)
            lin = jnp.clip(d[:, 128:], -SWIGLU_LIMIT, SWIGLU_LIMIT)
            act = gate * jax.nn.sigmoid(SWIGLU_ALPHA * gate) * (lin + 1.0)
            o_ref[:, c * 128:(c + 1) * 128] = act.astype(o_ref.dtype)

    @pl.when(i >= nv_ref[0])
    def _():
        o_ref[...] = jnp.zeros_like(o_ref)


def _moe_up(be, nv, xs, w1, b1, perm, layer, nblk):
    tn = 1024
    return pl.pallas_call(
        functools.partial(_moe_up_kernel, tn=tn),
        out_shape=jax.ShapeDtypeStruct((nblk * MOE_TM, D_FF), bf16),
        grid_spec=pltpu.PrefetchScalarGridSpec(
            num_scalar_prefetch=2,
            grid=(2 * D_FF // tn, nblk),
            in_specs=[pl.BlockSpec((MOE_TM, D), lambda j, i, be, nv: (i, 0)),
                      pl.BlockSpec((None, None, D, tn), lambda j, i, be, nv: (layer, be[i], 0, j)),
                      pl.BlockSpec((None, None, 1, tn), lambda j, i, be, nv: (layer, be[i], 0, j)),
                      pl.BlockSpec((256, 256), lambda j, i, be, nv: (0, 0))],
            out_specs=pl.BlockSpec((MOE_TM, tn // 2), lambda j, i, be, nv: (i, j)),
            scratch_shapes=[pltpu.VMEM((D, tn), bf16)]),
        compiler_params=_cp(("arbitrary", "arbitrary"), 48),
        name="moe_up",
    )(be, nv, xs, w1, b1, perm)


def _moe_down_kernel(be_ref, nv_ref, a_ref, w_ref, b_ref, g_ref, o_ref, wbf_ref):
    i = pl.program_id(1)

    @pl.when(_first_of_expert(be_ref, i))
    def _():
        wbf_ref[...] = w_ref[...].astype(bf16)

    @pl.when(i < nv_ref[0])
    def _():
        y = jnp.dot(a_ref[...], wbf_ref[...], preferred_element_type=f32) + b_ref[...]
        o_ref[...] = y * g_ref[...]

    @pl.when(i >= nv_ref[0])
    def _():
        o_ref[...] = jnp.zeros_like(o_ref)


def _moe_down(be, nv, act, w2, b2, g_slot, layer, nblk):
    tn = 1024
    return pl.pallas_call(
        _moe_down_kernel,
        out_shape=jax.ShapeDtypeStruct((nblk * MOE_TM, D), f32),
        grid_spec=pltpu.PrefetchScalarGridSpec(
            num_scalar_prefetch=2,
            grid=(D // tn, nblk),
            in_specs=[pl.BlockSpec((MOE_TM, D_FF), lambda j, i, be, nv: (i, 0)),
                      pl.BlockSpec((None, None, D_FF, tn), lambda j, i, be, nv: (layer, be[i], 0, j)),
                      pl.BlockSpec((None, None, 1, tn), lambda j, i, be, nv: (layer, be[i], 0, j)),
                      pl.BlockSpec((MOE_TM, 1), lambda j, i, be, nv: (i, 0))],
            out_specs=pl.BlockSpec((MOE_TM, tn), lambda j, i, be, nv: (i, j)),
            scratch_shapes=[pltpu.VMEM((D_FF, tn), bf16)]),
        compiler_params=_cp(("arbitrary", "arbitrary"), 48),
        name="moe_down",
    )(be, nv, act, w2, b2, g_slot)


def _combine_kernel(slot_ref, y_hbm, x_ref, gate_ref, o_ref, buf, sem, *, tm):
    base = pl.program_id(0) * tm

    def issue(r, c):
        for k in range(TOP_K):
            s = slot_ref[(base + r) * TOP_K + k]
            pltpu.make_async_copy(y_hbm.at[pl.ds(s, 1), :], buf.at[k, pl.ds(r, 1), :], sem).start()
        return c

    lax.fori_loop(0, tm, issue, 0)

    def wait(r, c):
        for k in range(TOP_K):
            pltpu.make_async_copy(y_hbm.at[pl.ds(0, 1), :], buf.at[k, pl.ds(0, 1), :], sem).wait()
        return c

    lax.fori_loop(0, tm, wait, 0)
    acc = (buf[0] + buf[1]) + (buf[2] + buf[3])
    o_ref[...] = x_ref[...] + gate_ref[...] * acc


def _combine(slot_flat, ys, x, mod3, gate_chunk, blk, mrow, nblk):
    tm = RB
    return pl.pallas_call(
        functools.partial(_combine_kernel, tm=tm),
        out_shape=jax.ShapeDtypeStruct((nblk * tm, D), f32),
        grid_spec=pltpu.PrefetchScalarGridSpec(
            num_scalar_prefetch=1,
            grid=(nblk,),
            in_specs=[pl.BlockSpec(memory_space=pl.ANY),
                      pl.BlockSpec((tm, D), lambda i, s: (blk(i), 0)),
                      pl.BlockSpec((None, 1, D), lambda i, s: (mrow(i), 0, gate_chunk))],
            out_specs=pl.BlockSpec((tm, D), lambda i, s: (i, 0)),
            scratch_shapes=[pltpu.VMEM((TOP_K, tm, D), f32), pltpu.SemaphoreType.DMA(())]),
        compiler_params=_cp(("arbitrary",)),
        name="moe_combine",
    )(slot_flat, ys, x, mod3)


def _moe_layer(x, g3, mod3, rw, rb3, w1, b1, w2, b2, perm, layer, blk, mrow, nblk):
    t = nblk * RB
    h, idx, gate, rank, cnt = _norm_router(x, g3, mod3, rw, rb3, layer, 3, blk, mrow, nblk)
    n_assign = t * TOP_K
    nb = n_assign // MOE_TM + N_EXPERTS
    nr = nb * MOE_TM
    counts = cnt[0].astype(jnp.int32)
    padded = (counts + MOE_TM - 1) // MOE_TM * MOE_TM
    pad_end = jnp.cumsum(padded)
    pad_start = pad_end - padded
    slot = pad_start[idx] + rank
    slot_flat = slot.reshape(-1)
    tok = jnp.repeat(jnp.arange(t, dtype=jnp.int32), TOP_K)
    slot_tok = jnp.zeros((nr,), jnp.int32).at[slot_flat].set(tok)
    g_slot = jnp.zeros((nr,), f32).at[slot_flat].set(gate.reshape(-1)).reshape(nr, 1)
    nv = (pad_end[-1] // MOE_TM).astype(jnp.int32)
    blocks = jnp.minimum(jnp.arange(nb, dtype=jnp.int32), nv - 1)
    be = jnp.minimum(jnp.searchsorted(pad_end, blocks * MOE_TM, side="right"),
                     N_EXPERTS - 1).astype(jnp.int32)
    nv1 = nv.reshape(1)
    xs = _gather_rows(slot_tok, h, nr)
    act = _moe_up(be, nv1, xs, w1, b1, perm, layer, nb)
    ys = _moe_down(be, nv1, act, w2, b2, g_slot, layer, nb)
    return _combine(slot_flat, ys, x, mod3, 5, blk, mrow, nblk)


def _dot_nt(a, b):
    return lax.dot_general(a, b, (((1,), (1,)), ((), ())), preferred_element_type=f32)


def _dot_tn(a, b):
    return lax.dot_general(a, b, (((0,), (0,)), ((), ())), preferred_element_type=f32)


def _mlstm_kernel(q_ref, k_ref, v_ref, o_ref, gc_ref, gr_ref, ng_ref, y_ref, hf_ref, hb_ref, ct_ref):
    T = A_T
    nch = LA // T
    ct_ref[...] = jnp.zeros_like(ct_ref)
    r_i = lax.broadcasted_iota(jnp.int32, (T, T), 0)
    c_i = lax.broadcasted_iota(jnp.int32, (T, T), 1)
    lane = lax.broadcasted_iota(jnp.int32, (T, 128), 1)
    ones_blk = jnp.where(lane == 0, 1.0, 0.0).astype(bf16)

    def chunk(c, dirn, h_ref):
        r0 = pl.multiple_of(c * T, T)
        q = q_ref[pl.ds(r0, T), :]
        k = k_ref[pl.ds(r0, T), :]
        v = v_ref[pl.ds(r0, T), :]
        gc = gc_ref[pl.ds(r0, T), :]
        o = 5 * dirn
        a_col, m_col, negm = gc[:, o:o + 1], gc[:, o + 1:o + 2], gc[:, o + 2:o + 3]
        m_end, m_prev = gc[:, o + 3:o + 4], gc[:, o + 4:o + 5]
        a_row = gr_ref[dirn:dirn + 1, pl.ds(r0, T)]
        mask = (c_i <= r_i) if dirn == 0 else (c_i >= r_i)
        w = jnp.exp(jnp.where(mask, a_row - m_col, -jnp.inf))
        s = _dot_nt(q, k) * (A_DQK ** -0.5) * w
        v_aug = jnp.concatenate([v, ones_blk], axis=1)
        ct = ct_ref[dirn]
        w_carry = jnp.exp(m_prev - m_col)
        num = (jnp.dot(s.astype(bf16), v_aug, preferred_element_type=f32)
               + w_carry * jnp.dot(q, ct.astype(bf16), preferred_element_type=f32))
        den = num[:, A_DV:A_DV + 1]
        h_ref[pl.ds(r0, T), :] = num[:, :A_DV] / jnp.maximum(jnp.abs(den), jnp.exp(negm))
        w_in = jnp.exp(a_col - m_end)
        decay = jnp.exp(m_prev[0:1, :] - m_end[0:1, :])
        kv = _dot_tn(k, (w_in * v_aug.astype(f32)).astype(bf16)) * (A_DQK ** -0.5)
        ct_ref[dirn] = decay * ct + kv

    def step(it, carry):
        chunk(it, 0, hf_ref)
        chunk(jnp.where(it == 0, 0, nch - it), 1, hb_ref)
        return carry

    lax.fori_loop(0, nch, step, 0)
    h = hf_ref[...] + hb_ref[...]
    hn = h * lax.rsqrt(jnp.mean(h * h, axis=-1, keepdims=True) + EPS) * ng_ref[...]
    y_ref[...] = (hn * jax.nn.sigmoid(o_ref[...].astype(f32))).astype(y_ref.dtype)


def _mlstm(proj, gcol, grow, ng3):
    qb, vb = A_DQK, A_DV
    return pl.pallas_call(
        _mlstm_kernel,
        out_shape=jax.ShapeDtypeStruct((ROWS, HALF), bf16),
        grid=(NB_, A_HEADS),
        in_specs=[pl.BlockSpec((LA, qb), lambda b, h: (b, h)),
                  pl.BlockSpec((LA, qb), lambda b, h: (b, A_HEADS + h)),
                  pl.BlockSpec((LA, vb), lambda b, h: (b, (2 * A_HEADS * qb) // vb + h)),
                  pl.BlockSpec((LA, vb), lambda b, h: (b, (2 * A_HEADS * qb + HALF) // vb + h)),
                  pl.BlockSpec((None, None, LA, 16), lambda b, h: (b, h, 0, 0)),
                  pl.BlockSpec((None, None, 8, LA), lambda b, h: (b, h, 0, 0)),
                  pl.BlockSpec((None, 1, vb), lambda b, h: (h, 0, 0))],
        out_specs=pl.BlockSpec((LA, vb), lambda b, h: (b, h)),
        scratch_shapes=[pltpu.VMEM((LA, vb), f32), pltpu.VMEM((LA, vb), f32),
                        pltpu.VMEM((2, qb, vb + 128), f32)],
        compiler_params=_cp(("arbitrary", "arbitrary")),
        name="mlstm",
    )(proj, proj, proj, proj, gcol, grow, ng3)


def _mlstm_gate_tables(gates):
    i_f, f_f, i_b, f_b = jnp.split(gates, 4, axis=-1)

    def scan_tables(i_pre, f_pre):
        lf = jax.nn.log_sigmoid(f_pre)
        big_f = jnp.cumsum(lf, axis=1)
        a = i_pre - big_f
        m_run = jnp.maximum(lax.cummax(a, axis=1), 0.0)
        m_end = jnp.repeat(m_run[:, A_T - 1::A_T], A_T, axis=1)
        m_prev = jnp.concatenate([jnp.zeros_like(m_end[:, :A_T]), m_end[:, :-A_T]], axis=1)
        return a, m_run, -(big_f + m_run), m_end, m_prev

    def to_bwd(t):
        return jnp.concatenate([jnp.flip(t[:, :CTX], axis=1), jnp.flip(t[:, CTX:], axis=1)], axis=1)

    fw = scan_tables(i_f, f_f)
    bw = [to_bwd(t) for t in scan_tables(to_bwd(i_b), to_bwd(f_b))]
    cols = jnp.stack(list(fw) + bw, axis=-1)
    cols = jnp.pad(cols, ((0, 0), (0, 0), (0, 0), (0, 6))).transpose(0, 2, 1, 3)
    rows = jnp.stack([fw[0], bw[0]], axis=1).transpose(0, 3, 1, 2)
    rows = jnp.pad(rows, ((0, 0), (0, 0), (0, 6), (0, 0)))
    return cols, rows


def _swap_halves(t):
    hd = t.shape[-1] // 2
    return jnp.concatenate([t[..., hd:], t[..., :hd]], axis=-1)


def _attn_kernel(q_ref, k_ref, v_ref, cos_ref, sin_ref, gq_ref, gk_ref, sink_ref, o_ref):
    n = pl.program_id(2)
    scale = B_DH ** -0.5
    nq = B_G * 128

    def norm(t, g_ref):
        return t * lax.rsqrt(jnp.mean(t * t, axis=-1, keepdims=True) + EPS) * g_ref[...]

    qn = norm(q_ref[...].astype(f32), gq_ref)
    kc = norm(k_ref[0:CTX, :].astype(f32), gk_ref).astype(bf16)
    vc = v_ref[0:CTX, :]
    sink = sink_ref[...]

    def finish(parts):
        m = sink
        for s, _ in parts:
            m = jnp.maximum(m, jnp.max(s, axis=-1, keepdims=True))
        den = jnp.exp(sink - m)
        acc = None
        for s, vv in parts:
            p = jnp.exp(s - m)
            den = den + jnp.sum(p, axis=-1, keepdims=True)
            pv = jnp.dot(p.astype(bf16), vv, preferred_element_type=f32)
            acc = pv if acc is None else acc + pv
        o_ref[...] = (acc / den).reshape(B_G, 128, B_DH).astype(o_ref.dtype)

    @pl.when(n < CTX // 128)
    def _():
        s_ctx = _dot_nt(qn.reshape(nq, B_DH).astype(bf16), kc) * scale
        finish([(s_ctx, vc)])

    @pl.when(n >= CTX // 128)
    def _():
        nl = n - CTX // 128
        pos0 = pl.multiple_of(nl * 128, 128)
        cq = cos_ref[pl.ds(pos0, 128), :]
        sq = sin_ref[pl.ds(pos0, 128), :]
        qr = qn * cq[None] + _swap_halves(qn) * sq[None]
        start = pl.multiple_of(jnp.clip((nl - 1) * 128, 0, SEQ - 3 * 128), 128)
        kb = norm(k_ref[pl.ds(CTX + start, 3 * 128), :].astype(f32), gk_ref)
        kb = kb * cos_ref[pl.ds(start, 3 * 128), :] + _swap_halves(kb) * sin_ref[pl.ds(start, 3 * 128), :]
        vb = v_ref[pl.ds(CTX + start, 3 * 128), :]
        s_band = _dot_nt(qr.reshape(nq, B_DH).astype(bf16), kb.astype(bf16)) * scale
        q_pos = pos0 + lax.broadcasted_iota(jnp.int32, (nq, 3 * 128), 0) % 128
        k_pos = start + lax.broadcasted_iota(jnp.int32, (nq, 3 * 128), 1)
        s_band = jnp.where(jnp.abs(q_pos - k_pos) <= WINDOW, s_band, -jnp.inf)
        s_ctx = _dot_nt(qn.reshape(nq, B_DH).astype(bf16), kc) * scale
        finish([(s_band, vb), (s_ctx, vc)])


def _attention(q5, k4, v4, cosf, sinf, gq, gk, sink_col):
    nblk = LA // 128
    return pl.pallas_call(
        _attn_kernel,
        out_shape=jax.ShapeDtypeStruct(q5.shape, bf16),
        grid=(NB_, B_KV, nblk),
        in_specs=[pl.BlockSpec((None, None, B_G, 128, B_DH), lambda b, kh, n: (b, kh, 0, n, 0)),
                  pl.BlockSpec((None, None, LA, B_DH), lambda b, kh, n: (b, kh, 0, 0)),
                  pl.BlockSpec((None, None, LA, B_DH), lambda b, kh, n: (b, kh, 0, 0)),
                  pl.BlockSpec((SEQ, B_DH), lambda b, kh, n: (0, 0)),
                  pl.BlockSpec((SEQ, B_DH), lambda b, kh, n: (0, 0)),
                  pl.BlockSpec((1, B_DH), lambda b, kh, n: (0, 0)),
                  pl.BlockSpec((1, B_DH), lambda b, kh, n: (0, 0)),
                  pl.BlockSpec((None, B_G * 128, 1), lambda b, kh, n: (kh, 0, 0))],
        out_specs=pl.BlockSpec((None, None, B_G, 128, B_DH), lambda b, kh, n: (b, kh, 0, n, 0)),
        compiler_params=_cp(("arbitrary", "arbitrary", "arbitrary")),
        name="window_attn",
    )(q5, k4, v4, cosf, sinf, gq, gk, sink_col)


def _rope_tables():
    rows = SEQ // GRID_W
    row = jnp.repeat(jnp.arange(rows), GRID_W).astype(f32)
    col = jnp.tile(jnp.arange(GRID_W), rows).astype(f32)
    n_freq = B_DH // 4
    inv = ROPE_BASE ** (-jnp.arange(n_freq, dtype=f32) / n_freq)
    ang = jnp.concatenate([row[:, None] * inv, col[:, None] * inv], axis=-1)
    cos, sin = jnp.cos(ang), jnp.sin(ang)
    return jnp.concatenate([cos, cos], axis=-1), jnp.concatenate([-sin, sin], axis=-1)


def _s5_kernel(u_ref, mi_ref, wif_ref, wib_ref, wof_ref, wob_ref, af_ref, ab_ref,
               y_ref, zf_ref, zb_ref, xf_ref, xb_ref):
    nck = LA // S5_T
    nctx = CTX // S5_T
    u = u_ref[...]
    zf_ref[...] = jnp.dot(u, wif_ref[...], preferred_element_type=f32)
    zb_ref[...] = jnp.dot(u, wib_ref[...], preferred_element_type=f32)
    a_f = af_ref[...]
    a_b = ab_ref[...]

    def cmul_add(a, x, z):
        ar, ai = a[:, :128], a[:, 128:]
        xr, xi = x[:, :128], x[:, 128:]
        return jnp.concatenate([ar * xr - ai * xi + z[:, :128], ar * xi + ai * xr + z[:, 128:]], axis=1)

    def step(j, carry):
        x_f, x_b = carry
        rf = pl.multiple_of(j * 8, 8)
        kb = jnp.where(j < nctx, nctx - 1 - j, nck + nctx - 1 - j)
        rb = pl.multiple_of(kb * 8, 8)
        xf_ref[pl.ds(rf, 8), :] = x_f
        xb_ref[pl.ds(rb, 8), :] = x_b
        return (cmul_add(a_f, x_f, zf_ref[pl.ds(rf, 8), :]),
                cmul_add(a_b, x_b, zb_ref[pl.ds(rb, 8), :]))

    zero = jnp.zeros((8, 256), f32)
    lax.fori_loop(0, nck, step, (zero, zero))
    y = jnp.dot(xf_ref[...].astype(bf16), wof_ref[...], preferred_element_type=f32)
    y = y + jnp.dot(xb_ref[...].astype(bf16), wob_ref[...], preferred_element_type=f32)
    y_ref[:, :256] = y[:, :256] + jnp.dot(u[:, :256], mi_ref[0], preferred_element_type=f32)
    y_ref[:, 256:] = y[:, 256:] + jnp.dot(u[:, 256:], mi_ref[1], preferred_element_type=f32)


def _s5_core(u3, mats):
    npair = C_GROUPS // 2
    rows = (LA // S5_T) * 8
    mi, wif, wib, wof, wob, af, ab = mats
    return pl.pallas_call(
        _s5_kernel,
        out_shape=jax.ShapeDtypeStruct((npair, rows, 512), f32),
        grid=(npair,),
        in_specs=[pl.BlockSpec((None, rows, 512), lambda p: (p, 0, 0)),
                  pl.BlockSpec((None, 2, 256, 256), lambda p: (p, 0, 0, 0)),
                  pl.BlockSpec((None, 512, 256), lambda p: (p, 0, 0)),
                  pl.BlockSpec((None, 512, 256), lambda p: (p, 0, 0)),
                  pl.BlockSpec((None, 256, 512), lambda p: (p, 0, 0)),
                  pl.BlockSpec((None, 256, 512), lambda p: (p, 0, 0)),
                  pl.BlockSpec((None, 1, 256), lambda p: (p, 0, 0)),
                  pl.BlockSpec((None, 1, 256), lambda p: (p, 0, 0))],
        out_specs=pl.BlockSpec((None, rows, 512), lambda p: (p, 0, 0)),
        scratch_shapes=[pltpu.VMEM((rows, 256), f32)] * 4,
        compiler_params=_cp(("arbitrary",)),
        name="s5_core",
    )(u3, mi, wif, wib, wof, wob, af, ab)


def _s5_matrices(a_re, a_im, log_dt, b_re, b_im, c_re, c_im):
    T = S5_T
    lam = lax.complex(a_re.astype(f32), a_im.astype(f32))
    dt = jnp.exp(log_dt.astype(f32))[..., None]
    a_bar = jnp.exp(lam * dt)
    b_bar = ((a_bar - 1) / lam)[..., None] * lax.complex(b_re.astype(f32), b_im.astype(f32))
    tau = jnp.arange(T + 1, dtype=f32)
    apow = jnp.exp((lam * dt)[..., None] * tau)
    cc = lax.complex(c_re.astype(f32), c_im.astype(f32))
    kern = jnp.einsum("gxp,dgpt,dgpc->dgtxc", cc, apow[..., :T], b_bar).real
    t_i = jnp.arange(T)
    lag = t_i[:, None] - t_i[None, :]
    kf = kern[0][:, jnp.clip(lag, 0, T - 1)] * (lag >= 0)[None, :, :, None, None]
    kb = kern[1][:, jnp.clip(-lag, 0, T - 1)] * (lag <= 0)[None, :, :, None, None]
    m_intra = (kf + kb).transpose(0, 2, 4, 1, 3).reshape(C_GROUPS, T * C_GROUP, T * C_GROUP)
    w_f = apow[0][:, :, T - 1 - t_i][..., None] * b_bar[0][:, :, None, :]
    w_b = apow[1][:, :, t_i][..., None] * b_bar[1][:, :, None, :]

    def win(w):
        w = w.transpose(0, 2, 3, 1).reshape(C_GROUPS, T * C_GROUP, C_STATE)
        return w.real, w.imag

    k_f = cc[:, :, :, None] * apow[0][:, None, :, 1 + t_i]
    k_b = cc[:, :, :, None] * apow[1][:, None, :, T - t_i]

    def wout(k):
        k = k.transpose(0, 2, 3, 1).reshape(C_GROUPS, C_STATE, T * C_GROUP)
        return k.real, -k.imag

    def pair_in(re, im):
        z = jnp.zeros_like(re)
        top = jnp.concatenate([re[0::2], z[0::2], im[0::2], z[0::2]], axis=-1)
        bot = jnp.concatenate([z[1::2], re[1::2], z[1::2], im[1::2]], axis=-1)
        return jnp.concatenate([top, bot], axis=1).astype(bf16)

    def pair_out(re, im):
        z = jnp.zeros_like(re)
        rows = [jnp.concatenate([re[0::2], z[0::2]], axis=-1), jnp.concatenate([z[1::2], re[1::2]], axis=-1),
                jnp.concatenate([im[0::2], z[0::2]], axis=-1), jnp.concatenate([z[1::2], im[1::2]], axis=-1)]
        return jnp.concatenate(rows, axis=1).astype(bf16)

    def pair_a(a):
        return jnp.concatenate([a.real[0::2], a.real[1::2], a.imag[0::2], a.imag[1::2]], axis=-1)[:, None, :]

    mi = m_intra.reshape(C_GROUPS // 2, 2, T * C_GROUP, T * C_GROUP).astype(bf16)
    return (mi, pair_in(*win(w_f)), pair_in(*win(w_b)), pair_out(*wout(k_f)), pair_out(*wout(k_b)),
            pair_a(apow[0][..., T]), pair_a(apow[1][..., T]))


def _s5_glu_kernel(y_ref, u_ref, d_ref, w_ref, b_ref, o_ref, wbf_ref):
    @pl.when(pl.program_id(0) == 0)
    def _():
        wbf_ref[...] = w_ref[...].astype(bf16)

    y = y_ref[...] + d_ref[...] * u_ref[...].astype(f32)
    y = jax.nn.gelu(y, approximate=True)
    z = jnp.dot(y.astype(bf16), wbf_ref[...], preferred_element_type=f32) + b_ref[...]
    o_ref[...] = (y * jax.nn.sigmoid(z)).astype(o_ref.dtype)


def _s5_glu(y_ssm, proj, d2, glu_w, glu_b2):
    nblk = NB_ * LBPB
    return pl.pallas_call(
        _s5_glu_kernel,
        out_shape=jax.ShapeDtypeStruct((nblk * RB, HALF), bf16),
        grid=(nblk,),
        in_specs=[pl.BlockSpec((RB, HALF), lambda i: (_lat_blk(i), 0)),
                  pl.BlockSpec((RB, HALF), lambda i: (_lat_blk(i), 0)),
                  pl.BlockSpec((1, HALF), lambda i: (0, 0)),
                  pl.BlockSpec((HALF, HALF), lambda i: (0, 0)),
                  pl.BlockSpec((1, HALF), lambda i: (0, 0))],
        out_specs=pl.BlockSpec((RB, HALF), lambda i: (i, 0)),
        scratch_shapes=[pltpu.VMEM((HALF, HALF), bf16)],
        compiler_params=_cp(("arbitrary",)),
        name="s5_glu",
    )(y_ssm, proj, d2, glu_w, glu_b2)


def _hyena_filter(L, w1, b1, w2, b2, w3, freq):
    t = jnp.linspace(0.0, 1.0, L, dtype=f32)[:, None]
    w = 2 * math.pi * jnp.arange(L, dtype=f32)[:, None] / L
    bands = jnp.linspace(1e-4, HY_BANDS - 1, HY_BANDS, dtype=f32)
    feats = jnp.concatenate([t, jnp.cos(bands * w), -jnp.sin(bands * w)], axis=-1)
    h = jnp.sin(freq * (feats @ w1 + b1))
    h = jnp.sin(freq * (h @ w2 + b2))
    h = h @ w3
    fast = abs(math.log(HY_DECAY_TARGET) / HY_FAST_PCT)
    slow = abs(math.log(HY_DECAY_TARGET) / HY_SLOW_PCT)
    deltas = jnp.tile(jnp.linspace(slow, fast, HALF, dtype=f32), 2)
    h = h * jnp.exp(-t * deltas)
    h_fwd, h_bwd = jnp.split(h, 2, axis=-1)
    filt = jnp.concatenate([h_fwd, jnp.zeros((1, HALF), f32), h_bwd[:0:-1]], axis=0)
    return filt / jnp.sum(jnp.abs(filt), axis=0, keepdims=True)


def _hyena_jnp(z, conv_w, conv_b, f_w1, f_b1, f_w2, f_b2, f_w3, f_freq, hy_bias):
    L = z.shape[1]
    zp = jnp.pad(z, ((0, 0), (1, 1), (0, 0)))
    zc = sum(zp[:, j:j + L] * conv_w[j] for j in range(3)) + conv_b
    x0, x1, v = jnp.split(zc, 3, axis=-1)
    filt = _hyena_filter(L, f_w1, f_b1, f_w2, f_b2, f_w3, f_freq)
    u = x1 * v
    y = jnp.fft.irfft(jnp.fft.rfft(u, n=2 * L, axis=1) * jnp.fft.rfft(filt, n=2 * L, axis=0),
                      n=2 * L, axis=1)[:, :L]
    return x0 * (y + u * hy_bias)


def _swiglu_perm():
    j = np.arange(256)
    src = np.where(j < 128, 2 * j, 2 * (j - 128) + 1)
    p = np.zeros((256, 256), np.float32)
    p[src, j] = 1.0
    return jnp.asarray(p, dtype=bf16)


def kernel(x, c, ctx, c_ctx, ada_w, ada_b, norm_mix_g, norm_ffn_g, ev_w_in, ev_gate_b, ev_h_norm_g,
           ev_q_norm_g, ev_k_norm_g, ev_sink, ev_w_out, od_w_in, od_a_re, od_a_im, od_log_dt, od_b_re,
           od_b_im, od_c_re, od_c_im, od_d, od_glu_w, od_glu_b, od_conv_w, od_conv_b, od_filt_w1,
           od_filt_b1, od_filt_w2, od_filt_b2, od_filt_w3, od_filt_freq, od_hy_bias, od_w_out,
           moe_router_w, moe_router_b, moe_w1, moe_b1, moe_w2, moe_b2):
    depth = ada_w.shape[0]
    xu = jnp.concatenate([ctx, x], axis=1).reshape(ROWS, D)
    cond8 = jnp.concatenate([c_ctx[None], c, jnp.zeros((3, D), f32)], axis=0)
    ada_b3 = ada_b.reshape(depth, 1, 6 * D)
    g_mix3 = norm_mix_g.reshape(depth, 1, D)
    g_ffn3 = norm_ffn_g.reshape(depth, 1, D)
    rb3 = moe_router_b.reshape(depth, 1, N_EXPERTS)
    b1_4 = moe_b1.reshape(depth, N_EXPERTS, 1, 2 * D_FF)
    b2_4 = moe_b2.reshape(depth, N_EXPERTS, 1, D)
    perm = _swiglu_perm()
    nall = ROWS // RB
    nlat = NB_ * LBPB

    mod3 = _ada(cond8, ada_w, ada_b3, 0).reshape(8, 1, 6 * D)
    h = _norm_mod(xu, g_mix3, mod3, 0, 0, _ident, _mod_row, nall, bf16)
    w_in = ev_w_in[0]
    n_a = 2 * A_HEADS * A_DQK + 2 * HALF
    n_g = 4 * A_HEADS
    w_main = jnp.concatenate([w_in[:, :n_a], w_in[:, n_a + n_g:]], axis=1)
    w_gate = jnp.pad(w_in[:, n_a:n_a + n_g], ((0, 0), (0, 128 - n_g)))
    proj = _matmul([h], w_main, (), w_main.shape[1], 512, nall, _ident, bf16, name="even_in")
    gates = _matmul([h], w_gate, (), 128, 128, nall, _ident, f32, name="even_gates")[:, :n_g]
    gates = (gates + ev_gate_b[0]).reshape(NB_, LA, n_g)
    gcol, grow = _mlstm_gate_tables(gates)
    y_a = _mlstm(proj, gcol, grow, ev_h_norm_g[0].reshape(A_HEADS, 1, A_DV))
    qa = proj[:, n_a:n_a + HALF].reshape(NB_, LA, B_KV, B_G, B_DH).transpose(0, 2, 3, 1, 4)
    ka = proj[:, n_a + HALF:n_a + HALF + B_KV * B_DH].reshape(NB_, LA, B_KV, B_DH).transpose(0, 2, 1, 3)
    va = proj[:, n_a + HALF + B_KV * B_DH:].reshape(NB_, LA, B_KV, B_DH).transpose(0, 2, 1, 3)
    cosf, sinf = _rope_tables()
    sink_col = jnp.repeat(ev_sink[0].astype(f32).reshape(B_KV, B_G), 128, axis=1)[..., None]
    o5 = _attention(qa, ka, va, cosf, sinf, ev_q_norm_g[0].reshape(1, B_DH),
                    ev_k_norm_g[0].reshape(1, B_DH), sink_col)
    y_b = o5.transpose(0, 3, 1, 2, 4).reshape(ROWS, HALF)
    xu = _matmul([y_a, y_b], ev_w_out, (0,), D, 512, nall, _ident, f32,
                 res=xu, res_blk=_ident, mrow=_mod_row, mod3=mod3, gate_chunk=2, name="even_out")
    xu = _moe_layer(xu, g_ffn3, mod3, moe_router_w, rb3, moe_w1, b1_4, moe_w2, b2_4, perm,
                    0, _ident, _mod_row, nall)

    mod3 = _ada(cond8, ada_w, ada_b3, 1).reshape(8, 1, 6 * D)
    h = _norm_mod(xu, g_mix3, mod3, 1, 0, _ident, _mod_row, nall, bf16)
    proj = _matmul([h], od_w_in, (0,), 4 * HALF, 512, nall, _ident, bf16, name="odd_in")
    nck = LA // S5_T
    u6 = proj[:, :HALF].reshape(NB_, nck, S5_T, C_GROUPS // 2, 2, C_GROUP)
    u3 = jnp.pad(u6.transpose(3, 1, 0, 4, 2, 5), ((0, 0), (0, 0), (0, 8 - NB_), (0, 0), (0, 0), (0, 0)))
    u3 = u3.reshape(C_GROUPS // 2, nck * 8, 2 * S5_T * C_GROUP)
    mats = _s5_matrices(od_a_re[0], od_a_im[0], od_log_dt[0], od_b_re[0], od_b_im[0], od_c_re[0], od_c_im[0])
    y3 = _s5_core(u3, mats)
    y_ssm = y3.reshape(C_GROUPS // 2, nck, 8, 2, S5_T, C_GROUP)[:, :, :NB_]
    y_ssm = y_ssm.transpose(2, 1, 4, 0, 3, 5).reshape(ROWS, HALF)
    s_l = _s5_glu(y_ssm, proj, od_d[0].reshape(1, HALF), od_glu_w[0], od_glu_b[0].reshape(1, HALF))
    z_l = proj.reshape(NB_, LA, 4 * HALF)[:, CTX:, HALF:].astype(f32)
    hy = _hyena_jnp(z_l, od_conv_w[0], od_conv_b[0], od_filt_w1[0], od_filt_b1[0], od_filt_w2[0],
                    od_filt_b2[0], od_filt_w3[0], od_filt_freq[0], od_hy_bias[0])
    hy_l = hy.reshape(NB_ * SEQ, HALF).astype(bf16)
    x_lat = _matmul([s_l, hy_l], od_w_out, (0,), D, 512, nlat, _ident, f32,
                    res=xu, res_blk=_lat_blk, mrow=_lat_mod_row, mod3=mod3, gate_chunk=2, name="odd_out")
    out = _moe_layer(x_lat, g_ffn3, mod3, moe_router_w, rb3, moe_w1, b1_4, moe_w2, b2_4, perm,
                     1, _ident, _lat_mod_row, nlat)
    return out.reshape(NB_, SEQ, D)
```

```python
import functools
import math

import jax
import jax.numpy as jnp
import numpy as np
from jax import lax
from jax.experimental import pallas as pl
from jax.experimental.pallas import tpu as pltpu

f32 = jnp.float32
bf16 = jnp.bfloat16

D = 2048
NB_ = 4
SEQ = 2048
CTX = 256
LA = SEQ + CTX
ROWS = NB_ * LA
RB = 256
BPB = LA // RB
LBPB = SEQ // RB
EPS = 1e-6
GRID_W = 64

HALF = D // 2
A_HEADS = 4
A_DV = HALF // A_HEADS
A_DQK = A_DV // 2
A_T = 256
B_HEADS = 16
B_KV = 4
B_G = B_HEADS // B_KV
B_DH = HALF // B_HEADS
WINDOW = 128
ROPE_BASE = 10000.0
C_GROUP = 16
C_GROUPS = HALF // C_GROUP
C_STATE = 64
S5_T = 16
HY_BANDS = 16
HY_DECAY_TARGET = 1e-2
HY_FAST_PCT = 0.3
HY_SLOW_PCT = 1.5
N_EXPERTS = 32
TOP_K = 4
D_FF = D
SWIGLU_LIMIT = 7.0
SWIGLU_ALPHA = 1.702
MOE_TM = 256

EVEN_SPLITS = (A_HEADS * A_DQK, A_HEADS * A_DQK, HALF, HALF, 4 * A_HEADS,
               B_HEADS * B_DH, B_KV * B_DH, B_KV * B_DH)

VMEM_MB = 1024 * 1024


def _cp(sem, vmem_mb=40):
    return pltpu.CompilerParams(dimension_semantics=sem, vmem_limit_bytes=vmem_mb * VMEM_MB)


def _lat_blk(i):
    return (i // LBPB) * BPB + 1 + i % LBPB


def _mod_row(u):
    return jnp.where(u % BPB == 0, 0, 1 + u // BPB)


def _lat_mod_row(i):
    return 1 + i // LBPB


def _ident(i):
    return i


def _ada_kernel(c_ref, w_ref, b_ref, o_ref):
    c = c_ref[...]
    s = c * jax.nn.sigmoid(c)
    o_ref[...] = jnp.dot(s.astype(bf16), w_ref[...].astype(bf16),
                         preferred_element_type=f32) + b_ref[...]


def _ada(cond8, ada_w, ada_b3, layer):
    tn = 1024
    n = 6 * D
    return pl.pallas_call(
        _ada_kernel,
        out_shape=jax.ShapeDtypeStruct((8, n), f32),
        grid=(n // tn,),
        in_specs=[pl.BlockSpec((8, D), lambda j: (0, 0)),
                  pl.BlockSpec((None, D, tn), lambda j: (layer, 0, j)),
                  pl.BlockSpec((None, 1, tn), lambda j: (layer, 0, j))],
        out_specs=pl.BlockSpec((8, tn), lambda j: (0, j)),
        compiler_params=_cp(("arbitrary",)),
        name="ada_params",
    )(cond8, ada_w, ada_b3)


def _norm_mod_kernel(x_ref, g_ref, sh_ref, sc_ref, o_ref):
    x = x_ref[...]
    y = x * lax.rsqrt(jnp.mean(x * x, axis=-1, keepdims=True) + EPS) * g_ref[...]
    o_ref[...] = (y * (1.0 + sc_ref[...]) + sh_ref[...]).astype(o_ref.dtype)


def _norm_mod(x, g3, mod3, layer, chunk, blk, mrow, nblk, out_dtype):
    return pl.pallas_call(
        _norm_mod_kernel,
        out_shape=jax.ShapeDtypeStruct((nblk * RB, D), out_dtype),
        grid=(nblk,),
        in_specs=[pl.BlockSpec((RB, D), lambda i: (blk(i), 0)),
                  pl.BlockSpec((None, 1, D), lambda i: (layer, 0, 0)),
                  pl.BlockSpec((None, 1, D), lambda i: (mrow(i), 0, chunk)),
                  pl.BlockSpec((None, 1, D), lambda i: (mrow(i), 0, chunk + 1))],
        out_specs=pl.BlockSpec((RB, D), lambda i: (i, 0)),
        compiler_params=_cp(("arbitrary",)),
        name="norm_mod",
    )(x, g3, mod3, mod3)


def _split_bf16(x):
    hi = x.astype(bf16)
    lo = (x - hi.astype(f32)).astype(bf16)
    return hi, lo


def _norm_router_kernel(x_ref, g_ref, sh_ref, sc_ref, rw_ref, rb_ref,
                        h_ref, idx_ref, gate_ref, rank_ref, cnt_ref, carry_ref):
    i = pl.program_id(0)

    @pl.when(i == 0)
    def _():
        carry_ref[...] = jnp.zeros_like(carry_ref)

    x = x_ref[...]
    y = x * lax.rsqrt(jnp.mean(x * x, axis=-1, keepdims=True) + EPS) * g_ref[...]
    h = y * (1.0 + sc_ref[...]) + sh_ref[...]
    h_ref[...] = h
    h_hi, h_lo = _split_bf16(h)
    w_hi, w_lo = _split_bf16(rw_ref[...])
    logits = (jnp.dot(h_hi, w_hi, preferred_element_type=f32)
              + jnp.dot(h_lo, w_hi, preferred_element_type=f32)
              + jnp.dot(h_hi, w_lo, preferred_element_type=f32)) + rb_ref[...]
    ne = logits.shape[-1]
    lane = lax.broadcasted_iota(jnp.int32, logits.shape, 1)
    l = logits
    vals, idxs = [], []
    for _ in range(TOP_K):
        m = jnp.max(l, axis=-1, keepdims=True)
        ix = jnp.min(jnp.where(l == m, lane, ne), axis=-1, keepdims=True)
        vals.append(m)
        idxs.append(ix)
        l = jnp.where(lane == ix, -jnp.inf, l)
    es = [jnp.exp(v - vals[0]) for v in vals]
    tot = es[0] + es[1] + es[2] + es[3]
    onehot = jnp.zeros(logits.shape, f32)
    for ix in idxs:
        onehot = onehot + jnp.where(lane == ix, 1.0, 0.0)
    r_i = lax.broadcasted_iota(jnp.int32, (RB, RB), 0)
    c_i = lax.broadcasted_iota(jnp.int32, (RB, RB), 1)
    tri = jnp.where(c_i < r_i, 1.0, 0.0).astype(bf16)
    prefix = jnp.dot(tri, onehot.astype(bf16), preferred_element_type=f32) + carry_ref[...]
    for k in range(TOP_K):
        idx_ref[:, k:k + 1] = idxs[k]
        gate_ref[:, k:k + 1] = es[k] / tot
        rk = jnp.sum(jnp.where(lane == idxs[k], prefix, 0.0), axis=-1, keepdims=True)
        rank_ref[:, k:k + 1] = rk.astype(jnp.int32)
    carry_ref[...] = carry_ref[...] + jnp.sum(onehot, axis=0, keepdims=True)
    cnt_ref[...] = carry_ref[...]


def _norm_router(x, g3, mod3, rw, rb3, layer, chunk, blk, mrow, nblk):
    t = nblk * RB
    return pl.pallas_call(
        _norm_router_kernel,
        out_shape=(jax.ShapeDtypeStruct((t, D), f32),
                   jax.ShapeDtypeStruct((t, TOP_K), jnp.int32),
                   jax.ShapeDtypeStruct((t, TOP_K), f32),
                   jax.ShapeDtypeStruct((t, TOP_K), jnp.int32),
                   jax.ShapeDtypeStruct((1, N_EXPERTS), f32)),
        grid=(nblk,),
        in_specs=[pl.BlockSpec((RB, D), lambda i: (blk(i), 0)),
                  pl.BlockSpec((None, 1, D), lambda i: (layer, 0, 0)),
                  pl.BlockSpec((None, 1, D), lambda i: (mrow(i), 0, chunk)),
                  pl.BlockSpec((None, 1, D), lambda i: (mrow(i), 0, chunk + 1)),
                  pl.BlockSpec((None, D, N_EXPERTS), lambda i: (layer, 0, 0)),
                  pl.BlockSpec((None, 1, N_EXPERTS), lambda i: (layer, 0, 0))],
        out_specs=(pl.BlockSpec((RB, D), lambda i: (i, 0)),
                   pl.BlockSpec((RB, TOP_K), lambda i: (i, 0)),
                   pl.BlockSpec((RB, TOP_K), lambda i: (i, 0)),
                   pl.BlockSpec((RB, TOP_K), lambda i: (i, 0)),
                   pl.BlockSpec((1, N_EXPERTS), lambda i: (0, 0))),
        scratch_shapes=[pltpu.VMEM((1, N_EXPERTS), f32)],
        compiler_params=_cp(("arbitrary",)),
        name="norm_router",
    )(x, g3, mod3, mod3, rw, rb3)


def _mm_kernel(*refs, n_a, k_split, epilogue):
    a_refs = refs[:n_a]
    w_ref = refs[n_a]
    pos = n_a + 1
    if epilogue:
        res_ref, gate_ref = refs[pos], refs[pos + 1]
        pos += 2
    o_ref, wbf_ref = refs[pos], refs[pos + 1]

    @pl.when(pl.program_id(1) == 0)
    def _():
        wbf_ref[...] = w_ref[...].astype(bf16)

    acc = None
    for t, a_ref in enumerate(a_refs):
        part = jnp.dot(a_ref[...], wbf_ref[t * k_split:(t + 1) * k_split, :],
                       preferred_element_type=f32)
        acc = part if acc is None else acc + part
    if epilogue:
        acc = res_ref[...] + gate_ref[...] * acc
    o_ref[...] = acc.astype(o_ref.dtype)


def _matmul(a_list, w, w_idx, n, tn, nblk, a_blk, out_dtype, *, a_col=None,
            res=None, res_blk=None, mrow=None, mod3=None, gate_chunk=None, name="matmul"):
    n_a = len(a_list)
    k = w.shape[-2]
    k_split = k // n_a
    lead = len(w_idx)
    a_col = a_col or [0] * n_a
    in_specs = [pl.BlockSpec((RB, k_split), functools.partial(lambda j, i, c: (a_blk(i), c), c=c))
                for c in a_col]
    in_specs.append(pl.BlockSpec((None,) * lead + (k, tn), lambda j, i: tuple(w_idx) + (0, j)))
    args = list(a_list) + [w]
    epilogue = res is not None
    if epilogue:
        per = D // tn
        in_specs.append(pl.BlockSpec((RB, tn), lambda j, i: (res_blk(i), j)))
        in_specs.append(pl.BlockSpec((None, 1, tn),
                                     lambda j, i: (mrow(i), 0, gate_chunk * per + j)))
        args += [res, mod3]
    return pl.pallas_call(
        functools.partial(_mm_kernel, n_a=n_a, k_split=k_split, epilogue=epilogue),
        out_shape=jax.ShapeDtypeStruct((nblk * RB, n), out_dtype),
        grid=(n // tn, nblk),
        in_specs=in_specs,
        out_specs=pl.BlockSpec((RB, tn), lambda j, i: (i, j)),
        scratch_shapes=[pltpu.VMEM((k, tn), bf16)],
        compiler_params=_cp(("arbitrary", "arbitrary")),
        name=name,
    )(*args)


def _gather_rows_kernel(tok_ref, h_hbm, o_ref, buf, sem, *, rows, nsteps):
    i = pl.program_id(0)

    def issue(blk, slot):
        base = blk * rows

        def body(r, c):
            t = tok_ref[base + r]
            pltpu.make_async_copy(h_hbm.at[pl.ds(t, 1), :], buf.at[slot, pl.ds(r, 1), :],
                                  sem.at[slot]).start()
            return c

        lax.fori_loop(0, rows, body, 0, unroll=8)

    @pl.when(i == 0)
    def _():
        issue(0, 0)

    @pl.when(i + 1 < nsteps)
    def _():
        issue(i + 1, (i + 1) % 2)

    slot = i % 2
    pltpu.make_async_copy(h_hbm.at[pl.ds(0, rows), :], buf.at[slot], sem.at[slot]).wait()
    o_ref[...] = buf[slot].astype(o_ref.dtype)


def _gather_rows(slot_tok, h, nr):
    rows = 512
    nsteps = nr // rows
    return pl.pallas_call(
        functools.partial(_gather_rows_kernel, rows=rows, nsteps=nsteps),
        out_shape=jax.ShapeDtypeStruct((nr, D), bf16),
        grid_spec=pltpu.PrefetchScalarGridSpec(
            num_scalar_prefetch=1,
            grid=(nsteps,),
            in_specs=[pl.BlockSpec(memory_space=pl.ANY)],
            out_specs=pl.BlockSpec((rows, D), lambda i, tok: (i, 0)),
            scratch_shapes=[pltpu.VMEM((2, rows, D), f32), pltpu.SemaphoreType.DMA((2,))]),
        compiler_params=_cp(("arbitrary",)),
        name="moe_gather",
    )(slot_tok, h)


def _first_of_expert(be_ref, i):
    return jnp.logical_or(i == 0, be_ref[i] != be_ref[jnp.maximum(i - 1, 0)])


def _moe_up_kernel(be_ref, nv_ref, x_ref, w_ref, b_ref, p_ref, o_ref, wbf_ref, *, tn):
    i = pl.program_id(1)

    @pl.when(_first_of_expert(be_ref, i))
    def _():
        wbf_ref[...] = w_ref[...].astype(bf16)

    @pl.when(i < nv_ref[0])
    def _():
        h = jnp.dot(x_ref[...], wbf_ref[...], preferred_element_type=f32) + b_ref[...]
        for c in range(tn // 256):
            d = jnp.dot(h[:, c * 256:(c + 1) * 256].astype(bf16), p_ref[...],
                        preferred_element_type=f32)
            gate = jnp.minimum(d[:, :128], SWIGLU_LIMIT)
            lin = jnp.clip(d[:, 128:], -SWIGLU_LIMIT, SWIGLU_LIMIT)
            act = gate * jax.nn.sigmoid(SWIGLU_ALPHA * gate) * (lin + 1.0)
            o_ref[:, c * 128:(c + 1) * 128] = act.astype(o_ref.dtype)

    @pl.when(i >= nv_ref[0])
    def _():
        o_ref[...] = jnp.zeros_like(o_ref)


def _moe_up(be, nv, xs, w1, b1, perm, layer, nblk):
    tn = 2048
    return pl.pallas_call(
        functools.partial(_moe_up_kernel, tn=tn),
        out_shape=jax.ShapeDtypeStruct((nblk * MOE_TM, D_FF), bf16),
        grid_spec=pltpu.PrefetchScalarGridSpec(
            num_scalar_prefetch=2,
            grid=(2 * D_FF // tn, nblk),
            in_specs=[pl.BlockSpec((MOE_TM, D), lambda j, i, be, nv: (i, 0)),
                      pl.BlockSpec((None, None, D, tn), lambda j, i, be, nv: (layer, be[i], 0, j)),
                      pl.BlockSpec((None, None, 1, tn), lambda j, i, be, nv: (layer, be[i], 0, j)),
                      pl.BlockSpec((256, 256), lambda j, i, be, nv: (0, 0))],
            out_specs=pl.BlockSpec((MOE_TM, tn // 2), lambda j, i, be, nv: (i, j)),
            scratch_shapes=[pltpu.VMEM((D, tn), bf16)]),
        compiler_params=_cp(("arbitrary", "arbitrary"), 56),
        name="moe_up",
    )(be, nv, xs, w1, b1, perm)


def _moe_down_kernel(be_ref, nv_ref, a_ref, w_ref, b_ref, o_ref, wbf_ref):
    i = pl.program_id(0)

    @pl.when(_first_of_expert(be_ref, i))
    def _():
        wbf_ref[...] = w_ref[...].astype(bf16)

    @pl.when(i < nv_ref[0])
    def _():
        o_ref[...] = jnp.dot(a_ref[...], wbf_ref[...], preferred_element_type=f32) + b_ref[...]

    @pl.when(i >= nv_ref[0])
    def _():
        o_ref[...] = jnp.zeros_like(o_ref)


def _moe_down(be, nv, act, w2, b2, layer, nblk):
    return pl.pallas_call(
        _moe_down_kernel,
        out_shape=jax.ShapeDtypeStruct((nblk * MOE_TM, D), f32),
        grid_spec=pltpu.PrefetchScalarGridSpec(
            num_scalar_prefetch=2,
            grid=(nblk,),
            in_specs=[pl.BlockSpec((MOE_TM, D_FF), lambda i, be, nv: (i, 0)),
                      pl.BlockSpec((None, None, D_FF, D), lambda i, be, nv: (layer, be[i], 0, 0)),
                      pl.BlockSpec((None, None, 1, D), lambda i, be, nv: (layer, be[i], 0, 0))],
            out_specs=pl.BlockSpec((MOE_TM, D), lambda i, be, nv: (i, 0)),
            scratch_shapes=[pltpu.VMEM((D_FF, D), bf16)]),
        compiler_params=_cp(("arbitrary",), 56),
        name="moe_down",
    )(be, nv, act, w2, b2)


def _combine_kernel(slot_ref, y_hbm, x_ref, gmod_ref, w_ref, o_ref, buf, sem, *, tm, nsteps):
    i = pl.program_id(0)

    def issue(blk, slot):
        base = blk * tm

        def body(r, c):
            for k in range(TOP_K):
                s = slot_ref[(base + r) * TOP_K + k]
                pltpu.make_async_copy(y_hbm.at[pl.ds(s, 1), :], buf.at[slot, k, pl.ds(r, 1), :],
                                      sem.at[slot]).start()
            return c

        lax.fori_loop(0, tm, body, 0, unroll=4)

    @pl.when(i == 0)
    def _():
        issue(0, 0)

    @pl.when(i + 1 < nsteps)
    def _():
        issue(i + 1, (i + 1) % 2)

    slot = i % 2
    for k in range(TOP_K):
        pltpu.make_async_copy(y_hbm.at[pl.ds(0, tm), :], buf.at[slot, k], sem.at[slot]).wait()
    w = w_ref[...]
    acc = ((w[:, 0:1] * buf[slot, 0] + w[:, 1:2] * buf[slot, 1])
           + (w[:, 2:3] * buf[slot, 2] + w[:, 3:4] * buf[slot, 3]))
    o_ref[...] = x_ref[...] + gmod_ref[...] * acc


def _combine(slot_flat, ys, x, mod3, gate, gate_chunk, blk, mrow, nblk):
    tm = RB
    return pl.pallas_call(
        functools.partial(_combine_kernel, tm=tm, nsteps=nblk),
        out_shape=jax.ShapeDtypeStruct((nblk * tm, D), f32),
        grid_spec=pltpu.PrefetchScalarGridSpec(
            num_scalar_prefetch=1,
            grid=(nblk,),
            in_specs=[pl.BlockSpec(memory_space=pl.ANY),
                      pl.BlockSpec((tm, D), lambda i, s: (blk(i), 0)),
                      pl.BlockSpec((None, 1, D), lambda i, s: (mrow(i), 0, gate_chunk)),
                      pl.BlockSpec((tm, TOP_K), lambda i, s: (i, 0))],
            out_specs=pl.BlockSpec((tm, D), lambda i, s: (i, 0)),
            scratch_shapes=[pltpu.VMEM((2, TOP_K, tm, D), f32), pltpu.SemaphoreType.DMA((2,))]),
        compiler_params=_cp(("arbitrary",)),
        name="moe_combine",
    )(slot_flat, ys, x, mod3, gate)


def _moe_layer(x, g3, mod3, rw, rb3, w1, b1, w2, b2, perm, layer, blk, mrow, nblk):
    t = nblk * RB
    h, idx, gate, rank, cnt = _norm_router(x, g3, mod3, rw, rb3, layer, 3, blk, mrow, nblk)
    n_assign = t * TOP_K
    nb = n_assign // MOE_TM + N_EXPERTS
    nr = nb * MOE_TM
    counts = cnt[0].astype(jnp.int32)
    padded = (counts + MOE_TM - 1) // MOE_TM * MOE_TM
    pad_end = jnp.cumsum(padded)
    pad_start = pad_end - padded
    slot = pad_start[idx] + rank
    slot_flat = slot.reshape(-1)
    tok = jnp.repeat(jnp.arange(t, dtype=jnp.int32), TOP_K)
    slot_tok = jnp.zeros((nr,), jnp.int32).at[slot_flat].set(tok)
    nv = (pad_end[-1] // MOE_TM).astype(jnp.int32)
    blocks = jnp.minimum(jnp.arange(nb, dtype=jnp.int32), nv - 1)
    be = jnp.sum((pad_end[None, :] <= (blocks * MOE_TM)[:, None]).astype(jnp.int32), axis=1)
    be = jnp.minimum(be, N_EXPERTS - 1)
    nv1 = nv.reshape(1)
    xs = _gather_rows(slot_tok, h, nr)
    act = _moe_up(be, nv1, xs, w1, b1, perm, layer, nb)
    ys = _moe_down(be, nv1, act, w2, b2, layer, nb)
    return _combine(slot_flat, ys, x, mod3, gate, 5, blk, mrow, nblk)


def _dot_nt(a, b):
    return lax.dot_general(a, b, (((1,), (1,)), ((), ())), preferred_element_type=f32)


def _dot_tn(a, b):
    return lax.dot_general(a, b, (((0,), (0,)), ((), ())), preferred_element_type=f32)


def _mlstm_kernel(q_ref, k_ref, v_ref, o_ref, gc_ref, gr_ref, ng_ref, y_ref, hf_ref, hb_ref, ct_ref):
    T = A_T
    nch = LA // T
    ct_ref[...] = jnp.zeros_like(ct_ref)
    r_i = lax.broadcasted_iota(jnp.int32, (T, T), 0)
    c_i = lax.broadcasted_iota(jnp.int32, (T, T), 1)
    lane = lax.broadcasted_iota(jnp.int32, (T, 128), 1)
    ones_blk = jnp.where(lane == 0, 1.0, 0.0).astype(bf16)

    def chunk(c, dirn, h_ref):
        r0 = pl.multiple_of(c * T, T)
        q = q_ref[pl.ds(r0, T), :]
        k = k_ref[pl.ds(r0, T), :]
        v = v_ref[pl.ds(r0, T), :]
        gc = gc_ref[pl.ds(r0, T), :]
        o = 5 * dirn
        a_col, m_col, negm = gc[:, o:o + 1], gc[:, o + 1:o + 2], gc[:, o + 2:o + 3]
        m_end, m_prev = gc[:, o + 3:o + 4], gc[:, o + 4:o + 5]
        a_row = gr_ref[dirn:dirn + 1, pl.ds(r0, T)]
        mask = (c_i <= r_i) if dirn == 0 else (c_i >= r_i)
        w = jnp.exp(jnp.where(mask, a_row - m_col, -jnp.inf))
        s = _dot_nt(q, k) * (A_DQK ** -0.5) * w
        v_aug = jnp.concatenate([v, ones_blk], axis=1)
        ct = ct_ref[dirn]
        w_carry = jnp.exp(m_prev - m_col)
        num = (jnp.dot(s.astype(bf16), v_aug, preferred_element_type=f32)
               + w_carry * jnp.dot(q, ct.astype(bf16), preferred_element_type=f32))
        den = num[:, A_DV:A_DV + 1]
        h_ref[pl.ds(r0, T), :] = num[:, :A_DV] / jnp.maximum(jnp.abs(den), jnp.exp(negm))
        w_in = jnp.exp(a_col - m_end)
        decay = jnp.exp(m_prev[0:1, :] - m_end[0:1, :])
        kv = _dot_tn(k, (w_in * v_aug.astype(f32)).astype(bf16)) * (A_DQK ** -0.5)
        ct_ref[dirn] = decay * ct + kv

    def step(it, carry):
        chunk(it, 0, hf_ref)
        chunk(jnp.where(it == 0, 0, nch - it), 1, hb_ref)
        return carry

    lax.fori_loop(0, nch, step, 0)
    h = hf_ref[...] + hb_ref[...]
    hn = h * lax.rsqrt(jnp.mean(h * h, axis=-1, keepdims=True) + EPS) * ng_ref[...]
    y_ref[...] = (hn * jax.nn.sigmoid(o_ref[...].astype(f32))).astype(y_ref.dtype)


def _mlstm(proj, gcol, grow, ng3):
    qb, vb = A_DQK, A_DV
    return pl.pallas_call(
        _mlstm_kernel,
        out_shape=jax.ShapeDtypeStruct((ROWS, HALF), bf16),
        grid=(NB_, A_HEADS),
        in_specs=[pl.BlockSpec((LA, qb), lambda b, h: (b, h)),
                  pl.BlockSpec((LA, qb), lambda b, h: (b, A_HEADS + h)),
                  pl.BlockSpec((LA, vb), lambda b, h: (b, (2 * A_HEADS * qb) // vb + h)),
                  pl.BlockSpec((LA, vb), lambda b, h: (b, (2 * A_HEADS * qb + HALF) // vb + h)),
                  pl.BlockSpec((None, None, LA, 16), lambda b, h: (b, h, 0, 0)),
                  pl.BlockSpec((None, None, 8, LA), lambda b, h: (b, h, 0, 0)),
                  pl.BlockSpec((None, 1, vb), lambda b, h: (h, 0, 0))],
        out_specs=pl.BlockSpec((LA, vb), lambda b, h: (b, h)),
        scratch_shapes=[pltpu.VMEM((LA, vb), f32), pltpu.VMEM((LA, vb), f32),
                        pltpu.VMEM((2, qb, vb + 128), f32)],
        compiler_params=_cp(("arbitrary", "arbitrary")),
        name="mlstm",
    )(proj, proj, proj, proj, gcol, grow, ng3)


def _mlstm_gate_tables(gates):
    i_f, f_f, i_b, f_b = jnp.split(gates, 4, axis=-1)

    def scan_tables(i_pre, f_pre):
        lf = jax.nn.log_sigmoid(f_pre)
        big_f = jnp.cumsum(lf, axis=1)
        a = i_pre - big_f
        m_run = jnp.maximum(lax.cummax(a, axis=1), 0.0)
        m_end = jnp.repeat(m_run[:, A_T - 1::A_T], A_T, axis=1)
        m_prev = jnp.concatenate([jnp.zeros_like(m_end[:, :A_T]), m_end[:, :-A_T]], axis=1)
        return a, m_run, -(big_f + m_run), m_end, m_prev

    def to_bwd(t):
        return jnp.concatenate([jnp.flip(t[:, :CTX], axis=1), jnp.flip(t[:, CTX:], axis=1)], axis=1)

    fw = scan_tables(i_f, f_f)
    bw = [to_bwd(t) for t in scan_tables(to_bwd(i_b), to_bwd(f_b))]
    cols = jnp.stack(list(fw) + bw, axis=-1)
    cols = jnp.pad(cols, ((0, 0), (0, 0), (0, 0), (0, 6))).transpose(0, 2, 1, 3)
    rows = jnp.stack([fw[0], bw[0]], axis=1).transpose(0, 3, 1, 2)
    rows = jnp.pad(rows, ((0, 0), (0, 0), (0, 6), (0, 0)))
    return cols, rows


def _swap_halves(t):
    hd = t.shape[-1] // 2
    return jnp.concatenate([t[..., hd:], t[..., :hd]], axis=-1)


def _attn_kernel(q_ref, k_ref, v_ref, cos_ref, sin_ref, gq_ref, gk_ref, sink_ref, o_ref):
    n = pl.program_id(2)
    scale = B_DH ** -0.5
    nq = B_G * 128

    def norm(t, g_ref):
        return t * lax.rsqrt(jnp.mean(t * t, axis=-1, keepdims=True) + EPS) * g_ref[...]

    qn = norm(q_ref[...].astype(f32), gq_ref)
    kc = norm(k_ref[0:CTX, :].astype(f32), gk_ref).astype(bf16)
    vc = v_ref[0:CTX, :]
    sink = sink_ref[...]

    def finish(parts):
        m = sink
        for s, _ in parts:
            m = jnp.maximum(m, jnp.max(s, axis=-1, keepdims=True))
        den = jnp.exp(sink - m)
        acc = None
        for s, vv in parts:
            p = jnp.exp(s - m)
            den = den + jnp.sum(p, axis=-1, keepdims=True)
            pv = jnp.dot(p.astype(bf16), vv, preferred_element_type=f32)
            acc = pv if acc is None else acc + pv
        o_ref[...] = (acc / den).reshape(B_G, 128, B_DH).astype(o_ref.dtype)

    @pl.when(n < CTX // 128)
    def _():
        s_ctx = _dot_nt(qn.reshape(nq, B_DH).astype(bf16), kc) * scale
        finish([(s_ctx, vc)])

    @pl.when(n >= CTX // 128)
    def _():
        nl = n - CTX // 128
        pos0 = pl.multiple_of(nl * 128, 128)
        cq = cos_ref[pl.ds(pos0, 128), :]
        sq = sin_ref[pl.ds(pos0, 128), :]
        qr = qn * cq[None] + _swap_halves(qn) * sq[None]
        start = pl.multiple_of(jnp.clip((nl - 1) * 128, 0, SEQ - 3 * 128), 128)
        kb = norm(k_ref[pl.ds(CTX + start, 3 * 128), :].astype(f32), gk_ref)
        kb = kb * cos_ref[pl.ds(start, 3 * 128), :] + _swap_halves(kb) * sin_ref[pl.ds(start, 3 * 128), :]
        vb = v_ref[pl.ds(CTX + start, 3 * 128), :]
        s_band = _dot_nt(qr.reshape(nq, B_DH).astype(bf16), kb.astype(bf16)) * scale
        q_pos = pos0 + lax.broadcasted_iota(jnp.int32, (nq, 3 * 128), 0) % 128
        k_pos = start + lax.broadcasted_iota(jnp.int32, (nq, 3 * 128), 1)
        s_band = jnp.where(jnp.abs(q_pos - k_pos) <= WINDOW, s_band, -jnp.inf)
        s_ctx = _dot_nt(qn.reshape(nq, B_DH).astype(bf16), kc) * scale
        finish([(s_band, vb), (s_ctx, vc)])


def _attention(q5, k4, v4, cosf, sinf, gq, gk, sink_col):
    nblk = LA // 128
    return pl.pallas_call(
        _attn_kernel,
        out_shape=jax.ShapeDtypeStruct(q5.shape, bf16),
        grid=(NB_, B_KV, nblk),
        in_specs=[pl.BlockSpec((None, None, B_G, 128, B_DH), lambda b, kh, n: (b, kh, 0, n, 0)),
                  pl.BlockSpec((None, None, LA, B_DH), lambda b, kh, n: (b, kh, 0, 0)),
                  pl.BlockSpec((None, None, LA, B_DH), lambda b, kh, n: (b, kh, 0, 0)),
                  pl.BlockSpec((SEQ, B_DH), lambda b, kh, n: (0, 0)),
                  pl.BlockSpec((SEQ, B_DH), lambda b, kh, n: (0, 0)),
                  pl.BlockSpec((1, B_DH), lambda b, kh, n: (0, 0)),
                  pl.BlockSpec((1, B_DH), lambda b, kh, n: (0, 0)),
                  pl.BlockSpec((None, B_G * 128, 1), lambda b, kh, n: (kh, 0, 0))],
        out_specs=pl.BlockSpec((None, None, B_G, 128, B_DH), lambda b, kh, n: (b, kh, 0, n, 0)),
        compiler_params=_cp(("arbitrary", "arbitrary", "arbitrary")),
        name="window_attn",
    )(q5, k4, v4, cosf, sinf, gq, gk, sink_col)


def _rope_tables():
    rows = SEQ // GRID_W
    row = jnp.repeat(jnp.arange(rows), GRID_W).astype(f32)
    col = jnp.tile(jnp.arange(GRID_W), rows).astype(f32)
    n_freq = B_DH // 4
    inv = ROPE_BASE ** (-jnp.arange(n_freq, dtype=f32) / n_freq)
    ang = jnp.concatenate([row[:, None] * inv, col[:, None] * inv], axis=-1)
    cos, sin = jnp.cos(ang), jnp.sin(ang)
    return jnp.concatenate([cos, cos], axis=-1), jnp.concatenate([-sin, sin], axis=-1)


def _s5_kernel(u_ref, mi_ref, wif_ref, wib_ref, wof_ref, wob_ref, af_ref, ab_ref,
               y_ref, zf_ref, zb_ref, xf_ref, xb_ref):
    nck = LA // S5_T
    nctx = CTX // S5_T
    u = u_ref[...]
    zf_ref[...] = jnp.dot(u, wif_ref[...], preferred_element_type=f32)
    zb_ref[...] = jnp.dot(u, wib_ref[...], preferred_element_type=f32)
    a_f = af_ref[...]
    a_b = ab_ref[...]

    def cmul_add(a, x, z):
        ar, ai = a[:, :128], a[:, 128:]
        xr, xi = x[:, :128], x[:, 128:]
        return jnp.concatenate([ar * xr - ai * xi + z[:, :128], ar * xi + ai * xr + z[:, 128:]], axis=1)

    def step(j, carry):
        x_f, x_b = carry
        rf = pl.multiple_of(j * 8, 8)
        kb = jnp.where(j < nctx, nctx - 1 - j, nck + nctx - 1 - j)
        rb = pl.multiple_of(kb * 8, 8)
        xf_ref[pl.ds(rf, 8), :] = x_f
        xb_ref[pl.ds(rb, 8), :] = x_b
        return (cmul_add(a_f, x_f, zf_ref[pl.ds(rf, 8), :]),
                cmul_add(a_b, x_b, zb_ref[pl.ds(rb, 8), :]))

    zero = jnp.zeros((8, 256), f32)
    lax.fori_loop(0, nck, step, (zero, zero))
    y = jnp.dot(xf_ref[...].astype(bf16), wof_ref[...], preferred_element_type=f32)
    y = y + jnp.dot(xb_ref[...].astype(bf16), wob_ref[...], preferred_element_type=f32)
    y_ref[:, :256] = y[:, :256] + jnp.dot(u[:, :256], mi_ref[0], preferred_element_type=f32)
    y_ref[:, 256:] = y[:, 256:] + jnp.dot(u[:, 256:], mi_ref[1], preferred_element_type=f32)


def _s5_core(u3, mats):
    npair = C_GROUPS // 2
    rows = (LA // S5_T) * 8
    mi, wif, wib, wof, wob, af, ab = mats
    return pl.pallas_call(
        _s5_kernel,
        out_shape=jax.ShapeDtypeStruct((npair, rows, 512), f32),
        grid=(npair,),
        in_specs=[pl.BlockSpec((None, rows, 512), lambda p: (p, 0, 0)),
                  pl.BlockSpec((None, 2, 256, 256), lambda p: (p, 0, 0, 0)),
                  pl.BlockSpec((None, 512, 256), lambda p: (p, 0, 0)),
                  pl.BlockSpec((None, 512, 256), lambda p: (p, 0, 0)),
                  pl.BlockSpec((None, 256, 512), lambda p: (p, 0, 0)),
                  pl.BlockSpec((None, 256, 512), lambda p: (p, 0, 0)),
                  pl.BlockSpec((None, 1, 256), lambda p: (p, 0, 0)),
                  pl.BlockSpec((None, 1, 256), lambda p: (p, 0, 0))],
        out_specs=pl.BlockSpec((None, rows, 512), lambda p: (p, 0, 0)),
        scratch_shapes=[pltpu.VMEM((rows, 256), f32)] * 4,
        compiler_params=_cp(("arbitrary",)),
        name="s5_core",
    )(u3, mi, wif, wib, wof, wob, af, ab)


def _s5_matrices(a_re, a_im, log_dt, b_re, b_im, c_re, c_im):
    T = S5_T
    lam = lax.complex(a_re.astype(f32), a_im.astype(f32))
    dt = jnp.exp(log_dt.astype(f32))[..., None]
    a_bar = jnp.exp(lam * dt)
    b_bar = ((a_bar - 1) / lam)[..., None] * lax.complex(b_re.astype(f32), b_im.astype(f32))
    tau = jnp.arange(T + 1, dtype=f32)
    apow = jnp.exp((lam * dt)[..., None] * tau)
    cc = lax.complex(c_re.astype(f32), c_im.astype(f32))
    kern = jnp.einsum("gxp,dgpt,dgpc->dgtxc", cc, apow[..., :T], b_bar).real
    t_i = jnp.arange(T)
    lag = t_i[:, None] - t_i[None, :]
    kf = kern[0][:, jnp.clip(lag, 0, T - 1)] * (lag >= 0)[None, :, :, None, None]
    kb = kern[1][:, jnp.clip(-lag, 0, T - 1)] * (lag <= 0)[None, :, :, None, None]
    m_intra = (kf + kb).transpose(0, 2, 4, 1, 3).reshape(C_GROUPS, T * C_GROUP, T * C_GROUP)
    w_f = apow[0][:, :, T - 1 - t_i][..., None] * b_bar[0][:, :, None, :]
    w_b = apow[1][:, :, t_i][..., None] * b_bar[1][:, :, None, :]

    def win(w):
        w = w.transpose(0, 2, 3, 1).reshape(C_GROUPS, T * C_GROUP, C_STATE)
        return w.real, w.imag

    k_f = cc[:, :, :, None] * apow[0][:, None, :, 1 + t_i]
    k_b = cc[:, :, :, None] * apow[1][:, None, :, T - t_i]

    def wout(k):
        k = k.transpose(0, 2, 3, 1).reshape(C_GROUPS, C_STATE, T * C_GROUP)
        return k.real, -k.imag

    def pair_in(re, im):
        z = jnp.zeros_like(re)
        top = jnp.concatenate([re[0::2], z[0::2], im[0::2], z[0::2]], axis=-1)
        bot = jnp.concatenate([z[1::2], re[1::2], z[1::2], im[1::2]], axis=-1)
        return jnp.concatenate([top, bot], axis=1).astype(bf16)

    def pair_out(re, im):
        z = jnp.zeros_like(re)
        rows = [jnp.concatenate([re[0::2], z[0::2]], axis=-1), jnp.concatenate([z[1::2], re[1::2]], axis=-1),
                jnp.concatenate([im[0::2], z[0::2]], axis=-1), jnp.concatenate([z[1::2], im[1::2]], axis=-1)]
        return jnp.concatenate(rows, axis=1).astype(bf16)

    def pair_a(a):
        return jnp.concatenate([a.real[0::2], a.real[1::2], a.imag[0::2], a.imag[1::2]], axis=-1)[:, None, :]

    mi = m_intra.reshape(C_GROUPS // 2, 2, T * C_GROUP, T * C_GROUP).astype(bf16)
    return (mi, pair_in(*win(w_f)), pair_in(*win(w_b)), pair_out(*wout(k_f)), pair_out(*wout(k_b)),
            pair_a(apow[0][..., T]), pair_a(apow[1][..., T]))


def _s5_glu_kernel(y_ref, u_ref, d_ref, w_ref, b_ref, o_ref, wbf_ref):
    @pl.when(pl.program_id(0) == 0)
    def _():
        wbf_ref[...] = w_ref[...].astype(bf16)

    y = y_ref[...] + d_ref[...] * u_ref[...].astype(f32)
    y = jax.nn.gelu(y, approximate=True)
    z = jnp.dot(y.astype(bf16), wbf_ref[...], preferred_element_type=f32) + b_ref[...]
    o_ref[...] = (y * jax.nn.sigmoid(z)).astype(o_ref.dtype)


def _s5_glu(y_ssm, proj, d2, glu_w, glu_b2):
    nblk = NB_ * LBPB
    return pl.pallas_call(
        _s5_glu_kernel,
        out_shape=jax.ShapeDtypeStruct((nblk * RB, HALF), bf16),
        grid=(nblk,),
        in_specs=[pl.BlockSpec((RB, HALF), lambda i: (_lat_blk(i), 0)),
                  pl.BlockSpec((RB, HALF), lambda i: (_lat_blk(i), 0)),
                  pl.BlockSpec((1, HALF), lambda i: (0, 0)),
                  pl.BlockSpec((HALF, HALF), lambda i: (0, 0)),
                  pl.BlockSpec((1, HALF), lambda i: (0, 0))],
        out_specs=pl.BlockSpec((RB, HALF), lambda i: (i, 0)),
        scratch_shapes=[pltpu.VMEM((HALF, HALF), bf16)],
        compiler_params=_cp(("arbitrary",)),
        name="s5_glu",
    )(y_ssm, proj, d2, glu_w, glu_b2)


def _hyena_filter(L, w1, b1, w2, b2, w3, freq):
    t = jnp.linspace(0.0, 1.0, L, dtype=f32)[:, None]
    w = 2 * math.pi * jnp.arange(L, dtype=f32)[:, None] / L
    bands = jnp.linspace(1e-4, HY_BANDS - 1, HY_BANDS, dtype=f32)
    feats = jnp.concatenate([t, jnp.cos(bands * w), -jnp.sin(bands * w)], axis=-1)
    h = jnp.sin(freq * (feats @ w1 + b1))
    h = jnp.sin(freq * (h @ w2 + b2))
    h = h @ w3
    fast = abs(math.log(HY_DECAY_TARGET) / HY_FAST_PCT)
    slow = abs(math.log(HY_DECAY_TARGET) / HY_SLOW_PCT)
    deltas = jnp.tile(jnp.linspace(slow, fast, HALF, dtype=f32), 2)
    h = h * jnp.exp(-t * deltas)
    h_fwd, h_bwd = jnp.split(h, 2, axis=-1)
    filt = jnp.concatenate([h_fwd, jnp.zeros((1, HALF), f32), h_bwd[:0:-1]], axis=0)
    return filt / jnp.sum(jnp.abs(filt), axis=0, keepdims=True)


def _hyena_jnp(z, conv_w, conv_b, f_w1, f_b1, f_w2, f_b2, f_w3, f_freq, hy_bias):
    L = z.shape[1]
    zp = jnp.pad(z, ((0, 0), (1, 1), (0, 0)))
    zc = sum(zp[:, j:j + L] * conv_w[j] for j in range(3)) + conv_b
    x0, x1, v = jnp.split(zc, 3, axis=-1)
    filt = _hyena_filter(L, f_w1, f_b1, f_w2, f_b2, f_w3, f_freq)
    u = x1 * v
    y = jnp.fft.irfft(jnp.fft.rfft(u, n=2 * L, axis=1) * jnp.fft.rfft(filt, n=2 * L, axis=0),
                      n=2 * L, axis=1)[:, :L]
    return x0 * (y + u * hy_bias)


def _swiglu_perm():
    j = np.arange(256)
    src = np.where(j < 128, 2 * j, 2 * (j - 128) + 1)
    p = np.zeros((256, 256), np.float32)
    p[src, j] = 1.0
    return jnp.asarray(p, dtype=bf16)


def kernel(x, c, ctx, c_ctx, ada_w, ada_b, norm_mix_g, norm_ffn_g, ev_w_in, ev_gate_b, ev_h_norm_g,
           ev_q_norm_g, ev_k_norm_g, ev_sink, ev_w_out, od_w_in, od_a_re, od_a_im, od_log_dt, od_b_re,
           od_b_im, od_c_re, od_c_im, od_d, od_glu_w, od_glu_b, od_conv_w, od_conv_b, od_filt_w1,
           od_filt_b1, od_filt_w2, od_filt_b2, od_filt_w3, od_filt_freq, od_hy_bias, od_w_out,
           moe_router_w, moe_router_b, moe_w1, moe_b1, moe_w2, moe_b2):
    depth = ada_w.shape[0]
    xu = jnp.concatenate([ctx, x], axis=1).reshape(ROWS, D)
    cond8 = jnp.concatenate([c_ctx[None], c, jnp.zeros((3, D), f32)], axis=0)
    ada_b3 = ada_b.reshape(depth, 1, 6 * D)
    g_mix3 = norm_mix_g.reshape(depth, 1, D)
    g_ffn3 = norm_ffn_g.reshape(depth, 1, D)
    rb3 = moe_router_b.reshape(depth, 1, N_EXPERTS)
    b1_4 = moe_b1.reshape(depth, N_EXPERTS, 1, 2 * D_FF)
    b2_4 = moe_b2.reshape(depth, N_EXPERTS, 1, D)
    perm = _swiglu_perm()
    nall = ROWS // RB
    nlat = NB_ * LBPB

    mod3 = _ada(cond8, ada_w, ada_b3, 0).reshape(8, 1, 6 * D)
    h = _norm_mod(xu, g_mix3, mod3, 0, 0, _ident, _mod_row, nall, bf16)
    w_in = ev_w_in[0]
    n_a = 2 * A_HEADS * A_DQK + 2 * HALF
    n_g = 4 * A_HEADS
    w_main = jnp.concatenate([w_in[:, :n_a], w_in[:, n_a + n_g:]], axis=1)
    w_gate = jnp.pad(w_in[:, n_a:n_a + n_g], ((0, 0), (0, 128 - n_g)))
    proj = _matmul([h], w_main, (), w_main.shape[1], 512, nall, _ident, bf16, name="even_in")
    gates = _matmul([h], w_gate, (), 128, 128, nall, _ident, f32, name="even_gates")[:, :n_g]
    gates = (gates + ev_gate_b[0]).reshape(NB_, LA, n_g)
    gcol, grow = _mlstm_gate_tables(gates)
    y_a = _mlstm(proj, gcol, grow, ev_h_norm_g[0].reshape(A_HEADS, 1, A_DV))
    qa = proj[:, n_a:n_a + HALF].reshape(NB_, LA, B_KV, B_G, B_DH).transpose(0, 2, 3, 1, 4)
    ka = proj[:, n_a + HALF:n_a + HALF + B_KV * B_DH].reshape(NB_, LA, B_KV, B_DH).transpose(0, 2, 1, 3)
    va = proj[:, n_a + HALF + B_KV * B_DH:].reshape(NB_, LA, B_KV, B_DH).transpose(0, 2, 1, 3)
    cosf, sinf = _rope_tables()
    sink_col = jnp.repeat(ev_sink[0].astype(f32).reshape(B_KV, B_G), 128, axis=1)[..., None]
    o5 = _attention(qa, ka, va, cosf, sinf, ev_q_norm_g[0].reshape(1, B_DH),
                    ev_k_norm_g[0].reshape(1, B_DH), sink_col)
    y_b = o5.transpose(0, 3, 1, 2, 4).reshape(ROWS, HALF)
    xu = _matmul([y_a, y_b], ev_w_out, (0,), D, 512, nall, _ident, f32,
                 res=xu, res_blk=_ident, mrow=_mod_row, mod3=mod3, gate_chunk=2, name="even_out")
    xu = _moe_layer(xu, g_ffn3, mod3, moe_router_w, rb3, moe_w1, b1_4, moe_w2, b2_4, perm,
                    0, _ident, _mod_row, nall)

    mod3 = _ada(cond8, ada_w, ada_b3, 1).reshape(8, 1, 6 * D)
    h = _norm_mod(xu, g_mix3, mod3, 1, 0, _ident, _mod_row, nall, bf16)
    proj = _matmul([h], od_w_in, (0,), 4 * HALF, 512, nall, _ident, bf16, name="odd_in")
    nck = LA // S5_T
    u6 = proj[:, :HALF].reshape(NB_, nck, S5_T, C_GROUPS // 2, 2, C_GROUP)
    u3 = jnp.pad(u6.transpose(3, 1, 0, 4, 2, 5), ((0, 0), (0, 0), (0, 8 - NB_), (0, 0), (0, 0), (0, 0)))
    u3 = u3.reshape(C_GROUPS // 2, nck * 8, 2 * S5_T * C_GROUP)
    mats = _s5_matrices(od_a_re[0], od_a_im[0], od_log_dt[0], od_b_re[0], od_b_im[0], od_c_re[0], od_c_im[0])
    y3 = _s5_core(u3, mats)
    y_ssm = y3.reshape(C_GROUPS // 2, nck, 8, 2, S5_T, C_GROUP)[:, :, :NB_]
    y_ssm = y_ssm.transpose(2, 1, 4, 0, 3, 5).reshape(ROWS, HALF)
    s_l = _s5_glu(y_ssm, proj, od_d[0].reshape(1, HALF), od_glu_w[0], od_glu_b[0].reshape(1, HALF))
    z_l = proj.reshape(NB_, LA, 4 * HALF)[:, CTX:, HALF:].astype(f32)
    hy = _hyena_jnp(z_l, od_conv_w[0], od_conv_b[0], od_filt_w1[0], od_filt_b1[0], od_filt_w2[0],
                    od_filt_b2[0], od_filt_w3[0], od_filt_freq[0], od_hy_bias[0])
    hy_l = hy.reshape(NB_ * SEQ, HALF).astype(bf16)
    x_lat = _matmul([s_l, hy_l], od_w_out, (0,), D, 512, nlat, _ident, f32,
                    res=xu, res_blk=_lat_blk, mrow=_lat_mod_row, mod3=mod3, gate_chunk=2, name="odd_out")
    out = _moe_layer(x_lat, g_ffn3, mod3, moe_router_w, rb3, moe_w1, b1_4, moe_w2, b2_4, perm,
                     1, _ident, _lat_mod_row, nlat)
    return out.reshape(NB_, SEQ, D)
```

```python
import functools
import math

import jax
import jax.numpy as jnp
import numpy as np
from jax import lax
from jax.experimental import pallas as pl
from jax.experimental.pallas import tpu as pltpu

f32 = jnp.float32
bf16 = jnp.bfloat16

D = 2048
NB_ = 4
SEQ = 2048
CTX = 256
LA = SEQ + CTX
ROWS = NB_ * LA
RB = 256
BPB = LA // RB
LBPB = SEQ // RB
EPS = 1e-6
GRID_W = 64

HALF = D // 2
A_HEADS = 4
A_DV = HALF // A_HEADS
A_DQK = A_DV // 2
A_T = 256
B_HEADS = 16
B_KV = 4
B_G = B_HEADS // B_KV
B_DH = HALF // B_HEADS
WINDOW = 128
ROPE_BASE = 10000.0
C_GROUP = 16
C_GROUPS = HALF // C_GROUP
C_STATE = 64
S5_T = 16
HY_TC = 256
HY_R = 64
HY_BANDS = 16
HY_DECAY_TARGET = 1e-2
HY_FAST_PCT = 0.3
HY_SLOW_PCT = 1.5
N_EXPERTS = 32
TOP_K = 4
D_FF = D
SWIGLU_LIMIT = 7.0
SWIGLU_ALPHA = 1.702
MOE_TM = 256

EVEN_SPLITS = (A_HEADS * A_DQK, A_HEADS * A_DQK, HALF, HALF, 4 * A_HEADS,
               B_HEADS * B_DH, B_KV * B_DH, B_KV * B_DH)

VMEM_MB = 1024 * 1024


def _cp(sem, vmem_mb=40):
    return pltpu.CompilerParams(dimension_semantics=sem, vmem_limit_bytes=vmem_mb * VMEM_MB)


def _lat_blk(i):
    return (i // LBPB) * BPB + 1 + i % LBPB


def _mod_row(u):
    return jnp.where(u % BPB == 0, 0, 1 + u // BPB)


def _lat_mod_row(i):
    return 1 + i // LBPB


def _ident(i):
    return i


def _ada_kernel(c_ref, w_ref, b_ref, o_ref):
    c = c_ref[...]
    s = c * jax.nn.sigmoid(c)
    o_ref[...] = jnp.dot(s.astype(bf16), w_ref[...].astype(bf16),
                         preferred_element_type=f32) + b_ref[...]


def _ada(cond8, ada_w, ada_b3, layer):
    tn = 1024
    n = 6 * D
    return pl.pallas_call(
        _ada_kernel,
        out_shape=jax.ShapeDtypeStruct((8, n), f32),
        grid=(n // tn,),
        in_specs=[pl.BlockSpec((8, D), lambda j: (0, 0)),
                  pl.BlockSpec((None, D, tn), lambda j: (layer, 0, j)),
                  pl.BlockSpec((None, 1, tn), lambda j: (layer, 0, j))],
        out_specs=pl.BlockSpec((8, tn), lambda j: (0, j)),
        compiler_params=_cp(("arbitrary",)),
        name="ada_params",
    )(cond8, ada_w, ada_b3)


def _norm_mod_kernel(x_ref, g_ref, sh_ref, sc_ref, o_ref):
    x = x_ref[...]
    y = x * lax.rsqrt(jnp.mean(x * x, axis=-1, keepdims=True) + EPS) * g_ref[...]
    o_ref[...] = (y * (1.0 + sc_ref[...]) + sh_ref[...]).astype(o_ref.dtype)


def _norm_mod(x, g3, mod3, layer, chunk, blk, mrow, nblk, out_dtype):
    return pl.pallas_call(
        _norm_mod_kernel,
        out_shape=jax.ShapeDtypeStruct((nblk * RB, D), out_dtype),
        grid=(nblk,),
        in_specs=[pl.BlockSpec((RB, D), lambda i: (blk(i), 0)),
                  pl.BlockSpec((None, 1, D), lambda i: (layer, 0, 0)),
                  pl.BlockSpec((None, 1, D), lambda i: (mrow(i), 0, chunk)),
                  pl.BlockSpec((None, 1, D), lambda i: (mrow(i), 0, chunk + 1))],
        out_specs=pl.BlockSpec((RB, D), lambda i: (i, 0)),
        compiler_params=_cp(("arbitrary",)),
        name="norm_mod",
    )(x, g3, mod3, mod3)


TT = D // 128


def _store_token_tiles(ref, val):
    n = val.shape[0]
    for s in range(TT):
        ref[pl.ds(s, n, stride=TT), :] = val[:, s * 128:(s + 1) * 128]


def _split_bf16(x):
    hi = x.astype(bf16)
    lo = (x - hi.astype(f32)).astype(bf16)
    return hi, lo


def _norm_router_kernel(x_ref, g_ref, sh_ref, sc_ref, rw_ref, rb_ref,
                        h_ref, idx_ref, gate_ref, rank_ref, cnt_ref, carry_ref):
    i = pl.program_id(0)

    @pl.when(i == 0)
    def _():
        carry_ref[...] = jnp.zeros_like(carry_ref)

    x = x_ref[...]
    y = x * lax.rsqrt(jnp.mean(x * x, axis=-1, keepdims=True) + EPS) * g_ref[...]
    h = y * (1.0 + sc_ref[...]) + sh_ref[...]
    _store_token_tiles(h_ref, h)
    h_hi, h_lo = _split_bf16(h)
    w_hi, w_lo = _split_bf16(rw_ref[...])
    logits = (jnp.dot(h_hi, w_hi, preferred_element_type=f32)
              + jnp.dot(h_lo, w_hi, preferred_element_type=f32)
              + jnp.dot(h_hi, w_lo, preferred_element_type=f32)) + rb_ref[...]
    ne = logits.shape[-1]
    lane = lax.broadcasted_iota(jnp.int32, logits.shape, 1)
    l = logits
    vals, idxs = [], []
    for _ in range(TOP_K):
        m = jnp.max(l, axis=-1, keepdims=True)
        ix = jnp.min(jnp.where(l == m, lane, ne), axis=-1, keepdims=True)
        vals.append(m)
        idxs.append(ix)
        l = jnp.where(lane == ix, -jnp.inf, l)
    es = [jnp.exp(v - vals[0]) for v in vals]
    tot = es[0] + es[1] + es[2] + es[3]
    onehot = jnp.zeros(logits.shape, f32)
    for ix in idxs:
        onehot = onehot + jnp.where(lane == ix, 1.0, 0.0)
    r_i = lax.broadcasted_iota(jnp.int32, (RB, RB), 0)
    c_i = lax.broadcasted_iota(jnp.int32, (RB, RB), 1)
    tri = jnp.where(c_i < r_i, 1.0, 0.0).astype(bf16)
    prefix = jnp.dot(tri, onehot.astype(bf16), preferred_element_type=f32) + carry_ref[...]
    for k in range(TOP_K):
        idx_ref[:, k:k + 1] = idxs[k]
        gate_ref[:, k:k + 1] = es[k] / tot
        rk = jnp.sum(jnp.where(lane == idxs[k], prefix, 0.0), axis=-1, keepdims=True)
        rank_ref[:, k:k + 1] = rk.astype(jnp.int32)
    carry_ref[...] = carry_ref[...] + jnp.sum(onehot, axis=0, keepdims=True)
    cnt_ref[...] = carry_ref[...]


def _norm_router(x, g3, mod3, rw, rb3, layer, chunk, blk, mrow, nblk):
    t = nblk * RB
    return pl.pallas_call(
        _norm_router_kernel,
        out_shape=(jax.ShapeDtypeStruct((t * TT, 128), f32),
                   jax.ShapeDtypeStruct((t, TOP_K), jnp.int32),
                   jax.ShapeDtypeStruct((t, TOP_K), f32),
                   jax.ShapeDtypeStruct((t, TOP_K), jnp.int32),
                   jax.ShapeDtypeStruct((1, N_EXPERTS), f32)),
        grid=(nblk,),
        in_specs=[pl.BlockSpec((RB, D), lambda i: (blk(i), 0)),
                  pl.BlockSpec((None, 1, D), lambda i: (layer, 0, 0)),
                  pl.BlockSpec((None, 1, D), lambda i: (mrow(i), 0, chunk)),
                  pl.BlockSpec((None, 1, D), lambda i: (mrow(i), 0, chunk + 1)),
                  pl.BlockSpec((None, D, N_EXPERTS), lambda i: (layer, 0, 0)),
                  pl.BlockSpec((None, 1, N_EXPERTS), lambda i: (layer, 0, 0))],
        out_specs=(pl.BlockSpec((RB * TT, 128), lambda i: (i, 0)),
                   pl.BlockSpec((RB, TOP_K), lambda i: (i, 0)),
                   pl.BlockSpec((RB, TOP_K), lambda i: (i, 0)),
                   pl.BlockSpec((RB, TOP_K), lambda i: (i, 0)),
                   pl.BlockSpec((1, N_EXPERTS), lambda i: (0, 0))),
        scratch_shapes=[pltpu.VMEM((1, N_EXPERTS), f32)],
        compiler_params=_cp(("arbitrary",)),
        name="norm_router",
    )(x, g3, mod3, mod3, rw, rb3)


def _mm_kernel(*refs, n_a, k_split, epilogue):
    a_refs = refs[:n_a]
    w_ref = refs[n_a]
    pos = n_a + 1
    if epilogue:
        res_ref, gate_ref = refs[pos], refs[pos + 1]
        pos += 2
    o_ref, wbf_ref = refs[pos], refs[pos + 1]

    @pl.when(pl.program_id(1) == 0)
    def _():
        wbf_ref[...] = w_ref[...].astype(bf16)

    acc = None
    for t, a_ref in enumerate(a_refs):
        part = jnp.dot(a_ref[...], wbf_ref[t * k_split:(t + 1) * k_split, :],
                       preferred_element_type=f32)
        acc = part if acc is None else acc + part
    if epilogue:
        acc = res_ref[...] + gate_ref[...] * acc
    o_ref[...] = acc.astype(o_ref.dtype)


def _matmul(a_list, w, w_idx, n, tn, nblk, a_blk, out_dtype, *, a_col=None, w_col0=0,
            res=None, res_blk=None, mrow=None, mod3=None, gate_chunk=None, name="matmul"):
    n_a = len(a_list)
    k = w.shape[-2]
    k_split = k // n_a
    lead = len(w_idx)
    a_col = a_col or [0] * n_a
    in_specs = [pl.BlockSpec((RB, k_split), functools.partial(lambda j, i, c: (a_blk(i), c), c=c))
                for c in a_col]
    in_specs.append(pl.BlockSpec((None,) * lead + (k, tn), lambda j, i: tuple(w_idx) + (0, j + w_col0)))
    args = list(a_list) + [w]
    epilogue = res is not None
    if epilogue:
        per = D // tn
        in_specs.append(pl.BlockSpec((RB, tn), lambda j, i: (res_blk(i), j)))
        in_specs.append(pl.BlockSpec((None, 1, tn),
                                     lambda j, i: (mrow(i), 0, gate_chunk * per + j)))
        args += [res, mod3]
    return pl.pallas_call(
        functools.partial(_mm_kernel, n_a=n_a, k_split=k_split, epilogue=epilogue),
        out_shape=jax.ShapeDtypeStruct((nblk * RB, n), out_dtype),
        grid=(n // tn, nblk),
        in_specs=in_specs,
        out_specs=pl.BlockSpec((RB, tn), lambda j, i: (i, j)),
        scratch_shapes=[pltpu.VMEM((k, tn), bf16)],
        compiler_params=_cp(("arbitrary", "arbitrary")),
        name=name,
    )(*args)


def _gather_rows_kernel(tok_ref, h_hbm, o_ref, buf, sem, *, rows, nsteps):
    i = pl.program_id(0)

    def issue(blk, slot):
        base = blk * rows

        def body(r, c):
            t0 = pl.multiple_of(tok_ref[base + r] * TT, TT)
            r0 = pl.multiple_of(r * TT, TT)
            pltpu.make_async_copy(h_hbm.at[pl.ds(t0, TT), :], buf.at[slot, pl.ds(r0, TT), :],
                                  sem.at[slot]).start()
            return c

        lax.fori_loop(0, rows, body, 0, unroll=8)

    @pl.when(i == 0)
    def _():
        issue(0, 0)

    @pl.when(i + 1 < nsteps)
    def _():
        issue(i + 1, (i + 1) % 2)

    slot = i % 2
    pltpu.make_async_copy(h_hbm.at[pl.ds(0, rows * TT), :], buf.at[slot], sem.at[slot]).wait()
    for s in range(TT):
        o_ref[:, s * 128:(s + 1) * 128] = buf[slot, pl.ds(s, rows, stride=TT), :].astype(o_ref.dtype)


def _gather_rows(slot_tok, h, nr):
    rows = 512
    nsteps = nr // rows
    return pl.pallas_call(
        functools.partial(_gather_rows_kernel, rows=rows, nsteps=nsteps),
        out_shape=jax.ShapeDtypeStruct((nr, D), bf16),
        grid_spec=pltpu.PrefetchScalarGridSpec(
            num_scalar_prefetch=1,
            grid=(nsteps,),
            in_specs=[pl.BlockSpec(memory_space=pl.ANY)],
            out_specs=pl.BlockSpec((rows, D), lambda i, tok: (i, 0)),
            scratch_shapes=[pltpu.VMEM((2, rows * TT, 128), f32), pltpu.SemaphoreType.DMA((2,))]),
        compiler_params=_cp(("arbitrary",)),
        name="moe_gather",
    )(slot_tok, h)


def _first_of_expert(be_ref, i):
    return jnp.logical_or(i == 0, be_ref[i] != be_ref[jnp.maximum(i - 1, 0)])


def _moe_up_kernel(be_ref, nv_ref, x_ref, w_ref, b_ref, p_ref, o_ref, wbf_ref, *, tn):
    i = pl.program_id(1)

    @pl.when(_first_of_expert(be_ref, i))
    def _():
        wbf_ref[...] = w_ref[...].astype(bf16)

    @pl.when(i < nv_ref[0])
    def _():
        h = jnp.dot(x_ref[...], wbf_ref[...], preferred_element_type=f32) + b_ref[...]
        for c in range(tn // 256):
            d = jnp.dot(h[:, c * 256:(c + 1) * 256].astype(bf16), p_ref[...],
                        preferred_element_type=f32)
            gate = jnp.minimum(d[:, :128], SWIGLU_LIMIT)
            lin = jnp.clip(d[:, 128:], -SWIGLU_LIMIT, SWIGLU_LIMIT)
            act = gate * jax.nn.sigmoid(SWIGLU_ALPHA * gate) * (lin + 1.0)
            o_ref[:, c * 128:(c + 1) * 128] = act.astype(o_ref.dtype)

    @pl.when(i >= nv_ref[0])
    def _():
        o_ref[...] = jnp.zeros_like(o_ref)


def _moe_up(be, nv, xs, w1, b1, perm, layer, nblk):
    tn = 2048
    return pl.pallas_call(
        functools.partial(_moe_up_kernel, tn=tn),
        out_shape=jax.ShapeDtypeStruct((nblk * MOE_TM, D_FF), bf16),
        grid_spec=pltpu.PrefetchScalarGridSpec(
            num_scalar_prefetch=2,
            grid=(2 * D_FF // tn, nblk),
            in_specs=[pl.BlockSpec((MOE_TM, D), lambda j, i, be, nv: (i, 0)),
                      pl.BlockSpec((None, None, D, tn), lambda j, i, be, nv: (layer, be[i], 0, j)),
                      pl.BlockSpec((None, None, 1, tn), lambda j, i, be, nv: (layer, be[i], 0, j)),
                      pl.BlockSpec((256, 256), lambda j, i, be, nv: (0, 0))],
            out_specs=pl.BlockSpec((MOE_TM, tn // 2), lambda j, i, be, nv: (i, j)),
            scratch_shapes=[pltpu.VMEM((D, tn), bf16)]),
        compiler_params=_cp(("arbitrary", "arbitrary"), 56),
        name="moe_up",
    )(be, nv, xs, w1, b1, perm)


def _moe_down_kernel(be_ref, nv_ref, a_ref, w_ref, b_ref, o_ref, wbf_ref):
    i = pl.program_id(0)

    @pl.when(_first_of_expert(be_ref, i))
    def _():
        wbf_ref[...] = w_ref[...].astype(bf16)

    @pl.when(i < nv_ref[0])
    def _():
        y = jnp.dot(a_ref[...], wbf_ref[...], preferred_element_type=f32) + b_ref[...]
        _store_token_tiles(o_ref, y)

    @pl.when(i >= nv_ref[0])
    def _():
        o_ref[...] = jnp.zeros_like(o_ref)


def _moe_down(be, nv, act, w2, b2, layer, nblk):
    return pl.pallas_call(
        _moe_down_kernel,
        out_shape=jax.ShapeDtypeStruct((nblk * MOE_TM * TT, 128), f32),
        grid_spec=pltpu.PrefetchScalarGridSpec(
            num_scalar_prefetch=2,
            grid=(nblk,),
            in_specs=[pl.BlockSpec((MOE_TM, D_FF), lambda i, be, nv: (i, 0)),
                      pl.BlockSpec((None, None, D_FF, D), lambda i, be, nv: (layer, be[i], 0, 0)),
                      pl.BlockSpec((None, None, 1, D), lambda i, be, nv: (layer, be[i], 0, 0))],
            out_specs=pl.BlockSpec((MOE_TM * TT, 128), lambda i, be, nv: (i, 0)),
            scratch_shapes=[pltpu.VMEM((D_FF, D), bf16)]),
        compiler_params=_cp(("arbitrary",), 56),
        name="moe_down",
    )(be, nv, act, w2, b2)


def _combine_kernel(slot_ref, y_hbm, x_ref, gmod_ref, w_ref, o_ref, buf, sem, *, tm, nsteps):
    i = pl.program_id(0)

    def issue(blk, slot):
        base = blk * tm

        def body(r, c):
            r0 = pl.multiple_of(r * TT, TT)
            for k in range(TOP_K):
                s0 = pl.multiple_of(slot_ref[(base + r) * TOP_K + k] * TT, TT)
                pltpu.make_async_copy(y_hbm.at[pl.ds(s0, TT), :], buf.at[slot, k, pl.ds(r0, TT), :],
                                      sem.at[slot]).start()
            return c

        lax.fori_loop(0, tm, body, 0, unroll=4)

    @pl.when(i == 0)
    def _():
        issue(0, 0)

    @pl.when(i + 1 < nsteps)
    def _():
        issue(i + 1, (i + 1) % 2)

    slot = i % 2
    for k in range(TOP_K):
        pltpu.make_async_copy(y_hbm.at[pl.ds(0, tm * TT), :], buf.at[slot, k], sem.at[slot]).wait()
    w = w_ref[...]
    for s in range(TT):
        sl = slice(s * 128, (s + 1) * 128)
        part = [w[:, k:k + 1] * buf[slot, k, pl.ds(s, tm, stride=TT), :] for k in range(TOP_K)]
        acc = (part[0] + part[1]) + (part[2] + part[3])
        o_ref[:, sl] = x_ref[:, sl] + gmod_ref[:, sl] * acc


def _combine(slot_flat, ys, x, mod3, gate, gate_chunk, blk, mrow, nblk):
    tm = RB
    return pl.pallas_call(
        functools.partial(_combine_kernel, tm=tm, nsteps=nblk),
        out_shape=jax.ShapeDtypeStruct((nblk * tm, D), f32),
        grid_spec=pltpu.PrefetchScalarGridSpec(
            num_scalar_prefetch=1,
            grid=(nblk,),
            in_specs=[pl.BlockSpec(memory_space=pl.ANY),
                      pl.BlockSpec((tm, D), lambda i, s: (blk(i), 0)),
                      pl.BlockSpec((None, 1, D), lambda i, s: (mrow(i), 0, gate_chunk)),
                      pl.BlockSpec((tm, TOP_K), lambda i, s: (i, 0))],
            out_specs=pl.BlockSpec((tm, D), lambda i, s: (i, 0)),
            scratch_shapes=[pltpu.VMEM((2, TOP_K, tm * TT, 128), f32), pltpu.SemaphoreType.DMA((2,))]),
        compiler_params=_cp(("arbitrary",)),
        name="moe_combine",
    )(slot_flat, ys, x, mod3, gate)


def _moe_layer(x, g3, mod3, rw, rb3, w1, b1, w2, b2, perm, layer, blk, mrow, nblk):
    t = nblk * RB
    h, idx, gate, rank, cnt = _norm_router(x, g3, mod3, rw, rb3, layer, 3, blk, mrow, nblk)
    n_assign = t * TOP_K
    nb = n_assign // MOE_TM + N_EXPERTS
    nr = nb * MOE_TM
    counts = cnt[0].astype(jnp.int32)
    padded = (counts + MOE_TM - 1) // MOE_TM * MOE_TM
    pad_end = jnp.cumsum(padded)
    pad_start = pad_end - padded
    slot = pad_start[idx] + rank
    slot_flat = slot.reshape(-1)
    tok = jnp.repeat(jnp.arange(t, dtype=jnp.int32), TOP_K)
    slot_tok = jnp.zeros((nr,), jnp.int32).at[slot_flat].set(tok)
    nv = (pad_end[-1] // MOE_TM).astype(jnp.int32)
    blocks = jnp.minimum(jnp.arange(nb, dtype=jnp.int32), nv - 1)
    be = jnp.sum((pad_end[None, :] <= (blocks * MOE_TM)[:, None]).astype(jnp.int32), axis=1)
    be = jnp.minimum(be, N_EXPERTS - 1)
    nv1 = nv.reshape(1)
    xs = _gather_rows(slot_tok, h, nr)
    act = _moe_up(be, nv1, xs, w1, b1, perm, layer, nb)
    ys = _moe_down(be, nv1, act, w2, b2, layer, nb)
    return _combine(slot_flat, ys, x, mod3, gate, 5, blk, mrow, nblk)


def _dot_nt(a, b):
    return lax.dot_general(a, b, (((1,), (1,)), ((), ())), preferred_element_type=f32)


def _dot_tn(a, b):
    return lax.dot_general(a, b, (((0,), (0,)), ((), ())), preferred_element_type=f32)


def _mlstm_kernel(q_ref, k_ref, v_ref, o_ref, gc_ref, gr_ref, ng_ref, y_ref, hf_ref, hb_ref, ct_ref):
    T = A_T
    nch = LA // T
    ct_ref[...] = jnp.zeros_like(ct_ref)
    r_i = lax.broadcasted_iota(jnp.int32, (T, T), 0)
    c_i = lax.broadcasted_iota(jnp.int32, (T, T), 1)
    lane = lax.broadcasted_iota(jnp.int32, (T, 128), 1)
    ones_blk = jnp.where(lane == 0, 1.0, 0.0).astype(bf16)

    def chunk(c, dirn, h_ref):
        r0 = pl.multiple_of(c * T, T)
        q = q_ref[pl.ds(r0, T), :]
        k = k_ref[pl.ds(r0, T), :]
        v = v_ref[pl.ds(r0, T), :]
        gc = gc_ref[pl.ds(r0, T), :]
        o = 5 * dirn
        a_col, m_col, negm = gc[:, o:o + 1], gc[:, o + 1:o + 2], gc[:, o + 2:o + 3]
        m_end, m_prev = gc[:, o + 3:o + 4], gc[:, o + 4:o + 5]
        a_row = gr_ref[dirn:dirn + 1, pl.ds(r0, T)]
        mask = (c_i <= r_i) if dirn == 0 else (c_i >= r_i)
        w = jnp.exp(jnp.where(mask, a_row - m_col, -jnp.inf))
        s = _dot_nt(q, k) * (A_DQK ** -0.5) * w
        v_aug = jnp.concatenate([v, ones_blk], axis=1)
        ct = ct_ref[dirn]
        w_carry = jnp.exp(m_prev - m_col)
        num = (jnp.dot(s.astype(bf16), v_aug, preferred_element_type=f32)
               + w_carry * jnp.dot(q, ct.astype(bf16), preferred_element_type=f32))
        den = num[:, A_DV:A_DV + 1]
        h_ref[pl.ds(r0, T), :] = num[:, :A_DV] / jnp.maximum(jnp.abs(den), jnp.exp(negm))
        w_in = jnp.exp(a_col - m_end)
        decay = jnp.exp(m_prev[0:1, :] - m_end[0:1, :])
        kv = _dot_tn(k, (w_in * v_aug.astype(f32)).astype(bf16)) * (A_DQK ** -0.5)
        ct_ref[dirn] = decay * ct + kv

    def step(it, carry):
        chunk(it, 0, hf_ref)
        chunk(jnp.where(it == 0, 0, nch - it), 1, hb_ref)
        return carry

    lax.fori_loop(0, nch, step, 0)
    h = hf_ref[...] + hb_ref[...]
    hn = h * lax.rsqrt(jnp.mean(h * h, axis=-1, keepdims=True) + EPS) * ng_ref[...]
    y_ref[...] = (hn * jax.nn.sigmoid(o_ref[...].astype(f32))).astype(y_ref.dtype)


def _mlstm(proj, gcol, grow, ng3):
    qb, vb = A_DQK, A_DV
    return pl.pallas_call(
        _mlstm_kernel,
        out_shape=jax.ShapeDtypeStruct((ROWS, HALF), bf16),
        grid=(NB_, A_HEADS),
        in_specs=[pl.BlockSpec((LA, qb), lambda b, h: (b, h)),
                  pl.BlockSpec((LA, qb), lambda b, h: (b, A_HEADS + h)),
                  pl.BlockSpec((LA, vb), lambda b, h: (b, (2 * A_HEADS * qb) // vb + h)),
                  pl.BlockSpec((LA, vb), lambda b, h: (b, (2 * A_HEADS * qb + HALF) // vb + h)),
                  pl.BlockSpec((None, None, LA, 16), lambda b, h: (b, h, 0, 0)),
                  pl.BlockSpec((None, None, 8, LA), lambda b, h: (b, h, 0, 0)),
                  pl.BlockSpec((None, 1, vb), lambda b, h: (h, 0, 0))],
        out_specs=pl.BlockSpec((LA, vb), lambda b, h: (b, h)),
        scratch_shapes=[pltpu.VMEM((LA, vb), f32), pltpu.VMEM((LA, vb), f32),
                        pltpu.VMEM((2, qb, vb + 128), f32)],
        compiler_params=_cp(("arbitrary", "arbitrary")),
        name="mlstm",
    )(proj, proj, proj, proj, gcol, grow, ng3)


def _mlstm_gate_tables(gates):
    i_f, f_f, i_b, f_b = jnp.split(gates, 4, axis=-1)

    def scan_tables(i_pre, f_pre):
        lf = jax.nn.log_sigmoid(f_pre)
        big_f = jnp.cumsum(lf, axis=1)
        a = i_pre - big_f
        m_run = jnp.maximum(lax.cummax(a, axis=1), 0.0)
        m_end = jnp.repeat(m_run[:, A_T - 1::A_T], A_T, axis=1)
        m_prev = jnp.concatenate([jnp.zeros_like(m_end[:, :A_T]), m_end[:, :-A_T]], axis=1)
        return a, m_run, -(big_f + m_run), m_end, m_prev

    def to_bwd(t):
        return jnp.concatenate([jnp.flip(t[:, :CTX], axis=1), jnp.flip(t[:, CTX:], axis=1)], axis=1)

    fw = scan_tables(i_f, f_f)
    bw = [to_bwd(t) for t in scan_tables(to_bwd(i_b), to_bwd(f_b))]
    cols = jnp.stack(list(fw) + bw, axis=-1)
    cols = jnp.pad(cols, ((0, 0), (0, 0), (0, 0), (0, 6))).transpose(0, 2, 1, 3)
    rows = jnp.stack([fw[0], bw[0]], axis=1).transpose(0, 3, 1, 2)
    rows = jnp.pad(rows, ((0, 0), (0, 0), (0, 6), (0, 0)))
    return cols, rows


def _swap_halves(t):
    hd = t.shape[-1] // 2
    return jnp.concatenate([t[..., hd:], t[..., :hd]], axis=-1)


def _attn_kernel(q_ref, k_ref, v_ref, cos_ref, sin_ref, gq_ref, gk_ref, sink_ref, o_ref):
    n = pl.program_id(2)
    scale = B_DH ** -0.5
    nq = B_G * 128

    def norm(t, g_ref):
        return t * lax.rsqrt(jnp.mean(t * t, axis=-1, keepdims=True) + EPS) * g_ref[...]

    qn = norm(q_ref[...].astype(f32), gq_ref)
    kc = norm(k_ref[0:CTX, :].astype(f32), gk_ref).astype(bf16)
    vc = v_ref[0:CTX, :]
    sink = sink_ref[...]

    def finish(parts):
        m = sink
        for s, _ in parts:
            m = jnp.maximum(m, jnp.max(s, axis=-1, keepdims=True))
        den = jnp.exp(sink - m)
        acc = None
        for s, vv in parts:
            p = jnp.exp(s - m)
            den = den + jnp.sum(p, axis=-1, keepdims=True)
            pv = jnp.dot(p.astype(bf16), vv, preferred_element_type=f32)
            acc = pv if acc is None else acc + pv
        o_ref[...] = (acc / den).reshape(B_G, 128, B_DH).astype(o_ref.dtype)

    @pl.when(n < CTX // 128)
    def _():
        s_ctx = _dot_nt(qn.reshape(nq, B_DH).astype(bf16), kc) * scale
        finish([(s_ctx, vc)])

    @pl.when(n >= CTX // 128)
    def _():
        nl = n - CTX // 128
        pos0 = pl.multiple_of(nl * 128, 128)
        cq = cos_ref[pl.ds(pos0, 128), :]
        sq = sin_ref[pl.ds(pos0, 128), :]
        qr = qn * cq[None] + _swap_halves(qn) * sq[None]
        start = pl.multiple_of(jnp.clip((nl - 1) * 128, 0, SEQ - 3 * 128), 128)
        kb = norm(k_ref[pl.ds(CTX + start, 3 * 128), :].astype(f32), gk_ref)
        kb = kb * cos_ref[pl.ds(start, 3 * 128), :] + _swap_halves(kb) * sin_ref[pl.ds(start, 3 * 128), :]
        vb = v_ref[pl.ds(CTX + start, 3 * 128), :]
        s_band = _dot_nt(qr.reshape(nq, B_DH).astype(bf16), kb.astype(bf16)) * scale
        q_pos = pos0 + lax.broadcasted_iota(jnp.int32, (nq, 3 * 128), 0) % 128
        k_pos = start + lax.broadcasted_iota(jnp.int32, (nq, 3 * 128), 1)
        s_band = jnp.where(jnp.abs(q_pos - k_pos) <= WINDOW, s_band, -jnp.inf)
        s_ctx = _dot_nt(qn.reshape(nq, B_DH).astype(bf16), kc) * scale
        finish([(s_band, vb), (s_ctx, vc)])


def _attention(q5, k4, v4, cosf, sinf, gq, gk, sink_col):
    nblk = LA // 128
    return pl.pallas_call(
        _attn_kernel,
        out_shape=jax.ShapeDtypeStruct(q5.shape, bf16),
        grid=(NB_, B_KV, nblk),
        in_specs=[pl.BlockSpec((None, None, B_G, 128, B_DH), lambda b, kh, n: (b, kh, 0, n, 0)),
                  pl.BlockSpec((None, None, LA, B_DH), lambda b, kh, n: (b, kh, 0, 0)),
                  pl.BlockSpec((None, None, LA, B_DH), lambda b, kh, n: (b, kh, 0, 0)),
                  pl.BlockSpec((SEQ, B_DH), lambda b, kh, n: (0, 0)),
                  pl.BlockSpec((SEQ, B_DH), lambda b, kh, n: (0, 0)),
                  pl.BlockSpec((1, B_DH), lambda b, kh, n: (0, 0)),
                  pl.BlockSpec((1, B_DH), lambda b, kh, n: (0, 0)),
                  pl.BlockSpec((None, B_G * 128, 1), lambda b, kh, n: (kh, 0, 0))],
        out_specs=pl.BlockSpec((None, None, B_G, 128, B_DH), lambda b, kh, n: (b, kh, 0, n, 0)),
        compiler_params=_cp(("arbitrary", "arbitrary", "arbitrary")),
        name="window_attn",
    )(q5, k4, v4, cosf, sinf, gq, gk, sink_col)


def _rope_tables():
    rows = SEQ // GRID_W
    row = jnp.repeat(jnp.arange(rows), GRID_W).astype(f32)
    col = jnp.tile(jnp.arange(GRID_W), rows).astype(f32)
    n_freq = B_DH // 4
    inv = ROPE_BASE ** (-jnp.arange(n_freq, dtype=f32) / n_freq)
    ang = jnp.concatenate([row[:, None] * inv, col[:, None] * inv], axis=-1)
    cos, sin = jnp.cos(ang), jnp.sin(ang)
    return jnp.concatenate([cos, cos], axis=-1), jnp.concatenate([-sin, sin], axis=-1)


def _s5_kernel(u_ref, mi_ref, wif_ref, wib_ref, wof_ref, wob_ref, af_ref, ab_ref,
               y_ref, zf_ref, zb_ref, xf_ref, xb_ref):
    nck = LA // S5_T
    nctx = CTX // S5_T
    u = u_ref[...]
    zf_ref[...] = jnp.dot(u, wif_ref[...], preferred_element_type=f32)
    zb_ref[...] = jnp.dot(u, wib_ref[...], preferred_element_type=f32)
    a_f = af_ref[...]
    a_b = ab_ref[...]

    def cmul_add(a, x, z):
        ar, ai = a[:, :128], a[:, 128:]
        xr, xi = x[:, :128], x[:, 128:]
        return jnp.concatenate([ar * xr - ai * xi + z[:, :128], ar * xi + ai * xr + z[:, 128:]], axis=1)

    def step(j, carry):
        x_f, x_b = carry
        rf = pl.multiple_of(j * 8, 8)
        kb = jnp.where(j < nctx, nctx - 1 - j, nck + nctx - 1 - j)
        rb = pl.multiple_of(kb * 8, 8)
        xf_ref[pl.ds(rf, 8), :] = x_f
        xb_ref[pl.ds(rb, 8), :] = x_b
        return (cmul_add(a_f, x_f, zf_ref[pl.ds(rf, 8), :]),
                cmul_add(a_b, x_b, zb_ref[pl.ds(rb, 8), :]))

    zero = jnp.zeros((8, 256), f32)
    lax.fori_loop(0, nck, step, (zero, zero))
    y = jnp.dot(xf_ref[...].astype(bf16), wof_ref[...], preferred_element_type=f32)
    y = y + jnp.dot(xb_ref[...].astype(bf16), wob_ref[...], preferred_element_type=f32)
    y_ref[:, :256] = y[:, :256] + jnp.dot(u[:, :256], mi_ref[0], preferred_element_type=f32)
    y_ref[:, 256:] = y[:, 256:] + jnp.dot(u[:, 256:], mi_ref[1], preferred_element_type=f32)


def _s5_core(u3, mats):
    npair = C_GROUPS // 2
    rows = (LA // S5_T) * 8
    mi, wif, wib, wof, wob, af, ab = mats
    return pl.pallas_call(
        _s5_kernel,
        out_shape=jax.ShapeDtypeStruct((npair, rows, 512), f32),
        grid=(npair,),
        in_specs=[pl.BlockSpec((None, rows, 512), lambda p: (p, 0, 0)),
                  pl.BlockSpec((None, 2, 256, 256), lambda p: (p, 0, 0, 0)),
                  pl.BlockSpec((None, 512, 256), lambda p: (p, 0, 0)),
                  pl.BlockSpec((None, 512, 256), lambda p: (p, 0, 0)),
                  pl.BlockSpec((None, 256, 512), lambda p: (p, 0, 0)),
                  pl.BlockSpec((None, 256, 512), lambda p: (p, 0, 0)),
                  pl.BlockSpec((None, 1, 256), lambda p: (p, 0, 0)),
                  pl.BlockSpec((None, 1, 256), lambda p: (p, 0, 0))],
        out_specs=pl.BlockSpec((None, rows, 512), lambda p: (p, 0, 0)),
        scratch_shapes=[pltpu.VMEM((rows, 256), f32)] * 4,
        compiler_params=_cp(("arbitrary",)),
        name="s5_core",
    )(u3, mi, wif, wib, wof, wob, af, ab)


def _s5_matrices(a_re, a_im, log_dt, b_re, b_im, c_re, c_im):
    T = S5_T
    lam = lax.complex(a_re.astype(f32), a_im.astype(f32))
    dt = jnp.exp(log_dt.astype(f32))[..., None]
    a_bar = jnp.exp(lam * dt)
    b_bar = ((a_bar - 1) / lam)[..., None] * lax.complex(b_re.astype(f32), b_im.astype(f32))
    tau = jnp.arange(T + 1, dtype=f32)
    apow = jnp.exp((lam * dt)[..., None] * tau)
    cc = lax.complex(c_re.astype(f32), c_im.astype(f32))
    kern = jnp.einsum("gxp,dgpt,dgpc->dgtxc", cc, apow[..., :T], b_bar).real
    t_i = jnp.arange(T)
    lag = t_i[:, None] - t_i[None, :]
    kf = kern[0][:, jnp.clip(lag, 0, T - 1)] * (lag >= 0)[None, :, :, None, None]
    kb = kern[1][:, jnp.clip(-lag, 0, T - 1)] * (lag <= 0)[None, :, :, None, None]
    m_intra = (kf + kb).transpose(0, 2, 4, 1, 3).reshape(C_GROUPS, T * C_GROUP, T * C_GROUP)
    w_f = apow[0][:, :, T - 1 - t_i][..., None] * b_bar[0][:, :, None, :]
    w_b = apow[1][:, :, t_i][..., None] * b_bar[1][:, :, None, :]

    def win(w):
        w = w.transpose(0, 2, 3, 1).reshape(C_GROUPS, T * C_GROUP, C_STATE)
        return w.real, w.imag

    k_f = cc[:, :, :, None] * apow[0][:, None, :, 1 + t_i]
    k_b = cc[:, :, :, None] * apow[1][:, None, :, T - t_i]

    def wout(k):
        k = k.transpose(0, 2, 3, 1).reshape(C_GROUPS, C_STATE, T * C_GROUP)
        return k.real, -k.imag

    def pair_in(re, im):
        z = jnp.zeros_like(re)
        top = jnp.concatenate([re[0::2], z[0::2], im[0::2], z[0::2]], axis=-1)
        bot = jnp.concatenate([z[1::2], re[1::2], z[1::2], im[1::2]], axis=-1)
        return jnp.concatenate([top, bot], axis=1).astype(bf16)

    def pair_out(re, im):
        z = jnp.zeros_like(re)
        rows = [jnp.concatenate([re[0::2], z[0::2]], axis=-1), jnp.concatenate([z[1::2], re[1::2]], axis=-1),
                jnp.concatenate([im[0::2], z[0::2]], axis=-1), jnp.concatenate([z[1::2], im[1::2]], axis=-1)]
        return jnp.concatenate(rows, axis=1).astype(bf16)

    def pair_a(a):
        return jnp.concatenate([a.real[0::2], a.real[1::2], a.imag[0::2], a.imag[1::2]], axis=-1)[:, None, :]

    mi = m_intra.reshape(C_GROUPS // 2, 2, T * C_GROUP, T * C_GROUP).astype(bf16)
    return (mi, pair_in(*win(w_f)), pair_in(*win(w_b)), pair_out(*wout(k_f)), pair_out(*wout(k_b)),
            pair_a(apow[0][..., T]), pair_a(apow[1][..., T]))


def _s5_glu_kernel(y_ref, u_ref, d_ref, w_ref, b_ref, o_ref, wbf_ref):
    @pl.when(pl.program_id(0) == 0)
    def _():
        wbf_ref[...] = w_ref[...].astype(bf16)

    y = y_ref[...] + d_ref[...] * u_ref[...].astype(f32)
    y = jax.nn.gelu(y, approximate=True)
    z = jnp.dot(y.astype(bf16), wbf_ref[...], preferred_element_type=f32) + b_ref[...]
    o_ref[...] = (y * jax.nn.sigmoid(z)).astype(o_ref.dtype)


def _s5_glu(y_ssm, proj, d2, glu_w, glu_b2):
    nblk = NB_ * LBPB
    return pl.pallas_call(
        _s5_glu_kernel,
        out_shape=jax.ShapeDtypeStruct((nblk * RB, HALF), bf16),
        grid=(nblk,),
        in_specs=[pl.BlockSpec((RB, HALF), lambda i: (_lat_blk(i), 0)),
                  pl.BlockSpec((RB, HALF), lambda i: (_lat_blk(i), 0)),
                  pl.BlockSpec((1, HALF), lambda i: (0, 0)),
                  pl.BlockSpec((HALF, HALF), lambda i: (0, 0)),
                  pl.BlockSpec((1, HALF), lambda i: (0, 0))],
        out_specs=pl.BlockSpec((RB, HALF), lambda i: (i, 0)),
        scratch_shapes=[pltpu.VMEM((HALF, HALF), bf16)],
        compiler_params=_cp(("arbitrary",)),
        name="s5_glu",
    )(y_ssm, proj, d2, glu_w, glu_b2)


def _hyena_filter(L, w1, b1, w2, b2, w3, freq):
    t = jnp.linspace(0.0, 1.0, L, dtype=f32)[:, None]
    w = 2 * math.pi * jnp.arange(L, dtype=f32)[:, None] / L
    bands = jnp.linspace(1e-4, HY_BANDS - 1, HY_BANDS, dtype=f32)
    feats = jnp.concatenate([t, jnp.cos(bands * w), -jnp.sin(bands * w)], axis=-1)
    h = jnp.sin(freq * (feats @ w1 + b1))
    h = jnp.sin(freq * (h @ w2 + b2))
    h = h @ w3
    fast = abs(math.log(HY_DECAY_TARGET) / HY_FAST_PCT)
    slow = abs(math.log(HY_DECAY_TARGET) / HY_SLOW_PCT)
    deltas = jnp.tile(jnp.linspace(slow, fast, HALF, dtype=f32), 2)
    h = h * jnp.exp(-t * deltas)
    h_fwd, h_bwd = jnp.split(h, 2, axis=-1)
    filt = jnp.concatenate([h_fwd, jnp.zeros((1, HALF), f32), h_bwd[:0:-1]], axis=0)
    return filt / jnp.sum(jnp.abs(filt), axis=0, keepdims=True)


def _hyena_dft_tables():
    r = HY_R
    n = r * r
    idx = np.arange(r)
    w64 = np.exp(-2j * np.pi * np.outer(idx, idx) / r)
    tw = np.exp(-2j * np.pi * np.outer(idx, idx) / n)
    f1 = w64[None, :, :r // 2] * tw[:, :, None]
    f1 = np.concatenate([f1.real, f1.imag], axis=1)
    f2 = np.block([[w64.real, -w64.imag], [w64.imag, w64.real]])
    g = np.conj(w64)[None, :, :] * np.conj(tw.T)[:, :, None]
    g2 = np.concatenate([np.concatenate([g.real, -g.imag], axis=2),
                         np.concatenate([g.imag, g.real], axis=2)], axis=1)
    c = np.conj(w64)[:r // 2, :]
    g1 = np.concatenate([c.real, -c.imag], axis=1) / n
    return (jnp.asarray(f1, bf16), jnp.asarray(f2, bf16), jnp.asarray(g2, bf16), jnp.asarray(g1, bf16))


def _hyena_kernel(x0_ref, x1_ref, v_ref, cw0_ref, cw1_ref, cw2_ref, cb0_ref, cb1_ref, cb2_ref, hb_ref,
                  f1_ref, f2_ref, g2_ref, g1_ref, hf_ref, o_ref, u_scr, a_scr, b_scr, y_scr):
    L = SEQ
    r = HY_R
    nt = HY_TC // 128
    row = lax.broadcasted_iota(jnp.int32, (L, HY_TC), 0)

    def sconv(z_ref, cw_ref, cb_ref):
        z = z_ref[...].astype(f32)
        zm = jnp.where(row == 0, 0.0, pltpu.roll(z, 1, axis=0))
        zp = jnp.where(row == L - 1, 0.0, pltpu.roll(z, L - 1, axis=0))
        return zm * cw_ref[0:1, :] + z * cw_ref[1:2, :] + zp * cw_ref[2:3, :] + cb_ref[...]

    u = sconv(x1_ref, cw1_ref, cb1_ref) * sconv(v_ref, cw2_ref, cb2_ref)
    for t in range(nt):
        u_scr[t] = u[:, t * 128:(t + 1) * 128]

    def stage1(n2, c):
        for t in range(nt):
            blk = u_scr[t, pl.ds(n2, r // 2, stride=r), :]
            a = jnp.dot(f1_ref[n2], blk.astype(bf16), preferred_element_type=f32)
            a_scr[t, pl.ds(pl.multiple_of(n2 * 2 * r, 2 * r), 2 * r), :] = a
        return c

    lax.fori_loop(0, r, stage1, 0)

    def stage2(k1, c):
        r0 = pl.multiple_of(k1 * 2 * r, 2 * r)
        for t in range(nt):
            blk = jnp.concatenate([a_scr[t, pl.ds(k1, r, stride=2 * r), :],
                                   a_scr[t, pl.ds(r + k1, r, stride=2 * r), :]], axis=0)
            x = jnp.dot(f2_ref[...], blk.astype(bf16), preferred_element_type=f32)
            h = hf_ref[pl.ds(r0, 2 * r), t * 128:(t + 1) * 128]
            xr, xi, hr, hi = x[:r], x[r:], h[:r], h[r:]
            y = jnp.concatenate([xr * hr - xi * hi, xr * hi + xi * hr], axis=0)
            b = jnp.dot(g2_ref[k1], y.astype(bf16), preferred_element_type=f32)
            b_scr[t, pl.ds(r0, 2 * r), :] = b
        return c

    lax.fori_loop(0, r, stage2, 0)

    def stage3(n2, c):
        for t in range(nt):
            blk = jnp.concatenate([b_scr[t, pl.ds(n2, r, stride=2 * r), :],
                                   b_scr[t, pl.ds(r + n2, r, stride=2 * r), :]], axis=0)
            y = jnp.dot(g1_ref[...], blk.astype(bf16), preferred_element_type=f32)
            y_scr[t, pl.ds(n2, r // 2, stride=r), :] = y
        return c

    lax.fori_loop(0, r, stage3, 0)
    x0 = sconv(x0_ref, cw0_ref, cb0_ref)
    for t in range(nt):
        sl = slice(t * 128, (t + 1) * 128)
        o_ref[:, sl] = (x0[:, sl] * (y_scr[t] + u_scr[t] * hb_ref[:, sl])).astype(o_ref.dtype)


def _hyena(z_lat, conv_w, conv_b2, hy_bias2, hf):
    f1, f2, g2, g1 = _hyena_dft_tables()
    nct = HALF // HY_TC
    r = HY_R
    nt = HY_TC // 128

    def zspec(k):
        return pl.BlockSpec((SEQ, HY_TC), lambda ct, b: (b, k * nct + ct))

    def wspec(k):
        return pl.BlockSpec((3, HY_TC), lambda ct, b: (0, k * nct + ct))

    def bspec(k):
        return pl.BlockSpec((1, HY_TC), lambda ct, b: (0, k * nct + ct))

    def full(a):
        return pl.BlockSpec(a.shape, lambda ct, b: (0,) * a.ndim)

    return pl.pallas_call(
        _hyena_kernel,
        out_shape=jax.ShapeDtypeStruct((NB_ * SEQ, HALF), bf16),
        grid=(nct, NB_),
        in_specs=[zspec(0), zspec(1), zspec(2), wspec(0), wspec(1), wspec(2), bspec(0), bspec(1), bspec(2),
                  pl.BlockSpec((1, HY_TC), lambda ct, b: (0, ct)),
                  full(f1), full(f2), full(g2), full(g1),
                  pl.BlockSpec((2 * r * r, HY_TC), lambda ct, b: (0, ct))],
        out_specs=pl.BlockSpec((SEQ, HY_TC), lambda ct, b: (b, ct)),
        scratch_shapes=[pltpu.VMEM((nt, SEQ, 128), f32), pltpu.VMEM((nt, 2 * r * r, 128), f32),
                        pltpu.VMEM((nt, 2 * r * r, 128), f32), pltpu.VMEM((nt, SEQ, 128), f32)],
        compiler_params=_cp(("arbitrary", "arbitrary"), 56),
        name="hyena",
    )(z_lat, z_lat, z_lat, conv_w, conv_w, conv_w, conv_b2, conv_b2, conv_b2, hy_bias2, f1, f2, g2, g1, hf)


def _hyena_filter_spectrum(filt):
    r = HY_R
    hk = jnp.fft.fft(filt.astype(jnp.complex64), axis=0)
    hk = hk.reshape(r, r, -1).transpose(1, 0, 2)
    return jnp.concatenate([hk.real, hk.imag], axis=1).reshape(2 * r * r, -1).astype(f32)


def _swiglu_perm():
    j = np.arange(256)
    src = np.where(j < 128, 2 * j, 2 * (j - 128) + 1)
    p = np.zeros((256, 256), np.float32)
    p[src, j] = 1.0
    return jnp.asarray(p, dtype=bf16)


def kernel(x, c, ctx, c_ctx, ada_w, ada_b, norm_mix_g, norm_ffn_g, ev_w_in, ev_gate_b, ev_h_norm_g,
           ev_q_norm_g, ev_k_norm_g, ev_sink, ev_w_out, od_w_in, od_a_re, od_a_im, od_log_dt, od_b_re,
           od_b_im, od_c_re, od_c_im, od_d, od_glu_w, od_glu_b, od_conv_w, od_conv_b, od_filt_w1,
           od_filt_b1, od_filt_w2, od_filt_b2, od_filt_w3, od_filt_freq, od_hy_bias, od_w_out,
           moe_router_w, moe_router_b, moe_w1, moe_b1, moe_w2, moe_b2):
    depth = ada_w.shape[0]
    xu = jnp.concatenate([ctx, x], axis=1).reshape(ROWS, D)
    cond8 = jnp.concatenate([c_ctx[None], c, jnp.zeros((3, D), f32)], axis=0)
    ada_b3 = ada_b.reshape(depth, 1, 6 * D)
    g_mix3 = norm_mix_g.reshape(depth, 1, D)
    g_ffn3 = norm_ffn_g.reshape(depth, 1, D)
    rb3 = moe_router_b.reshape(depth, 1, N_EXPERTS)
    b1_4 = moe_b1.reshape(depth, N_EXPERTS, 1, 2 * D_FF)
    b2_4 = moe_b2.reshape(depth, N_EXPERTS, 1, D)
    perm = _swiglu_perm()
    nall = ROWS // RB
    nlat = NB_ * LBPB

    mod3 = _ada(cond8, ada_w, ada_b3, 0).reshape(8, 1, 6 * D)
    h = _norm_mod(xu, g_mix3, mod3, 0, 0, _ident, _mod_row, nall, bf16)
    w_in = ev_w_in[0]
    n_a = 2 * A_HEADS * A_DQK + 2 * HALF
    n_g = 4 * A_HEADS
    w_main = jnp.concatenate([w_in[:, :n_a], w_in[:, n_a + n_g:]], axis=1)
    w_gate = jnp.pad(w_in[:, n_a:n_a + n_g], ((0, 0), (0, 128 - n_g)))
    proj = _matmul([h], w_main, (), w_main.shape[1], 512, nall, _ident, bf16, name="even_in")
    gates = _matmul([h], w_gate, (), 128, 128, nall, _ident, f32, name="even_gates")[:, :n_g]
    gates = (gates + ev_gate_b[0]).reshape(NB_, LA, n_g)
    gcol, grow = _mlstm_gate_tables(gates)
    y_a = _mlstm(proj, gcol, grow, ev_h_norm_g[0].reshape(A_HEADS, 1, A_DV))
    qa = proj[:, n_a:n_a + HALF].reshape(NB_, LA, B_KV, B_G, B_DH).transpose(0, 2, 3, 1, 4)
    ka = proj[:, n_a + HALF:n_a + HALF + B_KV * B_DH].reshape(NB_, LA, B_KV, B_DH).transpose(0, 2, 1, 3)
    va = proj[:, n_a + HALF + B_KV * B_DH:].reshape(NB_, LA, B_KV, B_DH).transpose(0, 2, 1, 3)
    cosf, sinf = _rope_tables()
    sink_col = jnp.repeat(ev_sink[0].astype(f32).reshape(B_KV, B_G), 128, axis=1)[..., None]
    o5 = _attention(qa, ka, va, cosf, sinf, ev_q_norm_g[0].reshape(1, B_DH),
                    ev_k_norm_g[0].reshape(1, B_DH), sink_col)
    y_b = o5.transpose(0, 3, 1, 2, 4).reshape(ROWS, HALF)
    xu = _matmul([y_a, y_b], ev_w_out, (0,), D, 512, nall, _ident, f32,
                 res=xu, res_blk=_ident, mrow=_mod_row, mod3=mod3, gate_chunk=2, name="even_out")
    xu = _moe_layer(xu, g_ffn3, mod3, moe_router_w, rb3, moe_w1, b1_4, moe_w2, b2_4, perm,
                    0, _ident, _mod_row, nall)

    mod3 = _ada(cond8, ada_w, ada_b3, 1).reshape(8, 1, 6 * D)
    h = _norm_mod(xu, g_mix3, mod3, 1, 0, _ident, _mod_row, nall, bf16)
    proj = _matmul([h], od_w_in, (0,), HALF, 512, nall, _ident, bf16, name="odd_in_u")
    z_lat = _matmul([h], od_w_in, (0,), 3 * HALF, 512, nlat, _lat_blk, bf16, w_col0=HALF // 512,
                    name="odd_in_z")
    nck = LA // S5_T
    u6 = proj[:, :HALF].reshape(NB_, nck, S5_T, C_GROUPS // 2, 2, C_GROUP)
    u3 = jnp.pad(u6.transpose(3, 1, 0, 4, 2, 5), ((0, 0), (0, 0), (0, 8 - NB_), (0, 0), (0, 0), (0, 0)))
    u3 = u3.reshape(C_GROUPS // 2, nck * 8, 2 * S5_T * C_GROUP)
    mats = _s5_matrices(od_a_re[0], od_a_im[0], od_log_dt[0], od_b_re[0], od_b_im[0], od_c_re[0], od_c_im[0])
    y3 = _s5_core(u3, mats)
    y_ssm = y3.reshape(C_GROUPS // 2, nck, 8, 2, S5_T, C_GROUP)[:, :, :NB_]
    y_ssm = y_ssm.transpose(2, 1, 4, 0, 3, 5).reshape(ROWS, HALF)
    s_l = _s5_glu(y_ssm, proj, od_d[0].reshape(1, HALF), od_glu_w[0], od_glu_b[0].reshape(1, HALF))
    filt = _hyena_filter(SEQ, od_filt_w1[0], od_filt_b1[0], od_filt_w2[0], od_filt_b2[0], od_filt_w3[0],
                         od_filt_freq[0])
    hy_l = _hyena(z_lat, od_conv_w[0], od_conv_b[0].reshape(1, 3 * HALF), od_hy_bias[0].reshape(1, HALF),
                  _hyena_filter_spectrum(filt))
    x_lat = _matmul([s_l, hy_l], od_w_out, (0,), D, 512, nlat, _ident, f32,
                    res=xu, res_blk=_lat_blk, mrow=_lat_mod_row, mod3=mod3, gate_chunk=2, name="odd_out")
    out = _moe_layer(x_lat, g_ffn3, mod3, moe_router_w, rb3, moe_w1, b1_4, moe_w2, b2_4, perm,
                     1, _ident, _lat_mod_row, nlat)
    return out.reshape(NB_, SEQ, D)
```

```python
import functools
import math

import jax
import jax.numpy as jnp
import numpy as np
from jax import lax
from jax.experimental import pallas as pl
from jax.experimental.pallas import tpu as pltpu

f32 = jnp.float32
bf16 = jnp.bfloat16

D = 2048
NB_ = 4
SEQ = 2048
CTX = 256
LA = SEQ + CTX
ROWS = NB_ * LA
RB = 256
BPB = LA // RB
LBPB = SEQ // RB
EPS = 1e-6
GRID_W = 64

HALF = D // 2
A_HEADS = 4
A_DV = HALF // A_HEADS
A_DQK = A_DV // 2
A_T = 256
B_HEADS = 16
B_KV = 4
B_G = B_HEADS // B_KV
B_DH = HALF // B_HEADS
WINDOW = 128
ROPE_BASE = 10000.0
C_GROUP = 16
C_GROUPS = HALF // C_GROUP
C_STATE = 64
S5_T = 8
S5_GT = 128 // C_GROUP
S5_NS = S5_GT * C_STATE
S5_NCH = ROWS // S5_T
S5_CPB = LA // S5_T
S5_CCTX = CTX // S5_T
HY_TC = 256
HY_R = 64
HY_BANDS = 16
HY_DECAY_TARGET = 1e-2
HY_FAST_PCT = 0.3
HY_SLOW_PCT = 1.5
N_EXPERTS = 32
TOP_K = 4
D_FF = D
SWIGLU_LIMIT = 7.0
SWIGLU_ALPHA = 1.702
MOE_TM = 256

EVEN_SPLITS = (A_HEADS * A_DQK, A_HEADS * A_DQK, HALF, HALF, 4 * A_HEADS,
               B_HEADS * B_DH, B_KV * B_DH, B_KV * B_DH)

VMEM_MB = 1024 * 1024


def _cp(sem, vmem_mb=40):
    return pltpu.CompilerParams(dimension_semantics=sem, vmem_limit_bytes=vmem_mb * VMEM_MB)


def _lat_blk(i):
    return (i // LBPB) * BPB + 1 + i % LBPB


def _mod_row(u):
    return jnp.where(u % BPB == 0, 0, 1 + u // BPB)


def _lat_mod_row(i):
    return 1 + i // LBPB


def _ident(i):
    return i


def _ada_kernel(c_ref, w_ref, b_ref, o_ref):
    c = c_ref[...]
    s = c * jax.nn.sigmoid(c)
    o_ref[...] = jnp.dot(s.astype(bf16), w_ref[...].astype(bf16),
                         preferred_element_type=f32) + b_ref[...]


def _ada(cond8, ada_w, ada_b3, layer):
    tn = 1024
    n = 6 * D
    return pl.pallas_call(
        _ada_kernel,
        out_shape=jax.ShapeDtypeStruct((8, n), f32),
        grid=(n // tn,),
        in_specs=[pl.BlockSpec((8, D), lambda j: (0, 0)),
                  pl.BlockSpec((None, D, tn), lambda j: (layer, 0, j)),
                  pl.BlockSpec((None, 1, tn), lambda j: (layer, 0, j))],
        out_specs=pl.BlockSpec((8, tn), lambda j: (0, j)),
        compiler_params=_cp(("arbitrary",)),
        name="ada_params",
    )(cond8, ada_w, ada_b3)


def _norm_mod_kernel(x_ref, g_ref, sh_ref, sc_ref, o_ref):
    x = x_ref[...]
    y = x * lax.rsqrt(jnp.mean(x * x, axis=-1, keepdims=True) + EPS) * g_ref[...]
    o_ref[...] = (y * (1.0 + sc_ref[...]) + sh_ref[...]).astype(o_ref.dtype)


def _norm_mod(x, g3, mod3, layer, chunk, blk, mrow, nblk, out_dtype):
    return pl.pallas_call(
        _norm_mod_kernel,
        out_shape=jax.ShapeDtypeStruct((nblk * RB, D), out_dtype),
        grid=(nblk,),
        in_specs=[pl.BlockSpec((RB, D), lambda i: (blk(i), 0)),
                  pl.BlockSpec((None, 1, D), lambda i: (layer, 0, 0)),
                  pl.BlockSpec((None, 1, D), lambda i: (mrow(i), 0, chunk)),
                  pl.BlockSpec((None, 1, D), lambda i: (mrow(i), 0, chunk + 1))],
        out_specs=pl.BlockSpec((RB, D), lambda i: (i, 0)),
        compiler_params=_cp(("arbitrary",)),
        name="norm_mod",
    )(x, g3, mod3, mod3)


def _split_bf16(x):
    hi = x.astype(bf16)
    lo = (x - hi.astype(f32)).astype(bf16)
    return hi, lo


def _norm_router_kernel(x_ref, g_ref, sh_ref, sc_ref, rw_ref, rb_ref,
                        h_ref, idx_ref, gate_ref, rank_ref, cnt_ref, carry_ref):
    i = pl.program_id(0)

    @pl.when(i == 0)
    def _():
        carry_ref[...] = jnp.zeros_like(carry_ref)

    x = x_ref[...]
    y = x * lax.rsqrt(jnp.mean(x * x, axis=-1, keepdims=True) + EPS) * g_ref[...]
    h = y * (1.0 + sc_ref[...]) + sh_ref[...]
    h_ref[...] = h
    h_hi, h_lo = _split_bf16(h)
    w_hi, w_lo = _split_bf16(rw_ref[...])
    logits = (jnp.dot(h_hi, w_hi, preferred_element_type=f32)
              + jnp.dot(h_lo, w_hi, preferred_element_type=f32)
              + jnp.dot(h_hi, w_lo, preferred_element_type=f32)) + rb_ref[...]
    ne = logits.shape[-1]
    lane = lax.broadcasted_iota(jnp.int32, logits.shape, 1)
    l = logits
    vals, idxs = [], []
    for _ in range(TOP_K):
        m = jnp.max(l, axis=-1, keepdims=True)
        ix = jnp.min(jnp.where(l == m, lane, ne), axis=-1, keepdims=True)
        vals.append(m)
        idxs.append(ix)
        l = jnp.where(lane == ix, -jnp.inf, l)
    es = [jnp.exp(v - vals[0]) for v in vals]
    tot = es[0] + es[1] + es[2] + es[3]
    onehot = jnp.zeros(logits.shape, f32)
    for ix in idxs:
        onehot = onehot + jnp.where(lane == ix, 1.0, 0.0)
    r_i = lax.broadcasted_iota(jnp.int32, (RB, RB), 0)
    c_i = lax.broadcasted_iota(jnp.int32, (RB, RB), 1)
    tri = jnp.where(c_i < r_i, 1.0, 0.0).astype(bf16)
    prefix = jnp.dot(tri, onehot.astype(bf16), preferred_element_type=f32) + carry_ref[...]
    for k in range(TOP_K):
        idx_ref[:, k:k + 1] = idxs[k]
        gate_ref[:, k:k + 1] = es[k] / tot
        rk = jnp.sum(jnp.where(lane == idxs[k], prefix, 0.0), axis=-1, keepdims=True)
        rank_ref[:, k:k + 1] = rk.astype(jnp.int32)
    carry_ref[...] = carry_ref[...] + jnp.sum(onehot, axis=0, keepdims=True)
    cnt_ref[...] = carry_ref[...]


def _norm_router(x, g3, mod3, rw, rb3, layer, chunk, blk, mrow, nblk):
    t = nblk * RB
    return pl.pallas_call(
        _norm_router_kernel,
        out_shape=(jax.ShapeDtypeStruct((t, D), f32),
                   jax.ShapeDtypeStruct((t, TOP_K), jnp.int32),
                   jax.ShapeDtypeStruct((t, TOP_K), f32),
                   jax.ShapeDtypeStruct((t, TOP_K), jnp.int32),
                   jax.ShapeDtypeStruct((1, N_EXPERTS), f32)),
        grid=(nblk,),
        in_specs=[pl.BlockSpec((RB, D), lambda i: (blk(i), 0)),
                  pl.BlockSpec((None, 1, D), lambda i: (layer, 0, 0)),
                  pl.BlockSpec((None, 1, D), lambda i: (mrow(i), 0, chunk)),
                  pl.BlockSpec((None, 1, D), lambda i: (mrow(i), 0, chunk + 1)),
                  pl.BlockSpec((None, D, N_EXPERTS), lambda i: (layer, 0, 0)),
                  pl.BlockSpec((None, 1, N_EXPERTS), lambda i: (layer, 0, 0))],
        out_specs=(pl.BlockSpec((RB, D), lambda i: (i, 0)),
                   pl.BlockSpec((RB, TOP_K), lambda i: (i, 0)),
                   pl.BlockSpec((RB, TOP_K), lambda i: (i, 0)),
                   pl.BlockSpec((RB, TOP_K), lambda i: (i, 0)),
                   pl.BlockSpec((1, N_EXPERTS), lambda i: (0, 0))),
        scratch_shapes=[pltpu.VMEM((1, N_EXPERTS), f32)],
        compiler_params=_cp(("arbitrary",)),
        name="norm_router",
    )(x, g3, mod3, mod3, rw, rb3)


def _mm_kernel(*refs, n_a, k_split, epilogue):
    a_refs = refs[:n_a]
    w_ref = refs[n_a]
    pos = n_a + 1
    if epilogue:
        res_ref, gate_ref = refs[pos], refs[pos + 1]
        pos += 2
    o_ref, wbf_ref = refs[pos], refs[pos + 1]

    @pl.when(pl.program_id(1) == 0)
    def _():
        wbf_ref[...] = w_ref[...].astype(bf16)

    acc = None
    for t, a_ref in enumerate(a_refs):
        part = jnp.dot(a_ref[...], wbf_ref[t * k_split:(t + 1) * k_split, :],
                       preferred_element_type=f32)
        acc = part if acc is None else acc + part
    if epilogue:
        acc = res_ref[...] + gate_ref[...] * acc
    o_ref[...] = acc.astype(o_ref.dtype)


def _matmul(a_list, w, w_idx, n, tn, nblk, a_blk, out_dtype, *, a_col=None, w_col0=0,
            res=None, res_blk=None, mrow=None, mod3=None, gate_chunk=None, name="matmul"):
    n_a = len(a_list)
    k = w.shape[-2]
    k_split = k // n_a
    lead = len(w_idx)
    a_col = a_col or [0] * n_a
    in_specs = [pl.BlockSpec((RB, k_split), functools.partial(lambda j, i, c: (a_blk(i), c), c=c))
                for c in a_col]
    in_specs.append(pl.BlockSpec((None,) * lead + (k, tn), lambda j, i: tuple(w_idx) + (0, j + w_col0)))
    args = list(a_list) + [w]
    epilogue = res is not None
    if epilogue:
        per = D // tn
        in_specs.append(pl.BlockSpec((RB, tn), lambda j, i: (res_blk(i), j)))
        in_specs.append(pl.BlockSpec((None, 1, tn),
                                     lambda j, i: (mrow(i), 0, gate_chunk * per + j)))
        args += [res, mod3]
    return pl.pallas_call(
        functools.partial(_mm_kernel, n_a=n_a, k_split=k_split, epilogue=epilogue),
        out_shape=jax.ShapeDtypeStruct((nblk * RB, n), out_dtype),
        grid=(n // tn, nblk),
        in_specs=in_specs,
        out_specs=pl.BlockSpec((RB, tn), lambda j, i: (i, j)),
        scratch_shapes=[pltpu.VMEM((k, tn), bf16)],
        compiler_params=_cp(("arbitrary", "arbitrary"), 48),
        name=name,
    )(*args)


def _gather_rows_kernel(tok_ref, h_hbm, o_ref, buf, sem, *, rows, nsteps):
    i = pl.program_id(0)

    def issue(blk, slot):
        base = blk * rows

        def body(r, c):
            t = tok_ref[base + r]
            pltpu.make_async_copy(h_hbm.at[pl.ds(t, 1), :], buf.at[slot, pl.ds(r, 1), :],
                                  sem.at[slot]).start()
            return c

        lax.fori_loop(0, rows, body, 0, unroll=8)

    @pl.when(i == 0)
    def _():
        issue(0, 0)

    @pl.when(i + 1 < nsteps)
    def _():
        issue(i + 1, (i + 1) % 2)

    slot = i % 2
    pltpu.make_async_copy(h_hbm.at[pl.ds(0, rows), :], buf.at[slot], sem.at[slot]).wait()
    o_ref[...] = buf[slot].astype(o_ref.dtype)


def _gather_rows(slot_tok, h, nr):
    rows = 512
    nsteps = nr // rows
    return pl.pallas_call(
        functools.partial(_gather_rows_kernel, rows=rows, nsteps=nsteps),
        out_shape=jax.ShapeDtypeStruct((nr, D), bf16),
        grid_spec=pltpu.PrefetchScalarGridSpec(
            num_scalar_prefetch=1,
            grid=(nsteps,),
            in_specs=[pl.BlockSpec(memory_space=pl.ANY)],
            out_specs=pl.BlockSpec((rows, D), lambda i, tok: (i, 0)),
            scratch_shapes=[pltpu.VMEM((2, rows, D), f32), pltpu.SemaphoreType.DMA((2,))]),
        compiler_params=_cp(("arbitrary",)),
        name="moe_gather",
    )(slot_tok, h)


def _moe_up_kernel(start_ref, nblk_ref, xs_hbm, w_ref, b_ref, p_ref, act_hbm,
                   wbf_ref, xbuf, obuf, xsem, osem, *, tn):
    j = pl.program_id(0)
    e = pl.program_id(1)
    nb = nblk_ref[e]
    row0 = start_ref[e]

    def x_copy(blk, slot):
        r = pl.multiple_of(row0 + blk * MOE_TM, MOE_TM)
        return pltpu.make_async_copy(xs_hbm.at[pl.ds(r, MOE_TM), :], xbuf.at[slot], xsem.at[slot])

    def o_copy(blk, slot):
        r = pl.multiple_of(row0 + blk * MOE_TM, MOE_TM)
        return pltpu.make_async_copy(obuf.at[slot], act_hbm.at[j, pl.ds(r, MOE_TM), :], osem.at[slot])

    @pl.when(nb > 0)
    def _():
        x_copy(0, 0).start()
        wbf_ref[...] = w_ref[...].astype(bf16)

    def body(blk, c):
        slot = blk % 2
        x_copy(blk, slot).wait()

        @pl.when(blk + 1 < nb)
        def _():
            x_copy(blk + 1, 1 - slot).start()

        h = jnp.dot(xbuf[slot], wbf_ref[...], preferred_element_type=f32) + b_ref[...]

        @pl.when(blk >= 2)
        def _():
            o_copy(blk - 2, slot).wait()

        for cc in range(tn // 256):
            d = jnp.dot(h[:, cc * 256:(cc + 1) * 256].astype(bf16), p_ref[...],
                        preferred_element_type=f32)
            gate = jnp.minimum(d[:, :128], SWIGLU_LIMIT)
            lin = jnp.clip(d[:, 128:], -SWIGLU_LIMIT, SWIGLU_LIMIT)
            act = gate * jax.nn.sigmoid(SWIGLU_ALPHA * gate) * (lin + 1.0)
            obuf[slot, :, cc * 128:(cc + 1) * 128] = act.astype(obuf.dtype)
        o_copy(blk, slot).start()
        return c

    lax.fori_loop(0, nb, body, 0)

    @pl.when(nb >= 2)
    def _():
        o_copy(nb - 2, nb % 2).wait()

    @pl.when(nb >= 1)
    def _():
        o_copy(nb - 1, (nb - 1) % 2).wait()

    @pl.when(e == N_EXPERTS - 1)
    def _():
        obuf[0] = jnp.zeros(obuf.shape[1:], obuf.dtype)

        def zero_tail(blk, c):
            r = pl.multiple_of(blk * MOE_TM, MOE_TM)
            cp = pltpu.make_async_copy(obuf.at[0], act_hbm.at[j, pl.ds(r, MOE_TM), :], osem.at[0])
            cp.start()
            cp.wait()
            return c

        lax.fori_loop(row0 // MOE_TM + nb, act_hbm.shape[1] // MOE_TM, zero_tail, 0)


def _moe_up(start, nblk_e, xs, w1, b1, perm, layer, nr):
    tn = 2048
    nj = 2 * D_FF // tn
    return pl.pallas_call(
        functools.partial(_moe_up_kernel, tn=tn),
        out_shape=jax.ShapeDtypeStruct((nj, nr, tn // 2), bf16),
        grid_spec=pltpu.PrefetchScalarGridSpec(
            num_scalar_prefetch=2,
            grid=(nj, N_EXPERTS),
            in_specs=[pl.BlockSpec(memory_space=pl.ANY),
                      pl.BlockSpec((None, None, D, tn), lambda j, e, st, nb: (layer, e, 0, j)),
                      pl.BlockSpec((None, None, 1, tn), lambda j, e, st, nb: (layer, e, 0, j)),
                      pl.BlockSpec((256, 256), lambda j, e, st, nb: (0, 0))],
            out_specs=pl.BlockSpec(memory_space=pl.ANY),
            scratch_shapes=[pltpu.VMEM((D, tn), bf16),
                            pltpu.VMEM((2, MOE_TM, D), bf16),
                            pltpu.VMEM((2, MOE_TM, tn // 2), bf16),
                            pltpu.SemaphoreType.DMA((2,)), pltpu.SemaphoreType.DMA((2,))]),
        compiler_params=_cp(("arbitrary", "arbitrary"), 56),
        name="moe_up",
    )(start, nblk_e, xs, w1, b1, perm)


def _moe_down_kernel(start_ref, nblk_ref, act_hbm, w_ref, b_ref, y_hbm, wbf_ref, abuf, obuf, asem, osem):
    e = pl.program_id(0)
    nb = nblk_ref[e]
    row0 = start_ref[e]
    nj = abuf.shape[1]
    kh = D_FF // nj

    def a_copy(blk, slot, jj):
        r = pl.multiple_of(row0 + blk * MOE_TM, MOE_TM)
        return pltpu.make_async_copy(act_hbm.at[jj, pl.ds(r, MOE_TM), :], abuf.at[slot, jj], asem.at[slot])

    def o_copy(blk, slot):
        r = pl.multiple_of(row0 + blk * MOE_TM, MOE_TM)
        return pltpu.make_async_copy(obuf.at[slot], y_hbm.at[pl.ds(r, MOE_TM), :], osem.at[slot])

    @pl.when(nb > 0)
    def _():
        for jj in range(nj):
            a_copy(0, 0, jj).start()
        wbf_ref[...] = w_ref[...].astype(bf16)

    def body(blk, c):
        slot = blk % 2
        for jj in range(nj):
            a_copy(blk, slot, jj).wait()

        @pl.when(blk + 1 < nb)
        def _():
            for jj in range(nj):
                a_copy(blk + 1, 1 - slot, jj).start()

        y = b_ref[...] + jnp.dot(abuf[slot, 0], wbf_ref[0:kh, :], preferred_element_type=f32)
        for jj in range(1, nj):
            y = y + jnp.dot(abuf[slot, jj], wbf_ref[jj * kh:(jj + 1) * kh, :], preferred_element_type=f32)

        @pl.when(blk >= 2)
        def _():
            o_copy(blk - 2, slot).wait()

        obuf[slot] = y
        o_copy(blk, slot).start()
        return c

    lax.fori_loop(0, nb, body, 0)

    @pl.when(nb >= 2)
    def _():
        o_copy(nb - 2, nb % 2).wait()

    @pl.when(nb >= 1)
    def _():
        o_copy(nb - 1, (nb - 1) % 2).wait()

    @pl.when(e == N_EXPERTS - 1)
    def _():
        obuf[0] = jnp.zeros(obuf.shape[1:], obuf.dtype)

        def zero_tail(blk, c):
            r = pl.multiple_of(blk * MOE_TM, MOE_TM)
            cp = pltpu.make_async_copy(obuf.at[0], y_hbm.at[pl.ds(r, MOE_TM), :], osem.at[0])
            cp.start()
            cp.wait()
            return c

        lax.fori_loop(row0 // MOE_TM + nb, y_hbm.shape[0] // MOE_TM, zero_tail, 0)


def _moe_down(start, nblk_e, act, w2, b2, layer, nr):
    nj = act.shape[0]
    return pl.pallas_call(
        _moe_down_kernel,
        out_shape=jax.ShapeDtypeStruct((nr, D), f32),
        grid_spec=pltpu.PrefetchScalarGridSpec(
            num_scalar_prefetch=2,
            grid=(N_EXPERTS,),
            in_specs=[pl.BlockSpec(memory_space=pl.ANY),
                      pl.BlockSpec((None, None, D_FF, D), lambda e, st, nb: (layer, e, 0, 0)),
                      pl.BlockSpec((None, None, 1, D), lambda e, st, nb: (layer, e, 0, 0))],
            out_specs=pl.BlockSpec(memory_space=pl.ANY),
            scratch_shapes=[pltpu.VMEM((D_FF, D), bf16),
                            pltpu.VMEM((2, nj, MOE_TM, D_FF // nj), bf16),
                            pltpu.VMEM((2, MOE_TM, D), f32),
                            pltpu.SemaphoreType.DMA((2,)), pltpu.SemaphoreType.DMA((2,))]),
        compiler_params=_cp(("arbitrary",), 56),
        name="moe_down",
    )(start, nblk_e, act, w2, b2)


def _combine_kernel(slot_ref, y_hbm, x_ref, gmod_ref, w_ref, o_ref, buf, sem, *, tm, nsteps):
    i = pl.program_id(0)

    def issue(blk, slot):
        base = blk * tm

        def body(r, c):
            for k in range(TOP_K):
                s = slot_ref[(base + r) * TOP_K + k]
                pltpu.make_async_copy(y_hbm.at[pl.ds(s, 1), :], buf.at[slot, k, pl.ds(r, 1), :],
                                      sem.at[slot]).start()
            return c

        lax.fori_loop(0, tm, body, 0, unroll=4)

    @pl.when(i == 0)
    def _():
        issue(0, 0)

    @pl.when(i + 1 < nsteps)
    def _():
        issue(i + 1, (i + 1) % 2)

    slot = i % 2
    for k in range(TOP_K):
        pltpu.make_async_copy(y_hbm.at[pl.ds(0, tm), :], buf.at[slot, k], sem.at[slot]).wait()
    w = w_ref[...]
    acc = ((w[:, 0:1] * buf[slot, 0] + w[:, 1:2] * buf[slot, 1])
           + (w[:, 2:3] * buf[slot, 2] + w[:, 3:4] * buf[slot, 3]))
    o_ref[...] = x_ref[...] + gmod_ref[...] * acc


def _combine(slot_flat, ys, x, mod3, gate, gate_chunk, blk, mrow, nblk):
    tm = RB
    return pl.pallas_call(
        functools.partial(_combine_kernel, tm=tm, nsteps=nblk),
        out_shape=jax.ShapeDtypeStruct((nblk * tm, D), f32),
        grid_spec=pltpu.PrefetchScalarGridSpec(
            num_scalar_prefetch=1,
            grid=(nblk,),
            in_specs=[pl.BlockSpec(memory_space=pl.ANY),
                      pl.BlockSpec((tm, D), lambda i, s: (blk(i), 0)),
                      pl.BlockSpec((None, 1, D), lambda i, s: (mrow(i), 0, gate_chunk)),
                      pl.BlockSpec((tm, TOP_K), lambda i, s: (i, 0))],
            out_specs=pl.BlockSpec((tm, D), lambda i, s: (i, 0)),
            scratch_shapes=[pltpu.VMEM((2, TOP_K, tm, D), f32), pltpu.SemaphoreType.DMA((2,))]),
        compiler_params=_cp(("arbitrary",)),
        name="moe_combine",
    )(slot_flat, ys, x, mod3, gate)


def _moe_layer(x, g3, mod3, rw, rb3, w1, b1, w2, b2, perm, layer, blk, mrow, nblk):
    t = nblk * RB
    h, idx, gate, rank, cnt = _norm_router(x, g3, mod3, rw, rb3, layer, 3, blk, mrow, nblk)
    n_assign = t * TOP_K
    nb = n_assign // MOE_TM + N_EXPERTS
    nr = nb * MOE_TM
    counts = cnt[0].astype(jnp.int32)
    padded = (counts + MOE_TM - 1) // MOE_TM * MOE_TM
    pad_end = jnp.cumsum(padded)
    pad_start = pad_end - padded
    slot = pad_start[idx] + rank
    slot_flat = slot.reshape(-1)
    tok = jnp.repeat(jnp.arange(t, dtype=jnp.int32), TOP_K)
    slot_tok = (jnp.arange(nr, dtype=jnp.int32) % t).at[slot_flat].set(tok)
    nblk_e = (padded // MOE_TM).astype(jnp.int32)
    xs = _gather_rows(slot_tok, h, nr)
    act = _moe_up(pad_start.astype(jnp.int32), nblk_e, xs, w1, b1, perm, layer, nr)
    ys = _moe_down(pad_start.astype(jnp.int32), nblk_e, act, w2, b2, layer, nr)
    return _combine(slot_flat, ys, x, mod3, gate, 5, blk, mrow, nblk)


def _dot_nt(a, b):
    return lax.dot_general(a, b, (((1,), (1,)), ((), ())), preferred_element_type=f32)


def _dot_tn(a, b):
    return lax.dot_general(a, b, (((0,), (0,)), ((), ())), preferred_element_type=f32)


def _mlstm_kernel(q_ref, k_ref, v_ref, o_ref, gc_ref, gr_ref, ng_ref, y_ref, hf_ref, hb_ref, ct_ref):
    T = A_T
    nch = LA // T
    ct_ref[...] = jnp.zeros_like(ct_ref)
    r_i = lax.broadcasted_iota(jnp.int32, (T, T), 0)
    c_i = lax.broadcasted_iota(jnp.int32, (T, T), 1)
    lane = lax.broadcasted_iota(jnp.int32, (T, 128), 1)
    ones_blk = jnp.where(lane == 0, 1.0, 0.0).astype(bf16)

    def chunk(c, dirn, h_ref):
        r0 = pl.multiple_of(c * T, T)
        q = q_ref[pl.ds(r0, T), :]
        k = k_ref[pl.ds(r0, T), :]
        v = v_ref[pl.ds(r0, T), :]
        gc = gc_ref[pl.ds(r0, T), :]
        o = 5 * dirn
        a_col, m_col, negm = gc[:, o:o + 1], gc[:, o + 1:o + 2], gc[:, o + 2:o + 3]
        m_end, m_prev = gc[:, o + 3:o + 4], gc[:, o + 4:o + 5]
        a_row = gr_ref[dirn:dirn + 1, pl.ds(r0, T)]
        mask = (c_i <= r_i) if dirn == 0 else (c_i >= r_i)
        w = jnp.exp(jnp.where(mask, a_row - m_col, -jnp.inf))
        s = _dot_nt(q, k) * (A_DQK ** -0.5) * w
        v_aug = jnp.concatenate([v, ones_blk], axis=1)
        ct = ct_ref[dirn]
        w_carry = jnp.exp(m_prev - m_col)
        num = (jnp.dot(s.astype(bf16), v_aug, preferred_element_type=f32)
               + w_carry * jnp.dot(q, ct.astype(bf16), preferred_element_type=f32))
        den = num[:, A_DV:A_DV + 1]
        h_ref[pl.ds(r0, T), :] = num[:, :A_DV] / jnp.maximum(jnp.abs(den), jnp.exp(negm))
        w_in = jnp.exp(a_col - m_end)
        decay = jnp.exp(m_prev[0:1, :] - m_end[0:1, :])
        kv = _dot_tn(k, (w_in * v_aug.astype(f32)).astype(bf16)) * (A_DQK ** -0.5)
        ct_ref[dirn] = decay * ct + kv

    def step(it, carry):
        chunk(it, 0, hf_ref)
        chunk(jnp.where(it == 0, 0, nch - it), 1, hb_ref)
        return carry

    lax.fori_loop(0, nch, step, 0)
    h = hf_ref[...] + hb_ref[...]
    hn = h * lax.rsqrt(jnp.mean(h * h, axis=-1, keepdims=True) + EPS) * ng_ref[...]
    y_ref[...] = (hn * jax.nn.sigmoid(o_ref[...].astype(f32))).astype(y_ref.dtype)


def _mlstm(proj, gcol, grow, ng3):
    qb, vb = A_DQK, A_DV
    return pl.pallas_call(
        _mlstm_kernel,
        out_shape=jax.ShapeDtypeStruct((ROWS, HALF), bf16),
        grid=(NB_, A_HEADS),
        in_specs=[pl.BlockSpec((LA, qb), lambda b, h: (b, h)),
                  pl.BlockSpec((LA, qb), lambda b, h: (b, A_HEADS + h)),
                  pl.BlockSpec((LA, vb), lambda b, h: (b, (2 * A_HEADS * qb) // vb + h)),
                  pl.BlockSpec((LA, vb), lambda b, h: (b, (2 * A_HEADS * qb + HALF) // vb + h)),
                  pl.BlockSpec((None, None, LA, 16), lambda b, h: (b, h, 0, 0)),
                  pl.BlockSpec((None, None, 8, LA), lambda b, h: (b, h, 0, 0)),
                  pl.BlockSpec((None, 1, vb), lambda b, h: (h, 0, 0))],
        out_specs=pl.BlockSpec((LA, vb), lambda b, h: (b, h)),
        scratch_shapes=[pltpu.VMEM((LA, vb), f32), pltpu.VMEM((LA, vb), f32),
                        pltpu.VMEM((2, qb, vb + 128), f32)],
        compiler_params=_cp(("arbitrary", "arbitrary")),
        name="mlstm",
    )(proj, proj, proj, proj, gcol, grow, ng3)


def _mlstm_gate_tables(gates):
    i_f, f_f, i_b, f_b = jnp.split(gates, 4, axis=-1)

    def scan_tables(i_pre, f_pre):
        lf = jax.nn.log_sigmoid(f_pre)
        big_f = jnp.cumsum(lf, axis=1)
        a = i_pre - big_f
        m_run = jnp.maximum(lax.cummax(a, axis=1), 0.0)
        m_end = jnp.repeat(m_run[:, A_T - 1::A_T], A_T, axis=1)
        m_prev = jnp.concatenate([jnp.zeros_like(m_end[:, :A_T]), m_end[:, :-A_T]], axis=1)
        return a, m_run, -(big_f + m_run), m_end, m_prev

    def to_bwd(t):
        return jnp.concatenate([jnp.flip(t[:, :CTX], axis=1), jnp.flip(t[:, CTX:], axis=1)], axis=1)

    fw = scan_tables(i_f, f_f)
    bw = [to_bwd(t) for t in scan_tables(to_bwd(i_b), to_bwd(f_b))]
    cols = jnp.stack(list(fw) + bw, axis=-1)
    cols = jnp.pad(cols, ((0, 0), (0, 0), (0, 0), (0, 6))).transpose(0, 2, 1, 3)
    rows = jnp.stack([fw[0], bw[0]], axis=1).transpose(0, 3, 1, 2)
    rows = jnp.pad(rows, ((0, 0), (0, 0), (0, 6), (0, 0)))
    return cols, rows


def _swap_halves(t):
    hd = t.shape[-1] // 2
    return jnp.concatenate([t[..., hd:], t[..., :hd]], axis=-1)


def _attn_kernel(q_ref, k_ref, v_ref, cos_ref, sin_ref, gq_ref, gk_ref, sink_ref, o_ref):
    n = pl.program_id(2)
    scale = B_DH ** -0.5
    nq = B_G * 128

    def norm(t, g_ref):
        return t * lax.rsqrt(jnp.mean(t * t, axis=-1, keepdims=True) + EPS) * g_ref[...]

    qn = norm(q_ref[...].astype(f32), gq_ref)
    kc = norm(k_ref[0:CTX, :].astype(f32), gk_ref).astype(bf16)
    vc = v_ref[0:CTX, :]
    sink = sink_ref[...]

    def finish(parts):
        m = sink
        for s, _ in parts:
            m = jnp.maximum(m, jnp.max(s, axis=-1, keepdims=True))
        den = jnp.exp(sink - m)
        acc = None
        for s, vv in parts:
            p = jnp.exp(s - m)
            den = den + jnp.sum(p, axis=-1, keepdims=True)
            pv = jnp.dot(p.astype(bf16), vv, preferred_element_type=f32)
            acc = pv if acc is None else acc + pv
        o_ref[...] = (acc / den).reshape(B_G, 128, B_DH).astype(o_ref.dtype)

    @pl.when(n < CTX // 128)
    def _():
        s_ctx = _dot_nt(qn.reshape(nq, B_DH).astype(bf16), kc) * scale
        finish([(s_ctx, vc)])

    @pl.when(n >= CTX // 128)
    def _():
        nl = n - CTX // 128
        pos0 = pl.multiple_of(nl * 128, 128)
        cq = cos_ref[pl.ds(pos0, 128), :]
        sq = sin_ref[pl.ds(pos0, 128), :]
        qr = qn * cq[None] + _swap_halves(qn) * sq[None]
        start = pl.multiple_of(jnp.clip((nl - 1) * 128, 0, SEQ - 3 * 128), 128)
        kb = norm(k_ref[pl.ds(CTX + start, 3 * 128), :].astype(f32), gk_ref)
        kb = kb * cos_ref[pl.ds(start, 3 * 128), :] + _swap_halves(kb) * sin_ref[pl.ds(start, 3 * 128), :]
        vb = v_ref[pl.ds(CTX + start, 3 * 128), :]
        s_band = _dot_nt(qr.reshape(nq, B_DH).astype(bf16), kb.astype(bf16)) * scale
        q_pos = pos0 + lax.broadcasted_iota(jnp.int32, (nq, 3 * 128), 0) % 128
        k_pos = start + lax.broadcasted_iota(jnp.int32, (nq, 3 * 128), 1)
        s_band = jnp.where(jnp.abs(q_pos - k_pos) <= WINDOW, s_band, -jnp.inf)
        s_ctx = _dot_nt(qn.reshape(nq, B_DH).astype(bf16), kc) * scale
        finish([(s_band, vb), (s_ctx, vc)])


def _attention(q5, k4, v4, cosf, sinf, gq, gk, sink_col):
    nblk = LA // 128
    return pl.pallas_call(
        _attn_kernel,
        out_shape=jax.ShapeDtypeStruct(q5.shape, bf16),
        grid=(NB_, B_KV, nblk),
        in_specs=[pl.BlockSpec((None, None, B_G, 128, B_DH), lambda b, kh, n: (b, kh, 0, n, 0)),
                  pl.BlockSpec((None, None, LA, B_DH), lambda b, kh, n: (b, kh, 0, 0)),
                  pl.BlockSpec((None, None, LA, B_DH), lambda b, kh, n: (b, kh, 0, 0)),
                  pl.BlockSpec((SEQ, B_DH), lambda b, kh, n: (0, 0)),
                  pl.BlockSpec((SEQ, B_DH), lambda b, kh, n: (0, 0)),
                  pl.BlockSpec((1, B_DH), lambda b, kh, n: (0, 0)),
                  pl.BlockSpec((1, B_DH), lambda b, kh, n: (0, 0)),
                  pl.BlockSpec((None, B_G * 128, 1), lambda b, kh, n: (kh, 0, 0))],
        out_specs=pl.BlockSpec((None, None, B_G, 128, B_DH), lambda b, kh, n: (b, kh, 0, n, 0)),
        compiler_params=_cp(("arbitrary", "arbitrary", "arbitrary")),
        name="window_attn",
    )(q5, k4, v4, cosf, sinf, gq, gk, sink_col)


def _rope_tables():
    rows = SEQ // GRID_W
    row = jnp.repeat(jnp.arange(rows), GRID_W).astype(f32)
    col = jnp.tile(jnp.arange(GRID_W), rows).astype(f32)
    n_freq = B_DH // 4
    inv = ROPE_BASE ** (-jnp.arange(n_freq, dtype=f32) / n_freq)
    ang = jnp.concatenate([row[:, None] * inv, col[:, None] * inv], axis=-1)
    cos, sin = jnp.cos(ang), jnp.sin(ang)
    return jnp.concatenate([cos, cos], axis=-1), jnp.concatenate([-sin, sin], axis=-1)


def _s5_kernel(u_ref, wy_hbm, wi_hbm, a_ref, y_ref, wy, wi, u32, ucat, zf, zb, sem):
    j = pl.program_id(0)
    T = S5_T
    cw = pltpu.make_async_copy(wy_hbm.at[j], wy, sem.at[0])
    ci = pltpu.make_async_copy(wi_hbm.at[j], wi, sem.at[1])
    cw.start()
    ci.start()
    u32[...] = u_ref[...].astype(f32)
    for s in range(T):
        for b in range(NB_):
            ucat[s, pl.ds(b, S5_CPB, stride=NB_), :] = u32[pl.ds(b * LA + s, S5_CPB, stride=T), :]
    ci.wait()
    uc = jnp.concatenate([ucat[s].astype(bf16) for s in range(T)], axis=1)
    zf[...] = jnp.dot(uc, wi[0], preferred_element_type=f32)
    zb[...] = jnp.dot(uc, wi[1], preferred_element_type=f32)
    a = a_ref[...]
    hs = S5_NS

    def cstep(av, x, z):
        ar, ai = av[:, :hs], av[:, hs:]
        xr, xi = x[:, :hs], x[:, hs:]
        return jnp.concatenate([ar * xr - ai * xi + z[:, :hs], ar * xi + ai * xr + z[:, hs:]], axis=1)

    def step(i, carry):
        x_f, x_b = carry
        rf = pl.multiple_of(i * 2 * NB_, 2 * NB_)
        mb = jnp.where(i < S5_CCTX // 2, S5_CCTX // 2 - 1 - i, S5_CPB // 2 + S5_CCTX // 2 - 1 - i)
        rb = pl.multiple_of(mb * 2 * NB_, 2 * NB_)
        z8 = zf[pl.ds(rf, 2 * NB_), :]
        x1 = cstep(a[0:1], x_f, z8[:NB_])
        zf[pl.ds(rf, 2 * NB_), :] = jnp.concatenate([x_f, x1], axis=0)
        x_f2 = cstep(a[0:1], x1, z8[NB_:])
        w8 = zb[pl.ds(rb, 2 * NB_), :]
        y1 = cstep(a[1:2], x_b, w8[NB_:])
        zb[pl.ds(rb, 2 * NB_), :] = jnp.concatenate([y1, x_b], axis=0)
        x_b2 = cstep(a[1:2], y1, w8[:NB_])
        return x_f2, x_b2

    zero = jnp.zeros((NB_, 2 * hs), f32)
    lax.fori_loop(0, S5_CPB // 2, step, (zero, zero))
    cw.wait()
    xf = zf[...].astype(bf16)
    xb = zb[...].astype(bf16)
    ku = T * 128
    for t in range(T):
        u32[pl.ds(t * S5_NCH, S5_NCH), :] = (
            jnp.dot(uc, wy[t, 0:ku, :], preferred_element_type=f32)
            + jnp.dot(xf, wy[t, ku:ku + 2 * hs, :], preferred_element_type=f32)
            + jnp.dot(xb, wy[t, ku + 2 * hs:ku + 4 * hs, :], preferred_element_type=f32))
    for t in range(T):
        for b in range(NB_):
            y_ref[pl.ds(b * LA + t, S5_CPB, stride=T), :] = u32[pl.ds(t * S5_NCH + b, S5_CPB, stride=NB_), :]


def _s5_core(u_all, mats):
    wy, wi, a = mats
    nt = HALF // 128
    kf = S5_T * 128 + 4 * S5_NS
    return pl.pallas_call(
        _s5_kernel,
        out_shape=jax.ShapeDtypeStruct((ROWS, HALF), f32),
        grid=(nt,),
        in_specs=[pl.BlockSpec((ROWS, 128), lambda j: (0, j)),
                  pl.BlockSpec(memory_space=pl.ANY),
                  pl.BlockSpec(memory_space=pl.ANY),
                  pl.BlockSpec((None, 2, 2 * S5_NS), lambda j: (j, 0, 0))],
        out_specs=pl.BlockSpec((ROWS, 128), lambda j: (0, j)),
        scratch_shapes=[pltpu.VMEM((S5_T, kf, 128), bf16),
                        pltpu.VMEM((2, S5_T * 128, 2 * S5_NS), bf16),
                        pltpu.VMEM((ROWS, 128), f32),
                        pltpu.VMEM((S5_T, S5_NCH, 128), f32),
                        pltpu.VMEM((S5_NCH, 2 * S5_NS), f32),
                        pltpu.VMEM((S5_NCH, 2 * S5_NS), f32),
                        pltpu.SemaphoreType.DMA((2,))],
        compiler_params=_cp(("arbitrary",), 56),
        name="s5_core",
    )(u_all, wy, wi, a)


def _s5_matrices(a_re, a_im, log_dt, b_re, b_im, c_re, c_im):
    T = S5_T
    G, P, C = C_GROUPS, C_STATE, C_GROUP
    nt = G // S5_GT
    lam = lax.complex(a_re.astype(f32), a_im.astype(f32))
    dt = jnp.exp(log_dt.astype(f32))[..., None]
    a_bar = jnp.exp(lam * dt)
    b_bar = ((a_bar - 1) / lam)[..., None] * lax.complex(b_re.astype(f32), b_im.astype(f32))
    tau = jnp.arange(T + 1, dtype=f32)
    apow = jnp.exp((lam * dt)[..., None] * tau)
    cc = lax.complex(c_re.astype(f32), c_im.astype(f32))
    kern = jnp.einsum("gxp,dgpt,dgpc->dgtxc", cc, apow[..., :T], b_bar).real
    t_i = jnp.arange(T)
    lag = t_i[:, None] - t_i[None, :]
    kf = kern[0][:, jnp.clip(lag, 0, T - 1)] * (lag >= 0)[None, :, :, None, None]
    kb = kern[1][:, jnp.clip(-lag, 0, T - 1)] * (lag <= 0)[None, :, :, None, None]
    km = (kf + kb).reshape(nt, S5_GT, T, T, C, C)
    eye = jnp.eye(S5_GT, dtype=f32)
    wy_u = jnp.einsum("jgtsxc,gh->jtsgchx", km, eye).reshape(nt, T, T * 128, 128)
    k_f = cc[:, :, :, None] * apow[0][:, None, :, 1 + t_i]
    k_b = cc[:, :, :, None] * apow[1][:, None, :, T - t_i]

    def wy_state(k):
        k = k.reshape(nt, S5_GT, C, P, T)
        re = jnp.einsum("jgxpt,gh->jtgphx", k.real, eye).reshape(nt, T, S5_NS, 128)
        im = jnp.einsum("jgxpt,gh->jtgphx", -k.imag, eye).reshape(nt, T, S5_NS, 128)
        return jnp.concatenate([re, im], axis=2)

    wy = jnp.concatenate([wy_u, wy_state(k_f), wy_state(k_b)], axis=2).astype(bf16)
    w_f = apow[0][:, :, T - 1 - t_i][..., None] * b_bar[0][:, :, None, :]
    w_b = apow[1][:, :, t_i][..., None] * b_bar[1][:, :, None, :]

    def wi_dir(w):
        w = w.reshape(nt, S5_GT, P, T, C)
        re = jnp.einsum("jgpsc,gh->jsgchp", w.real, eye).reshape(nt, T * 128, S5_NS)
        im = jnp.einsum("jgpsc,gh->jsgchp", w.imag, eye).reshape(nt, T * 128, S5_NS)
        return jnp.concatenate([re, im], axis=2)

    wi = jnp.stack([wi_dir(w_f), wi_dir(w_b)], axis=1).astype(bf16)
    at = apow[..., T].reshape(2, nt, S5_NS)
    a = jnp.concatenate([at.real, at.imag], axis=-1).transpose(1, 0, 2)
    return wy, wi, a


def _s5_glu_kernel(y_ref, u_ref, d_ref, w_ref, b_ref, o_ref, wbf_ref):
    @pl.when(pl.program_id(0) == 0)
    def _():
        wbf_ref[...] = w_ref[...].astype(bf16)

    y = y_ref[...] + d_ref[...] * u_ref[...].astype(f32)
    y = jax.nn.gelu(y, approximate=True)
    z = jnp.dot(y.astype(bf16), wbf_ref[...], preferred_element_type=f32) + b_ref[...]
    o_ref[...] = (y * jax.nn.sigmoid(z)).astype(o_ref.dtype)


def _s5_glu(y_ssm, proj, d2, glu_w, glu_b2):
    nblk = NB_ * LBPB
    return pl.pallas_call(
        _s5_glu_kernel,
        out_shape=jax.ShapeDtypeStruct((nblk * RB, HALF), bf16),
        grid=(nblk,),
        in_specs=[pl.BlockSpec((RB, HALF), lambda i: (_lat_blk(i), 0)),
                  pl.BlockSpec((RB, HALF), lambda i: (_lat_blk(i), 0)),
                  pl.BlockSpec((1, HALF), lambda i: (0, 0)),
                  pl.BlockSpec((HALF, HALF), lambda i: (0, 0)),
                  pl.BlockSpec((1, HALF), lambda i: (0, 0))],
        out_specs=pl.BlockSpec((RB, HALF), lambda i: (i, 0)),
        scratch_shapes=[pltpu.VMEM((HALF, HALF), bf16)],
        compiler_params=_cp(("arbitrary",)),
        name="s5_glu",
    )(y_ssm, proj, d2, glu_w, glu_b2)


def _hyena_filter(L, w1, b1, w2, b2, w3, freq):
    t = jnp.linspace(0.0, 1.0, L, dtype=f32)[:, None]
    w = 2 * math.pi * jnp.arange(L, dtype=f32)[:, None] / L
    bands = jnp.linspace(1e-4, HY_BANDS - 1, HY_BANDS, dtype=f32)
    feats = jnp.concatenate([t, jnp.cos(bands * w), -jnp.sin(bands * w)], axis=-1)
    h = jnp.sin(freq * (feats @ w1 + b1))
    h = jnp.sin(freq * (h @ w2 + b2))
    h = h @ w3
    fast = abs(math.log(HY_DECAY_TARGET) / HY_FAST_PCT)
    slow = abs(math.log(HY_DECAY_TARGET) / HY_SLOW_PCT)
    deltas = jnp.tile(jnp.linspace(slow, fast, HALF, dtype=f32), 2)
    h = h * jnp.exp(-t * deltas)
    h_fwd, h_bwd = jnp.split(h, 2, axis=-1)
    filt = jnp.concatenate([h_fwd, jnp.zeros((1, HALF), f32), h_bwd[:0:-1]], axis=0)
    return filt / jnp.sum(jnp.abs(filt), axis=0, keepdims=True)


def _hyena_dft_tables():
    r = HY_R
    n = r * r
    idx = np.arange(r)
    w64 = np.exp(-2j * np.pi * np.outer(idx, idx) / r)
    tw = np.exp(-2j * np.pi * np.outer(idx, idx) / n)
    f1 = w64[None, :, :r // 2] * tw[:, :, None]
    f1 = np.concatenate([f1.real, f1.imag], axis=1)
    f2 = np.block([[w64.real, -w64.imag], [w64.imag, w64.real]])
    g = np.conj(w64)[None, :, :] * np.conj(tw.T)[:, :, None]
    g2 = np.concatenate([np.concatenate([g.real, -g.imag], axis=2),
                         np.concatenate([g.imag, g.real], axis=2)], axis=1)
    c = np.conj(w64)[:r // 2, :]
    g1 = np.concatenate([c.real, -c.imag], axis=1) / n
    return (jnp.asarray(f1, bf16), jnp.asarray(f2, bf16), jnp.asarray(g2, bf16), jnp.asarray(g1, bf16))


def _hyena_kernel(x0_ref, x1_ref, v_ref, cw0_ref, cw1_ref, cw2_ref, cb0_ref, cb1_ref, cb2_ref, hb_ref,
                  f1_ref, f2_ref, g2_ref, g1_ref, hf_ref, o_ref, u_scr, a_scr, b_scr, y_scr):
    L = SEQ
    r = HY_R
    nt = HY_TC // 128
    row = lax.broadcasted_iota(jnp.int32, (L, HY_TC), 0)

    def sconv(z_ref, cw_ref, cb_ref):
        z = z_ref[...].astype(f32)
        zm = jnp.where(row == 0, 0.0, pltpu.roll(z, 1, axis=0))
        zp = jnp.where(row == L - 1, 0.0, pltpu.roll(z, L - 1, axis=0))
        return zm * cw_ref[0:1, :] + z * cw_ref[1:2, :] + zp * cw_ref[2:3, :] + cb_ref[...]

    u = sconv(x1_ref, cw1_ref, cb1_ref) * sconv(v_ref, cw2_ref, cb2_ref)
    for t in range(nt):
        u_scr[t] = u[:, t * 128:(t + 1) * 128]

    def stage1(n2, c):
        for t in range(nt):
            blk = u_scr[t, pl.ds(n2, r // 2, stride=r), :]
            a = jnp.dot(f1_ref[n2], blk.astype(bf16), preferred_element_type=f32)
            a_scr[t, pl.ds(pl.multiple_of(n2 * 2 * r, 2 * r), 2 * r), :] = a
        return c

    lax.fori_loop(0, r, stage1, 0)

    def stage2(k1, c):
        r0 = pl.multiple_of(k1 * 2 * r, 2 * r)
        for t in range(nt):
            blk = jnp.concatenate([a_scr[t, pl.ds(k1, r, stride=2 * r), :],
                                   a_scr[t, pl.ds(r + k1, r, stride=2 * r), :]], axis=0)
            x = jnp.dot(f2_ref[...], blk.astype(bf16), preferred_element_type=f32)
            h = hf_ref[pl.ds(r0, 2 * r), t * 128:(t + 1) * 128]
            xr, xi, hr, hi = x[:r], x[r:], h[:r], h[r:]
            y = jnp.concatenate([xr * hr - xi * hi, xr * hi + xi * hr], axis=0)
            b = jnp.dot(g2_ref[k1], y.astype(bf16), preferred_element_type=f32)
            b_scr[t, pl.ds(r0, 2 * r), :] = b
        return c

    lax.fori_loop(0, r, stage2, 0)

    def stage3(n2, c):
        for t in range(nt):
            blk = jnp.concatenate([b_scr[t, pl.ds(n2, r, stride=2 * r), :],
                                   b_scr[t, pl.ds(r + n2, r, stride=2 * r), :]], axis=0)
            y = jnp.dot(g1_ref[...], blk.astype(bf16), preferred_element_type=f32)
            y_scr[t, pl.ds(n2, r // 2, stride=r), :] = y
        return c

    lax.fori_loop(0, r, stage3, 0)
    x0 = sconv(x0_ref, cw0_ref, cb0_ref)
    for t in range(nt):
        sl = slice(t * 128, (t + 1) * 128)
        o_ref[:, sl] = (x0[:, sl] * (y_scr[t] + u_scr[t] * hb_ref[:, sl])).astype(o_ref.dtype)


def _hyena(z_lat, conv_w, conv_b2, hy_bias2, hf):
    f1, f2, g2, g1 = _hyena_dft_tables()
    nct = HALF // HY_TC
    r = HY_R
    nt = HY_TC // 128

    def zspec(k):
        return pl.BlockSpec((SEQ, HY_TC), lambda ct, b: (b, k * nct + ct))

    def wspec(k):
        return pl.BlockSpec((3, HY_TC), lambda ct, b: (0, k * nct + ct))

    def bspec(k):
        return pl.BlockSpec((1, HY_TC), lambda ct, b: (0, k * nct + ct))

    def full(a):
        return pl.BlockSpec(a.shape, lambda ct, b: (0,) * a.ndim)

    return pl.pallas_call(
        _hyena_kernel,
        out_shape=jax.ShapeDtypeStruct((NB_ * SEQ, HALF), bf16),
        grid=(nct, NB_),
        in_specs=[zspec(0), zspec(1), zspec(2), wspec(0), wspec(1), wspec(2), bspec(0), bspec(1), bspec(2),
                  pl.BlockSpec((1, HY_TC), lambda ct, b: (0, ct)),
                  full(f1), full(f2), full(g2), full(g1),
                  pl.BlockSpec((2 * r * r, HY_TC), lambda ct, b: (0, ct))],
        out_specs=pl.BlockSpec((SEQ, HY_TC), lambda ct, b: (b, ct)),
        scratch_shapes=[pltpu.VMEM((nt, SEQ, 128), f32), pltpu.VMEM((nt, 2 * r * r, 128), f32),
                        pltpu.VMEM((nt, 2 * r * r, 128), f32), pltpu.VMEM((nt, SEQ, 128), f32)],
        compiler_params=_cp(("arbitrary", "arbitrary"), 56),
        name="hyena",
    )(z_lat, z_lat, z_lat, conv_w, conv_w, conv_w, conv_b2, conv_b2, conv_b2, hy_bias2, f1, f2, g2, g1, hf)


def _hyena_filter_spectrum(filt):
    r = HY_R
    hk = jnp.fft.fft(filt.astype(jnp.complex64), axis=0)
    hk = hk.reshape(r, r, -1).transpose(1, 0, 2)
    return jnp.concatenate([hk.real, hk.imag], axis=1).reshape(2 * r * r, -1).astype(f32)


def _swiglu_perm():
    j = np.arange(256)
    src = np.where(j < 128, 2 * j, 2 * (j - 128) + 1)
    p = np.zeros((256, 256), np.float32)
    p[src, j] = 1.0
    return jnp.asarray(p, dtype=bf16)


def kernel(x, c, ctx, c_ctx, ada_w, ada_b, norm_mix_g, norm_ffn_g, ev_w_in, ev_gate_b, ev_h_norm_g,
           ev_q_norm_g, ev_k_norm_g, ev_sink, ev_w_out, od_w_in, od_a_re, od_a_im, od_log_dt, od_b_re,
           od_b_im, od_c_re, od_c_im, od_d, od_glu_w, od_glu_b, od_conv_w, od_conv_b, od_filt_w1,
           od_filt_b1, od_filt_w2, od_filt_b2, od_filt_w3, od_filt_freq, od_hy_bias, od_w_out,
           moe_router_w, moe_router_b, moe_w1, moe_b1, moe_w2, moe_b2):
    depth = ada_w.shape[0]
    xu = jnp.concatenate([ctx, x], axis=1).reshape(ROWS, D)
    cond8 = jnp.concatenate([c_ctx[None], c, jnp.zeros((3, D), f32)], axis=0)
    ada_b3 = ada_b.reshape(depth, 1, 6 * D)
    g_mix3 = norm_mix_g.reshape(depth, 1, D)
    g_ffn3 = norm_ffn_g.reshape(depth, 1, D)
    rb3 = moe_router_b.reshape(depth, 1, N_EXPERTS)
    b1_4 = moe_b1.reshape(depth, N_EXPERTS, 1, 2 * D_FF)
    b2_4 = moe_b2.reshape(depth, N_EXPERTS, 1, D)
    perm = _swiglu_perm()
    nall = ROWS // RB
    nlat = NB_ * LBPB

    mod3 = _ada(cond8, ada_w, ada_b3, 0).reshape(8, 1, 6 * D)
    h = _norm_mod(xu, g_mix3, mod3, 0, 0, _ident, _mod_row, nall, bf16)
    w_in = ev_w_in[0]
    n_a = 2 * A_HEADS * A_DQK + 2 * HALF
    n_g = 4 * A_HEADS
    w_main = jnp.concatenate([w_in[:, :n_a], w_in[:, n_a + n_g:]], axis=1)
    w_gate = jnp.pad(w_in[:, n_a:n_a + n_g], ((0, 0), (0, 128 - n_g)))
    proj = _matmul([h], w_main, (), w_main.shape[1], 1536, nall, _ident, bf16, name="even_in")
    gates = _matmul([h], w_gate, (), 128, 128, nall, _ident, f32, name="even_gates")[:, :n_g]
    gates = (gates + ev_gate_b[0]).reshape(NB_, LA, n_g)
    gcol, grow = _mlstm_gate_tables(gates)
    y_a = _mlstm(proj, gcol, grow, ev_h_norm_g[0].reshape(A_HEADS, 1, A_DV))
    qa = proj[:, n_a:n_a + HALF].reshape(NB_, LA, B_KV, B_G, B_DH).transpose(0, 2, 3, 1, 4)
    ka = proj[:, n_a + HALF:n_a + HALF + B_KV * B_DH].reshape(NB_, LA, B_KV, B_DH).transpose(0, 2, 1, 3)
    va = proj[:, n_a + HALF + B_KV * B_DH:].reshape(NB_, LA, B_KV, B_DH).transpose(0, 2, 1, 3)
    cosf, sinf = _rope_tables()
    sink_col = jnp.repeat(ev_sink[0].astype(f32).reshape(B_KV, B_G), 128, axis=1)[..., None]
    o5 = _attention(qa, ka, va, cosf, sinf, ev_q_norm_g[0].reshape(1, B_DH),
                    ev_k_norm_g[0].reshape(1, B_DH), sink_col)
    y_b = o5.transpose(0, 3, 1, 2, 4).reshape(ROWS, HALF)
    xu = _matmul([y_a, y_b], ev_w_out, (0,), D, 1024, nall, _ident, f32,
                 res=xu, res_blk=_ident, mrow=_mod_row, mod3=mod3, gate_chunk=2, name="even_out")
    xu = _moe_layer(xu, g_ffn3, mod3, moe_router_w, rb3, moe_w1, b1_4, moe_w2, b2_4, perm,
                    0, _ident, _mod_row, nall)

    mod3 = _ada(cond8, ada_w, ada_b3, 1).reshape(8, 1, 6 * D)
    h = _norm_mod(xu, g_mix3, mod3, 1, 0, _ident, _mod_row, nall, bf16)
    proj = _matmul([h], od_w_in, (0,), HALF, 1024, nall, _ident, bf16, name="odd_in_u")
    z_lat = _matmul([h], od_w_in, (0,), 3 * HALF, 1024, nlat, _lat_blk, bf16, w_col0=HALF // 1024,
                    name="odd_in_z")
    mats = _s5_matrices(od_a_re[0], od_a_im[0], od_log_dt[0], od_b_re[0], od_b_im[0], od_c_re[0], od_c_im[0])
    y_ssm = _s5_core(proj, mats)
    s_l = _s5_glu(y_ssm, proj, od_d[0].reshape(1, HALF), od_glu_w[0], od_glu_b[0].reshape(1, HALF))
    filt = _hyena_filter(SEQ, od_filt_w1[0], od_filt_b1[0], od_filt_w2[0], od_filt_b2[0], od_filt_w3[0],
                         od_filt_freq[0])
    hy_l = _hyena(z_lat, od_conv_w[0], od_conv_b[0].reshape(1, 3 * HALF), od_hy_bias[0].reshape(1, HALF),
                  _hyena_filter_spectrum(filt))
    x_lat = _matmul([s_l, hy_l], od_w_out, (0,), D, 1024, nlat, _ident, f32,
                    res=xu, res_blk=_lat_blk, mrow=_lat_mod_row, mod3=mod3, gate_chunk=2, name="odd_out")
    out = _moe_layer(x_lat, g_ffn3, mod3, moe_router_w, rb3, moe_w1, b1_4, moe_w2, b2_4, perm,
                     1, _ident, _lat_mod_row, nlat)
    return out.reshape(NB_, SEQ, D)
```

```python
import functools
import math

import jax
import jax.numpy as jnp
import numpy as np
from jax import lax
from jax.experimental import pallas as pl
from jax.experimental.pallas import tpu as pltpu

f32 = jnp.float32
bf16 = jnp.bfloat16

D = 2048
NB_ = 4
SEQ = 2048
CTX = 256
LA = SEQ + CTX
ROWS = NB_ * LA
RB = 256
BPB = LA // RB
LBPB = SEQ // RB
EPS = 1e-6
GRID_W = 64

HALF = D // 2
A_HEADS = 4
A_DV = HALF // A_HEADS
A_DQK = A_DV // 2
A_T = 256
B_HEADS = 16
B_KV = 4
B_G = B_HEADS // B_KV
B_DH = HALF // B_HEADS
WINDOW = 128
ROPE_BASE = 10000.0
C_GROUP = 16
C_GROUPS = HALF // C_GROUP
C_STATE = 64
S5_T = 8
S5_GT = 128 // C_GROUP
S5_NS = S5_GT * C_STATE
S5_NCH = ROWS // S5_T
S5_CPB = LA // S5_T
S5_CCTX = CTX // S5_T
HY_TC = 256
HY_KT = 512
HY_BANDS = 16
HY_DECAY_TARGET = 1e-2
HY_FAST_PCT = 0.3
HY_SLOW_PCT = 1.5
N_EXPERTS = 32
TOP_K = 4
D_FF = D
SWIGLU_LIMIT = 7.0
SWIGLU_ALPHA = 1.702
MOE_TM = 256

EVEN_SPLITS = (A_HEADS * A_DQK, A_HEADS * A_DQK, HALF, HALF, 4 * A_HEADS,
               B_HEADS * B_DH, B_KV * B_DH, B_KV * B_DH)

VMEM_MB = 1024 * 1024


def _cp(sem, vmem_mb=40):
    return pltpu.CompilerParams(dimension_semantics=sem, vmem_limit_bytes=vmem_mb * VMEM_MB)


def _lat_blk(i):
    return (i // LBPB) * BPB + 1 + i % LBPB


def _mod_row(u):
    return jnp.where(u % BPB == 0, 0, 1 + u // BPB)


def _lat_mod_row(i):
    return 1 + i // LBPB


def _ident(i):
    return i


def _ada_kernel(c_ref, w_ref, b_ref, o_ref):
    c = c_ref[...]
    s = c * jax.nn.sigmoid(c)
    o_ref[...] = jnp.dot(s.astype(bf16), w_ref[...].astype(bf16),
                         preferred_element_type=f32) + b_ref[...]


def _ada(cond8, ada_w, ada_b3, layer):
    tn = 1024
    n = 6 * D
    return pl.pallas_call(
        _ada_kernel,
        out_shape=jax.ShapeDtypeStruct((8, n), f32),
        grid=(n // tn,),
        in_specs=[pl.BlockSpec((8, D), lambda j: (0, 0)),
                  pl.BlockSpec((None, D, tn), lambda j: (layer, 0, j)),
                  pl.BlockSpec((None, 1, tn), lambda j: (layer, 0, j))],
        out_specs=pl.BlockSpec((8, tn), lambda j: (0, j)),
        compiler_params=_cp(("arbitrary",)),
        name="ada_params",
    )(cond8, ada_w, ada_b3)


def _norm_mod_kernel(x_ref, g_ref, sh_ref, sc_ref, o_ref):
    x = x_ref[...]
    y = x * lax.rsqrt(jnp.mean(x * x, axis=-1, keepdims=True) + EPS) * g_ref[...]
    o_ref[...] = (y * (1.0 + sc_ref[...]) + sh_ref[...]).astype(o_ref.dtype)


def _norm_mod(x, g3, mod3, layer, chunk, blk, mrow, nblk, out_dtype):
    return pl.pallas_call(
        _norm_mod_kernel,
        out_shape=jax.ShapeDtypeStruct((nblk * RB, D), out_dtype),
        grid=(nblk,),
        in_specs=[pl.BlockSpec((RB, D), lambda i: (blk(i), 0)),
                  pl.BlockSpec((None, 1, D), lambda i: (layer, 0, 0)),
                  pl.BlockSpec((None, 1, D), lambda i: (mrow(i), 0, chunk)),
                  pl.BlockSpec((None, 1, D), lambda i: (mrow(i), 0, chunk + 1))],
        out_specs=pl.BlockSpec((RB, D), lambda i: (i, 0)),
        compiler_params=_cp(("arbitrary",)),
        name="norm_mod",
    )(x, g3, mod3, mod3)


def _split_bf16(x):
    hi = x.astype(bf16)
    lo = (x - hi.astype(f32)).astype(bf16)
    return hi, lo


def _norm_router_kernel(x_ref, g_ref, sh_ref, sc_ref, rw_ref, rb_ref,
                        h_ref, idx_ref, gate_ref, rank_ref, cnt_ref, carry_ref):
    i = pl.program_id(0)

    @pl.when(i == 0)
    def _():
        carry_ref[...] = jnp.zeros_like(carry_ref)

    x = x_ref[...]
    y = x * lax.rsqrt(jnp.mean(x * x, axis=-1, keepdims=True) + EPS) * g_ref[...]
    h = y * (1.0 + sc_ref[...]) + sh_ref[...]
    h_ref[...] = h
    h_hi, h_lo = _split_bf16(h)
    w_hi, w_lo = _split_bf16(rw_ref[...])
    logits = (jnp.dot(h_hi, w_hi, preferred_element_type=f32)
              + jnp.dot(h_lo, w_hi, preferred_element_type=f32)
              + jnp.dot(h_hi, w_lo, preferred_element_type=f32)) + rb_ref[...]
    ne = logits.shape[-1]
    lane = lax.broadcasted_iota(jnp.int32, logits.shape, 1)
    l = logits
    vals, idxs = [], []
    for _ in range(TOP_K):
        m = jnp.max(l, axis=-1, keepdims=True)
        ix = jnp.min(jnp.where(l == m, lane, ne), axis=-1, keepdims=True)
        vals.append(m)
        idxs.append(ix)
        l = jnp.where(lane == ix, -jnp.inf, l)
    es = [jnp.exp(v - vals[0]) for v in vals]
    tot = es[0] + es[1] + es[2] + es[3]
    onehot = jnp.zeros(logits.shape, f32)
    for ix in idxs:
        onehot = onehot + jnp.where(lane == ix, 1.0, 0.0)
    r_i = lax.broadcasted_iota(jnp.int32, (RB, RB), 0)
    c_i = lax.broadcasted_iota(jnp.int32, (RB, RB), 1)
    tri = jnp.where(c_i < r_i, 1.0, 0.0).astype(bf16)
    prefix = jnp.dot(tri, onehot.astype(bf16), preferred_element_type=f32) + carry_ref[...]
    for k in range(TOP_K):
        idx_ref[:, k:k + 1] = idxs[k]
        gate_ref[:, k:k + 1] = es[k] / tot
        rk = jnp.sum(jnp.where(lane == idxs[k], prefix, 0.0), axis=-1, keepdims=True)
        rank_ref[:, k:k + 1] = rk.astype(jnp.int32)
    carry_ref[...] = carry_ref[...] + jnp.sum(onehot, axis=0, keepdims=True)
    cnt_ref[...] = carry_ref[...]


def _norm_router(x, g3, mod3, rw, rb3, layer, chunk, blk, mrow, nblk):
    t = nblk * RB
    return pl.pallas_call(
        _norm_router_kernel,
        out_shape=(jax.ShapeDtypeStruct((t, D), f32),
                   jax.ShapeDtypeStruct((t, TOP_K), jnp.int32),
                   jax.ShapeDtypeStruct((t, TOP_K), f32),
                   jax.ShapeDtypeStruct((t, TOP_K), jnp.int32),
                   jax.ShapeDtypeStruct((1, N_EXPERTS), f32)),
        grid=(nblk,),
        in_specs=[pl.BlockSpec((RB, D), lambda i: (blk(i), 0)),
                  pl.BlockSpec((None, 1, D), lambda i: (layer, 0, 0)),
                  pl.BlockSpec((None, 1, D), lambda i: (mrow(i), 0, chunk)),
                  pl.BlockSpec((None, 1, D), lambda i: (mrow(i), 0, chunk + 1)),
                  pl.BlockSpec((None, D, N_EXPERTS), lambda i: (layer, 0, 0)),
                  pl.BlockSpec((None, 1, N_EXPERTS), lambda i: (layer, 0, 0))],
        out_specs=(pl.BlockSpec((RB, D), lambda i: (i, 0)),
                   pl.BlockSpec((RB, TOP_K), lambda i: (i, 0)),
                   pl.BlockSpec((RB, TOP_K), lambda i: (i, 0)),
                   pl.BlockSpec((RB, TOP_K), lambda i: (i, 0)),
                   pl.BlockSpec((1, N_EXPERTS), lambda i: (0, 0))),
        scratch_shapes=[pltpu.VMEM((1, N_EXPERTS), f32)],
        compiler_params=_cp(("arbitrary",)),
        name="norm_router",
    )(x, g3, mod3, mod3, rw, rb3)


def _mm_kernel(*refs, n_a, k_split, epilogue):
    a_refs = refs[:n_a]
    w_ref = refs[n_a]
    pos = n_a + 1
    if epilogue:
        res_ref, gate_ref = refs[pos], refs[pos + 1]
        pos += 2
    o_ref, wbf_ref = refs[pos], refs[pos + 1]

    @pl.when(pl.program_id(1) == 0)
    def _():
        wbf_ref[...] = w_ref[...].astype(bf16)

    acc = None
    for t, a_ref in enumerate(a_refs):
        part = jnp.dot(a_ref[...], wbf_ref[t * k_split:(t + 1) * k_split, :],
                       preferred_element_type=f32)
        acc = part if acc is None else acc + part
    if epilogue:
        acc = res_ref[...] + gate_ref[...] * acc
    o_ref[...] = acc.astype(o_ref.dtype)


def _matmul(a_list, w, w_idx, n, tn, nblk, a_blk, out_dtype, *, a_col=None, w_col0=0,
            res=None, res_blk=None, mrow=None, mod3=None, gate_chunk=None, name="matmul"):
    n_a = len(a_list)
    k = w.shape[-2]
    k_split = k // n_a
    lead = len(w_idx)
    a_col = a_col or [0] * n_a
    in_specs = [pl.BlockSpec((RB, k_split), functools.partial(lambda j, i, c: (a_blk(i), c), c=c))
                for c in a_col]
    in_specs.append(pl.BlockSpec((None,) * lead + (k, tn), lambda j, i: tuple(w_idx) + (0, j + w_col0)))
    args = list(a_list) + [w]
    epilogue = res is not None
    if epilogue:
        per = D // tn
        in_specs.append(pl.BlockSpec((RB, tn), lambda j, i: (res_blk(i), j)))
        in_specs.append(pl.BlockSpec((None, 1, tn),
                                     lambda j, i: (mrow(i), 0, gate_chunk * per + j)))
        args += [res, mod3]
    return pl.pallas_call(
        functools.partial(_mm_kernel, n_a=n_a, k_split=k_split, epilogue=epilogue),
        out_shape=jax.ShapeDtypeStruct((nblk * RB, n), out_dtype),
        grid=(n // tn, nblk),
        in_specs=in_specs,
        out_specs=pl.BlockSpec((RB, tn), lambda j, i: (i, j)),
        scratch_shapes=[pltpu.VMEM((k, tn), bf16)],
        compiler_params=_cp(("arbitrary", "arbitrary"), 48),
        name=name,
    )(*args)


def _gather_rows_kernel(tok_ref, h_hbm, o_ref, buf, sem, *, rows, nsteps):
    i = pl.program_id(0)

    def issue(blk, slot):
        base = blk * rows

        def body(r, c):
            t = tok_ref[base + r]
            pltpu.make_async_copy(h_hbm.at[pl.ds(t, 1), :], buf.at[slot, pl.ds(r, 1), :],
                                  sem.at[slot]).start()
            return c

        lax.fori_loop(0, rows, body, 0, unroll=8)

    @pl.when(i == 0)
    def _():
        issue(0, 0)

    @pl.when(i + 1 < nsteps)
    def _():
        issue(i + 1, (i + 1) % 2)

    slot = i % 2
    pltpu.make_async_copy(h_hbm.at[pl.ds(0, rows), :], buf.at[slot], sem.at[slot]).wait()
    o_ref[...] = buf[slot].astype(o_ref.dtype)


def _gather_rows(slot_tok, h, nr):
    rows = 512
    nsteps = nr // rows
    return pl.pallas_call(
        functools.partial(_gather_rows_kernel, rows=rows, nsteps=nsteps),
        out_shape=jax.ShapeDtypeStruct((nr, D), bf16),
        grid_spec=pltpu.PrefetchScalarGridSpec(
            num_scalar_prefetch=1,
            grid=(nsteps,),
            in_specs=[pl.BlockSpec(memory_space=pl.ANY)],
            out_specs=pl.BlockSpec((rows, D), lambda i, tok: (i, 0)),
            scratch_shapes=[pltpu.VMEM((2, rows, D), f32), pltpu.SemaphoreType.DMA((2,))]),
        compiler_params=_cp(("arbitrary",)),
        name="moe_gather",
    )(slot_tok, h)


def _moe_up_kernel(start_ref, nblk_ref, xs_hbm, w_hbm, b_ref, p_ref, act_hbm,
                   wbf_ref, wf32, xbuf, obuf, wsem, xsem, osem, *, tn, layer):
    j = pl.program_id(0)
    e = pl.program_id(1)
    nb = nblk_ref[e]
    row0 = start_ref[e]
    g = j * N_EXPERTS + e

    def w_copy(gg, slot):
        c0 = pl.multiple_of((gg // N_EXPERTS) * tn, tn)
        return pltpu.make_async_copy(w_hbm.at[layer, gg % N_EXPERTS, :, pl.ds(c0, tn)], wf32.at[slot],
                                     wsem.at[slot])

    def x_copy(blk, slot):
        r = pl.multiple_of(row0 + blk * MOE_TM, MOE_TM)
        return pltpu.make_async_copy(xs_hbm.at[pl.ds(r, MOE_TM), :], xbuf.at[slot], xsem.at[slot])

    def o_copy(blk, slot):
        r = pl.multiple_of(row0 + blk * MOE_TM, MOE_TM)
        return pltpu.make_async_copy(obuf.at[slot], act_hbm.at[j, pl.ds(r, MOE_TM), :], osem.at[slot])

    @pl.when(g == 0)
    def _():
        w_copy(0, 0).start(priority=1)

    @pl.when(nb > 0)
    def _():
        x_copy(0, 0).start()

    @pl.when(g + 1 < pl.num_programs(0) * N_EXPERTS)
    def _():
        w_copy(g + 1, (g + 1) % 2).start(priority=1)

    w_copy(g, g % 2).wait()

    @pl.when(nb > 0)
    def _():
        wbf_ref[...] = wf32[g % 2].astype(bf16)

    def body(blk, c):
        slot = blk % 2
        x_copy(blk, slot).wait()

        @pl.when(blk + 1 < nb)
        def _():
            x_copy(blk + 1, 1 - slot).start()

        h = jnp.dot(xbuf[slot], wbf_ref[...], preferred_element_type=f32) + b_ref[...]

        @pl.when(blk >= 2)
        def _():
            o_copy(blk - 2, slot).wait()

        for cc in range(tn // 256):
            d = jnp.dot(h[:, cc * 256:(cc + 1) * 256].astype(bf16), p_ref[...],
                        preferred_element_type=f32)
            gate = jnp.minimum(d[:, :128], SWIGLU_LIMIT)
            lin = jnp.clip(d[:, 128:], -SWIGLU_LIMIT, SWIGLU_LIMIT)
            act = gate * jax.nn.sigmoid(SWIGLU_ALPHA * gate) * (lin + 1.0)
            obuf[slot, :, cc * 128:(cc + 1) * 128] = act.astype(obuf.dtype)
        o_copy(blk, slot).start()
        return c

    lax.fori_loop(0, nb, body, 0)

    @pl.when(nb >= 2)
    def _():
        o_copy(nb - 2, nb % 2).wait()

    @pl.when(nb >= 1)
    def _():
        o_copy(nb - 1, (nb - 1) % 2).wait()

    @pl.when(e == N_EXPERTS - 1)
    def _():
        obuf[0] = jnp.zeros(obuf.shape[1:], obuf.dtype)

        def zero_tail(blk, c):
            r = pl.multiple_of(blk * MOE_TM, MOE_TM)
            cp = pltpu.make_async_copy(obuf.at[0], act_hbm.at[j, pl.ds(r, MOE_TM), :], osem.at[0])
            cp.start()
            cp.wait()
            return c

        lax.fori_loop(row0 // MOE_TM + nb, act_hbm.shape[1] // MOE_TM, zero_tail, 0)


def _moe_up(start, nblk_e, xs, w1, b1, perm, layer, nr):
    tn = 2048
    nj = 2 * D_FF // tn
    return pl.pallas_call(
        functools.partial(_moe_up_kernel, tn=tn, layer=layer),
        out_shape=jax.ShapeDtypeStruct((nj, nr, tn // 2), bf16),
        grid_spec=pltpu.PrefetchScalarGridSpec(
            num_scalar_prefetch=2,
            grid=(nj, N_EXPERTS),
            in_specs=[pl.BlockSpec(memory_space=pl.ANY),
                      pl.BlockSpec(memory_space=pl.ANY),
                      pl.BlockSpec((None, None, 1, tn), lambda j, e, st, nb: (layer, e, 0, j)),
                      pl.BlockSpec((256, 256), lambda j, e, st, nb: (0, 0))],
            out_specs=pl.BlockSpec(memory_space=pl.ANY),
            scratch_shapes=[pltpu.VMEM((D, tn), bf16),
                            pltpu.VMEM((2, D, tn), f32),
                            pltpu.VMEM((2, MOE_TM, D), bf16),
                            pltpu.VMEM((2, MOE_TM, tn // 2), bf16),
                            pltpu.SemaphoreType.DMA((2,)), pltpu.SemaphoreType.DMA((2,)),
                            pltpu.SemaphoreType.DMA((2,))]),
        compiler_params=_cp(("arbitrary", "arbitrary"), 56),
        name="moe_up",
    )(start, nblk_e, xs, w1, b1, perm)


def _moe_down_kernel(start_ref, nblk_ref, act_hbm, w_hbm, b_ref, y_hbm, wbf_ref, wf32, abuf, obuf,
                     wsem, asem, osem, *, layer):
    e = pl.program_id(0)
    nb = nblk_ref[e]
    row0 = start_ref[e]
    nj = abuf.shape[1]
    kh = D_FF // nj

    def w_copy(ee, slot):
        return pltpu.make_async_copy(w_hbm.at[layer, ee], wf32.at[slot], wsem.at[slot])

    def a_copy(blk, slot, jj):
        r = pl.multiple_of(row0 + blk * MOE_TM, MOE_TM)
        return pltpu.make_async_copy(act_hbm.at[jj, pl.ds(r, MOE_TM), :], abuf.at[slot, jj], asem.at[slot])

    def o_copy(blk, slot):
        r = pl.multiple_of(row0 + blk * MOE_TM, MOE_TM)
        return pltpu.make_async_copy(obuf.at[slot], y_hbm.at[pl.ds(r, MOE_TM), :], osem.at[slot])

    @pl.when(e == 0)
    def _():
        w_copy(0, 0).start(priority=1)

    @pl.when(nb > 0)
    def _():
        for jj in range(nj):
            a_copy(0, 0, jj).start()

    @pl.when(e + 1 < N_EXPERTS)
    def _():
        w_copy(e + 1, (e + 1) % 2).start(priority=1)

    w_copy(e, e % 2).wait()

    @pl.when(nb > 0)
    def _():
        wbf_ref[...] = wf32[e % 2].astype(bf16)

    def body(blk, c):
        slot = blk % 2
        for jj in range(nj):
            a_copy(blk, slot, jj).wait()

        @pl.when(blk + 1 < nb)
        def _():
            for jj in range(nj):
                a_copy(blk + 1, 1 - slot, jj).start()

        y = b_ref[...] + jnp.dot(abuf[slot, 0], wbf_ref[0:kh, :], preferred_element_type=f32)
        for jj in range(1, nj):
            y = y + jnp.dot(abuf[slot, jj], wbf_ref[jj * kh:(jj + 1) * kh, :], preferred_element_type=f32)

        @pl.when(blk >= 2)
        def _():
            o_copy(blk - 2, slot).wait()

        obuf[slot] = y
        o_copy(blk, slot).start()
        return c

    lax.fori_loop(0, nb, body, 0)

    @pl.when(nb >= 2)
    def _():
        o_copy(nb - 2, nb % 2).wait()

    @pl.when(nb >= 1)
    def _():
        o_copy(nb - 1, (nb - 1) % 2).wait()

    @pl.when(e == N_EXPERTS - 1)
    def _():
        obuf[0] = jnp.zeros(obuf.shape[1:], obuf.dtype)

        def zero_tail(blk, c):
            r = pl.multiple_of(blk * MOE_TM, MOE_TM)
            cp = pltpu.make_async_copy(obuf.at[0], y_hbm.at[pl.ds(r, MOE_TM), :], osem.at[0])
            cp.start()
            cp.wait()
            return c

        lax.fori_loop(row0 // MOE_TM + nb, y_hbm.shape[0] // MOE_TM, zero_tail, 0)


def _moe_down(start, nblk_e, act, w2, b2, layer, nr):
    nj = act.shape[0]
    return pl.pallas_call(
        functools.partial(_moe_down_kernel, layer=layer),
        out_shape=jax.ShapeDtypeStruct((nr, D), f32),
        grid_spec=pltpu.PrefetchScalarGridSpec(
            num_scalar_prefetch=2,
            grid=(N_EXPERTS,),
            in_specs=[pl.BlockSpec(memory_space=pl.ANY),
                      pl.BlockSpec(memory_space=pl.ANY),
                      pl.BlockSpec((None, None, 1, D), lambda e, st, nb: (layer, e, 0, 0))],
            out_specs=pl.BlockSpec(memory_space=pl.ANY),
            scratch_shapes=[pltpu.VMEM((D_FF, D), bf16),
                            pltpu.VMEM((2, D_FF, D), f32),
                            pltpu.VMEM((2, nj, MOE_TM, D_FF // nj), bf16),
                            pltpu.VMEM((2, MOE_TM, D), f32),
                            pltpu.SemaphoreType.DMA((2,)), pltpu.SemaphoreType.DMA((2,)),
                            pltpu.SemaphoreType.DMA((2,))]),
        compiler_params=_cp(("arbitrary",), 56),
        name="moe_down",
    )(start, nblk_e, act, w2, b2)


def _combine_kernel(slot_ref, y_hbm, x_ref, gmod_ref, w_ref, o_ref, buf, sem, *, tm, nsteps):
    i = pl.program_id(0)

    def issue(blk, slot):
        base = blk * tm

        def body(r, c):
            for k in range(TOP_K):
                s = slot_ref[(base + r) * TOP_K + k]
                pltpu.make_async_copy(y_hbm.at[pl.ds(s, 1), :], buf.at[slot, k, pl.ds(r, 1), :],
                                      sem.at[slot]).start()
            return c

        lax.fori_loop(0, tm, body, 0, unroll=4)

    @pl.when(i == 0)
    def _():
        issue(0, 0)

    @pl.when(i + 1 < nsteps)
    def _():
        issue(i + 1, (i + 1) % 2)

    slot = i % 2
    for k in range(TOP_K):
        pltpu.make_async_copy(y_hbm.at[pl.ds(0, tm), :], buf.at[slot, k], sem.at[slot]).wait()
    w = w_ref[...]
    acc = ((w[:, 0:1] * buf[slot, 0] + w[:, 1:2] * buf[slot, 1])
           + (w[:, 2:3] * buf[slot, 2] + w[:, 3:4] * buf[slot, 3]))
    o_ref[...] = x_ref[...] + gmod_ref[...] * acc


def _combine(slot_flat, ys, x, mod3, gate, gate_chunk, blk, mrow, nblk):
    tm = RB
    return pl.pallas_call(
        functools.partial(_combine_kernel, tm=tm, nsteps=nblk),
        out_shape=jax.ShapeDtypeStruct((nblk * tm, D), f32),
        grid_spec=pltpu.PrefetchScalarGridSpec(
            num_scalar_prefetch=1,
            grid=(nblk,),
            in_specs=[pl.BlockSpec(memory_space=pl.ANY),
                      pl.BlockSpec((tm, D), lambda i, s: (blk(i), 0)),
                      pl.BlockSpec((None, 1, D), lambda i, s: (mrow(i), 0, gate_chunk)),
                      pl.BlockSpec((tm, TOP_K), lambda i, s: (i, 0))],
            out_specs=pl.BlockSpec((tm, D), lambda i, s: (i, 0)),
            scratch_shapes=[pltpu.VMEM((2, TOP_K, tm, D), f32), pltpu.SemaphoreType.DMA((2,))]),
        compiler_params=_cp(("arbitrary",)),
        name="moe_combine",
    )(slot_flat, ys, x, mod3, gate)


def _moe_layer(x, g3, mod3, rw, rb3, w1, b1, w2, b2, perm, layer, blk, mrow, nblk):
    t = nblk * RB
    h, idx, gate, rank, cnt = _norm_router(x, g3, mod3, rw, rb3, layer, 3, blk, mrow, nblk)
    n_assign = t * TOP_K
    nb = n_assign // MOE_TM + N_EXPERTS
    nr = nb * MOE_TM
    counts = cnt[0].astype(jnp.int32)
    padded = (counts + MOE_TM - 1) // MOE_TM * MOE_TM
    pad_end = jnp.cumsum(padded)
    pad_start = pad_end - padded
    slot = pad_start[idx] + rank
    slot_flat = slot.reshape(-1)
    tok = jnp.repeat(jnp.arange(t, dtype=jnp.int32), TOP_K)
    slot_tok = (jnp.arange(nr, dtype=jnp.int32) % t).at[slot_flat].set(tok)
    nblk_e = (padded // MOE_TM).astype(jnp.int32)
    xs = _gather_rows(slot_tok, h, nr)
    act = _moe_up(pad_start.astype(jnp.int32), nblk_e, xs, w1, b1, perm, layer, nr)
    ys = _moe_down(pad_start.astype(jnp.int32), nblk_e, act, w2, b2, layer, nr)
    return _combine(slot_flat, ys, x, mod3, gate, 5, blk, mrow, nblk)


def _dot_nt(a, b):
    return lax.dot_general(a, b, (((1,), (1,)), ((), ())), preferred_element_type=f32)


def _dot_tn(a, b):
    return lax.dot_general(a, b, (((0,), (0,)), ((), ())), preferred_element_type=f32)


def _mlstm_kernel(q_ref, k_ref, v_ref, o_ref, gc_ref, gr_ref, ng_ref, y_ref, hf_ref, hb_ref, ct_ref):
    T = A_T
    nch = LA // T
    ct_ref[...] = jnp.zeros_like(ct_ref)
    r_i = lax.broadcasted_iota(jnp.int32, (T, T), 0)
    c_i = lax.broadcasted_iota(jnp.int32, (T, T), 1)
    lane = lax.broadcasted_iota(jnp.int32, (T, 128), 1)
    ones_blk = jnp.where(lane == 0, 1.0, 0.0).astype(bf16)

    def chunk(c, dirn, h_ref):
        r0 = pl.multiple_of(c * T, T)
        q = q_ref[pl.ds(r0, T), :]
        k = k_ref[pl.ds(r0, T), :]
        v = v_ref[pl.ds(r0, T), :]
        gc = gc_ref[pl.ds(r0, T), :]
        o = 5 * dirn
        a_col, m_col, negm = gc[:, o:o + 1], gc[:, o + 1:o + 2], gc[:, o + 2:o + 3]
        m_end, m_prev = gc[:, o + 3:o + 4], gc[:, o + 4:o + 5]
        a_row = gr_ref[dirn:dirn + 1, pl.ds(r0, T)]
        mask = (c_i <= r_i) if dirn == 0 else (c_i >= r_i)
        w = jnp.exp(jnp.where(mask, a_row - m_col, -jnp.inf))
        s = _dot_nt(q, k) * (A_DQK ** -0.5) * w
        v_aug = jnp.concatenate([v, ones_blk], axis=1)
        ct = ct_ref[dirn]
        w_carry = jnp.exp(m_prev - m_col)
        num = (jnp.dot(s.astype(bf16), v_aug, preferred_element_type=f32)
               + w_carry * jnp.dot(q, ct.astype(bf16), preferred_element_type=f32))
        den = num[:, A_DV:A_DV + 1]
        h_ref[pl.ds(r0, T), :] = num[:, :A_DV] / jnp.maximum(jnp.abs(den), jnp.exp(negm))
        w_in = jnp.exp(a_col - m_end)
        decay = jnp.exp(m_prev[0:1, :] - m_end[0:1, :])
        kv = _dot_tn(k, (w_in * v_aug.astype(f32)).astype(bf16)) * (A_DQK ** -0.5)
        ct_ref[dirn] = decay * ct + kv

    def step(it, carry):
        chunk(it, 0, hf_ref)
        chunk(jnp.where(it == 0, 0, nch - it), 1, hb_ref)
        return carry

    lax.fori_loop(0, nch, step, 0)
    h = hf_ref[...] + hb_ref[...]
    hn = h * lax.rsqrt(jnp.mean(h * h, axis=-1, keepdims=True) + EPS) * ng_ref[...]
    y_ref[...] = (hn * jax.nn.sigmoid(o_ref[...].astype(f32))).astype(y_ref.dtype)


def _mlstm(proj, gcol, grow, ng3):
    qb, vb = A_DQK, A_DV
    return pl.pallas_call(
        _mlstm_kernel,
        out_shape=jax.ShapeDtypeStruct((ROWS, HALF), bf16),
        grid=(NB_, A_HEADS),
        in_specs=[pl.BlockSpec((LA, qb), lambda b, h: (b, h)),
                  pl.BlockSpec((LA, qb), lambda b, h: (b, A_HEADS + h)),
                  pl.BlockSpec((LA, vb), lambda b, h: (b, (2 * A_HEADS * qb) // vb + h)),
                  pl.BlockSpec((LA, vb), lambda b, h: (b, (2 * A_HEADS * qb + HALF) // vb + h)),
                  pl.BlockSpec((None, None, LA, 16), lambda b, h: (b, h, 0, 0)),
                  pl.BlockSpec((None, None, 8, LA), lambda b, h: (b, h, 0, 0)),
                  pl.BlockSpec((None, 1, vb), lambda b, h: (h, 0, 0))],
        out_specs=pl.BlockSpec((LA, vb), lambda b, h: (b, h)),
        scratch_shapes=[pltpu.VMEM((LA, vb), f32), pltpu.VMEM((LA, vb), f32),
                        pltpu.VMEM((2, qb, vb + 128), f32)],
        compiler_params=_cp(("arbitrary", "arbitrary")),
        name="mlstm",
    )(proj, proj, proj, proj, gcol, grow, ng3)


def _mlstm_gate_tables(gates):
    i_f, f_f, i_b, f_b = jnp.split(gates, 4, axis=-1)

    def scan_tables(i_pre, f_pre):
        lf = jax.nn.log_sigmoid(f_pre)
        big_f = jnp.cumsum(lf, axis=1)
        a = i_pre - big_f
        m_run = jnp.maximum(lax.cummax(a, axis=1), 0.0)
        m_end = jnp.repeat(m_run[:, A_T - 1::A_T], A_T, axis=1)
        m_prev = jnp.concatenate([jnp.zeros_like(m_end[:, :A_T]), m_end[:, :-A_T]], axis=1)
        return a, m_run, -(big_f + m_run), m_end, m_prev

    def to_bwd(t):
        return jnp.concatenate([jnp.flip(t[:, :CTX], axis=1), jnp.flip(t[:, CTX:], axis=1)], axis=1)

    fw = scan_tables(i_f, f_f)
    bw = [to_bwd(t) for t in scan_tables(to_bwd(i_b), to_bwd(f_b))]
    cols = jnp.stack(list(fw) + bw, axis=-1)
    cols = jnp.pad(cols, ((0, 0), (0, 0), (0, 0), (0, 6))).transpose(0, 2, 1, 3)
    rows = jnp.stack([fw[0], bw[0]], axis=1).transpose(0, 3, 1, 2)
    rows = jnp.pad(rows, ((0, 0), (0, 0), (0, 6), (0, 0)))
    return cols, rows


def _swap_halves(t):
    hd = t.shape[-1] // 2
    return jnp.concatenate([t[..., hd:], t[..., :hd]], axis=-1)


def _attn_kernel(q_ref, k_ref, v_ref, cos_ref, sin_ref, gq_ref, gk_ref, sink_ref, o_ref):
    n = pl.program_id(2)
    scale = B_DH ** -0.5
    nq = B_G * 128

    def norm(t, g_ref):
        return t * lax.rsqrt(jnp.mean(t * t, axis=-1, keepdims=True) + EPS) * g_ref[...]

    qn = norm(q_ref[...].astype(f32), gq_ref)
    kc = norm(k_ref[0:CTX, :].astype(f32), gk_ref).astype(bf16)
    vc = v_ref[0:CTX, :]
    sink = sink_ref[...]

    def finish(parts):
        m = sink
        for s, _ in parts:
            m = jnp.maximum(m, jnp.max(s, axis=-1, keepdims=True))
        den = jnp.exp(sink - m)
        acc = None
        for s, vv in parts:
            p = jnp.exp(s - m)
            den = den + jnp.sum(p, axis=-1, keepdims=True)
            pv = jnp.dot(p.astype(bf16), vv, preferred_element_type=f32)
            acc = pv if acc is None else acc + pv
        o_ref[...] = (acc / den).reshape(B_G, 128, B_DH).astype(o_ref.dtype)

    @pl.when(n < CTX // 128)
    def _():
        s_ctx = _dot_nt(qn.reshape(nq, B_DH).astype(bf16), kc) * scale
        finish([(s_ctx, vc)])

    @pl.when(n >= CTX // 128)
    def _():
        nl = n - CTX // 128
        pos0 = pl.multiple_of(nl * 128, 128)
        cq = cos_ref[pl.ds(pos0, 128), :]
        sq = sin_ref[pl.ds(pos0, 128), :]
        qr = qn * cq[None] + _swap_halves(qn) * sq[None]
        start = pl.multiple_of(jnp.clip((nl - 1) * 128, 0, SEQ - 3 * 128), 128)
        kb = norm(k_ref[pl.ds(CTX + start, 3 * 128), :].astype(f32), gk_ref)
        kb = kb * cos_ref[pl.ds(start, 3 * 128), :] + _swap_halves(kb) * sin_ref[pl.ds(start, 3 * 128), :]
        vb = v_ref[pl.ds(CTX + start, 3 * 128), :]
        s_band = _dot_nt(qr.reshape(nq, B_DH).astype(bf16), kb.astype(bf16)) * scale
        q_pos = pos0 + lax.broadcasted_iota(jnp.int32, (nq, 3 * 128), 0) % 128
        k_pos = start + lax.broadcasted_iota(jnp.int32, (nq, 3 * 128), 1)
        s_band = jnp.where(jnp.abs(q_pos - k_pos) <= WINDOW, s_band, -jnp.inf)
        s_ctx = _dot_nt(qn.reshape(nq, B_DH).astype(bf16), kc) * scale
        finish([(s_band, vb), (s_ctx, vc)])


def _attention(q5, k4, v4, cosf, sinf, gq, gk, sink_col):
    nblk = LA // 128
    return pl.pallas_call(
        _attn_kernel,
        out_shape=jax.ShapeDtypeStruct(q5.shape, bf16),
        grid=(NB_, B_KV, nblk),
        in_specs=[pl.BlockSpec((None, None, B_G, 128, B_DH), lambda b, kh, n: (b, kh, 0, n, 0)),
                  pl.BlockSpec((None, None, LA, B_DH), lambda b, kh, n: (b, kh, 0, 0)),
                  pl.BlockSpec((None, None, LA, B_DH), lambda b, kh, n: (b, kh, 0, 0)),
                  pl.BlockSpec((SEQ, B_DH), lambda b, kh, n: (0, 0)),
                  pl.BlockSpec((SEQ, B_DH), lambda b, kh, n: (0, 0)),
                  pl.BlockSpec((1, B_DH), lambda b, kh, n: (0, 0)),
                  pl.BlockSpec((1, B_DH), lambda b, kh, n: (0, 0)),
                  pl.BlockSpec((None, B_G * 128, 1), lambda b, kh, n: (kh, 0, 0))],
        out_specs=pl.BlockSpec((None, None, B_G, 128, B_DH), lambda b, kh, n: (b, kh, 0, n, 0)),
        compiler_params=_cp(("arbitrary", "arbitrary", "arbitrary")),
        name="window_attn",
    )(q5, k4, v4, cosf, sinf, gq, gk, sink_col)


def _rope_tables():
    rows = SEQ // GRID_W
    row = jnp.repeat(jnp.arange(rows), GRID_W).astype(f32)
    col = jnp.tile(jnp.arange(GRID_W), rows).astype(f32)
    n_freq = B_DH // 4
    inv = ROPE_BASE ** (-jnp.arange(n_freq, dtype=f32) / n_freq)
    ang = jnp.concatenate([row[:, None] * inv, col[:, None] * inv], axis=-1)
    cos, sin = jnp.cos(ang), jnp.sin(ang)
    return jnp.concatenate([cos, cos], axis=-1), jnp.concatenate([-sin, sin], axis=-1)


def _s5_kernel(u_ref, wy_hbm, wi_hbm, a_ref, y_ref, wy, wi, u32, ucat, zf, zb, sem):
    j = pl.program_id(0)
    T = S5_T
    cw = pltpu.make_async_copy(wy_hbm.at[j], wy, sem.at[0])
    ci = pltpu.make_async_copy(wi_hbm.at[j], wi, sem.at[1])
    cw.start()
    ci.start()
    u32[...] = u_ref[...].astype(f32)
    for s in range(T):
        for b in range(NB_):
            ucat[s, pl.ds(b, S5_CPB, stride=NB_), :] = u32[pl.ds(b * LA + s, S5_CPB, stride=T), :]
    ci.wait()
    uc = jnp.concatenate([ucat[s].astype(bf16) for s in range(T)], axis=1)
    zf[...] = jnp.dot(uc, wi[0], preferred_element_type=f32)
    zb[...] = jnp.dot(uc, wi[1], preferred_element_type=f32)
    a = a_ref[...]
    hs = S5_NS

    def cstep(av, x, z):
        ar, ai = av[:, :hs], av[:, hs:]
        xr, xi = x[:, :hs], x[:, hs:]
        return jnp.concatenate([ar * xr - ai * xi + z[:, :hs], ar * xi + ai * xr + z[:, hs:]], axis=1)

    def step(i, carry):
        x_f, x_b = carry
        rf = pl.multiple_of(i * 2 * NB_, 2 * NB_)
        mb = jnp.where(i < S5_CCTX // 2, S5_CCTX // 2 - 1 - i, S5_CPB // 2 + S5_CCTX // 2 - 1 - i)
        rb = pl.multiple_of(mb * 2 * NB_, 2 * NB_)
        z8 = zf[pl.ds(rf, 2 * NB_), :]
        x1 = cstep(a[0:1], x_f, z8[:NB_])
        zf[pl.ds(rf, 2 * NB_), :] = jnp.concatenate([x_f, x1], axis=0)
        x_f2 = cstep(a[0:1], x1, z8[NB_:])
        w8 = zb[pl.ds(rb, 2 * NB_), :]
        y1 = cstep(a[1:2], x_b, w8[NB_:])
        zb[pl.ds(rb, 2 * NB_), :] = jnp.concatenate([y1, x_b], axis=0)
        x_b2 = cstep(a[1:2], y1, w8[:NB_])
        return x_f2, x_b2

    zero = jnp.zeros((NB_, 2 * hs), f32)
    lax.fori_loop(0, S5_CPB // 2, step, (zero, zero))
    cw.wait()
    xf = zf[...].astype(bf16)
    xb = zb[...].astype(bf16)
    ku = T * 128
    for t in range(T):
        u32[pl.ds(t * S5_NCH, S5_NCH), :] = (
            jnp.dot(uc, wy[t, 0:ku, :], preferred_element_type=f32)
            + jnp.dot(xf, wy[t, ku:ku + 2 * hs, :], preferred_element_type=f32)
            + jnp.dot(xb, wy[t, ku + 2 * hs:ku + 4 * hs, :], preferred_element_type=f32))
    for t in range(T):
        for b in range(NB_):
            y_ref[pl.ds(b * LA + t, S5_CPB, stride=T), :] = u32[pl.ds(t * S5_NCH + b, S5_CPB, stride=NB_), :]


def _s5_core(u_all, mats):
    wy, wi, a = mats
    nt = HALF // 128
    kf = S5_T * 128 + 4 * S5_NS
    return pl.pallas_call(
        _s5_kernel,
        out_shape=jax.ShapeDtypeStruct((ROWS, HALF), f32),
        grid=(nt,),
        in_specs=[pl.BlockSpec((ROWS, 128), lambda j: (0, j)),
                  pl.BlockSpec(memory_space=pl.ANY),
                  pl.BlockSpec(memory_space=pl.ANY),
                  pl.BlockSpec((None, 2, 2 * S5_NS), lambda j: (j, 0, 0))],
        out_specs=pl.BlockSpec((ROWS, 128), lambda j: (0, j)),
        scratch_shapes=[pltpu.VMEM((S5_T, kf, 128), bf16),
                        pltpu.VMEM((2, S5_T * 128, 2 * S5_NS), bf16),
                        pltpu.VMEM((ROWS, 128), f32),
                        pltpu.VMEM((S5_T, S5_NCH, 128), f32),
                        pltpu.VMEM((S5_NCH, 2 * S5_NS), f32),
                        pltpu.VMEM((S5_NCH, 2 * S5_NS), f32),
                        pltpu.SemaphoreType.DMA((2,))],
        compiler_params=_cp(("arbitrary",), 56),
        name="s5_core",
    )(u_all, wy, wi, a)


def _s5_matrices(a_re, a_im, log_dt, b_re, b_im, c_re, c_im):
    T = S5_T
    G, P, C = C_GROUPS, C_STATE, C_GROUP
    nt = G // S5_GT
    lam = lax.complex(a_re.astype(f32), a_im.astype(f32))
    dt = jnp.exp(log_dt.astype(f32))[..., None]
    a_bar = jnp.exp(lam * dt)
    b_bar = ((a_bar - 1) / lam)[..., None] * lax.complex(b_re.astype(f32), b_im.astype(f32))
    tau = jnp.arange(T + 1, dtype=f32)
    apow = jnp.exp((lam * dt)[..., None] * tau)
    cc = lax.complex(c_re.astype(f32), c_im.astype(f32))
    kern = jnp.einsum("gxp,dgpt,dgpc->dgtxc", cc, apow[..., :T], b_bar).real
    t_i = jnp.arange(T)
    lag = t_i[:, None] - t_i[None, :]
    kf = kern[0][:, jnp.clip(lag, 0, T - 1)] * (lag >= 0)[None, :, :, None, None]
    kb = kern[1][:, jnp.clip(-lag, 0, T - 1)] * (lag <= 0)[None, :, :, None, None]
    km = (kf + kb).reshape(nt, S5_GT, T, T, C, C)
    eye = jnp.eye(S5_GT, dtype=f32)
    wy_u = jnp.einsum("jgtsxc,gh->jtsgchx", km, eye).reshape(nt, T, T * 128, 128)
    k_f = cc[:, :, :, None] * apow[0][:, None, :, 1 + t_i]
    k_b = cc[:, :, :, None] * apow[1][:, None, :, T - t_i]

    def wy_state(k):
        k = k.reshape(nt, S5_GT, C, P, T)
        re = jnp.einsum("jgxpt,gh->jtgphx", k.real, eye).reshape(nt, T, S5_NS, 128)
        im = jnp.einsum("jgxpt,gh->jtgphx", -k.imag, eye).reshape(nt, T, S5_NS, 128)
        return jnp.concatenate([re, im], axis=2)

    wy = jnp.concatenate([wy_u, wy_state(k_f), wy_state(k_b)], axis=2).astype(bf16)
    w_f = apow[0][:, :, T - 1 - t_i][..., None] * b_bar[0][:, :, None, :]
    w_b = apow[1][:, :, t_i][..., None] * b_bar[1][:, :, None, :]

    def wi_dir(w):
        w = w.reshape(nt, S5_GT, P, T, C)
        re = jnp.einsum("jgpsc,gh->jsgchp", w.real, eye).reshape(nt, T * 128, S5_NS)
        im = jnp.einsum("jgpsc,gh->jsgchp", w.imag, eye).reshape(nt, T * 128, S5_NS)
        return jnp.concatenate([re, im], axis=2)

    wi = jnp.stack([wi_dir(w_f), wi_dir(w_b)], axis=1).astype(bf16)
    at = apow[..., T].reshape(2, nt, S5_NS)
    a = jnp.concatenate([at.real, at.imag], axis=-1).transpose(1, 0, 2)
    return wy, wi, a


def _s5_glu_kernel(y_ref, u_ref, d_ref, w_ref, b_ref, o_ref, wbf_ref):
    @pl.when(pl.program_id(0) == 0)
    def _():
        wbf_ref[...] = w_ref[...].astype(bf16)

    y = y_ref[...] + d_ref[...] * u_ref[...].astype(f32)
    y = jax.nn.gelu(y, approximate=True)
    z = jnp.dot(y.astype(bf16), wbf_ref[...], preferred_element_type=f32) + b_ref[...]
    o_ref[...] = (y * jax.nn.sigmoid(z)).astype(o_ref.dtype)


def _s5_glu(y_ssm, proj, d2, glu_w, glu_b2):
    nblk = NB_ * LBPB
    return pl.pallas_call(
        _s5_glu_kernel,
        out_shape=jax.ShapeDtypeStruct((nblk * RB, HALF), bf16),
        grid=(nblk,),
        in_specs=[pl.BlockSpec((RB, HALF), lambda i: (_lat_blk(i), 0)),
                  pl.BlockSpec((RB, HALF), lambda i: (_lat_blk(i), 0)),
                  pl.BlockSpec((1, HALF), lambda i: (0, 0)),
                  pl.BlockSpec((HALF, HALF), lambda i: (0, 0)),
                  pl.BlockSpec((1, HALF), lambda i: (0, 0))],
        out_specs=pl.BlockSpec((RB, HALF), lambda i: (i, 0)),
        scratch_shapes=[pltpu.VMEM((HALF, HALF), bf16)],
        compiler_params=_cp(("arbitrary",)),
        name="s5_glu",
    )(y_ssm, proj, d2, glu_w, glu_b2)


def _hyena_filter(L, w1, b1, w2, b2, w3, freq):
    t = jnp.linspace(0.0, 1.0, L, dtype=f32)[:, None]
    w = 2 * math.pi * jnp.arange(L, dtype=f32)[:, None] / L
    bands = jnp.linspace(1e-4, HY_BANDS - 1, HY_BANDS, dtype=f32)
    feats = jnp.concatenate([t, jnp.cos(bands * w), -jnp.sin(bands * w)], axis=-1)
    h = jnp.sin(freq * (feats @ w1 + b1))
    h = jnp.sin(freq * (h @ w2 + b2))
    h = h @ w3
    fast = abs(math.log(HY_DECAY_TARGET) / HY_FAST_PCT)
    slow = abs(math.log(HY_DECAY_TARGET) / HY_SLOW_PCT)
    deltas = jnp.tile(jnp.linspace(slow, fast, HALF, dtype=f32), 2)
    h = h * jnp.exp(-t * deltas)
    h_fwd, h_bwd = jnp.split(h, 2, axis=-1)
    filt = jnp.concatenate([h_fwd, jnp.zeros((1, HALF), f32), h_bwd[:0:-1]], axis=0)
    return filt / jnp.sum(jnp.abs(filt), axis=0, keepdims=True)


def _hyena_dft_tables():
    n2 = 2 * SEQ
    n = jnp.arange(SEQ, dtype=jnp.int32)[:, None]
    k = jnp.arange(SEQ, dtype=jnp.int32)[None, :]
    ang = ((n * k) % n2).astype(f32) * (2.0 * math.pi / n2)
    c, s = jnp.cos(ang), jnp.sin(ang)
    nyq = jnp.where(n % 2 == 0, 1.0, -1.0).astype(f32)
    f_re, f_im = c, jnp.where(k == 0, nyq, -s)
    g_re = jnp.where(k == 0, 1.0, 2.0 * c) / n2
    g_im = jnp.where(k == 0, nyq, -2.0 * s) / n2

    def tiles(re, im):
        nk = SEQ // HY_KT
        return jnp.concatenate([re.reshape(SEQ, nk, 1, HY_KT), im.reshape(SEQ, nk, 1, HY_KT)],
                               axis=2).reshape(SEQ, 2 * SEQ).astype(bf16)

    return tiles(f_re, f_im), tiles(g_re, g_im)


def _hyena_kernel(x0_ref, x1_ref, v_ref, cw0_ref, cw1_ref, cw2_ref, cb0_ref, cb1_ref, cb2_ref, hb_ref,
                  f_ref, g_ref, ht_ref, o_ref, ut_scr, u_scr, y_scr):
    kt = pl.program_id(2)
    L = SEQ
    row = lax.broadcasted_iota(jnp.int32, (L, HY_TC), 0)

    def sconv(z_ref, cw_ref, cb_ref):
        z = z_ref[...].astype(f32)
        zm = jnp.where(row == 0, 0.0, pltpu.roll(z, 1, axis=0))
        zp = jnp.where(row == L - 1, 0.0, pltpu.roll(z, L - 1, axis=0))
        return zm * cw_ref[0:1, :] + z * cw_ref[1:2, :] + zp * cw_ref[2:3, :] + cb_ref[...]

    @pl.when(kt == 0)
    def _():
        u = sconv(x1_ref, cw1_ref, cb1_ref) * sconv(v_ref, cw2_ref, cb2_ref)
        u_scr[...] = u
        ut_scr[...] = u.T.astype(bf16)
        y_scr[...] = jnp.zeros_like(y_scr)

    xt = jnp.dot(ut_scr[...], f_ref[...], preferred_element_type=f32)
    h = ht_ref[...]
    xr, xi, hr, hi = xt[:, :HY_KT], xt[:, HY_KT:], h[:, :HY_KT], h[:, HY_KT:]
    col = lax.broadcasted_iota(jnp.int32, (HY_TC, HY_KT), 1) + kt * HY_KT
    dc = col == 0
    yr = xr * hr - jnp.where(dc, 0.0, xi * hi)
    yi = jnp.where(dc, xi * hi, xr * hi + xi * hr)
    yt = jnp.concatenate([yr, yi], axis=1).astype(bf16)
    y_scr[...] += _dot_nt(g_ref[...], yt)

    @pl.when(kt == pl.num_programs(2) - 1)
    def _():
        o_ref[...] = (sconv(x0_ref, cw0_ref, cb0_ref)
                      * (y_scr[...] + u_scr[...] * hb_ref[...])).astype(o_ref.dtype)


def _hyena(z_lat, conv_w, conv_b2, hy_bias2, ht):
    ft, gt = _hyena_dft_tables()
    nct = HALF // HY_TC

    def zspec(k):
        return pl.BlockSpec((SEQ, HY_TC), lambda ct, b, kt: (b, k * nct + ct))

    def wspec(k):
        return pl.BlockSpec((3, HY_TC), lambda ct, b, kt: (0, k * nct + ct))

    def bspec(k):
        return pl.BlockSpec((1, HY_TC), lambda ct, b, kt: (0, k * nct + ct))

    return pl.pallas_call(
        _hyena_kernel,
        out_shape=jax.ShapeDtypeStruct((NB_ * SEQ, HALF), bf16),
        grid=(nct, NB_, SEQ // HY_KT),
        in_specs=[zspec(0), zspec(1), zspec(2), wspec(0), wspec(1), wspec(2), bspec(0), bspec(1), bspec(2),
                  pl.BlockSpec((1, HY_TC), lambda ct, b, kt: (0, ct)),
                  pl.BlockSpec((SEQ, 2 * HY_KT), lambda ct, b, kt: (0, kt)),
                  pl.BlockSpec((SEQ, 2 * HY_KT), lambda ct, b, kt: (0, kt)),
                  pl.BlockSpec((HY_TC, 2 * HY_KT), lambda ct, b, kt: (ct, kt))],
        out_specs=pl.BlockSpec((SEQ, HY_TC), lambda ct, b, kt: (b, ct)),
        scratch_shapes=[pltpu.VMEM((HY_TC, SEQ), bf16), pltpu.VMEM((SEQ, HY_TC), f32),
                        pltpu.VMEM((SEQ, HY_TC), f32)],
        compiler_params=_cp(("arbitrary", "arbitrary", "arbitrary"), 48),
        name="hyena",
    )(z_lat, z_lat, z_lat, conv_w, conv_w, conv_w, conv_b2, conv_b2, conv_b2, hy_bias2, ft, gt, ht)


def _hyena_filter_spectrum(filt):
    hk = jnp.fft.rfft(filt.astype(f32), axis=0)
    re = hk.real[:SEQ].T
    im = hk.imag[:SEQ].T.at[:, 0].set(hk.real[SEQ])
    nk = SEQ // HY_KT
    c = re.shape[0]
    return jnp.concatenate([re.reshape(c, nk, 1, HY_KT), im.reshape(c, nk, 1, HY_KT)], axis=2).reshape(c, 2 * SEQ)


def _swiglu_perm():
    j = np.arange(256)
    src = np.where(j < 128, 2 * j, 2 * (j - 128) + 1)
    p = np.zeros((256, 256), np.float32)
    p[src, j] = 1.0
    return jnp.asarray(p, dtype=bf16)


def kernel(x, c, ctx, c_ctx, ada_w, ada_b, norm_mix_g, norm_ffn_g, ev_w_in, ev_gate_b, ev_h_norm_g,
           ev_q_norm_g, ev_k_norm_g, ev_sink, ev_w_out, od_w_in, od_a_re, od_a_im, od_log_dt, od_b_re,
           od_b_im, od_c_re, od_c_im, od_d, od_glu_w, od_glu_b, od_conv_w, od_conv_b, od_filt_w1,
           od_filt_b1, od_filt_w2, od_filt_b2, od_filt_w3, od_filt_freq, od_hy_bias, od_w_out,
           moe_router_w, moe_router_b, moe_w1, moe_b1, moe_w2, moe_b2):
    depth = ada_w.shape[0]
    xu = jnp.concatenate([ctx, x], axis=1).reshape(ROWS, D)
    cond8 = jnp.concatenate([c_ctx[None], c, jnp.zeros((3, D), f32)], axis=0)
    ada_b3 = ada_b.reshape(depth, 1, 6 * D)
    g_mix3 = norm_mix_g.reshape(depth, 1, D)
    g_ffn3 = norm_ffn_g.reshape(depth, 1, D)
    rb3 = moe_router_b.reshape(depth, 1, N_EXPERTS)
    b1_4 = moe_b1.reshape(depth, N_EXPERTS, 1, 2 * D_FF)
    b2_4 = moe_b2.reshape(depth, N_EXPERTS, 1, D)
    perm = _swiglu_perm()
    nall = ROWS // RB
    nlat = NB_ * LBPB

    mod3 = _ada(cond8, ada_w, ada_b3, 0).reshape(8, 1, 6 * D)
    h = _norm_mod(xu, g_mix3, mod3, 0, 0, _ident, _mod_row, nall, bf16)
    w_in = ev_w_in[0]
    n_a = 2 * A_HEADS * A_DQK + 2 * HALF
    n_g = 4 * A_HEADS
    w_main = jnp.concatenate([w_in[:, :n_a], w_in[:, n_a + n_g:]], axis=1)
    w_gate = jnp.pad(w_in[:, n_a:n_a + n_g], ((0, 0), (0, 128 - n_g)))
    proj = _matmul([h], w_main, (), w_main.shape[1], 1536, nall, _ident, bf16, name="even_in")
    gates = _matmul([h], w_gate, (), 128, 128, nall, _ident, f32, name="even_gates")[:, :n_g]
    gates = (gates + ev_gate_b[0]).reshape(NB_, LA, n_g)
    gcol, grow = _mlstm_gate_tables(gates)
    y_a = _mlstm(proj, gcol, grow, ev_h_norm_g[0].reshape(A_HEADS, 1, A_DV))
    qa = proj[:, n_a:n_a + HALF].reshape(NB_, LA, B_KV, B_G, B_DH).transpose(0, 2, 3, 1, 4)
    ka = proj[:, n_a + HALF:n_a + HALF + B_KV * B_DH].reshape(NB_, LA, B_KV, B_DH).transpose(0, 2, 1, 3)
    va = proj[:, n_a + HALF + B_KV * B_DH:].reshape(NB_, LA, B_KV, B_DH).transpose(0, 2, 1, 3)
    cosf, sinf = _rope_tables()
    sink_col = jnp.repeat(ev_sink[0].astype(f32).reshape(B_KV, B_G), 128, axis=1)[..., None]
    o5 = _attention(qa, ka, va, cosf, sinf, ev_q_norm_g[0].reshape(1, B_DH),
                    ev_k_norm_g[0].reshape(1, B_DH), sink_col)
    y_b = o5.transpose(0, 3, 1, 2, 4).reshape(ROWS, HALF)
    xu = _matmul([y_a, y_b], ev_w_out, (0,), D, 1024, nall, _ident, f32,
                 res=xu, res_blk=_ident, mrow=_mod_row, mod3=mod3, gate_chunk=2, name="even_out")
    xu = _moe_layer(xu, g_ffn3, mod3, moe_router_w, rb3, moe_w1, b1_4, moe_w2, b2_4, perm,
                    0, _ident, _mod_row, nall)

    mod3 = _ada(cond8, ada_w, ada_b3, 1).reshape(8, 1, 6 * D)
    h = _norm_mod(xu, g_mix3, mod3, 1, 0, _ident, _mod_row, nall, bf16)
    proj = _matmul([h], od_w_in, (0,), HALF, 1024, nall, _ident, bf16, name="odd_in_u")
    z_lat = _matmul([h], od_w_in, (0,), 3 * HALF, 1024, nlat, _lat_blk, bf16, w_col0=HALF // 1024,
                    name="odd_in_z")
    mats = _s5_matrices(od_a_re[0], od_a_im[0], od_log_dt[0], od_b_re[0], od_b_im[0], od_c_re[0], od_c_im[0])
    y_ssm = _s5_core(proj, mats)
    s_l = _s5_glu(y_ssm, proj, od_d[0].reshape(1, HALF), od_glu_w[0], od_glu_b[0].reshape(1, HALF))
    filt = _hyena_filter(SEQ, od_filt_w1[0], od_filt_b1[0], od_filt_w2[0], od_filt_b2[0], od_filt_w3[0],
                         od_filt_freq[0])
    hy_l = _hyena(z_lat, od_conv_w[0], od_conv_b[0].reshape(1, 3 * HALF), od_hy_bias[0].reshape(1, HALF),
                  _hyena_filter_spectrum(filt))
    x_lat = _matmul([s_l, hy_l], od_w_out, (0,), D, 1024, nlat, _ident, f32,
                    res=xu, res_blk=_lat_blk, mrow=_lat_mod_row, mod3=mod3, gate_chunk=2, name="odd_out")
    out = _moe_layer(x_lat, g_ffn3, mod3, moe_router_w, rb3, moe_w1, b1_4, moe_w2, b2_4, perm,
                     1, _ident, _lat_mod_row, nlat)
    return out.reshape(NB_, SEQ, D)
```

```python
import functools
import math

import jax
import jax.numpy as jnp
import numpy as np
from jax import lax
from jax.experimental import pallas as pl
from jax.experimental.pallas import tpu as pltpu

f32 = jnp.float32
bf16 = jnp.bfloat16

D = 2048
NB_ = 4
SEQ = 2048
CTX = 256
LA = SEQ + CTX
ROWS = NB_ * LA
RB = 256
BPB = LA // RB
LBPB = SEQ // RB
EPS = 1e-6
GRID_W = 64

HALF = D // 2
A_HEADS = 4
A_DV = HALF // A_HEADS
A_DQK = A_DV // 2
A_T = 256
B_HEADS = 16
B_KV = 4
B_G = B_HEADS // B_KV
B_DH = HALF // B_HEADS
WINDOW = 128
ROPE_BASE = 10000.0
C_GROUP = 16
C_GROUPS = HALF // C_GROUP
C_STATE = 64
S5_T = 8
S5_GT = 128 // C_GROUP
S5_NS = S5_GT * C_STATE
S5_NCH = ROWS // S5_T
S5_CPB = LA // S5_T
S5_CCTX = CTX // S5_T
HY_TC = 256
HY_KT = 512
HY_BANDS = 16
HY_DECAY_TARGET = 1e-2
HY_FAST_PCT = 0.3
HY_SLOW_PCT = 1.5
N_EXPERTS = 32
TOP_K = 4
D_FF = D
SWIGLU_LIMIT = 7.0
SWIGLU_ALPHA = 1.702
MOE_TM = 256

EVEN_SPLITS = (A_HEADS * A_DQK, A_HEADS * A_DQK, HALF, HALF, 4 * A_HEADS,
               B_HEADS * B_DH, B_KV * B_DH, B_KV * B_DH)

VMEM_MB = 1024 * 1024


def _cp(sem, vmem_mb=40):
    return pltpu.CompilerParams(dimension_semantics=sem, vmem_limit_bytes=vmem_mb * VMEM_MB)


def _lat_blk(i):
    return (i // LBPB) * BPB + 1 + i % LBPB


def _mod_row(u):
    return jnp.where(u % BPB == 0, 0, 1 + u // BPB)


def _lat_mod_row(i):
    return 1 + i // LBPB


def _ident(i):
    return i


def _ada_kernel(c_ref, w_ref, b_ref, o_ref):
    c = c_ref[...]
    s = c * jax.nn.sigmoid(c)
    o_ref[...] = jnp.dot(s.astype(bf16), w_ref[...].astype(bf16),
                         preferred_element_type=f32) + b_ref[...]


def _ada(cond8, ada_w, ada_b3, layer):
    tn = 1024
    n = 6 * D
    return pl.pallas_call(
        _ada_kernel,
        out_shape=jax.ShapeDtypeStruct((8, n), f32),
        grid=(n // tn,),
        in_specs=[pl.BlockSpec((8, D), lambda j: (0, 0)),
                  pl.BlockSpec((None, D, tn), lambda j: (layer, 0, j)),
                  pl.BlockSpec((None, 1, tn), lambda j: (layer, 0, j))],
        out_specs=pl.BlockSpec((8, tn), lambda j: (0, j)),
        compiler_params=_cp(("arbitrary",)),
        name="ada_params",
    )(cond8, ada_w, ada_b3)


def _norm_mod_kernel(x_ref, g_ref, sh_ref, sc_ref, o_ref):
    x = x_ref[...]
    y = x * lax.rsqrt(jnp.mean(x * x, axis=-1, keepdims=True) + EPS) * g_ref[...]
    o_ref[...] = (y * (1.0 + sc_ref[...]) + sh_ref[...]).astype(o_ref.dtype)


def _norm_mod(x, g3, mod3, layer, chunk, blk, mrow, nblk, out_dtype):
    return pl.pallas_call(
        _norm_mod_kernel,
        out_shape=jax.ShapeDtypeStruct((nblk * RB, D), out_dtype),
        grid=(nblk,),
        in_specs=[pl.BlockSpec((RB, D), lambda i: (blk(i), 0)),
                  pl.BlockSpec((None, 1, D), lambda i: (layer, 0, 0)),
                  pl.BlockSpec((None, 1, D), lambda i: (mrow(i), 0, chunk)),
                  pl.BlockSpec((None, 1, D), lambda i: (mrow(i), 0, chunk + 1))],
        out_specs=pl.BlockSpec((RB, D), lambda i: (i, 0)),
        compiler_params=_cp(("arbitrary",)),
        name="norm_mod",
    )(x, g3, mod3, mod3)


def _split_bf16(x):
    hi = x.astype(bf16)
    lo = (x - hi.astype(f32)).astype(bf16)
    return hi, lo


def _norm_router_kernel(x_ref, g_ref, sh_ref, sc_ref, rw_ref, rb_ref,
                        h_ref, idx_ref, gate_ref, rank_ref, cnt_ref, carry_ref):
    i = pl.program_id(0)

    @pl.when(i == 0)
    def _():
        carry_ref[...] = jnp.zeros_like(carry_ref)

    x = x_ref[...]
    y = x * lax.rsqrt(jnp.mean(x * x, axis=-1, keepdims=True) + EPS) * g_ref[...]
    h = y * (1.0 + sc_ref[...]) + sh_ref[...]
    h_ref[...] = h
    h_hi, h_lo = _split_bf16(h)
    w_hi, w_lo = _split_bf16(rw_ref[...])
    logits = (jnp.dot(h_hi, w_hi, preferred_element_type=f32)
              + jnp.dot(h_lo, w_hi, preferred_element_type=f32)
              + jnp.dot(h_hi, w_lo, preferred_element_type=f32)) + rb_ref[...]
    ne = logits.shape[-1]
    lane = lax.broadcasted_iota(jnp.int32, logits.shape, 1)
    l = logits
    vals, idxs = [], []
    for _ in range(TOP_K):
        m = jnp.max(l, axis=-1, keepdims=True)
        ix = jnp.min(jnp.where(l == m, lane, ne), axis=-1, keepdims=True)
        vals.append(m)
        idxs.append(ix)
        l = jnp.where(lane == ix, -jnp.inf, l)
    es = [jnp.exp(v - vals[0]) for v in vals]
    tot = es[0] + es[1] + es[2] + es[3]
    onehot = jnp.zeros(logits.shape, f32)
    for ix in idxs:
        onehot = onehot + jnp.where(lane == ix, 1.0, 0.0)
    r_i = lax.broadcasted_iota(jnp.int32, (RB, RB), 0)
    c_i = lax.broadcasted_iota(jnp.int32, (RB, RB), 1)
    tri = jnp.where(c_i < r_i, 1.0, 0.0).astype(bf16)
    prefix = jnp.dot(tri, onehot.astype(bf16), preferred_element_type=f32) + carry_ref[...]
    for k in range(TOP_K):
        idx_ref[:, k:k + 1] = idxs[k]
        gate_ref[:, k:k + 1] = es[k] / tot
        rk = jnp.sum(jnp.where(lane == idxs[k], prefix, 0.0), axis=-1, keepdims=True)
        rank_ref[:, k:k + 1] = rk.astype(jnp.int32)
    carry_ref[...] = carry_ref[...] + jnp.sum(onehot, axis=0, keepdims=True)
    cnt_ref[...] = carry_ref[...]


def _norm_router(x, g3, mod3, rw, rb3, layer, chunk, blk, mrow, nblk):
    t = nblk * RB
    return pl.pallas_call(
        _norm_router_kernel,
        out_shape=(jax.ShapeDtypeStruct((t, D), f32),
                   jax.ShapeDtypeStruct((t, TOP_K), jnp.int32),
                   jax.ShapeDtypeStruct((t, TOP_K), f32),
                   jax.ShapeDtypeStruct((t, TOP_K), jnp.int32),
                   jax.ShapeDtypeStruct((1, N_EXPERTS), f32)),
        grid=(nblk,),
        in_specs=[pl.BlockSpec((RB, D), lambda i: (blk(i), 0)),
                  pl.BlockSpec((None, 1, D), lambda i: (layer, 0, 0)),
                  pl.BlockSpec((None, 1, D), lambda i: (mrow(i), 0, chunk)),
                  pl.BlockSpec((None, 1, D), lambda i: (mrow(i), 0, chunk + 1)),
                  pl.BlockSpec((None, D, N_EXPERTS), lambda i: (layer, 0, 0)),
                  pl.BlockSpec((None, 1, N_EXPERTS), lambda i: (layer, 0, 0))],
        out_specs=(pl.BlockSpec((RB, D), lambda i: (i, 0)),
                   pl.BlockSpec((RB, TOP_K), lambda i: (i, 0)),
                   pl.BlockSpec((RB, TOP_K), lambda i: (i, 0)),
                   pl.BlockSpec((RB, TOP_K), lambda i: (i, 0)),
                   pl.BlockSpec((1, N_EXPERTS), lambda i: (0, 0))),
        scratch_shapes=[pltpu.VMEM((1, N_EXPERTS), f32)],
        compiler_params=_cp(("arbitrary",)),
        name="norm_router",
    )(x, g3, mod3, mod3, rw, rb3)


def _mm_kernel(*refs, n_a, k_split, epilogue):
    a_refs = refs[:n_a]
    w_ref = refs[n_a]
    pos = n_a + 1
    if epilogue:
        res_ref, gate_ref = refs[pos], refs[pos + 1]
        pos += 2
    o_ref, wbf_ref = refs[pos], refs[pos + 1]

    @pl.when(pl.program_id(1) == 0)
    def _():
        wbf_ref[...] = w_ref[...].astype(bf16)

    acc = None
    for t, a_ref in enumerate(a_refs):
        part = jnp.dot(a_ref[...], wbf_ref[t * k_split:(t + 1) * k_split, :],
                       preferred_element_type=f32)
        acc = part if acc is None else acc + part
    if epilogue:
        acc = res_ref[...] + gate_ref[...] * acc
    o_ref[...] = acc.astype(o_ref.dtype)


def _matmul(a_list, w, w_idx, n, tn, nblk, a_blk, out_dtype, *, a_col=None, w_col0=0,
            res=None, res_blk=None, mrow=None, mod3=None, gate_chunk=None, name="matmul"):
    n_a = len(a_list)
    k = w.shape[-2]
    k_split = k // n_a
    lead = len(w_idx)
    a_col = a_col or [0] * n_a
    in_specs = [pl.BlockSpec((RB, k_split), functools.partial(lambda j, i, c: (a_blk(i), c), c=c))
                for c in a_col]
    in_specs.append(pl.BlockSpec((None,) * lead + (k, tn), lambda j, i: tuple(w_idx) + (0, j + w_col0)))
    args = list(a_list) + [w]
    epilogue = res is not None
    if epilogue:
        per = D // tn
        in_specs.append(pl.BlockSpec((RB, tn), lambda j, i: (res_blk(i), j)))
        in_specs.append(pl.BlockSpec((None, 1, tn),
                                     lambda j, i: (mrow(i), 0, gate_chunk * per + j)))
        args += [res, mod3]
    return pl.pallas_call(
        functools.partial(_mm_kernel, n_a=n_a, k_split=k_split, epilogue=epilogue),
        out_shape=jax.ShapeDtypeStruct((nblk * RB, n), out_dtype),
        grid=(n // tn, nblk),
        in_specs=in_specs,
        out_specs=pl.BlockSpec((RB, tn), lambda j, i: (i, j)),
        scratch_shapes=[pltpu.VMEM((k, tn), bf16)],
        compiler_params=_cp(("arbitrary", "arbitrary"), 48),
        name=name,
    )(*args)


def _gather_rows_kernel(tok_ref, h_hbm, o_ref, buf, sem, *, rows, nsteps):
    i = pl.program_id(0)

    def issue(blk, slot):
        base = blk * rows

        def body(r2, c):
            for q in range(2):
                r = r2 * 2 + q
                t = tok_ref[base + r]
                pltpu.make_async_copy(h_hbm.at[pl.ds(t, 1), :], buf.at[slot, pl.ds(r, 1), :],
                                      sem.at[slot]).start(priority=q)
            return c

        lax.fori_loop(0, rows // 2, body, 0, unroll=4)

    @pl.when(i == 0)
    def _():
        issue(0, 0)

    @pl.when(i + 1 < nsteps)
    def _():
        issue(i + 1, (i + 1) % 2)

    slot = i % 2
    pltpu.make_async_copy(h_hbm.at[pl.ds(0, rows), :], buf.at[slot], sem.at[slot]).wait()
    o_ref[...] = buf[slot].astype(o_ref.dtype)


def _gather_rows(slot_tok, h, nr):
    rows = 512
    nsteps = nr // rows
    return pl.pallas_call(
        functools.partial(_gather_rows_kernel, rows=rows, nsteps=nsteps),
        out_shape=jax.ShapeDtypeStruct((nr, D), bf16),
        grid_spec=pltpu.PrefetchScalarGridSpec(
            num_scalar_prefetch=1,
            grid=(nsteps,),
            in_specs=[pl.BlockSpec(memory_space=pl.ANY)],
            out_specs=pl.BlockSpec((rows, D), lambda i, tok: (i, 0)),
            scratch_shapes=[pltpu.VMEM((2, rows, D), f32), pltpu.SemaphoreType.DMA((2,))]),
        compiler_params=_cp(("arbitrary",)),
        name="moe_gather",
    )(slot_tok, h)


def _moe_up_kernel(start_ref, nblk_ref, xs_hbm, w_hbm, b_ref, p_ref, act_hbm,
                   wbf_ref, wf32, xbuf, obuf, wsem, xsem, osem, *, tn, layer):
    j = pl.program_id(0)
    e = pl.program_id(1)
    nb = nblk_ref[e]
    row0 = start_ref[e]
    g = j * N_EXPERTS + e

    def w_copy(gg, slot):
        c0 = pl.multiple_of((gg // N_EXPERTS) * tn, tn)
        return pltpu.make_async_copy(w_hbm.at[layer, gg % N_EXPERTS, :, pl.ds(c0, tn)], wf32.at[slot],
                                     wsem.at[slot])

    def x_copy(blk, slot):
        r = pl.multiple_of(row0 + blk * MOE_TM, MOE_TM)
        return pltpu.make_async_copy(xs_hbm.at[pl.ds(r, MOE_TM), :], xbuf.at[slot], xsem.at[slot])

    def o_copy(blk, slot):
        r = pl.multiple_of(row0 + blk * MOE_TM, MOE_TM)
        return pltpu.make_async_copy(obuf.at[slot], act_hbm.at[j, pl.ds(r, MOE_TM), :], osem.at[slot])

    @pl.when(g == 0)
    def _():
        w_copy(0, 0).start(priority=1)

    @pl.when(nb > 0)
    def _():
        x_copy(0, 0).start()

    @pl.when(g + 1 < pl.num_programs(0) * N_EXPERTS)
    def _():
        w_copy(g + 1, (g + 1) % 2).start(priority=1)

    w_copy(g, g % 2).wait()

    @pl.when(nb > 0)
    def _():
        wbf_ref[...] = wf32[g % 2].astype(bf16)

    def body(blk, c):
        slot = blk % 2
        x_copy(blk, slot).wait()

        @pl.when(blk + 1 < nb)
        def _():
            x_copy(blk + 1, 1 - slot).start()

        h = jnp.dot(xbuf[slot], wbf_ref[...], preferred_element_type=f32) + b_ref[...]

        @pl.when(blk >= 2)
        def _():
            o_copy(blk - 2, slot).wait()

        for cc in range(tn // 256):
            d = jnp.dot(h[:, cc * 256:(cc + 1) * 256].astype(bf16), p_ref[...],
                        preferred_element_type=f32)
            gate = jnp.minimum(d[:, :128], SWIGLU_LIMIT)
            lin = jnp.clip(d[:, 128:], -SWIGLU_LIMIT, SWIGLU_LIMIT)
            act = gate * jax.nn.sigmoid(SWIGLU_ALPHA * gate) * (lin + 1.0)
            obuf[slot, :, cc * 128:(cc + 1) * 128] = act.astype(obuf.dtype)
        o_copy(blk, slot).start()
        return c

    lax.fori_loop(0, nb, body, 0)

    @pl.when(nb >= 2)
    def _():
        o_copy(nb - 2, nb % 2).wait()

    @pl.when(nb >= 1)
    def _():
        o_copy(nb - 1, (nb - 1) % 2).wait()

    @pl.when(e == N_EXPERTS - 1)
    def _():
        obuf[0] = jnp.zeros(obuf.shape[1:], obuf.dtype)

        def zero_tail(blk, c):
            r = pl.multiple_of(blk * MOE_TM, MOE_TM)
            cp = pltpu.make_async_copy(obuf.at[0], act_hbm.at[j, pl.ds(r, MOE_TM), :], osem.at[0])
            cp.start()
            cp.wait()
            return c

        lax.fori_loop(row0 // MOE_TM + nb, act_hbm.shape[1] // MOE_TM, zero_tail, 0)


def _moe_up(start, nblk_e, xs, w1, b1, perm, layer, nr):
    tn = 2048
    nj = 2 * D_FF // tn
    return pl.pallas_call(
        functools.partial(_moe_up_kernel, tn=tn, layer=layer),
        out_shape=jax.ShapeDtypeStruct((nj, nr, tn // 2), bf16),
        grid_spec=pltpu.PrefetchScalarGridSpec(
            num_scalar_prefetch=2,
            grid=(nj, N_EXPERTS),
            in_specs=[pl.BlockSpec(memory_space=pl.ANY),
                      pl.BlockSpec(memory_space=pl.ANY),
                      pl.BlockSpec((None, None, 1, tn), lambda j, e, st, nb: (layer, e, 0, j)),
                      pl.BlockSpec((256, 256), lambda j, e, st, nb: (0, 0))],
            out_specs=pl.BlockSpec(memory_space=pl.ANY),
            scratch_shapes=[pltpu.VMEM((D, tn), bf16),
                            pltpu.VMEM((2, D, tn), f32),
                            pltpu.VMEM((2, MOE_TM, D), bf16),
                            pltpu.VMEM((2, MOE_TM, tn // 2), bf16),
                            pltpu.SemaphoreType.DMA((2,)), pltpu.SemaphoreType.DMA((2,)),
                            pltpu.SemaphoreType.DMA((2,))]),
        compiler_params=_cp(("arbitrary", "arbitrary"), 56),
        name="moe_up",
    )(start, nblk_e, xs, w1, b1, perm)


def _moe_down_kernel(start_ref, nblk_ref, act_hbm, w_hbm, b_ref, y_hbm, wbf_ref, wf32, abuf, obuf,
                     wsem, asem, osem, *, layer):
    e = pl.program_id(0)
    nb = nblk_ref[e]
    row0 = start_ref[e]
    nj = abuf.shape[1]
    kh = D_FF // nj

    def w_copy(ee, slot):
        return pltpu.make_async_copy(w_hbm.at[layer, ee], wf32.at[slot], wsem.at[slot])

    def a_copy(blk, slot, jj):
        r = pl.multiple_of(row0 + blk * MOE_TM, MOE_TM)
        return pltpu.make_async_copy(act_hbm.at[jj, pl.ds(r, MOE_TM), :], abuf.at[slot, jj], asem.at[slot])

    def o_copy(blk, slot):
        r = pl.multiple_of(row0 + blk * MOE_TM, MOE_TM)
        return pltpu.make_async_copy(obuf.at[slot], y_hbm.at[pl.ds(r, MOE_TM), :], osem.at[slot])

    @pl.when(e == 0)
    def _():
        w_copy(0, 0).start(priority=1)

    @pl.when(nb > 0)
    def _():
        for jj in range(nj):
            a_copy(0, 0, jj).start()

    @pl.when(e + 1 < N_EXPERTS)
    def _():
        w_copy(e + 1, (e + 1) % 2).start(priority=1)

    w_copy(e, e % 2).wait()

    @pl.when(nb > 0)
    def _():
        wbf_ref[...] = wf32[e % 2].astype(bf16)

    def body(blk, c):
        slot = blk % 2
        for jj in range(nj):
            a_copy(blk, slot, jj).wait()

        @pl.when(blk + 1 < nb)
        def _():
            for jj in range(nj):
                a_copy(blk + 1, 1 - slot, jj).start()

        y = b_ref[...] + jnp.dot(abuf[slot, 0], wbf_ref[0:kh, :], preferred_element_type=f32)
        for jj in range(1, nj):
            y = y + jnp.dot(abuf[slot, jj], wbf_ref[jj * kh:(jj + 1) * kh, :], preferred_element_type=f32)

        @pl.when(blk >= 2)
        def _():
            o_copy(blk - 2, slot).wait()

        obuf[slot] = y
        o_copy(blk, slot).start()
        return c

    lax.fori_loop(0, nb, body, 0)

    @pl.when(nb >= 2)
    def _():
        o_copy(nb - 2, nb % 2).wait()

    @pl.when(nb >= 1)
    def _():
        o_copy(nb - 1, (nb - 1) % 2).wait()

    @pl.when(e == N_EXPERTS - 1)
    def _():
        obuf[0] = jnp.zeros(obuf.shape[1:], obuf.dtype)

        def zero_tail(blk, c):
            r = pl.multiple_of(blk * MOE_TM, MOE_TM)
            cp = pltpu.make_async_copy(obuf.at[0], y_hbm.at[pl.ds(r, MOE_TM), :], osem.at[0])
            cp.start()
            cp.wait()
            return c

        lax.fori_loop(row0 // MOE_TM + nb, y_hbm.shape[0] // MOE_TM, zero_tail, 0)


def _moe_down(start, nblk_e, act, w2, b2, layer, nr):
    nj = act.shape[0]
    return pl.pallas_call(
        functools.partial(_moe_down_kernel, layer=layer),
        out_shape=jax.ShapeDtypeStruct((nr, D), f32),
        grid_spec=pltpu.PrefetchScalarGridSpec(
            num_scalar_prefetch=2,
            grid=(N_EXPERTS,),
            in_specs=[pl.BlockSpec(memory_space=pl.ANY),
                      pl.BlockSpec(memory_space=pl.ANY),
                      pl.BlockSpec((None, None, 1, D), lambda e, st, nb: (layer, e, 0, 0))],
            out_specs=pl.BlockSpec(memory_space=pl.ANY),
            scratch_shapes=[pltpu.VMEM((D_FF, D), bf16),
                            pltpu.VMEM((2, D_FF, D), f32),
                            pltpu.VMEM((2, nj, MOE_TM, D_FF // nj), bf16),
                            pltpu.VMEM((2, MOE_TM, D), f32),
                            pltpu.SemaphoreType.DMA((2,)), pltpu.SemaphoreType.DMA((2,)),
                            pltpu.SemaphoreType.DMA((2,))]),
        compiler_params=_cp(("arbitrary",), 56),
        name="moe_down",
    )(start, nblk_e, act, w2, b2)


def _combine_kernel(slot_ref, y_hbm, x_ref, gmod_ref, w_ref, o_ref, buf, sem, *, tm, nsteps):
    i = pl.program_id(0)

    def issue(blk, slot):
        base = blk * tm

        def body(r, c):
            for k in range(TOP_K):
                s = slot_ref[(base + r) * TOP_K + k]
                pltpu.make_async_copy(y_hbm.at[pl.ds(s, 1), :], buf.at[slot, k, pl.ds(r, 1), :],
                                      sem.at[slot]).start(priority=k % 2)
            return c

        lax.fori_loop(0, tm, body, 0, unroll=4)

    @pl.when(i == 0)
    def _():
        issue(0, 0)

    @pl.when(i + 1 < nsteps)
    def _():
        issue(i + 1, (i + 1) % 2)

    slot = i % 2
    for k in range(TOP_K):
        pltpu.make_async_copy(y_hbm.at[pl.ds(0, tm), :], buf.at[slot, k], sem.at[slot]).wait()
    w = w_ref[...]
    acc = ((w[:, 0:1] * buf[slot, 0] + w[:, 1:2] * buf[slot, 1])
           + (w[:, 2:3] * buf[slot, 2] + w[:, 3:4] * buf[slot, 3]))
    o_ref[...] = x_ref[...] + gmod_ref[...] * acc


def _combine(slot_flat, ys, x, mod3, gate, gate_chunk, blk, mrow, nblk):
    tm = RB
    return pl.pallas_call(
        functools.partial(_combine_kernel, tm=tm, nsteps=nblk),
        out_shape=jax.ShapeDtypeStruct((nblk * tm, D), f32),
        grid_spec=pltpu.PrefetchScalarGridSpec(
            num_scalar_prefetch=1,
            grid=(nblk,),
            in_specs=[pl.BlockSpec(memory_space=pl.ANY),
                      pl.BlockSpec((tm, D), lambda i, s: (blk(i), 0)),
                      pl.BlockSpec((None, 1, D), lambda i, s: (mrow(i), 0, gate_chunk)),
                      pl.BlockSpec((tm, TOP_K), lambda i, s: (i, 0))],
            out_specs=pl.BlockSpec((tm, D), lambda i, s: (i, 0)),
            scratch_shapes=[pltpu.VMEM((2, TOP_K, tm, D), f32), pltpu.SemaphoreType.DMA((2,))]),
        compiler_params=_cp(("arbitrary",)),
        name="moe_combine",
    )(slot_flat, ys, x, mod3, gate)


def _moe_layer(x, g3, mod3, rw, rb3, w1, b1, w2, b2, perm, layer, blk, mrow, nblk):
    t = nblk * RB
    h, idx, gate, rank, cnt = _norm_router(x, g3, mod3, rw, rb3, layer, 3, blk, mrow, nblk)
    n_assign = t * TOP_K
    nb = n_assign // MOE_TM + N_EXPERTS
    nr = nb * MOE_TM
    counts = cnt[0].astype(jnp.int32)
    padded = (counts + MOE_TM - 1) // MOE_TM * MOE_TM
    pad_end = jnp.cumsum(padded)
    pad_start = pad_end - padded
    slot = pad_start[idx] + rank
    slot_flat = slot.reshape(-1)
    tok = jnp.repeat(jnp.arange(t, dtype=jnp.int32), TOP_K)
    slot_tok = (jnp.arange(nr, dtype=jnp.int32) % t).at[slot_flat].set(tok)
    nblk_e = (padded // MOE_TM).astype(jnp.int32)
    xs = _gather_rows(slot_tok, h, nr)
    act = _moe_up(pad_start.astype(jnp.int32), nblk_e, xs, w1, b1, perm, layer, nr)
    ys = _moe_down(pad_start.astype(jnp.int32), nblk_e, act, w2, b2, layer, nr)
    return _combine(slot_flat, ys, x, mod3, gate, 5, blk, mrow, nblk)


def _dot_nt(a, b):
    return lax.dot_general(a, b, (((1,), (1,)), ((), ())), preferred_element_type=f32)


def _dot_tn(a, b):
    return lax.dot_general(a, b, (((0,), (0,)), ((), ())), preferred_element_type=f32)


def _mlstm_kernel(q_ref, k_ref, v_ref, o_ref, gc_ref, gr_ref, ng_ref, y_ref, hf_ref, hb_ref, ct_ref):
    T = A_T
    nch = LA // T
    ct_ref[...] = jnp.zeros_like(ct_ref)
    r_i = lax.broadcasted_iota(jnp.int32, (T, T), 0)
    c_i = lax.broadcasted_iota(jnp.int32, (T, T), 1)
    lane = lax.broadcasted_iota(jnp.int32, (T, 128), 1)
    ones_blk = jnp.where(lane == 0, 1.0, 0.0).astype(bf16)

    def chunk(c, dirn, h_ref):
        r0 = pl.multiple_of(c * T, T)
        q = q_ref[pl.ds(r0, T), :]
        k = k_ref[pl.ds(r0, T), :]
        v = v_ref[pl.ds(r0, T), :]
        gc = gc_ref[pl.ds(r0, T), :]
        o = 5 * dirn
        a_col, m_col, negm = gc[:, o:o + 1], gc[:, o + 1:o + 2], gc[:, o + 2:o + 3]
        m_end, m_prev = gc[:, o + 3:o + 4], gc[:, o + 4:o + 5]
        a_row = gr_ref[dirn:dirn + 1, pl.ds(r0, T)]
        mask = (c_i <= r_i) if dirn == 0 else (c_i >= r_i)
        w = jnp.exp(jnp.where(mask, a_row - m_col, -jnp.inf))
        s = _dot_nt(q, k) * (A_DQK ** -0.5) * w
        v_aug = jnp.concatenate([v, ones_blk], axis=1)
        ct = ct_ref[dirn]
        w_carry = jnp.exp(m_prev - m_col)
        num = (jnp.dot(s.astype(bf16), v_aug, preferred_element_type=f32)
               + w_carry * jnp.dot(q, ct.astype(bf16), preferred_element_type=f32))
        den = num[:, A_DV:A_DV + 1]
        h_ref[pl.ds(r0, T), :] = num[:, :A_DV] / jnp.maximum(jnp.abs(den), jnp.exp(negm))
        w_in = jnp.exp(a_col - m_end)
        decay = jnp.exp(m_prev[0:1, :] - m_end[0:1, :])
        kv = _dot_tn(k, (w_in * v_aug.astype(f32)).astype(bf16)) * (A_DQK ** -0.5)
        ct_ref[dirn] = decay * ct + kv

    def step(it, carry):
        chunk(it, 0, hf_ref)
        chunk(jnp.where(it == 0, 0, nch - it), 1, hb_ref)
        return carry

    lax.fori_loop(0, nch, step, 0)
    h = hf_ref[...] + hb_ref[...]
    hn = h * lax.rsqrt(jnp.mean(h * h, axis=-1, keepdims=True) + EPS) * ng_ref[...]
    y_ref[...] = (hn * jax.nn.sigmoid(o_ref[...].astype(f32))).astype(y_ref.dtype)


def _mlstm(proj, gcol, grow, ng3):
    qb, vb = A_DQK, A_DV
    return pl.pallas_call(
        _mlstm_kernel,
        out_shape=jax.ShapeDtypeStruct((ROWS, HALF), bf16),
        grid=(NB_, A_HEADS),
        in_specs=[pl.BlockSpec((LA, qb), lambda b, h: (b, h)),
                  pl.BlockSpec((LA, qb), lambda b, h: (b, A_HEADS + h)),
                  pl.BlockSpec((LA, vb), lambda b, h: (b, (2 * A_HEADS * qb) // vb + h)),
                  pl.BlockSpec((LA, vb), lambda b, h: (b, (2 * A_HEADS * qb + HALF) // vb + h)),
                  pl.BlockSpec((None, None, LA, 16), lambda b, h: (b, h, 0, 0)),
                  pl.BlockSpec((None, None, 8, LA), lambda b, h: (b, h, 0, 0)),
                  pl.BlockSpec((None, 1, vb), lambda b, h: (h, 0, 0))],
        out_specs=pl.BlockSpec((LA, vb), lambda b, h: (b, h)),
        scratch_shapes=[pltpu.VMEM((LA, vb), f32), pltpu.VMEM((LA, vb), f32),
                        pltpu.VMEM((2, qb, vb + 128), f32)],
        compiler_params=_cp(("arbitrary", "arbitrary")),
        name="mlstm",
    )(proj, proj, proj, proj, gcol, grow, ng3)


def _mlstm_gate_tables(gates):
    i_f, f_f, i_b, f_b = jnp.split(gates, 4, axis=-1)

    def scan_tables(i_pre, f_pre):
        lf = jax.nn.log_sigmoid(f_pre)
        big_f = jnp.cumsum(lf, axis=1)
        a = i_pre - big_f
        m_run = jnp.maximum(lax.cummax(a, axis=1), 0.0)
        m_end = jnp.repeat(m_run[:, A_T - 1::A_T], A_T, axis=1)
        m_prev = jnp.concatenate([jnp.zeros_like(m_end[:, :A_T]), m_end[:, :-A_T]], axis=1)
        return a, m_run, -(big_f + m_run), m_end, m_prev

    def to_bwd(t):
        return jnp.concatenate([jnp.flip(t[:, :CTX], axis=1), jnp.flip(t[:, CTX:], axis=1)], axis=1)

    fw = scan_tables(i_f, f_f)
    bw = [to_bwd(t) for t in scan_tables(to_bwd(i_b), to_bwd(f_b))]
    cols = jnp.stack(list(fw) + bw, axis=-1)
    cols = jnp.pad(cols, ((0, 0), (0, 0), (0, 0), (0, 6))).transpose(0, 2, 1, 3)
    rows = jnp.stack([fw[0], bw[0]], axis=1).transpose(0, 3, 1, 2)
    rows = jnp.pad(rows, ((0, 0), (0, 0), (0, 6), (0, 0)))
    return cols, rows


def _swap_halves(t):
    hd = t.shape[-1] // 2
    return jnp.concatenate([t[..., hd:], t[..., :hd]], axis=-1)


def _attn_kernel(q_ref, k_ref, v_ref, cos_ref, sin_ref, gq_ref, gk_ref, sink_ref, o_ref):
    n = pl.program_id(2)
    scale = B_DH ** -0.5
    nq = B_G * 128

    def norm(t, g_ref):
        return t * lax.rsqrt(jnp.mean(t * t, axis=-1, keepdims=True) + EPS) * g_ref[...]

    qn = norm(q_ref[...].astype(f32), gq_ref)
    kc = norm(k_ref[0:CTX, :].astype(f32), gk_ref).astype(bf16)
    vc = v_ref[0:CTX, :]
    sink = sink_ref[...]

    def finish(parts):
        m = sink
        for s, _ in parts:
            m = jnp.maximum(m, jnp.max(s, axis=-1, keepdims=True))
        den = jnp.exp(sink - m)
        acc = None
        for s, vv in parts:
            p = jnp.exp(s - m)
            den = den + jnp.sum(p, axis=-1, keepdims=True)
            pv = jnp.dot(p.astype(bf16), vv, preferred_element_type=f32)
            acc = pv if acc is None else acc + pv
        o_ref[...] = (acc / den).reshape(B_G, 128, B_DH).astype(o_ref.dtype)

    @pl.when(n < CTX // 128)
    def _():
        s_ctx = _dot_nt(qn.reshape(nq, B_DH).astype(bf16), kc) * scale
        finish([(s_ctx, vc)])

    @pl.when(n >= CTX // 128)
    def _():
        nl = n - CTX // 128
        pos0 = pl.multiple_of(nl * 128, 128)
        cq = cos_ref[pl.ds(pos0, 128), :]
        sq = sin_ref[pl.ds(pos0, 128), :]
        qr = qn * cq[None] + _swap_halves(qn) * sq[None]
        start = pl.multiple_of(jnp.clip((nl - 1) * 128, 0, SEQ - 3 * 128), 128)
        kb = norm(k_ref[pl.ds(CTX + start, 3 * 128), :].astype(f32), gk_ref)
        kb = kb * cos_ref[pl.ds(start, 3 * 128), :] + _swap_halves(kb) * sin_ref[pl.ds(start, 3 * 128), :]
        vb = v_ref[pl.ds(CTX + start, 3 * 128), :]
        s_band = _dot_nt(qr.reshape(nq, B_DH).astype(bf16), kb.astype(bf16)) * scale
        q_pos = pos0 + lax.broadcasted_iota(jnp.int32, (nq, 3 * 128), 0) % 128
        k_pos = start + lax.broadcasted_iota(jnp.int32, (nq, 3 * 128), 1)
        s_band = jnp.where(jnp.abs(q_pos - k_pos) <= WINDOW, s_band, -jnp.inf)
        s_ctx = _dot_nt(qn.reshape(nq, B_DH).astype(bf16), kc) * scale
        finish([(s_band, vb), (s_ctx, vc)])


def _attention(q5, k4, v4, cosf, sinf, gq, gk, sink_col):
    nblk = LA // 128
    return pl.pallas_call(
        _attn_kernel,
        out_shape=jax.ShapeDtypeStruct(q5.shape, bf16),
        grid=(NB_, B_KV, nblk),
        in_specs=[pl.BlockSpec((None, None, B_G, 128, B_DH), lambda b, kh, n: (b, kh, 0, n, 0)),
                  pl.BlockSpec((None, None, LA, B_DH), lambda b, kh, n: (b, kh, 0, 0)),
                  pl.BlockSpec((None, None, LA, B_DH), lambda b, kh, n: (b, kh, 0, 0)),
                  pl.BlockSpec((SEQ, B_DH), lambda b, kh, n: (0, 0)),
                  pl.BlockSpec((SEQ, B_DH), lambda b, kh, n: (0, 0)),
                  pl.BlockSpec((1, B_DH), lambda b, kh, n: (0, 0)),
                  pl.BlockSpec((1, B_DH), lambda b, kh, n: (0, 0)),
                  pl.BlockSpec((None, B_G * 128, 1), lambda b, kh, n: (kh, 0, 0))],
        out_specs=pl.BlockSpec((None, None, B_G, 128, B_DH), lambda b, kh, n: (b, kh, 0, n, 0)),
        compiler_params=_cp(("arbitrary", "arbitrary", "arbitrary")),
        name="window_attn",
    )(q5, k4, v4, cosf, sinf, gq, gk, sink_col)


def _rope_tables():
    rows = SEQ // GRID_W
    row = jnp.repeat(jnp.arange(rows), GRID_W).astype(f32)
    col = jnp.tile(jnp.arange(GRID_W), rows).astype(f32)
    n_freq = B_DH // 4
    inv = ROPE_BASE ** (-jnp.arange(n_freq, dtype=f32) / n_freq)
    ang = jnp.concatenate([row[:, None] * inv, col[:, None] * inv], axis=-1)
    cos, sin = jnp.cos(ang), jnp.sin(ang)
    return jnp.concatenate([cos, cos], axis=-1), jnp.concatenate([-sin, sin], axis=-1)


def _s5_kernel(u_ref, wy_hbm, wi_hbm, a_ref, y_ref, wy, wi, u32, ucat, zf, zb, sem):
    j = pl.program_id(0)
    T = S5_T
    cw = pltpu.make_async_copy(wy_hbm.at[j], wy, sem.at[0])
    ci = pltpu.make_async_copy(wi_hbm.at[j], wi, sem.at[1])
    cw.start()
    ci.start()
    u32[...] = u_ref[...].astype(f32)
    for s in range(T):
        for b in range(NB_):
            ucat[s, pl.ds(b, S5_CPB, stride=NB_), :] = u32[pl.ds(b * LA + s, S5_CPB, stride=T), :]
    ci.wait()
    uc = jnp.concatenate([ucat[s].astype(bf16) for s in range(T)], axis=1)
    zf[...] = jnp.dot(uc, wi[0], preferred_element_type=f32)
    zb[...] = jnp.dot(uc, wi[1], preferred_element_type=f32)
    a = a_ref[...]
    hs = S5_NS

    def cstep(av, x, z):
        ar, ai = av[:, :hs], av[:, hs:]
        xr, xi = x[:, :hs], x[:, hs:]
        return jnp.concatenate([ar * xr - ai * xi + z[:, :hs], ar * xi + ai * xr + z[:, hs:]], axis=1)

    def step(i, carry):
        x_f, x_b = carry
        rf = pl.multiple_of(i * 2 * NB_, 2 * NB_)
        mb = jnp.where(i < S5_CCTX // 2, S5_CCTX // 2 - 1 - i, S5_CPB // 2 + S5_CCTX // 2 - 1 - i)
        rb = pl.multiple_of(mb * 2 * NB_, 2 * NB_)
        z8 = zf[pl.ds(rf, 2 * NB_), :]
        x1 = cstep(a[0:1], x_f, z8[:NB_])
        zf[pl.ds(rf, 2 * NB_), :] = jnp.concatenate([x_f, x1], axis=0)
        x_f2 = cstep(a[0:1], x1, z8[NB_:])
        w8 = zb[pl.ds(rb, 2 * NB_), :]
        y1 = cstep(a[1:2], x_b, w8[NB_:])
        zb[pl.ds(rb, 2 * NB_), :] = jnp.concatenate([y1, x_b], axis=0)
        x_b2 = cstep(a[1:2], y1, w8[:NB_])
        return x_f2, x_b2

    zero = jnp.zeros((NB_, 2 * hs), f32)
    lax.fori_loop(0, S5_CPB // 2, step, (zero, zero))
    cw.wait()
    xf = zf[...].astype(bf16)
    xb = zb[...].astype(bf16)
    ku = T * 128
    for t in range(T):
        u32[pl.ds(t * S5_NCH, S5_NCH), :] = (
            jnp.dot(uc, wy[t, 0:ku, :], preferred_element_type=f32)
            + jnp.dot(xf, wy[t, ku:ku + 2 * hs, :], preferred_element_type=f32)
            + jnp.dot(xb, wy[t, ku + 2 * hs:ku + 4 * hs, :], preferred_element_type=f32))
    for t in range(T):
        for b in range(NB_):
            y_ref[pl.ds(b * LA + t, S5_CPB, stride=T), :] = u32[pl.ds(t * S5_NCH + b, S5_CPB, stride=NB_), :]


def _s5_core(u_all, mats):
    wy, wi, a = mats
    nt = HALF // 128
    kf = S5_T * 128 + 4 * S5_NS
    return pl.pallas_call(
        _s5_kernel,
        out_shape=jax.ShapeDtypeStruct((ROWS, HALF), f32),
        grid=(nt,),
        in_specs=[pl.BlockSpec((ROWS, 128), lambda j: (0, j)),
                  pl.BlockSpec(memory_space=pl.ANY),
                  pl.BlockSpec(memory_space=pl.ANY),
                  pl.BlockSpec((None, 2, 2 * S5_NS), lambda j: (j, 0, 0))],
        out_specs=pl.BlockSpec((ROWS, 128), lambda j: (0, j)),
        scratch_shapes=[pltpu.VMEM((S5_T, kf, 128), bf16),
                        pltpu.VMEM((2, S5_T * 128, 2 * S5_NS), bf16),
                        pltpu.VMEM((ROWS, 128), f32),
                        pltpu.VMEM((S5_T, S5_NCH, 128), f32),
                        pltpu.VMEM((S5_NCH, 2 * S5_NS), f32),
                        pltpu.VMEM((S5_NCH, 2 * S5_NS), f32),
                        pltpu.SemaphoreType.DMA((2,))],
        compiler_params=_cp(("arbitrary",), 56),
        name="s5_core",
    )(u_all, wy, wi, a)


def _s5_matrices(a_re, a_im, log_dt, b_re, b_im, c_re, c_im):
    T = S5_T
    G, P, C = C_GROUPS, C_STATE, C_GROUP
    nt = G // S5_GT
    lam = lax.complex(a_re.astype(f32), a_im.astype(f32))
    dt = jnp.exp(log_dt.astype(f32))[..., None]
    a_bar = jnp.exp(lam * dt)
    b_bar = ((a_bar - 1) / lam)[..., None] * lax.complex(b_re.astype(f32), b_im.astype(f32))
    tau = jnp.arange(T + 1, dtype=f32)
    apow = jnp.exp((lam * dt)[..., None] * tau)
    cc = lax.complex(c_re.astype(f32), c_im.astype(f32))
    kern = jnp.einsum("gxp,dgpt,dgpc->dgtxc", cc, apow[..., :T], b_bar).real
    t_i = jnp.arange(T)
    lag = t_i[:, None] - t_i[None, :]
    kf = kern[0][:, jnp.clip(lag, 0, T - 1)] * (lag >= 0)[None, :, :, None, None]
    kb = kern[1][:, jnp.clip(-lag, 0, T - 1)] * (lag <= 0)[None, :, :, None, None]
    eye = jnp.eye(S5_GT, dtype=f32)

    def expand(comp):
        out = comp[..., :, :, None, :] * eye[:, None, :, None]
        return out.reshape(comp.shape[:-3] + (S5_GT * comp.shape[-2], S5_GT * comp.shape[-1])).astype(bf16)

    km = (kf + kb).reshape(nt, S5_GT, T, T, C, C).transpose(0, 2, 3, 1, 5, 4)
    wy_u = expand(km).reshape(nt, T, T * 128, 128)
    k_f = cc[:, :, :, None] * apow[0][:, None, :, 1 + t_i]
    k_b = cc[:, :, :, None] * apow[1][:, None, :, T - t_i]

    def wy_state(k):
        k = k.reshape(nt, S5_GT, C, P, T).transpose(0, 4, 1, 3, 2)
        return jnp.concatenate([expand(k.real), expand(-k.imag)], axis=2)

    wy = jnp.concatenate([wy_u, wy_state(k_f), wy_state(k_b)], axis=2)
    w_f = apow[0][:, :, T - 1 - t_i][..., None] * b_bar[0][:, :, None, :]
    w_b = apow[1][:, :, t_i][..., None] * b_bar[1][:, :, None, :]

    def wi_dir(w):
        w = w.reshape(nt, S5_GT, P, T, C).transpose(0, 3, 1, 4, 2)
        return jnp.concatenate([expand(w.real).reshape(nt, T * 128, S5_NS),
                                expand(w.imag).reshape(nt, T * 128, S5_NS)], axis=2)

    wi = jnp.stack([wi_dir(w_f), wi_dir(w_b)], axis=1)
    at = apow[..., T].reshape(2, nt, S5_NS)
    a = jnp.concatenate([at.real, at.imag], axis=-1).transpose(1, 0, 2)
    return wy, wi, a


def _s5_glu_kernel(y_ref, u_ref, d_ref, w_ref, b_ref, o_ref, wbf_ref):
    @pl.when(pl.program_id(0) == 0)
    def _():
        wbf_ref[...] = w_ref[...].astype(bf16)

    y = y_ref[...] + d_ref[...] * u_ref[...].astype(f32)
    y = jax.nn.gelu(y, approximate=True)
    z = jnp.dot(y.astype(bf16), wbf_ref[...], preferred_element_type=f32) + b_ref[...]
    o_ref[...] = (y * jax.nn.sigmoid(z)).astype(o_ref.dtype)


def _s5_glu(y_ssm, proj, d2, glu_w, glu_b2):
    nblk = NB_ * LBPB
    return pl.pallas_call(
        _s5_glu_kernel,
        out_shape=jax.ShapeDtypeStruct((nblk * RB, HALF), bf16),
        grid=(nblk,),
        in_specs=[pl.BlockSpec((RB, HALF), lambda i: (_lat_blk(i), 0)),
                  pl.BlockSpec((RB, HALF), lambda i: (_lat_blk(i), 0)),
                  pl.BlockSpec((1, HALF), lambda i: (0, 0)),
                  pl.BlockSpec((HALF, HALF), lambda i: (0, 0)),
                  pl.BlockSpec((1, HALF), lambda i: (0, 0))],
        out_specs=pl.BlockSpec((RB, HALF), lambda i: (i, 0)),
        scratch_shapes=[pltpu.VMEM((HALF, HALF), bf16)],
        compiler_params=_cp(("arbitrary",)),
        name="s5_glu",
    )(y_ssm, proj, d2, glu_w, glu_b2)


def _hyena_filter(L, w1, b1, w2, b2, w3, freq):
    t = jnp.linspace(0.0, 1.0, L, dtype=f32)[:, None]
    w = 2 * math.pi * jnp.arange(L, dtype=f32)[:, None] / L
    bands = jnp.linspace(1e-4, HY_BANDS - 1, HY_BANDS, dtype=f32)
    feats = jnp.concatenate([t, jnp.cos(bands * w), -jnp.sin(bands * w)], axis=-1)
    h = jnp.sin(freq * (feats @ w1 + b1))
    h = jnp.sin(freq * (h @ w2 + b2))
    h = h @ w3
    fast = abs(math.log(HY_DECAY_TARGET) / HY_FAST_PCT)
    slow = abs(math.log(HY_DECAY_TARGET) / HY_SLOW_PCT)
    deltas = jnp.tile(jnp.linspace(slow, fast, HALF, dtype=f32), 2)
    h = h * jnp.exp(-t * deltas)
    h_fwd, h_bwd = jnp.split(h, 2, axis=-1)
    filt = jnp.concatenate([h_fwd, jnp.zeros((1, HALF), f32), h_bwd[:0:-1]], axis=0)
    return filt / jnp.sum(jnp.abs(filt), axis=0, keepdims=True)


def _hyena_dft_tables():
    n2 = 2 * SEQ
    n = jnp.arange(SEQ, dtype=jnp.int32)[:, None]
    k = jnp.arange(SEQ, dtype=jnp.int32)[None, :]
    kr = 64
    k1 = jnp.arange(SEQ // kr, dtype=jnp.int32)[None, :]
    k0 = jnp.arange(kr, dtype=jnp.int32)[None, :]
    ang_a = ((n * k1 * kr) % n2).astype(f32) * (2.0 * math.pi / n2)
    ang_b = ((n * k0) % n2).astype(f32) * (2.0 * math.pi / n2)
    ca, sa = jnp.cos(ang_a)[:, :, None], jnp.sin(ang_a)[:, :, None]
    cb, sb = jnp.cos(ang_b)[:, None, :], jnp.sin(ang_b)[:, None, :]
    c = (ca * cb - sa * sb).reshape(SEQ, SEQ)
    s = (sa * cb + ca * sb).reshape(SEQ, SEQ)
    nyq = jnp.where(n % 2 == 0, 1.0, -1.0).astype(f32)
    f_re, f_im = c, jnp.where(k == 0, nyq, -s)
    g_re = jnp.where(k == 0, 1.0, 2.0 * c) / n2
    g_im = jnp.where(k == 0, nyq, -2.0 * s) / n2

    def tiles(re, im):
        nk = SEQ // HY_KT
        return jnp.concatenate([re.reshape(SEQ, nk, 1, HY_KT), im.reshape(SEQ, nk, 1, HY_KT)],
                               axis=2).reshape(SEQ, 2 * SEQ).astype(bf16)

    return tiles(f_re, f_im), tiles(g_re, g_im)


def _hyena_kernel(x0_ref, x1_ref, v_ref, cw0_ref, cw1_ref, cw2_ref, cb0_ref, cb1_ref, cb2_ref, hb_ref,
                  f_ref, g_ref, ht_ref, o_ref, ut_scr, u_scr, y_scr):
    kt = pl.program_id(2)
    L = SEQ
    row = lax.broadcasted_iota(jnp.int32, (L, HY_TC), 0)

    def sconv(z_ref, cw_ref, cb_ref):
        z = z_ref[...].astype(f32)
        zm = jnp.where(row == 0, 0.0, pltpu.roll(z, 1, axis=0))
        zp = jnp.where(row == L - 1, 0.0, pltpu.roll(z, L - 1, axis=0))
        return zm * cw_ref[0:1, :] + z * cw_ref[1:2, :] + zp * cw_ref[2:3, :] + cb_ref[...]

    @pl.when(kt == 0)
    def _():
        u = sconv(x1_ref, cw1_ref, cb1_ref) * sconv(v_ref, cw2_ref, cb2_ref)
        u_scr[...] = u
        ut_scr[...] = u.T.astype(bf16)
        y_scr[...] = jnp.zeros_like(y_scr)

    xt = jnp.dot(ut_scr[...], f_ref[...], preferred_element_type=f32)
    h = ht_ref[...]
    xr, xi, hr, hi = xt[:, :HY_KT], xt[:, HY_KT:], h[:, :HY_KT], h[:, HY_KT:]
    col = lax.broadcasted_iota(jnp.int32, (HY_TC, HY_KT), 1) + kt * HY_KT
    dc = col == 0
    yr = xr * hr - jnp.where(dc, 0.0, xi * hi)
    yi = jnp.where(dc, xi * hi, xr * hi + xi * hr)
    yt = jnp.concatenate([yr, yi], axis=1).astype(bf16)
    y_scr[...] += _dot_nt(g_ref[...], yt)

    @pl.when(kt == pl.num_programs(2) - 1)
    def _():
        o_ref[...] = (sconv(x0_ref, cw0_ref, cb0_ref)
                      * (y_scr[...] + u_scr[...] * hb_ref[...])).astype(o_ref.dtype)


def _hyena(z_lat, conv_w, conv_b2, hy_bias2, ht):
    ft, gt = _hyena_dft_tables()
    nct = HALF // HY_TC

    def zspec(k):
        return pl.BlockSpec((SEQ, HY_TC), lambda ct, b, kt: (b, k * nct + ct))

    def wspec(k):
        return pl.BlockSpec((3, HY_TC), lambda ct, b, kt: (0, k * nct + ct))

    def bspec(k):
        return pl.BlockSpec((1, HY_TC), lambda ct, b, kt: (0, k * nct + ct))

    return pl.pallas_call(
        _hyena_kernel,
        out_shape=jax.ShapeDtypeStruct((NB_ * SEQ, HALF), bf16),
        grid=(nct, NB_, SEQ // HY_KT),
        in_specs=[zspec(0), zspec(1), zspec(2), wspec(0), wspec(1), wspec(2), bspec(0), bspec(1), bspec(2),
                  pl.BlockSpec((1, HY_TC), lambda ct, b, kt: (0, ct)),
                  pl.BlockSpec((SEQ, 2 * HY_KT), lambda ct, b, kt: (0, kt)),
                  pl.BlockSpec((SEQ, 2 * HY_KT), lambda ct, b, kt: (0, kt)),
                  pl.BlockSpec((HY_TC, 2 * HY_KT), lambda ct, b, kt: (ct, kt))],
        out_specs=pl.BlockSpec((SEQ, HY_TC), lambda ct, b, kt: (b, ct)),
        scratch_shapes=[pltpu.VMEM((HY_TC, SEQ), bf16), pltpu.VMEM((SEQ, HY_TC), f32),
                        pltpu.VMEM((SEQ, HY_TC), f32)],
        compiler_params=_cp(("arbitrary", "arbitrary", "arbitrary"), 48),
        name="hyena",
    )(z_lat, z_lat, z_lat, conv_w, conv_w, conv_w, conv_b2, conv_b2, conv_b2, hy_bias2, ft, gt, ht)


def _hyena_filter_spectrum(filt):
    hk = jnp.fft.rfft(filt.astype(f32), axis=0)
    re = hk.real[:SEQ].T
    im = hk.imag[:SEQ].T.at[:, 0].set(hk.real[SEQ])
    nk = SEQ // HY_KT
    c = re.shape[0]
    return jnp.concatenate([re.reshape(c, nk, 1, HY_KT), im.reshape(c, nk, 1, HY_KT)], axis=2).reshape(c, 2 * SEQ)


def _swiglu_perm():
    j = np.arange(256)
    src = np.where(j < 128, 2 * j, 2 * (j - 128) + 1)
    p = np.zeros((256, 256), np.float32)
    p[src, j] = 1.0
    return jnp.asarray(p, dtype=bf16)


def kernel(x, c, ctx, c_ctx, ada_w, ada_b, norm_mix_g, norm_ffn_g, ev_w_in, ev_gate_b, ev_h_norm_g,
           ev_q_norm_g, ev_k_norm_g, ev_sink, ev_w_out, od_w_in, od_a_re, od_a_im, od_log_dt, od_b_re,
           od_b_im, od_c_re, od_c_im, od_d, od_glu_w, od_glu_b, od_conv_w, od_conv_b, od_filt_w1,
           od_filt_b1, od_filt_w2, od_filt_b2, od_filt_w3, od_filt_freq, od_hy_bias, od_w_out,
           moe_router_w, moe_router_b, moe_w1, moe_b1, moe_w2, moe_b2):
    depth = ada_w.shape[0]
    xu = jnp.concatenate([ctx, x], axis=1).reshape(ROWS, D)
    cond8 = jnp.concatenate([c_ctx[None], c, jnp.zeros((3, D), f32)], axis=0)
    ada_b3 = ada_b.reshape(depth, 1, 6 * D)
    g_mix3 = norm_mix_g.reshape(depth, 1, D)
    g_ffn3 = norm_ffn_g.reshape(depth, 1, D)
    rb3 = moe_router_b.reshape(depth, 1, N_EXPERTS)
    b1_4 = moe_b1.reshape(depth, N_EXPERTS, 1, 2 * D_FF)
    b2_4 = moe_b2.reshape(depth, N_EXPERTS, 1, D)
    perm = _swiglu_perm()
    nall = ROWS // RB
    nlat = NB_ * LBPB

    mod3 = _ada(cond8, ada_w, ada_b3, 0).reshape(8, 1, 6 * D)
    h = _norm_mod(xu, g_mix3, mod3, 0, 0, _ident, _mod_row, nall, bf16)
    w_in = ev_w_in[0]
    n_a = 2 * A_HEADS * A_DQK + 2 * HALF
    n_g = 4 * A_HEADS
    w_main = jnp.concatenate([w_in[:, :n_a], w_in[:, n_a + n_g:]], axis=1)
    w_gate = jnp.pad(w_in[:, n_a:n_a + n_g], ((0, 0), (0, 128 - n_g)))
    proj = _matmul([h], w_main, (), w_main.shape[1], 1536, nall, _ident, bf16, name="even_in")
    gates = _matmul([h], w_gate, (), 128, 128, nall, _ident, f32, name="even_gates")[:, :n_g]
    gates = (gates + ev_gate_b[0]).reshape(NB_, LA, n_g)
    gcol, grow = _mlstm_gate_tables(gates)
    y_a = _mlstm(proj, gcol, grow, ev_h_norm_g[0].reshape(A_HEADS, 1, A_DV))
    qa = proj[:, n_a:n_a + HALF].reshape(NB_, LA, B_KV, B_G, B_DH).transpose(0, 2, 3, 1, 4)
    ka = proj[:, n_a + HALF:n_a + HALF + B_KV * B_DH].reshape(NB_, LA, B_KV, B_DH).transpose(0, 2, 1, 3)
    va = proj[:, n_a + HALF + B_KV * B_DH:].reshape(NB_, LA, B_KV, B_DH).transpose(0, 2, 1, 3)
    cosf, sinf = _rope_tables()
    sink_col = jnp.repeat(ev_sink[0].astype(f32).reshape(B_KV, B_G), 128, axis=1)[..., None]
    o5 = _attention(qa, ka, va, cosf, sinf, ev_q_norm_g[0].reshape(1, B_DH),
                    ev_k_norm_g[0].reshape(1, B_DH), sink_col)
    y_b = o5.transpose(0, 3, 1, 2, 4).reshape(ROWS, HALF)
    xu = _matmul([y_a, y_b], ev_w_out, (0,), D, 1024, nall, _ident, f32,
                 res=xu, res_blk=_ident, mrow=_mod_row, mod3=mod3, gate_chunk=2, name="even_out")
    xu = _moe_layer(xu, g_ffn3, mod3, moe_router_w, rb3, moe_w1, b1_4, moe_w2, b2_4, perm,
                    0, _ident, _mod_row, nall)

    mod3 = _ada(cond8, ada_w, ada_b3, 1).reshape(8, 1, 6 * D)
    h = _norm_mod(xu, g_mix3, mod3, 1, 0, _ident, _mod_row, nall, bf16)
    proj = _matmul([h], od_w_in, (0,), HALF, 1024, nall, _ident, bf16, name="odd_in_u")
    z_lat = _matmul([h], od_w_in, (0,), 3 * HALF, 1024, nlat, _lat_blk, bf16, w_col0=HALF // 1024,
                    name="odd_in_z")
    mats = _s5_matrices(od_a_re[0], od_a_im[0], od_log_dt[0], od_b_re[0], od_b_im[0], od_c_re[0], od_c_im[0])
    y_ssm = _s5_core(proj, mats)
    s_l = _s5_glu(y_ssm, proj, od_d[0].reshape(1, HALF), od_glu_w[0], od_glu_b[0].reshape(1, HALF))
    filt = _hyena_filter(SEQ, od_filt_w1[0], od_filt_b1[0], od_filt_w2[0], od_filt_b2[0], od_filt_w3[0],
                         od_filt_freq[0])
    hy_l = _hyena(z_lat, od_conv_w[0], od_conv_b[0].reshape(1, 3 * HALF), od_hy_bias[0].reshape(1, HALF),
                  _hyena_filter_spectrum(filt))
    x_lat = _matmul([s_l, hy_l], od_w_out, (0,), D, 1024, nlat, _ident, f32,
                    res=xu, res_blk=_lat_blk, mrow=_lat_mod_row, mod3=mod3, gate_chunk=2, name="odd_out")
    out = _moe_layer(x_lat, g_ffn3, mod3, moe_router_w, rb3, moe_w1, b1_4, moe_w2, b2_4, perm,
                     1, _ident, _lat_mod_row, nlat)
    return out.reshape(NB_, SEQ, D)
```

```python
import functools
import math

import jax
import jax.numpy as jnp
import numpy as np
from jax import lax
from jax.experimental import pallas as pl
from jax.experimental.pallas import tpu as pltpu

f32 = jnp.float32
bf16 = jnp.bfloat16

D = 2048
NB_ = 4
SEQ = 2048
CTX = 256
LA = SEQ + CTX
ROWS = NB_ * LA
RB = 256
BPB = LA // RB
LBPB = SEQ // RB
EPS = 1e-6
GRID_W = 64

HALF = D // 2
A_HEADS = 4
A_DV = HALF // A_HEADS
A_DQK = A_DV // 2
A_T = 256
B_HEADS = 16
B_KV = 4
B_G = B_HEADS // B_KV
B_DH = HALF // B_HEADS
B_KP = 2
ATT_COL0 = 2 * A_HEADS * A_DQK + 2 * HALF
WINDOW = 128
ROPE_BASE = 10000.0
C_GROUP = 16
C_GROUPS = HALF // C_GROUP
C_STATE = 64
S5_T = 8
S5_GT = 128 // C_GROUP
S5_NS = S5_GT * C_STATE
S5_NCH = ROWS // S5_T
S5_CPB = LA // S5_T
S5_CCTX = CTX // S5_T
HY_TC = 256
HY_KT = 512
HY_BANDS = 16
HY_DECAY_TARGET = 1e-2
HY_FAST_PCT = 0.3
HY_SLOW_PCT = 1.5
N_EXPERTS = 32
TOP_K = 4
D_FF = D
SWIGLU_LIMIT = 7.0
SWIGLU_ALPHA = 1.702
MOE_TM = 256

EVEN_SPLITS = (A_HEADS * A_DQK, A_HEADS * A_DQK, HALF, HALF, 4 * A_HEADS,
               B_HEADS * B_DH, B_KV * B_DH, B_KV * B_DH)

VMEM_MB = 1024 * 1024


def _cp(sem, vmem_mb=40):
    return pltpu.CompilerParams(dimension_semantics=sem, vmem_limit_bytes=vmem_mb * VMEM_MB)


def _lat_blk(i):
    return (i // LBPB) * BPB + 1 + i % LBPB


def _mod_row(u):
    return jnp.where(u % BPB == 0, 0, 1 + u // BPB)


def _lat_mod_row(i):
    return 1 + i // LBPB


def _ident(i):
    return i


def _ada_kernel(c_ref, w_ref, b_ref, o_ref):
    c = c_ref[...]
    s = c * jax.nn.sigmoid(c)
    o_ref[...] = jnp.dot(s.astype(bf16), w_ref[...].astype(bf16),
                         preferred_element_type=f32) + b_ref[...]


def _ada(cond8, ada_w, ada_b3, layer):
    tn = 1024
    n = 6 * D
    return pl.pallas_call(
        _ada_kernel,
        out_shape=jax.ShapeDtypeStruct((8, n), f32),
        grid=(n // tn,),
        in_specs=[pl.BlockSpec((8, D), lambda j: (0, 0)),
                  pl.BlockSpec((None, D, tn), lambda j: (layer, 0, j)),
                  pl.BlockSpec((None, 1, tn), lambda j: (layer, 0, j))],
        out_specs=pl.BlockSpec((8, tn), lambda j: (0, j)),
        compiler_params=_cp(("arbitrary",)),
        name="ada_params",
    )(cond8, ada_w, ada_b3)


def _norm_mod_kernel(x_ref, g_ref, sh_ref, sc_ref, o_ref):
    x = x_ref[...]
    y = x * lax.rsqrt(jnp.mean(x * x, axis=-1, keepdims=True) + EPS) * g_ref[...]
    o_ref[...] = (y * (1.0 + sc_ref[...]) + sh_ref[...]).astype(o_ref.dtype)


def _norm_mod(x, g3, mod3, layer, chunk, blk, mrow, nblk, out_dtype):
    return pl.pallas_call(
        _norm_mod_kernel,
        out_shape=jax.ShapeDtypeStruct((nblk * RB, D), out_dtype),
        grid=(nblk,),
        in_specs=[pl.BlockSpec((RB, D), lambda i: (blk(i), 0)),
                  pl.BlockSpec((None, 1, D), lambda i: (layer, 0, 0)),
                  pl.BlockSpec((None, 1, D), lambda i: (mrow(i), 0, chunk)),
                  pl.BlockSpec((None, 1, D), lambda i: (mrow(i), 0, chunk + 1))],
        out_specs=pl.BlockSpec((RB, D), lambda i: (i, 0)),
        compiler_params=_cp(("arbitrary",)),
        name="norm_mod",
    )(x, g3, mod3, mod3)


def _split_bf16(x):
    hi = x.astype(bf16)
    lo = (x - hi.astype(f32)).astype(bf16)
    return hi, lo


def _norm_router_kernel(x_ref, g_ref, sh_ref, sc_ref, rw_ref, rb_ref,
                        h_ref, idx_ref, gate_ref, rank_ref, cnt_ref, carry_ref):
    i = pl.program_id(0)

    @pl.when(i == 0)
    def _():
        carry_ref[...] = jnp.zeros_like(carry_ref)

    x = x_ref[...]
    y = x * lax.rsqrt(jnp.mean(x * x, axis=-1, keepdims=True) + EPS) * g_ref[...]
    h = y * (1.0 + sc_ref[...]) + sh_ref[...]
    h_ref[...] = h
    h_hi, h_lo = _split_bf16(h)
    w_hi, w_lo = _split_bf16(rw_ref[...])
    logits = (jnp.dot(h_hi, w_hi, preferred_element_type=f32)
              + jnp.dot(h_lo, w_hi, preferred_element_type=f32)
              + jnp.dot(h_hi, w_lo, preferred_element_type=f32)) + rb_ref[...]
    ne = logits.shape[-1]
    lane = lax.broadcasted_iota(jnp.int32, logits.shape, 1)
    l = logits
    vals, idxs = [], []
    for _ in range(TOP_K):
        m = jnp.max(l, axis=-1, keepdims=True)
        ix = jnp.min(jnp.where(l == m, lane, ne), axis=-1, keepdims=True)
        vals.append(m)
        idxs.append(ix)
        l = jnp.where(lane == ix, -jnp.inf, l)
    es = [jnp.exp(v - vals[0]) for v in vals]
    tot = es[0] + es[1] + es[2] + es[3]
    onehot = jnp.zeros(logits.shape, f32)
    for ix in idxs:
        onehot = onehot + jnp.where(lane == ix, 1.0, 0.0)
    r_i = lax.broadcasted_iota(jnp.int32, (RB, RB), 0)
    c_i = lax.broadcasted_iota(jnp.int32, (RB, RB), 1)
    tri = jnp.where(c_i < r_i, 1.0, 0.0).astype(bf16)
    prefix = jnp.dot(tri, onehot.astype(bf16), preferred_element_type=f32) + carry_ref[...]
    for k in range(TOP_K):
        idx_ref[:, k:k + 1] = idxs[k]
        gate_ref[:, k:k + 1] = es[k] / tot
        rk = jnp.sum(jnp.where(lane == idxs[k], prefix, 0.0), axis=-1, keepdims=True)
        rank_ref[:, k:k + 1] = rk.astype(jnp.int32)
    carry_ref[...] = carry_ref[...] + jnp.sum(onehot, axis=0, keepdims=True)
    cnt_ref[...] = carry_ref[...]


def _norm_router(x, g3, mod3, rw, rb3, layer, chunk, blk, mrow, nblk):
    t = nblk * RB
    return pl.pallas_call(
        _norm_router_kernel,
        out_shape=(jax.ShapeDtypeStruct((t, D), f32),
                   jax.ShapeDtypeStruct((t, TOP_K), jnp.int32),
                   jax.ShapeDtypeStruct((t, TOP_K), f32),
                   jax.ShapeDtypeStruct((t, TOP_K), jnp.int32),
                   jax.ShapeDtypeStruct((1, N_EXPERTS), f32)),
        grid=(nblk,),
        in_specs=[pl.BlockSpec((RB, D), lambda i: (blk(i), 0)),
                  pl.BlockSpec((None, 1, D), lambda i: (layer, 0, 0)),
                  pl.BlockSpec((None, 1, D), lambda i: (mrow(i), 0, chunk)),
                  pl.BlockSpec((None, 1, D), lambda i: (mrow(i), 0, chunk + 1)),
                  pl.BlockSpec((None, D, N_EXPERTS), lambda i: (layer, 0, 0)),
                  pl.BlockSpec((None, 1, N_EXPERTS), lambda i: (layer, 0, 0))],
        out_specs=(pl.BlockSpec((RB, D), lambda i: (i, 0)),
                   pl.BlockSpec((RB, TOP_K), lambda i: (i, 0)),
                   pl.BlockSpec((RB, TOP_K), lambda i: (i, 0)),
                   pl.BlockSpec((RB, TOP_K), lambda i: (i, 0)),
                   pl.BlockSpec((1, N_EXPERTS), lambda i: (0, 0))),
        scratch_shapes=[pltpu.VMEM((1, N_EXPERTS), f32)],
        compiler_params=_cp(("arbitrary",)),
        name="norm_router",
    )(x, g3, mod3, mod3, rw, rb3)


def _mm_kernel(*refs, n_a, k_split, epilogue):
    a_refs = refs[:n_a]
    w_ref = refs[n_a]
    pos = n_a + 1
    if epilogue:
        res_ref, gate_ref = refs[pos], refs[pos + 1]
        pos += 2
    o_ref, wbf_ref = refs[pos], refs[pos + 1]

    @pl.when(pl.program_id(1) == 0)
    def _():
        wbf_ref[...] = w_ref[...].astype(bf16)

    acc = None
    for t, a_ref in enumerate(a_refs):
        part = jnp.dot(a_ref[...], wbf_ref[t * k_split:(t + 1) * k_split, :],
                       preferred_element_type=f32)
        acc = part if acc is None else acc + part
    if epilogue:
        acc = res_ref[...] + gate_ref[...] * acc
    o_ref[...] = acc.astype(o_ref.dtype)


def _matmul(a_list, w, w_idx, n, tn, nblk, a_blk, out_dtype, *, a_col=None, w_col0=0,
            res=None, res_blk=None, mrow=None, mod3=None, gate_chunk=None, name="matmul"):
    n_a = len(a_list)
    k = w.shape[-2]
    k_split = k // n_a
    lead = len(w_idx)
    a_col = a_col or [0] * n_a
    in_specs = [pl.BlockSpec((RB, k_split), functools.partial(lambda j, i, c: (a_blk(i), c), c=c))
                for c in a_col]
    in_specs.append(pl.BlockSpec((None,) * lead + (k, tn), lambda j, i: tuple(w_idx) + (0, j + w_col0)))
    args = list(a_list) + [w]
    epilogue = res is not None
    if epilogue:
        per = D // tn
        in_specs.append(pl.BlockSpec((RB, tn), lambda j, i: (res_blk(i), j)))
        in_specs.append(pl.BlockSpec((None, 1, tn),
                                     lambda j, i: (mrow(i), 0, gate_chunk * per + j)))
        args += [res, mod3]
    return pl.pallas_call(
        functools.partial(_mm_kernel, n_a=n_a, k_split=k_split, epilogue=epilogue),
        out_shape=jax.ShapeDtypeStruct((nblk * RB, n), out_dtype),
        grid=(n // tn, nblk),
        in_specs=in_specs,
        out_specs=pl.BlockSpec((RB, tn), lambda j, i: (i, j)),
        scratch_shapes=[pltpu.VMEM((k, tn), bf16)],
        compiler_params=_cp(("arbitrary", "arbitrary"), 48),
        name=name,
    )(*args)


def _gather_rows_kernel(tok_ref, h_hbm, o_ref, buf, sem, *, rows, nsteps):
    i = pl.program_id(0)

    def issue(blk, slot):
        base = blk * rows

        def body(r2, c):
            for q in range(2):
                r = r2 * 2 + q
                t = tok_ref[base + r]
                pltpu.make_async_copy(h_hbm.at[pl.ds(t, 1), :], buf.at[slot, pl.ds(r, 1), :],
                                      sem.at[slot]).start(priority=q)
            return c

        lax.fori_loop(0, rows // 2, body, 0, unroll=4)

    @pl.when(i == 0)
    def _():
        issue(0, 0)

    @pl.when(i + 1 < nsteps)
    def _():
        issue(i + 1, (i + 1) % 2)

    slot = i % 2
    pltpu.make_async_copy(h_hbm.at[pl.ds(0, rows), :], buf.at[slot], sem.at[slot]).wait()
    o_ref[...] = buf[slot].astype(o_ref.dtype)


def _gather_rows(slot_tok, h, nr):
    rows = 512
    nsteps = nr // rows
    return pl.pallas_call(
        functools.partial(_gather_rows_kernel, rows=rows, nsteps=nsteps),
        out_shape=jax.ShapeDtypeStruct((nr, D), bf16),
        grid_spec=pltpu.PrefetchScalarGridSpec(
            num_scalar_prefetch=1,
            grid=(nsteps,),
            in_specs=[pl.BlockSpec(memory_space=pl.ANY)],
            out_specs=pl.BlockSpec((rows, D), lambda i, tok: (i, 0)),
            scratch_shapes=[pltpu.VMEM((2, rows, D), f32), pltpu.SemaphoreType.DMA((2,))]),
        compiler_params=_cp(("arbitrary",)),
        name="moe_gather",
    )(slot_tok, h)


def _moe_up_kernel(start_ref, nblk_ref, xs_hbm, w_hbm, b_ref, p_ref, act_hbm,
                   wbf_ref, wf32, xbuf, obuf, wsem, xsem, osem, *, tn, layer):
    j = pl.program_id(0)
    e = pl.program_id(1)
    nb = nblk_ref[e]
    row0 = start_ref[e]
    g = j * N_EXPERTS + e

    def w_copy(gg, slot):
        c0 = pl.multiple_of((gg // N_EXPERTS) * tn, tn)
        return pltpu.make_async_copy(w_hbm.at[layer, gg % N_EXPERTS, :, pl.ds(c0, tn)], wf32.at[slot],
                                     wsem.at[slot])

    def x_copy(blk, slot):
        r = pl.multiple_of(row0 + blk * MOE_TM, MOE_TM)
        return pltpu.make_async_copy(xs_hbm.at[pl.ds(r, MOE_TM), :], xbuf.at[slot], xsem.at[slot])

    def o_copy(blk, slot):
        r = pl.multiple_of(row0 + blk * MOE_TM, MOE_TM)
        return pltpu.make_async_copy(obuf.at[slot], act_hbm.at[j, pl.ds(r, MOE_TM), :], osem.at[slot])

    @pl.when(g == 0)
    def _():
        w_copy(0, 0).start(priority=1)

    @pl.when(nb > 0)
    def _():
        x_copy(0, 0).start()

    @pl.when(g + 1 < pl.num_programs(0) * N_EXPERTS)
    def _():
        w_copy(g + 1, (g + 1) % 2).start(priority=1)

    w_copy(g, g % 2).wait()

    @pl.when(nb > 0)
    def _():
        wbf_ref[...] = wf32[g % 2].astype(bf16)

    def body(blk, c):
        slot = blk % 2
        x_copy(blk, slot).wait()

        @pl.when(blk + 1 < nb)
        def _():
            x_copy(blk + 1, 1 - slot).start()

        h = jnp.dot(xbuf[slot], wbf_ref[...], preferred_element_type=f32) + b_ref[...]

        @pl.when(blk >= 2)
        def _():
            o_copy(blk - 2, slot).wait()

        for cc in range(tn // 256):
            d = jnp.dot(h[:, cc * 256:(cc + 1) * 256].astype(bf16), p_ref[...],
                        preferred_element_type=f32)
            gate = jnp.minimum(d[:, :128], SWIGLU_LIMIT)
            lin = jnp.clip(d[:, 128:], -SWIGLU_LIMIT, SWIGLU_LIMIT)
            act = gate * jax.nn.sigmoid(SWIGLU_ALPHA * gate) * (lin + 1.0)
            obuf[slot, :, cc * 128:(cc + 1) * 128] = act.astype(obuf.dtype)
        o_copy(blk, slot).start()
        return c

    lax.fori_loop(0, nb, body, 0)

    @pl.when(nb >= 2)
    def _():
        o_copy(nb - 2, nb % 2).wait()

    @pl.when(nb >= 1)
    def _():
        o_copy(nb - 1, (nb - 1) % 2).wait()

    @pl.when(e == N_EXPERTS - 1)
    def _():
        obuf[0] = jnp.zeros(obuf.shape[1:], obuf.dtype)

        def zero_tail(blk, c):
            r = pl.multiple_of(blk * MOE_TM, MOE_TM)
            cp = pltpu.make_async_copy(obuf.at[0], act_hbm.at[j, pl.ds(r, MOE_TM), :], osem.at[0])
            cp.start()
            cp.wait()
            return c

        lax.fori_loop(row0 // MOE_TM + nb, act_hbm.shape[1] // MOE_TM, zero_tail, 0)


def _moe_up(start, nblk_e, xs, w1, b1, perm, layer, nr):
    tn = 2048
    nj = 2 * D_FF // tn
    return pl.pallas_call(
        functools.partial(_moe_up_kernel, tn=tn, layer=layer),
        out_shape=jax.ShapeDtypeStruct((nj, nr, tn // 2), bf16),
        grid_spec=pltpu.PrefetchScalarGridSpec(
            num_scalar_prefetch=2,
            grid=(nj, N_EXPERTS),
            in_specs=[pl.BlockSpec(memory_space=pl.ANY),
                      pl.BlockSpec(memory_space=pl.ANY),
                      pl.BlockSpec((None, None, 1, tn), lambda j, e, st, nb: (layer, e, 0, j)),
                      pl.BlockSpec((256, 256), lambda j, e, st, nb: (0, 0))],
            out_specs=pl.BlockSpec(memory_space=pl.ANY),
            scratch_shapes=[pltpu.VMEM((D, tn), bf16),
                            pltpu.VMEM((2, D, tn), f32),
                            pltpu.VMEM((2, MOE_TM, D), bf16),
                            pltpu.VMEM((2, MOE_TM, tn // 2), bf16),
                            pltpu.SemaphoreType.DMA((2,)), pltpu.SemaphoreType.DMA((2,)),
                            pltpu.SemaphoreType.DMA((2,))]),
        compiler_params=_cp(("arbitrary", "arbitrary"), 56),
        name="moe_up",
    )(start, nblk_e, xs, w1, b1, perm)


def _moe_down_kernel(start_ref, nblk_ref, act_hbm, w_hbm, b_ref, y_hbm, wbf_ref, wf32, abuf, obuf,
                     wsem, asem, osem, *, layer):
    e = pl.program_id(0)
    nb = nblk_ref[e]
    row0 = start_ref[e]
    nj = abuf.shape[1]
    kh = D_FF // nj

    def w_copy(ee, slot):
        return pltpu.make_async_copy(w_hbm.at[layer, ee], wf32.at[slot], wsem.at[slot])

    def a_copy(blk, slot, jj):
        r = pl.multiple_of(row0 + blk * MOE_TM, MOE_TM)
        return pltpu.make_async_copy(act_hbm.at[jj, pl.ds(r, MOE_TM), :], abuf.at[slot, jj], asem.at[slot])

    def o_copy(blk, slot):
        r = pl.multiple_of(row0 + blk * MOE_TM, MOE_TM)
        return pltpu.make_async_copy(obuf.at[slot], y_hbm.at[pl.ds(r, MOE_TM), :], osem.at[slot])

    @pl.when(e == 0)
    def _():
        w_copy(0, 0).start(priority=1)

    @pl.when(nb > 0)
    def _():
        for jj in range(nj):
            a_copy(0, 0, jj).start()

    @pl.when(e + 1 < N_EXPERTS)
    def _():
        w_copy(e + 1, (e + 1) % 2).start(priority=1)

    w_copy(e, e % 2).wait()

    @pl.when(nb > 0)
    def _():
        wbf_ref[...] = wf32[e % 2].astype(bf16)

    def body(blk, c):
        slot = blk % 2
        for jj in range(nj):
            a_copy(blk, slot, jj).wait()

        @pl.when(blk + 1 < nb)
        def _():
            for jj in range(nj):
                a_copy(blk + 1, 1 - slot, jj).start()

        y = b_ref[...] + jnp.dot(abuf[slot, 0], wbf_ref[0:kh, :], preferred_element_type=f32)
        for jj in range(1, nj):
            y = y + jnp.dot(abuf[slot, jj], wbf_ref[jj * kh:(jj + 1) * kh, :], preferred_element_type=f32)

        @pl.when(blk >= 2)
        def _():
            o_copy(blk - 2, slot).wait()

        obuf[slot] = y
        o_copy(blk, slot).start()
        return c

    lax.fori_loop(0, nb, body, 0)

    @pl.when(nb >= 2)
    def _():
        o_copy(nb - 2, nb % 2).wait()

    @pl.when(nb >= 1)
    def _():
        o_copy(nb - 1, (nb - 1) % 2).wait()

    @pl.when(e == N_EXPERTS - 1)
    def _():
        obuf[0] = jnp.zeros(obuf.shape[1:], obuf.dtype)

        def zero_tail(blk, c):
            r = pl.multiple_of(blk * MOE_TM, MOE_TM)
            cp = pltpu.make_async_copy(obuf.at[0], y_hbm.at[pl.ds(r, MOE_TM), :], osem.at[0])
            cp.start()
            cp.wait()
            return c

        lax.fori_loop(row0 // MOE_TM + nb, y_hbm.shape[0] // MOE_TM, zero_tail, 0)


def _moe_down(start, nblk_e, act, w2, b2, layer, nr):
    nj = act.shape[0]
    return pl.pallas_call(
        functools.partial(_moe_down_kernel, layer=layer),
        out_shape=jax.ShapeDtypeStruct((nr, D), f32),
        grid_spec=pltpu.PrefetchScalarGridSpec(
            num_scalar_prefetch=2,
            grid=(N_EXPERTS,),
            in_specs=[pl.BlockSpec(memory_space=pl.ANY),
                      pl.BlockSpec(memory_space=pl.ANY),
                      pl.BlockSpec((None, None, 1, D), lambda e, st, nb: (layer, e, 0, 0))],
            out_specs=pl.BlockSpec(memory_space=pl.ANY),
            scratch_shapes=[pltpu.VMEM((D_FF, D), bf16),
                            pltpu.VMEM((2, D_FF, D), f32),
                            pltpu.VMEM((2, nj, MOE_TM, D_FF // nj), bf16),
                            pltpu.VMEM((2, MOE_TM, D), f32),
                            pltpu.SemaphoreType.DMA((2,)), pltpu.SemaphoreType.DMA((2,)),
                            pltpu.SemaphoreType.DMA((2,))]),
        compiler_params=_cp(("arbitrary",), 56),
        name="moe_down",
    )(start, nblk_e, act, w2, b2)


def _combine_kernel(slot_ref, y_hbm, x_ref, gmod_ref, w_ref, o_ref, buf, sem, *, tm, nsteps):
    i = pl.program_id(0)

    def issue(blk, slot):
        base = blk * tm

        def body(r, c):
            for k in range(TOP_K):
                s = slot_ref[(base + r) * TOP_K + k]
                pltpu.make_async_copy(y_hbm.at[pl.ds(s, 1), :], buf.at[slot, k, pl.ds(r, 1), :],
                                      sem.at[slot]).start(priority=k % 2)
            return c

        lax.fori_loop(0, tm, body, 0, unroll=4)

    @pl.when(i == 0)
    def _():
        issue(0, 0)

    @pl.when(i + 1 < nsteps)
    def _():
        issue(i + 1, (i + 1) % 2)

    slot = i % 2
    for k in range(TOP_K):
        pltpu.make_async_copy(y_hbm.at[pl.ds(0, tm), :], buf.at[slot, k], sem.at[slot]).wait()
    w = w_ref[...]
    acc = ((w[:, 0:1] * buf[slot, 0] + w[:, 1:2] * buf[slot, 1])
           + (w[:, 2:3] * buf[slot, 2] + w[:, 3:4] * buf[slot, 3]))
    o_ref[...] = x_ref[...] + gmod_ref[...] * acc


def _combine(slot_flat, ys, x, mod3, gate, gate_chunk, blk, mrow, nblk):
    tm = RB
    return pl.pallas_call(
        functools.partial(_combine_kernel, tm=tm, nsteps=nblk),
        out_shape=jax.ShapeDtypeStruct((nblk * tm, D), f32),
        grid_spec=pltpu.PrefetchScalarGridSpec(
            num_scalar_prefetch=1,
            grid=(nblk,),
            in_specs=[pl.BlockSpec(memory_space=pl.ANY),
                      pl.BlockSpec((tm, D), lambda i, s: (blk(i), 0)),
                      pl.BlockSpec((None, 1, D), lambda i, s: (mrow(i), 0, gate_chunk)),
                      pl.BlockSpec((tm, TOP_K), lambda i, s: (i, 0))],
            out_specs=pl.BlockSpec((tm, D), lambda i, s: (i, 0)),
            scratch_shapes=[pltpu.VMEM((2, TOP_K, tm, D), f32), pltpu.SemaphoreType.DMA((2,))]),
        compiler_params=_cp(("arbitrary",)),
        name="moe_combine",
    )(slot_flat, ys, x, mod3, gate)


def _moe_layer(x, g3, mod3, rw, rb3, w1, b1, w2, b2, perm, layer, blk, mrow, nblk):
    t = nblk * RB
    h, idx, gate, rank, cnt = _norm_router(x, g3, mod3, rw, rb3, layer, 3, blk, mrow, nblk)
    n_assign = t * TOP_K
    nb = n_assign // MOE_TM + N_EXPERTS
    nr = nb * MOE_TM
    counts = cnt[0].astype(jnp.int32)
    padded = (counts + MOE_TM - 1) // MOE_TM * MOE_TM
    pad_end = jnp.cumsum(padded)
    pad_start = pad_end - padded
    slot = pad_start[idx] + rank
    slot_flat = slot.reshape(-1)
    tok = jnp.repeat(jnp.arange(t, dtype=jnp.int32), TOP_K)
    slot_tok = (jnp.arange(nr, dtype=jnp.int32) % t).at[slot_flat].set(tok)
    nblk_e = (padded // MOE_TM).astype(jnp.int32)
    xs = _gather_rows(slot_tok, h, nr)
    act = _moe_up(pad_start.astype(jnp.int32), nblk_e, xs, w1, b1, perm, layer, nr)
    ys = _moe_down(pad_start.astype(jnp.int32), nblk_e, act, w2, b2, layer, nr)
    return _combine(slot_flat, ys, x, mod3, gate, 5, blk, mrow, nblk)


def _dot_nt(a, b):
    return lax.dot_general(a, b, (((1,), (1,)), ((), ())), preferred_element_type=f32)


def _dot_tn(a, b):
    return lax.dot_general(a, b, (((0,), (0,)), ((), ())), preferred_element_type=f32)


def _mlstm_kernel(q_ref, k_ref, v_ref, o_ref, gc_ref, gr_ref, ng_ref, y_ref, hf_ref, hb_ref, ct_ref):
    T = A_T
    nch = LA // T
    ct_ref[...] = jnp.zeros_like(ct_ref)
    r_i = lax.broadcasted_iota(jnp.int32, (T, T), 0)
    c_i = lax.broadcasted_iota(jnp.int32, (T, T), 1)
    lane = lax.broadcasted_iota(jnp.int32, (T, 128), 1)
    ones_blk = jnp.where(lane == 0, 1.0, 0.0).astype(bf16)

    def chunk(c, dirn, h_ref):
        r0 = pl.multiple_of(c * T, T)
        q = q_ref[pl.ds(r0, T), :]
        k = k_ref[pl.ds(r0, T), :]
        v = v_ref[pl.ds(r0, T), :]
        gc = gc_ref[pl.ds(r0, T), :]
        o = 5 * dirn
        a_col, m_col, negm = gc[:, o:o + 1], gc[:, o + 1:o + 2], gc[:, o + 2:o + 3]
        m_end, m_prev = gc[:, o + 3:o + 4], gc[:, o + 4:o + 5]
        a_row = gr_ref[dirn:dirn + 1, pl.ds(r0, T)]
        mask = (c_i <= r_i) if dirn == 0 else (c_i >= r_i)
        w = jnp.exp(jnp.where(mask, a_row - m_col, -jnp.inf))
        s = _dot_nt(q, k) * (A_DQK ** -0.5) * w
        v_aug = jnp.concatenate([v, ones_blk], axis=1)
        ct = ct_ref[dirn]
        w_carry = jnp.exp(m_prev - m_col)
        num = (jnp.dot(s.astype(bf16), v_aug, preferred_element_type=f32)
               + w_carry * jnp.dot(q, ct.astype(bf16), preferred_element_type=f32))
        den = num[:, A_DV:A_DV + 1]
        h_ref[pl.ds(r0, T), :] = num[:, :A_DV] / jnp.maximum(jnp.abs(den), jnp.exp(negm))
        w_in = jnp.exp(a_col - m_end)
        decay = jnp.exp(m_prev[0:1, :] - m_end[0:1, :])
        kv = _dot_tn(k, (w_in * v_aug.astype(f32)).astype(bf16)) * (A_DQK ** -0.5)
        ct_ref[dirn] = decay * ct + kv

    def step(it, carry):
        chunk(it, 0, hf_ref)
        chunk(jnp.where(it == 0, 0, nch - it), 1, hb_ref)
        return carry

    lax.fori_loop(0, nch, step, 0)
    h = hf_ref[...] + hb_ref[...]
    hn = h * lax.rsqrt(jnp.mean(h * h, axis=-1, keepdims=True) + EPS) * ng_ref[...]
    y_ref[...] = (hn * jax.nn.sigmoid(o_ref[...].astype(f32))).astype(y_ref.dtype)


def _mlstm(proj, gcol, grow, ng3):
    qb, vb = A_DQK, A_DV
    return pl.pallas_call(
        _mlstm_kernel,
        out_shape=jax.ShapeDtypeStruct((ROWS, HALF), bf16),
        grid=(NB_, A_HEADS),
        in_specs=[pl.BlockSpec((LA, qb), lambda b, h: (b, h)),
                  pl.BlockSpec((LA, qb), lambda b, h: (b, A_HEADS + h)),
                  pl.BlockSpec((LA, vb), lambda b, h: (b, (2 * A_HEADS * qb) // vb + h)),
                  pl.BlockSpec((LA, vb), lambda b, h: (b, (2 * A_HEADS * qb + HALF) // vb + h)),
                  pl.BlockSpec((None, None, LA, 16), lambda b, h: (b, h, 0, 0)),
                  pl.BlockSpec((None, None, 8, LA), lambda b, h: (b, h, 0, 0)),
                  pl.BlockSpec((None, 1, vb), lambda b, h: (h, 0, 0))],
        out_specs=pl.BlockSpec((LA, vb), lambda b, h: (b, h)),
        scratch_shapes=[pltpu.VMEM((LA, vb), f32), pltpu.VMEM((LA, vb), f32),
                        pltpu.VMEM((2, qb, vb + 128), f32)],
        compiler_params=_cp(("arbitrary", "arbitrary")),
        name="mlstm",
    )(proj, proj, proj, proj, gcol, grow, ng3)


def _mlstm_gate_tables(gates):
    i_f, f_f, i_b, f_b = jnp.split(gates, 4, axis=-1)

    def scan_tables(i_pre, f_pre):
        lf = jax.nn.log_sigmoid(f_pre)
        big_f = jnp.cumsum(lf, axis=1)
        a = i_pre - big_f
        m_run = jnp.maximum(lax.cummax(a, axis=1), 0.0)
        m_end = jnp.repeat(m_run[:, A_T - 1::A_T], A_T, axis=1)
        m_prev = jnp.concatenate([jnp.zeros_like(m_end[:, :A_T]), m_end[:, :-A_T]], axis=1)
        return a, m_run, -(big_f + m_run), m_end, m_prev

    def to_bwd(t):
        return jnp.concatenate([jnp.flip(t[:, :CTX], axis=1), jnp.flip(t[:, CTX:], axis=1)], axis=1)

    fw = scan_tables(i_f, f_f)
    bw = [to_bwd(t) for t in scan_tables(to_bwd(i_b), to_bwd(f_b))]
    cols = jnp.stack(list(fw) + bw, axis=-1)
    cols = jnp.pad(cols, ((0, 0), (0, 0), (0, 0), (0, 6))).transpose(0, 2, 1, 3)
    rows = jnp.stack([fw[0], bw[0]], axis=1).transpose(0, 3, 1, 2)
    rows = jnp.pad(rows, ((0, 0), (0, 0), (0, 6), (0, 0)))
    return cols, rows


def _swap_halves(t):
    hd = t.shape[-1] // 2
    return jnp.concatenate([t[..., hd:], t[..., :hd]], axis=-1)


def _attn_kernel(q_ref, k_ref, v_ref, cos_ref, sin_ref, gq_ref, gk_ref, sink_ref, o_ref):
    n = pl.program_id(2)
    scale = B_DH ** -0.5
    nq = B_G * 128

    def norm(t, g_ref):
        return t * lax.rsqrt(jnp.mean(t * t, axis=-1, keepdims=True) + EPS) * g_ref[...]

    def head_cols(ref, rows, h):
        return ref[rows, h * B_DH:(h + 1) * B_DH]

    def finish(j, parts):
        sink = sink_ref[j]
        m = sink
        for s, _ in parts:
            m = jnp.maximum(m, jnp.max(s, axis=-1, keepdims=True))
        den = jnp.exp(sink - m)
        acc = None
        for s, vv in parts:
            p = jnp.exp(s - m)
            den = den + jnp.sum(p, axis=-1, keepdims=True)
            pv = jnp.dot(p.astype(bf16), vv, preferred_element_type=f32)
            acc = pv if acc is None else acc + pv
        o = (acc / den).astype(o_ref.dtype)
        for g in range(B_G):
            h = j * B_G + g
            o_ref[:, h * B_DH:(h + 1) * B_DH] = o[g * 128:(g + 1) * 128]

    def q_heads(j):
        q = jnp.concatenate([head_cols(q_ref, slice(None), j * B_G + g) for g in range(B_G)], axis=0)
        return norm(q.astype(f32), gq_ref)

    for j in range(B_KP):
        kc = norm(head_cols(k_ref, pl.ds(0, CTX), j).astype(f32), gk_ref).astype(bf16)
        vc = head_cols(v_ref, pl.ds(0, CTX), j)

        @pl.when(n < CTX // 128)
        def _():
            s_ctx = _dot_nt(q_heads(j).astype(bf16), kc) * scale
            finish(j, [(s_ctx, vc)])

        @pl.when(n >= CTX // 128)
        def _():
            nl = n - CTX // 128
            pos0 = pl.multiple_of(nl * 128, 128)
            qn = q_heads(j)
            cq = jnp.concatenate([cos_ref[pl.ds(pos0, 128), :]] * B_G, axis=0)
            sq = jnp.concatenate([sin_ref[pl.ds(pos0, 128), :]] * B_G, axis=0)
            qr = qn * cq + _swap_halves(qn) * sq
            start = pl.multiple_of(jnp.clip((nl - 1) * 128, 0, SEQ - 3 * 128), 128)
            kb = norm(head_cols(k_ref, pl.ds(CTX + start, 3 * 128), j).astype(f32), gk_ref)
            kb = kb * cos_ref[pl.ds(start, 3 * 128), :] + _swap_halves(kb) * sin_ref[pl.ds(start, 3 * 128), :]
            vb = head_cols(v_ref, pl.ds(CTX + start, 3 * 128), j)
            s_band = _dot_nt(qr.astype(bf16), kb.astype(bf16)) * scale
            q_pos = pos0 + lax.broadcasted_iota(jnp.int32, (nq, 3 * 128), 0) % 128
            k_pos = start + lax.broadcasted_iota(jnp.int32, (nq, 3 * 128), 1)
            s_band = jnp.where(jnp.abs(q_pos - k_pos) <= WINDOW, s_band, -jnp.inf)
            s_ctx = _dot_nt(qn.astype(bf16), kc) * scale
            finish(j, [(s_band, vb), (s_ctx, vc)])


def _attention(proj, cosf, sinf, gq, gk, sink_col):
    nblk = LA // 128
    qw = B_KP * B_G * B_DH
    kw = B_KP * B_DH
    return pl.pallas_call(
        _attn_kernel,
        out_shape=jax.ShapeDtypeStruct((ROWS, HALF), bf16),
        grid=(NB_, B_KV // B_KP, nblk),
        in_specs=[pl.BlockSpec((128, qw), lambda b, kp, n: (b * nblk + n, ATT_COL0 // qw + kp)),
                  pl.BlockSpec((LA, kw), lambda b, kp, n: (b, (ATT_COL0 + HALF) // kw + kp)),
                  pl.BlockSpec((LA, kw), lambda b, kp, n: (b, (ATT_COL0 + HALF + B_KV * B_DH) // kw + kp)),
                  pl.BlockSpec((SEQ, B_DH), lambda b, kp, n: (0, 0)),
                  pl.BlockSpec((SEQ, B_DH), lambda b, kp, n: (0, 0)),
                  pl.BlockSpec((1, B_DH), lambda b, kp, n: (0, 0)),
                  pl.BlockSpec((1, B_DH), lambda b, kp, n: (0, 0)),
                  pl.BlockSpec((B_KP, B_G * 128, 1), lambda b, kp, n: (kp, 0, 0))],
        out_specs=pl.BlockSpec((128, qw), lambda b, kp, n: (b * nblk + n, kp)),
        compiler_params=_cp(("arbitrary", "arbitrary", "arbitrary")),
        name="window_attn",
    )(proj, proj, proj, cosf, sinf, gq, gk, sink_col)


def _rope_tables():
    rows = SEQ // GRID_W
    row = jnp.repeat(jnp.arange(rows), GRID_W).astype(f32)
    col = jnp.tile(jnp.arange(GRID_W), rows).astype(f32)
    n_freq = B_DH // 4
    inv = ROPE_BASE ** (-jnp.arange(n_freq, dtype=f32) / n_freq)
    ang = jnp.concatenate([row[:, None] * inv, col[:, None] * inv], axis=-1)
    cos, sin = jnp.cos(ang), jnp.sin(ang)
    return jnp.concatenate([cos, cos], axis=-1), jnp.concatenate([-sin, sin], axis=-1)


def _s5_kernel(u_ref, cwy_ref, cwi_ref, a_ref, y_ref, wy, wi, u32, ucat, zf, zb):
    T = S5_T
    hs = S5_NS
    ku = T * 128

    def onehot_tile(n, reps):
        r = lax.broadcasted_iota(jnp.int32, (n, n * reps), 0)
        c = lax.broadcasted_iota(jnp.int32, (n, n * reps), 1)
        return jnp.where(c % n == r, 1.0, 0.0).astype(bf16)

    def group_mask(rows, row_group, width):
        r = lax.broadcasted_iota(jnp.int32, (rows, S5_GT * width), 0)
        c = lax.broadcasted_iota(jnp.int32, (rows, S5_GT * width), 1)
        return row_group(r) == c // width

    rep_p = onehot_tile(C_STATE, S5_GT)
    m_i = group_mask(ku, lambda r: (r // C_GROUP) % S5_GT, C_STATE)
    nrp = (ku + 4 * hs) // S5_GT
    sa = lax.broadcasted_iota(jnp.int32, (128, 128), 0)
    sb = lax.broadcasted_iota(jnp.int32, (128, 128), 1)
    rr = lax.broadcasted_iota(jnp.int32, (nrp, 128), 0)
    cg = lax.broadcasted_iota(jnp.int32, (nrp, 128), 1) // C_GROUP
    for q in range(S5_GT):
        sel = jnp.where(sa == q * C_GROUP + sb % C_GROUP, 1.0, 0.0).astype(bf16)
        r = rr + q * nrp
        grp = jnp.where(r < ku, (r // C_GROUP) % S5_GT, ((r - ku) % hs) // C_STATE)
        for t in range(T):
            e = jnp.dot(cwy_ref[t], sel, preferred_element_type=f32)
            wy[t, q * nrp:(q + 1) * nrp, :] = jnp.where(grp == cg, e, 0.0).astype(bf16)
    for d in range(2):
        for q in range(2):
            e = jnp.dot(cwi_ref[d, :, q * C_STATE:(q + 1) * C_STATE], rep_p, preferred_element_type=f32)
            wi[d, :, q * hs:(q + 1) * hs] = jnp.where(m_i, e, 0.0).astype(bf16)

    u32[...] = u_ref[...].astype(f32)
    for s in range(T):
        for b in range(NB_):
            ucat[s, pl.ds(b, S5_CPB, stride=NB_), :] = u32[pl.ds(b * LA + s, S5_CPB, stride=T), :]
    uc = jnp.concatenate([ucat[s].astype(bf16) for s in range(T)], axis=1)
    zf[...] = jnp.dot(uc, wi[0], preferred_element_type=f32)
    zb[...] = jnp.dot(uc, wi[1], preferred_element_type=f32)
    a = a_ref[...]

    def cstep(av, x, z):
        ar, ai = av[:, :hs], av[:, hs:]
        xr, xi = x[:, :hs], x[:, hs:]
        return jnp.concatenate([ar * xr - ai * xi + z[:, :hs], ar * xi + ai * xr + z[:, hs:]], axis=1)

    def step(i, carry):
        x_f, x_b = carry
        rf = pl.multiple_of(i * 2 * NB_, 2 * NB_)
        mb = jnp.where(i < S5_CCTX // 2, S5_CCTX // 2 - 1 - i, S5_CPB // 2 + S5_CCTX // 2 - 1 - i)
        rb = pl.multiple_of(mb * 2 * NB_, 2 * NB_)
        z8 = zf[pl.ds(rf, 2 * NB_), :]
        x1 = cstep(a[0:1], x_f, z8[:NB_])
        zf[pl.ds(rf, 2 * NB_), :] = jnp.concatenate([x_f, x1], axis=0)
        x_f2 = cstep(a[0:1], x1, z8[NB_:])
        w8 = zb[pl.ds(rb, 2 * NB_), :]
        y1 = cstep(a[1:2], x_b, w8[NB_:])
        zb[pl.ds(rb, 2 * NB_), :] = jnp.concatenate([y1, x_b], axis=0)
        x_b2 = cstep(a[1:2], y1, w8[:NB_])
        return x_f2, x_b2

    zero = jnp.zeros((NB_, 2 * hs), f32)
    lax.fori_loop(0, S5_CPB // 2, step, (zero, zero))
    xf = zf[...].astype(bf16)
    xb = zb[...].astype(bf16)
    for t in range(T):
        u32[pl.ds(t * S5_NCH, S5_NCH), :] = (
            jnp.dot(uc, wy[t, 0:ku, :], preferred_element_type=f32)
            + jnp.dot(xf, wy[t, ku:ku + 2 * hs, :], preferred_element_type=f32)
            + jnp.dot(xb, wy[t, ku + 2 * hs:ku + 4 * hs, :], preferred_element_type=f32))
    for t in range(T):
        for b in range(NB_):
            y_ref[pl.ds(b * LA + t, S5_CPB, stride=T), :] = u32[pl.ds(t * S5_NCH + b, S5_CPB, stride=NB_), :]


def _s5_core(u_all, mats):
    cwy, cwi, a = mats
    nt = HALF // 128
    kf = S5_T * 128 + 4 * S5_NS
    return pl.pallas_call(
        _s5_kernel,
        out_shape=jax.ShapeDtypeStruct((ROWS, HALF), f32),
        grid=(nt,),
        in_specs=[pl.BlockSpec((ROWS, 128), lambda j: (0, j)),
                  pl.BlockSpec((None, S5_T, kf // S5_GT, 128), lambda j: (j, 0, 0, 0)),
                  pl.BlockSpec((None, 2, S5_T * 128, 2 * C_STATE), lambda j: (j, 0, 0, 0)),
                  pl.BlockSpec((None, 2, 2 * S5_NS), lambda j: (j, 0, 0))],
        out_specs=pl.BlockSpec((ROWS, 128), lambda j: (0, j)),
        scratch_shapes=[pltpu.VMEM((S5_T, kf, 128), bf16),
                        pltpu.VMEM((2, S5_T * 128, 2 * S5_NS), bf16),
                        pltpu.VMEM((ROWS, 128), f32),
                        pltpu.VMEM((S5_T, S5_NCH, 128), f32),
                        pltpu.VMEM((S5_NCH, 2 * S5_NS), f32),
                        pltpu.VMEM((S5_NCH, 2 * S5_NS), f32)],
        compiler_params=_cp(("arbitrary",), 56),
        name="s5_core",
    )(u_all, cwy, cwi, a)


def _s5_matrices(a_re, a_im, log_dt, b_re, b_im, c_re, c_im):
    T = S5_T
    G, P, C = C_GROUPS, C_STATE, C_GROUP
    nt = G // S5_GT
    lam = lax.complex(a_re.astype(f32), a_im.astype(f32))
    dt = jnp.exp(log_dt.astype(f32))[..., None]
    a_bar = jnp.exp(lam * dt)
    b_bar = ((a_bar - 1) / lam)[..., None] * lax.complex(b_re.astype(f32), b_im.astype(f32))
    tau = jnp.arange(T + 1, dtype=f32)
    apow = jnp.exp((lam * dt)[..., None] * tau)
    cc = lax.complex(c_re.astype(f32), c_im.astype(f32))
    kern = jnp.einsum("gxp,dgpt,dgpc->dgtxc", cc, apow[..., :T], b_bar).real
    t_i = jnp.arange(T)
    lag = t_i[:, None] - t_i[None, :]
    kf = kern[0][:, jnp.clip(lag, 0, T - 1)] * (lag >= 0)[None, :, :, None, None]
    kb = kern[1][:, jnp.clip(-lag, 0, T - 1)] * (lag <= 0)[None, :, :, None, None]
    km = (kf + kb).reshape(nt, S5_GT, T, T, C, C).transpose(0, 2, 3, 1, 5, 4)
    wy_u = km.reshape(nt, T, T * 128, C)
    k_f = cc[:, :, :, None] * apow[0][:, None, :, 1 + t_i]
    k_b = cc[:, :, :, None] * apow[1][:, None, :, T - t_i]

    def wy_state(k):
        k = k.reshape(nt, S5_GT, C, P, T).transpose(0, 4, 1, 3, 2).reshape(nt, T, S5_NS, C)
        return jnp.concatenate([k.real, -k.imag], axis=2)

    cwy = jnp.concatenate([wy_u, wy_state(k_f), wy_state(k_b)], axis=2).astype(bf16)
    nrow = cwy.shape[2]
    cwy = cwy.reshape(nt, T, S5_GT, nrow // S5_GT, C).transpose(0, 1, 3, 2, 4).reshape(nt, T, nrow // S5_GT, 128)
    w_f = apow[0][:, :, T - 1 - t_i][..., None] * b_bar[0][:, :, None, :]
    w_b = apow[1][:, :, t_i][..., None] * b_bar[1][:, :, None, :]

    def wi_dir(w):
        w = w.reshape(nt, S5_GT, P, T, C).transpose(0, 3, 1, 4, 2).reshape(nt, T * 128, P)
        return jnp.concatenate([w.real, w.imag], axis=2)

    cwi = jnp.stack([wi_dir(w_f), wi_dir(w_b)], axis=1).astype(bf16)
    at = apow[..., T].reshape(2, nt, S5_NS)
    a = jnp.concatenate([at.real, at.imag], axis=-1).transpose(1, 0, 2)
    return cwy, cwi, a


def _s5_glu_kernel(y_ref, u_ref, d_ref, w_ref, b_ref, o_ref, wbf_ref):
    @pl.when(pl.program_id(0) == 0)
    def _():
        wbf_ref[...] = w_ref[...].astype(bf16)

    y = y_ref[...] + d_ref[...] * u_ref[...].astype(f32)
    y = jax.nn.gelu(y, approximate=True)
    z = jnp.dot(y.astype(bf16), wbf_ref[...], preferred_element_type=f32) + b_ref[...]
    o_ref[...] = (y * jax.nn.sigmoid(z)).astype(o_ref.dtype)


def _s5_glu(y_ssm, proj, d2, glu_w, glu_b2):
    nblk = NB_ * LBPB
    return pl.pallas_call(
        _s5_glu_kernel,
        out_shape=jax.ShapeDtypeStruct((nblk * RB, HALF), bf16),
        grid=(nblk,),
        in_specs=[pl.BlockSpec((RB, HALF), lambda i: (_lat_blk(i), 0)),
                  pl.BlockSpec((RB, HALF), lambda i: (_lat_blk(i), 0)),
                  pl.BlockSpec((1, HALF), lambda i: (0, 0)),
                  pl.BlockSpec((HALF, HALF), lambda i: (0, 0)),
                  pl.BlockSpec((1, HALF), lambda i: (0, 0))],
        out_specs=pl.BlockSpec((RB, HALF), lambda i: (i, 0)),
        scratch_shapes=[pltpu.VMEM((HALF, HALF), bf16)],
        compiler_params=_cp(("arbitrary",)),
        name="s5_glu",
    )(y_ssm, proj, d2, glu_w, glu_b2)


def _hyena_filter(L, w1, b1, w2, b2, w3, freq):
    t = jnp.linspace(0.0, 1.0, L, dtype=f32)[:, None]
    w = 2 * math.pi * jnp.arange(L, dtype=f32)[:, None] / L
    bands = jnp.linspace(1e-4, HY_BANDS - 1, HY_BANDS, dtype=f32)
    feats = jnp.concatenate([t, jnp.cos(bands * w), -jnp.sin(bands * w)], axis=-1)
    h = jnp.sin(freq * (feats @ w1 + b1))
    h = jnp.sin(freq * (h @ w2 + b2))
    h = h @ w3
    fast = abs(math.log(HY_DECAY_TARGET) / HY_FAST_PCT)
    slow = abs(math.log(HY_DECAY_TARGET) / HY_SLOW_PCT)
    deltas = jnp.tile(jnp.linspace(slow, fast, HALF, dtype=f32), 2)
    h = h * jnp.exp(-t * deltas)
    h_fwd, h_bwd = jnp.split(h, 2, axis=-1)
    h_bwd = h_bwd.at[0].set(0.0)
    norm = jnp.sum(jnp.abs(h_fwd), axis=0, keepdims=True) + jnp.sum(jnp.abs(h_bwd), axis=0, keepdims=True)
    return h_fwd / norm, h_bwd / norm


def _hyena_dft_tables():
    n2 = 2 * SEQ
    n = jnp.arange(SEQ, dtype=jnp.int32)[:, None]
    k = jnp.arange(SEQ, dtype=jnp.int32)[None, :]
    kr = 64
    k1 = jnp.arange(SEQ // kr, dtype=jnp.int32)[None, :]
    k0 = jnp.arange(kr, dtype=jnp.int32)[None, :]
    ang_a = ((n * k1 * kr) % n2).astype(f32) * (2.0 * math.pi / n2)
    ang_b = ((n * k0) % n2).astype(f32) * (2.0 * math.pi / n2)
    ca, sa = jnp.cos(ang_a)[:, :, None], jnp.sin(ang_a)[:, :, None]
    cb, sb = jnp.cos(ang_b)[:, None, :], jnp.sin(ang_b)[:, None, :]
    c = (ca * cb - sa * sb).reshape(SEQ, SEQ)
    s = (sa * cb + ca * sb).reshape(SEQ, SEQ)
    nyq = jnp.where(n % 2 == 0, 1.0, -1.0).astype(f32)
    f_re, f_im = c, jnp.where(k == 0, nyq, -s)
    g_re = jnp.where(k == 0, 1.0, 2.0 * c) / n2
    g_im = jnp.where(k == 0, nyq, -2.0 * s) / n2

    def tiles(re, im):
        nk = SEQ // HY_KT
        return jnp.concatenate([re.reshape(SEQ, nk, 1, HY_KT), im.reshape(SEQ, nk, 1, HY_KT)],
                               axis=2).reshape(SEQ, 2 * SEQ).astype(bf16)

    return tiles(f_re, f_im), tiles(g_re, g_im)


def _hyena_kernel(x0_ref, x1_ref, v_ref, cw0_ref, cw1_ref, cw2_ref, cb0_ref, cb1_ref, cb2_ref, hb_ref,
                  f_ref, g_ref, ht_ref, o_ref, ut_scr, u_scr, y_scr):
    kt = pl.program_id(2)
    L = SEQ
    row = lax.broadcasted_iota(jnp.int32, (L, HY_TC), 0)

    def sconv(z_ref, cw_ref, cb_ref):
        z = z_ref[...].astype(f32)
        zm = jnp.where(row == 0, 0.0, pltpu.roll(z, 1, axis=0))
        zp = jnp.where(row == L - 1, 0.0, pltpu.roll(z, L - 1, axis=0))
        return zm * cw_ref[0:1, :] + z * cw_ref[1:2, :] + zp * cw_ref[2:3, :] + cb_ref[...]

    @pl.when(kt == 0)
    def _():
        u = sconv(x1_ref, cw1_ref, cb1_ref) * sconv(v_ref, cw2_ref, cb2_ref)
        u_scr[...] = u
        ut_scr[...] = u.T.astype(bf16)
        y_scr[...] = jnp.zeros_like(y_scr)

    xt = jnp.dot(ut_scr[...], f_ref[...], preferred_element_type=f32)
    h = ht_ref[...]
    xr, xi, hr, hi = xt[:, :HY_KT], xt[:, HY_KT:], h[:, :HY_KT], h[:, HY_KT:]
    col = lax.broadcasted_iota(jnp.int32, (HY_TC, HY_KT), 1) + kt * HY_KT
    dc = col == 0
    yr = xr * hr - jnp.where(dc, 0.0, xi * hi)
    yi = jnp.where(dc, xi * hi, xr * hi + xi * hr)
    yt = jnp.concatenate([yr, yi], axis=1).astype(bf16)
    y_scr[...] += _dot_nt(g_ref[...], yt)

    @pl.when(kt == pl.num_programs(2) - 1)
    def _():
        o_ref[...] = (sconv(x0_ref, cw0_ref, cb0_ref)
                      * (y_scr[...] + u_scr[...] * hb_ref[...])).astype(o_ref.dtype)


def _hyena(z_lat, conv_w, conv_b2, hy_bias2, ht, ft, gt):
    nct = HALF // HY_TC

    def zspec(k):
        return pl.BlockSpec((SEQ, HY_TC), lambda ct, b, kt: (b, k * nct + ct))

    def wspec(k):
        return pl.BlockSpec((3, HY_TC), lambda ct, b, kt: (0, k * nct + ct))

    def bspec(k):
        return pl.BlockSpec((1, HY_TC), lambda ct, b, kt: (0, k * nct + ct))

    return pl.pallas_call(
        _hyena_kernel,
        out_shape=jax.ShapeDtypeStruct((NB_ * SEQ, HALF), bf16),
        grid=(nct, NB_, SEQ // HY_KT),
        in_specs=[zspec(0), zspec(1), zspec(2), wspec(0), wspec(1), wspec(2), bspec(0), bspec(1), bspec(2),
                  pl.BlockSpec((1, HY_TC), lambda ct, b, kt: (0, ct)),
                  pl.BlockSpec((SEQ, 2 * HY_KT), lambda ct, b, kt: (0, kt)),
                  pl.BlockSpec((SEQ, 2 * HY_KT), lambda ct, b, kt: (0, kt)),
                  pl.BlockSpec((HY_TC, 2 * HY_KT), lambda ct, b, kt: (ct, kt))],
        out_specs=pl.BlockSpec((SEQ, HY_TC), lambda ct, b, kt: (b, ct)),
        scratch_shapes=[pltpu.VMEM((HY_TC, SEQ), bf16), pltpu.VMEM((SEQ, HY_TC), f32),
                        pltpu.VMEM((SEQ, HY_TC), f32)],
        compiler_params=_cp(("arbitrary", "arbitrary", "arbitrary"), 48),
        name="hyena",
    )(z_lat, z_lat, z_lat, conv_w, conv_w, conv_w, conv_b2, conv_b2, conv_b2, hy_bias2, ft, gt, ht)


def _hyena_filter_spectrum(h_fwd, h_bwd, ft):
    c = h_fwd.shape[1]
    ht2 = jnp.concatenate([h_fwd.T, h_bwd.T], axis=0).astype(bf16)
    xt = _matmul([ht2], ft, (), 2 * SEQ, 2 * HY_KT, 2 * c // RB, _ident, f32, name="hyena_filter_dft")
    col = jnp.arange(2 * SEQ)
    sgn = jnp.where((col % (2 * HY_KT) >= HY_KT) & (col != HY_KT), -1.0, 1.0).astype(f32)
    return xt[:c] + xt[c:] * sgn[None, :]


def _swiglu_perm():
    j = np.arange(256)
    src = np.where(j < 128, 2 * j, 2 * (j - 128) + 1)
    p = np.zeros((256, 256), np.float32)
    p[src, j] = 1.0
    return jnp.asarray(p, dtype=bf16)


def kernel(x, c, ctx, c_ctx, ada_w, ada_b, norm_mix_g, norm_ffn_g, ev_w_in, ev_gate_b, ev_h_norm_g,
           ev_q_norm_g, ev_k_norm_g, ev_sink, ev_w_out, od_w_in, od_a_re, od_a_im, od_log_dt, od_b_re,
           od_b_im, od_c_re, od_c_im, od_d, od_glu_w, od_glu_b, od_conv_w, od_conv_b, od_filt_w1,
           od_filt_b1, od_filt_w2, od_filt_b2, od_filt_w3, od_filt_freq, od_hy_bias, od_w_out,
           moe_router_w, moe_router_b, moe_w1, moe_b1, moe_w2, moe_b2):
    depth = ada_w.shape[0]
    xu = jnp.concatenate([ctx, x], axis=1).reshape(ROWS, D)
    cond8 = jnp.concatenate([c_ctx[None], c, jnp.zeros((3, D), f32)], axis=0)
    ada_b3 = ada_b.reshape(depth, 1, 6 * D)
    g_mix3 = norm_mix_g.reshape(depth, 1, D)
    g_ffn3 = norm_ffn_g.reshape(depth, 1, D)
    rb3 = moe_router_b.reshape(depth, 1, N_EXPERTS)
    b1_4 = moe_b1.reshape(depth, N_EXPERTS, 1, 2 * D_FF)
    b2_4 = moe_b2.reshape(depth, N_EXPERTS, 1, D)
    perm = _swiglu_perm()
    nall = ROWS // RB
    nlat = NB_ * LBPB

    mod3 = _ada(cond8, ada_w, ada_b3, 0).reshape(8, 1, 6 * D)
    h = _norm_mod(xu, g_mix3, mod3, 0, 0, _ident, _mod_row, nall, bf16)
    w_in = ev_w_in[0]
    n_a = 2 * A_HEADS * A_DQK + 2 * HALF
    n_g = 4 * A_HEADS
    w_main = jnp.concatenate([w_in[:, :n_a], w_in[:, n_a + n_g:]], axis=1)
    w_gate = jnp.pad(w_in[:, n_a:n_a + n_g], ((0, 0), (0, 128 - n_g)))
    proj = _matmul([h], w_main, (), w_main.shape[1], 1536, nall, _ident, bf16, name="even_in")
    gates = _matmul([h], w_gate, (), 128, 128, nall, _ident, f32, name="even_gates")[:, :n_g]
    gates = (gates + ev_gate_b[0]).reshape(NB_, LA, n_g)
    gcol, grow = _mlstm_gate_tables(gates)
    y_a = _mlstm(proj, gcol, grow, ev_h_norm_g[0].reshape(A_HEADS, 1, A_DV))
    cosf, sinf = _rope_tables()
    sink_col = jnp.repeat(ev_sink[0].astype(f32).reshape(B_KV, B_G), 128, axis=1)[..., None]
    y_b = _attention(proj, cosf, sinf, ev_q_norm_g[0].reshape(1, B_DH), ev_k_norm_g[0].reshape(1, B_DH),
                     sink_col)
    xu = _matmul([y_a, y_b], ev_w_out, (0,), D, 1024, nall, _ident, f32,
                 res=xu, res_blk=_ident, mrow=_mod_row, mod3=mod3, gate_chunk=2, name="even_out")
    xu = _moe_layer(xu, g_ffn3, mod3, moe_router_w, rb3, moe_w1, b1_4, moe_w2, b2_4, perm,
                    0, _ident, _mod_row, nall)

    mod3 = _ada(cond8, ada_w, ada_b3, 1).reshape(8, 1, 6 * D)
    h = _norm_mod(xu, g_mix3, mod3, 1, 0, _ident, _mod_row, nall, bf16)
    proj = _matmul([h], od_w_in, (0,), HALF, 1024, nall, _ident, bf16, name="odd_in_u")
    z_lat = _matmul([h], od_w_in, (0,), 3 * HALF, 1024, nlat, _lat_blk, bf16, w_col0=HALF // 1024,
                    name="odd_in_z")
    mats = _s5_matrices(od_a_re[0], od_a_im[0], od_log_dt[0], od_b_re[0], od_b_im[0], od_c_re[0], od_c_im[0])
    y_ssm = _s5_core(proj, mats)
    s_l = _s5_glu(y_ssm, proj, od_d[0].reshape(1, HALF), od_glu_w[0], od_glu_b[0].reshape(1, HALF))
    h_fwd, h_bwd = _hyena_filter(SEQ, od_filt_w1[0], od_filt_b1[0], od_filt_w2[0], od_filt_b2[0],
                                 od_filt_w3[0], od_filt_freq[0])
    ft, gt = _hyena_dft_tables()
    hy_l = _hyena(z_lat, od_conv_w[0], od_conv_b[0].reshape(1, 3 * HALF), od_hy_bias[0].reshape(1, HALF),
                  _hyena_filter_spectrum(h_fwd, h_bwd, ft), ft, gt)
    x_lat = _matmul([s_l, hy_l], od_w_out, (0,), D, 1024, nlat, _ident, f32,
                    res=xu, res_blk=_lat_blk, mrow=_lat_mod_row, mod3=mod3, gate_chunk=2, name="odd_out")
    out = _moe_layer(x_lat, g_ffn3, mod3, moe_router_w, rb3, moe_w1, b1_4, moe_w2, b2_4, perm,
                     1, _ident, _lat_mod_row, nlat)
    return out.reshape(NB_, SEQ, D)
```

```python
import functools
import math

import jax
import jax.numpy as jnp
import numpy as np
from jax import lax
from jax.experimental import pallas as pl
from jax.experimental.pallas import tpu as pltpu

f32 = jnp.float32
bf16 = jnp.bfloat16

D = 2048
NB_ = 4
SEQ = 2048
CTX = 256
LA = SEQ + CTX
ROWS = NB_ * LA
RB = 256
BPB = LA // RB
LBPB = SEQ // RB
EPS = 1e-6
GRID_W = 64

HALF = D // 2
A_HEADS = 4
A_DV = HALF // A_HEADS
A_DQK = A_DV // 2
A_T = 256
B_HEADS = 16
B_KV = 4
B_G = B_HEADS // B_KV
B_DH = HALF // B_HEADS
B_KP = 2
ATT_COL0 = 2 * A_HEADS * A_DQK + 2 * HALF
WINDOW = 128
ROPE_BASE = 10000.0
C_GROUP = 16
C_GROUPS = HALF // C_GROUP
C_STATE = 64
S5_T = 8
S5_GT = 128 // C_GROUP
S5_NS = S5_GT * C_STATE
S5_NCH = ROWS // S5_T
S5_CPB = LA // S5_T
S5_CCTX = CTX // S5_T
HY_TC = 256
HY_KT = 512
HY_BANDS = 16
HY_DECAY_TARGET = 1e-2
HY_FAST_PCT = 0.3
HY_SLOW_PCT = 1.5
N_EXPERTS = 32
TOP_K = 4
D_FF = D
SWIGLU_LIMIT = 7.0
SWIGLU_ALPHA = 1.702
MOE_TM = 256

EVEN_SPLITS = (A_HEADS * A_DQK, A_HEADS * A_DQK, HALF, HALF, 4 * A_HEADS,
               B_HEADS * B_DH, B_KV * B_DH, B_KV * B_DH)

VMEM_MB = 1024 * 1024


def _cp(sem, vmem_mb=40):
    return pltpu.CompilerParams(dimension_semantics=sem, vmem_limit_bytes=vmem_mb * VMEM_MB)


def _lat_blk(i):
    return (i // LBPB) * BPB + 1 + i % LBPB


def _mod_row(u):
    return jnp.where(u % BPB == 0, 0, 1 + u // BPB)


def _lat_mod_row(i):
    return 1 + i // LBPB


def _ident(i):
    return i


def _ada_kernel(c_ref, w_ref, b_ref, o_ref):
    c = c_ref[...]
    s = c * jax.nn.sigmoid(c)
    o_ref[...] = jnp.dot(s.astype(bf16), w_ref[...].astype(bf16),
                         preferred_element_type=f32) + b_ref[...]


def _ada(cond8, ada_w, ada_b3, layer):
    tn = 1024
    n = 6 * D
    return pl.pallas_call(
        _ada_kernel,
        out_shape=jax.ShapeDtypeStruct((8, n), f32),
        grid=(n // tn,),
        in_specs=[pl.BlockSpec((8, D), lambda j: (0, 0)),
                  pl.BlockSpec((None, D, tn), lambda j: (layer, 0, j)),
                  pl.BlockSpec((None, 1, tn), lambda j: (layer, 0, j))],
        out_specs=pl.BlockSpec((8, tn), lambda j: (0, j)),
        compiler_params=_cp(("arbitrary",)),
        name="ada_params",
    )(cond8, ada_w, ada_b3)


def _norm_mod_kernel(x_ref, g_ref, sh_ref, sc_ref, o_ref):
    x = x_ref[...]
    y = x * lax.rsqrt(jnp.mean(x * x, axis=-1, keepdims=True) + EPS) * g_ref[...]
    o_ref[...] = (y * (1.0 + sc_ref[...]) + sh_ref[...]).astype(o_ref.dtype)


def _norm_mod(x, g3, mod3, layer, chunk, blk, mrow, nblk, out_dtype):
    return pl.pallas_call(
        _norm_mod_kernel,
        out_shape=jax.ShapeDtypeStruct((nblk * RB, D), out_dtype),
        grid=(nblk,),
        in_specs=[pl.BlockSpec((RB, D), lambda i: (blk(i), 0)),
                  pl.BlockSpec((None, 1, D), lambda i: (layer, 0, 0)),
                  pl.BlockSpec((None, 1, D), lambda i: (mrow(i), 0, chunk)),
                  pl.BlockSpec((None, 1, D), lambda i: (mrow(i), 0, chunk + 1))],
        out_specs=pl.BlockSpec((RB, D), lambda i: (i, 0)),
        compiler_params=_cp(("arbitrary",)),
        name="norm_mod",
    )(x, g3, mod3, mod3)


def _split_bf16(x):
    hi = x.astype(bf16)
    lo = (x - hi.astype(f32)).astype(bf16)
    return hi, lo


def _norm_router_kernel(x_ref, g_ref, sh_ref, sc_ref, rw_ref, rb_ref,
                        h_ref, idx_ref, gate_ref, rank_ref, cnt_ref, carry_ref):
    i = pl.program_id(0)

    @pl.when(i == 0)
    def _():
        carry_ref[...] = jnp.zeros_like(carry_ref)

    x = x_ref[...]
    y = x * lax.rsqrt(jnp.mean(x * x, axis=-1, keepdims=True) + EPS) * g_ref[...]
    h = y * (1.0 + sc_ref[...]) + sh_ref[...]
    h_ref[...] = h
    h_hi, h_lo = _split_bf16(h)
    w_hi, w_lo = _split_bf16(rw_ref[...])
    logits = (jnp.dot(h_hi, w_hi, preferred_element_type=f32)
              + jnp.dot(h_lo, w_hi, preferred_element_type=f32)
              + jnp.dot(h_hi, w_lo, preferred_element_type=f32)) + rb_ref[...]
    ne = logits.shape[-1]
    lane = lax.broadcasted_iota(jnp.int32, logits.shape, 1)
    l = logits
    vals, idxs = [], []
    for _ in range(TOP_K):
        m = jnp.max(l, axis=-1, keepdims=True)
        ix = jnp.min(jnp.where(l == m, lane, ne), axis=-1, keepdims=True)
        vals.append(m)
        idxs.append(ix)
        l = jnp.where(lane == ix, -jnp.inf, l)
    es = [jnp.exp(v - vals[0]) for v in vals]
    tot = es[0] + es[1] + es[2] + es[3]
    onehot = jnp.zeros(logits.shape, f32)
    for ix in idxs:
        onehot = onehot + jnp.where(lane == ix, 1.0, 0.0)
    r_i = lax.broadcasted_iota(jnp.int32, (RB, RB), 0)
    c_i = lax.broadcasted_iota(jnp.int32, (RB, RB), 1)
    tri = jnp.where(c_i < r_i, 1.0, 0.0).astype(bf16)
    prefix = jnp.dot(tri, onehot.astype(bf16), preferred_element_type=f32) + carry_ref[...]
    for k in range(TOP_K):
        idx_ref[:, k:k + 1] = idxs[k]
        gate_ref[:, k:k + 1] = es[k] / tot
        rk = jnp.sum(jnp.where(lane == idxs[k], prefix, 0.0), axis=-1, keepdims=True)
        rank_ref[:, k:k + 1] = rk.astype(jnp.int32)
    carry_ref[...] = carry_ref[...] + jnp.sum(onehot, axis=0, keepdims=True)
    cnt_ref[...] = carry_ref[...]


def _norm_router(x, g3, mod3, rw, rb3, layer, chunk, blk, mrow, nblk):
    t = nblk * RB
    return pl.pallas_call(
        _norm_router_kernel,
        out_shape=(jax.ShapeDtypeStruct((t, D), f32),
                   jax.ShapeDtypeStruct((t, TOP_K), jnp.int32),
                   jax.ShapeDtypeStruct((t, TOP_K), f32),
                   jax.ShapeDtypeStruct((t, TOP_K), jnp.int32),
                   jax.ShapeDtypeStruct((1, N_EXPERTS), f32)),
        grid=(nblk,),
        in_specs=[pl.BlockSpec((RB, D), lambda i: (blk(i), 0)),
                  pl.BlockSpec((None, 1, D), lambda i: (layer, 0, 0)),
                  pl.BlockSpec((None, 1, D), lambda i: (mrow(i), 0, chunk)),
                  pl.BlockSpec((None, 1, D), lambda i: (mrow(i), 0, chunk + 1)),
                  pl.BlockSpec((None, D, N_EXPERTS), lambda i: (layer, 0, 0)),
                  pl.BlockSpec((None, 1, N_EXPERTS), lambda i: (layer, 0, 0))],
        out_specs=(pl.BlockSpec((RB, D), lambda i: (i, 0)),
                   pl.BlockSpec((RB, TOP_K), lambda i: (i, 0)),
                   pl.BlockSpec((RB, TOP_K), lambda i: (i, 0)),
                   pl.BlockSpec((RB, TOP_K), lambda i: (i, 0)),
                   pl.BlockSpec((1, N_EXPERTS), lambda i: (0, 0))),
        scratch_shapes=[pltpu.VMEM((1, N_EXPERTS), f32)],
        compiler_params=_cp(("arbitrary",)),
        name="norm_router",
    )(x, g3, mod3, mod3, rw, rb3)


def _mm_kernel(*refs, n_a, k_split, epilogue):
    a_refs = refs[:n_a]
    w_ref = refs[n_a]
    pos = n_a + 1
    if epilogue:
        res_ref, gate_ref = refs[pos], refs[pos + 1]
        pos += 2
    o_ref, wbf_ref = refs[pos], refs[pos + 1]

    @pl.when(pl.program_id(1) == 0)
    def _():
        wbf_ref[...] = w_ref[...].astype(bf16)

    acc = None
    for t, a_ref in enumerate(a_refs):
        part = jnp.dot(a_ref[...], wbf_ref[t * k_split:(t + 1) * k_split, :],
                       preferred_element_type=f32)
        acc = part if acc is None else acc + part
    if epilogue:
        acc = res_ref[...] + gate_ref[...] * acc
    o_ref[...] = acc.astype(o_ref.dtype)


def _matmul(a_list, w, w_idx, n, tn, nblk, a_blk, out_dtype, *, a_col=None, w_col0=0,
            res=None, res_blk=None, mrow=None, mod3=None, gate_chunk=None, name="matmul"):
    n_a = len(a_list)
    k = w.shape[-2]
    k_split = k // n_a
    lead = len(w_idx)
    a_col = a_col or [0] * n_a
    in_specs = [pl.BlockSpec((RB, k_split), functools.partial(lambda j, i, c: (a_blk(i), c), c=c))
                for c in a_col]
    in_specs.append(pl.BlockSpec((None,) * lead + (k, tn), lambda j, i: tuple(w_idx) + (0, j + w_col0)))
    args = list(a_list) + [w]
    epilogue = res is not None
    if epilogue:
        per = D // tn
        in_specs.append(pl.BlockSpec((RB, tn), lambda j, i: (res_blk(i), j)))
        in_specs.append(pl.BlockSpec((None, 1, tn),
                                     lambda j, i: (mrow(i), 0, gate_chunk * per + j)))
        args += [res, mod3]
    return pl.pallas_call(
        functools.partial(_mm_kernel, n_a=n_a, k_split=k_split, epilogue=epilogue),
        out_shape=jax.ShapeDtypeStruct((nblk * RB, n), out_dtype),
        grid=(n // tn, nblk),
        in_specs=in_specs,
        out_specs=pl.BlockSpec((RB, tn), lambda j, i: (i, j)),
        scratch_shapes=[pltpu.VMEM((k, tn), bf16)],
        compiler_params=_cp(("arbitrary", "arbitrary"), 48),
        name=name,
    )(*args)


def _gather_rows_kernel(tok_ref, h_hbm, o_ref, buf, sem, *, rows, nsteps):
    i = pl.program_id(0)

    def issue(blk, slot):
        base = blk * rows

        def body(r2, c):
            for q in range(2):
                r = r2 * 2 + q
                t = tok_ref[base + r]
                pltpu.make_async_copy(h_hbm.at[pl.ds(t, 1), :], buf.at[slot, pl.ds(r, 1), :],
                                      sem.at[slot]).start(priority=q)
            return c

        lax.fori_loop(0, rows // 2, body, 0, unroll=4)

    @pl.when(i == 0)
    def _():
        issue(0, 0)

    @pl.when(i + 1 < nsteps)
    def _():
        issue(i + 1, (i + 1) % 2)

    slot = i % 2
    pltpu.make_async_copy(h_hbm.at[pl.ds(0, rows), :], buf.at[slot], sem.at[slot]).wait()
    o_ref[...] = buf[slot].astype(o_ref.dtype)


def _gather_rows(slot_tok, h, nr):
    rows = 512
    nsteps = nr // rows
    return pl.pallas_call(
        functools.partial(_gather_rows_kernel, rows=rows, nsteps=nsteps),
        out_shape=jax.ShapeDtypeStruct((nr, D), bf16),
        grid_spec=pltpu.PrefetchScalarGridSpec(
            num_scalar_prefetch=1,
            grid=(nsteps,),
            in_specs=[pl.BlockSpec(memory_space=pl.ANY)],
            out_specs=pl.BlockSpec((rows, D), lambda i, tok: (i, 0)),
            scratch_shapes=[pltpu.VMEM((2, rows, D), f32), pltpu.SemaphoreType.DMA((2,))]),
        compiler_params=_cp(("arbitrary",)),
        name="moe_gather",
    )(slot_tok, h)


def _moe_up_kernel(start_ref, nblk_ref, xs_hbm, w_hbm, b_ref, p_ref, act_hbm,
                   wbf_ref, wf32, xbuf, obuf, wsem, xsem, osem, *, tn, layer):
    j = pl.program_id(0)
    e = pl.program_id(1)
    nb = nblk_ref[e]
    row0 = start_ref[e]
    g = j * N_EXPERTS + e

    def w_copy(gg, slot):
        c0 = pl.multiple_of((gg // N_EXPERTS) * tn, tn)
        return pltpu.make_async_copy(w_hbm.at[layer, gg % N_EXPERTS, :, pl.ds(c0, tn)], wf32.at[slot],
                                     wsem.at[slot])

    def x_copy(blk, slot):
        r = pl.multiple_of(row0 + blk * MOE_TM, MOE_TM)
        return pltpu.make_async_copy(xs_hbm.at[pl.ds(r, MOE_TM), :], xbuf.at[slot], xsem.at[slot])

    def o_copy(blk, slot):
        r = pl.multiple_of(row0 + blk * MOE_TM, MOE_TM)
        return pltpu.make_async_copy(obuf.at[slot], act_hbm.at[j, pl.ds(r, MOE_TM), :], osem.at[slot])

    @pl.when(g == 0)
    def _():
        w_copy(0, 0).start(priority=1)

    @pl.when(nb > 0)
    def _():
        x_copy(0, 0).start()

    @pl.when(g + 1 < pl.num_programs(0) * N_EXPERTS)
    def _():
        w_copy(g + 1, (g + 1) % 2).start(priority=1)

    w_copy(g, g % 2).wait()

    @pl.when(nb > 0)
    def _():
        wbf_ref[...] = wf32[g % 2].astype(bf16)

    def body(blk, c):
        slot = blk % 2
        x_copy(blk, slot).wait()

        @pl.when(blk + 1 < nb)
        def _():
            x_copy(blk + 1, 1 - slot).start()

        h = jnp.dot(xbuf[slot], wbf_ref[...], preferred_element_type=f32) + b_ref[...]

        @pl.when(blk >= 2)
        def _():
            o_copy(blk - 2, slot).wait()

        for cc in range(tn // 256):
            d = jnp.dot(h[:, cc * 256:(cc + 1) * 256].astype(bf16), p_ref[...],
                        preferred_element_type=f32)
            gate = jnp.minimum(d[:, :128], SWIGLU_LIMIT)
            lin = jnp.clip(d[:, 128:], -SWIGLU_LIMIT, SWIGLU_LIMIT)
            act = gate * jax.nn.sigmoid(SWIGLU_ALPHA * gate) * (lin + 1.0)
            obuf[slot, :, cc * 128:(cc + 1) * 128] = act.astype(obuf.dtype)
        o_copy(blk, slot).start()
        return c

    lax.fori_loop(0, nb, body, 0)

    @pl.when(nb >= 2)
    def _():
        o_copy(nb - 2, nb % 2).wait()

    @pl.when(nb >= 1)
    def _():
        o_copy(nb - 1, (nb - 1) % 2).wait()

    @pl.when(e == N_EXPERTS - 1)
    def _():
        obuf[0] = jnp.zeros(obuf.shape[1:], obuf.dtype)

        def zero_tail(blk, c):
            r = pl.multiple_of(blk * MOE_TM, MOE_TM)
            cp = pltpu.make_async_copy(obuf.at[0], act_hbm.at[j, pl.ds(r, MOE_TM), :], osem.at[0])
            cp.start()
            cp.wait()
            return c

        lax.fori_loop(row0 // MOE_TM + nb, act_hbm.shape[1] // MOE_TM, zero_tail, 0)


def _moe_up(start, nblk_e, xs, w1, b1, perm, layer, nr):
    tn = 2048
    nj = 2 * D_FF // tn
    return pl.pallas_call(
        functools.partial(_moe_up_kernel, tn=tn, layer=layer),
        out_shape=jax.ShapeDtypeStruct((nj, nr, tn // 2), bf16),
        grid_spec=pltpu.PrefetchScalarGridSpec(
            num_scalar_prefetch=2,
            grid=(nj, N_EXPERTS),
            in_specs=[pl.BlockSpec(memory_space=pl.ANY),
                      pl.BlockSpec(memory_space=pl.ANY),
                      pl.BlockSpec((None, None, 1, tn), lambda j, e, st, nb: (layer, e, 0, j)),
                      pl.BlockSpec((256, 256), lambda j, e, st, nb: (0, 0))],
            out_specs=pl.BlockSpec(memory_space=pl.ANY),
            scratch_shapes=[pltpu.VMEM((D, tn), bf16),
                            pltpu.VMEM((2, D, tn), f32),
                            pltpu.VMEM((2, MOE_TM, D), bf16),
                            pltpu.VMEM((2, MOE_TM, tn // 2), bf16),
                            pltpu.SemaphoreType.DMA((2,)), pltpu.SemaphoreType.DMA((2,)),
                            pltpu.SemaphoreType.DMA((2,))]),
        compiler_params=_cp(("arbitrary", "arbitrary"), 56),
        name="moe_up",
    )(start, nblk_e, xs, w1, b1, perm)


def _moe_down_kernel(start_ref, nblk_ref, act_hbm, w_hbm, b_ref, y_hbm, wbf_ref, wf32, abuf, obuf,
                     wsem, asem, osem, *, layer):
    e = pl.program_id(0)
    nb = nblk_ref[e]
    row0 = start_ref[e]
    nj = abuf.shape[1]
    kh = D_FF // nj

    def w_copy(ee, slot):
        return pltpu.make_async_copy(w_hbm.at[layer, ee], wf32.at[slot], wsem.at[slot])

    def a_copy(blk, slot, jj):
        r = pl.multiple_of(row0 + blk * MOE_TM, MOE_TM)
        return pltpu.make_async_copy(act_hbm.at[jj, pl.ds(r, MOE_TM), :], abuf.at[slot, jj], asem.at[slot])

    def o_copy(blk, slot):
        r = pl.multiple_of(row0 + blk * MOE_TM, MOE_TM)
        return pltpu.make_async_copy(obuf.at[slot], y_hbm.at[pl.ds(r, MOE_TM), :], osem.at[slot])

    @pl.when(e == 0)
    def _():
        w_copy(0, 0).start(priority=1)

    @pl.when(nb > 0)
    def _():
        for jj in range(nj):
            a_copy(0, 0, jj).start()

    @pl.when(e + 1 < N_EXPERTS)
    def _():
        w_copy(e + 1, (e + 1) % 2).start(priority=1)

    w_copy(e, e % 2).wait()

    @pl.when(nb > 0)
    def _():
        wbf_ref[...] = wf32[e % 2].astype(bf16)

    def body(blk, c):
        slot = blk % 2
        for jj in range(nj):
            a_copy(blk, slot, jj).wait()

        @pl.when(blk + 1 < nb)
        def _():
            for jj in range(nj):
                a_copy(blk + 1, 1 - slot, jj).start()

        y = b_ref[...] + jnp.dot(abuf[slot, 0], wbf_ref[0:kh, :], preferred_element_type=f32)
        for jj in range(1, nj):
            y = y + jnp.dot(abuf[slot, jj], wbf_ref[jj * kh:(jj + 1) * kh, :], preferred_element_type=f32)

        @pl.when(blk >= 2)
        def _():
            o_copy(blk - 2, slot).wait()

        obuf[slot] = y
        o_copy(blk, slot).start()
        return c

    lax.fori_loop(0, nb, body, 0)

    @pl.when(nb >= 2)
    def _():
        o_copy(nb - 2, nb % 2).wait()

    @pl.when(nb >= 1)
    def _():
        o_copy(nb - 1, (nb - 1) % 2).wait()

    @pl.when(e == N_EXPERTS - 1)
    def _():
        obuf[0] = jnp.zeros(obuf.shape[1:], obuf.dtype)

        def zero_tail(blk, c):
            r = pl.multiple_of(blk * MOE_TM, MOE_TM)
            cp = pltpu.make_async_copy(obuf.at[0], y_hbm.at[pl.ds(r, MOE_TM), :], osem.at[0])
            cp.start()
            cp.wait()
            return c

        lax.fori_loop(row0 // MOE_TM + nb, y_hbm.shape[0] // MOE_TM, zero_tail, 0)


def _moe_down(start, nblk_e, act, w2, b2, layer, nr):
    nj = act.shape[0]
    return pl.pallas_call(
        functools.partial(_moe_down_kernel, layer=layer),
        out_shape=jax.ShapeDtypeStruct((nr, D), f32),
        grid_spec=pltpu.PrefetchScalarGridSpec(
            num_scalar_prefetch=2,
            grid=(N_EXPERTS,),
            in_specs=[pl.BlockSpec(memory_space=pl.ANY),
                      pl.BlockSpec(memory_space=pl.ANY),
                      pl.BlockSpec((None, None, 1, D), lambda e, st, nb: (layer, e, 0, 0))],
            out_specs=pl.BlockSpec(memory_space=pl.ANY),
            scratch_shapes=[pltpu.VMEM((D_FF, D), bf16),
                            pltpu.VMEM((2, D_FF, D), f32),
                            pltpu.VMEM((2, nj, MOE_TM, D_FF // nj), bf16),
                            pltpu.VMEM((2, MOE_TM, D), f32),
                            pltpu.SemaphoreType.DMA((2,)), pltpu.SemaphoreType.DMA((2,)),
                            pltpu.SemaphoreType.DMA((2,))]),
        compiler_params=_cp(("arbitrary",), 56),
        name="moe_down",
    )(start, nblk_e, act, w2, b2)


def _combine_kernel(slot_ref, y_hbm, x_ref, gmod_ref, w_ref, o_ref, buf, sem, *, tm, nsteps):
    i = pl.program_id(0)

    def issue(blk, slot):
        base = blk * tm

        def body(r, c):
            for k in range(TOP_K):
                s = slot_ref[(base + r) * TOP_K + k]
                pltpu.make_async_copy(y_hbm.at[pl.ds(s, 1), :], buf.at[slot, k, pl.ds(r, 1), :],
                                      sem.at[slot]).start(priority=k % 2)
            return c

        lax.fori_loop(0, tm, body, 0, unroll=4)

    @pl.when(i == 0)
    def _():
        issue(0, 0)

    @pl.when(i + 1 < nsteps)
    def _():
        issue(i + 1, (i + 1) % 2)

    slot = i % 2
    for k in range(TOP_K):
        pltpu.make_async_copy(y_hbm.at[pl.ds(0, tm), :], buf.at[slot, k], sem.at[slot]).wait()
    w = w_ref[...]
    acc = ((w[:, 0:1] * buf[slot, 0] + w[:, 1:2] * buf[slot, 1])
           + (w[:, 2:3] * buf[slot, 2] + w[:, 3:4] * buf[slot, 3]))
    o_ref[...] = x_ref[...] + gmod_ref[...] * acc


def _combine(slot_flat, ys, x, mod3, gate, gate_chunk, blk, mrow, nblk):
    tm = RB
    return pl.pallas_call(
        functools.partial(_combine_kernel, tm=tm, nsteps=nblk),
        out_shape=jax.ShapeDtypeStruct((nblk * tm, D), f32),
        grid_spec=pltpu.PrefetchScalarGridSpec(
            num_scalar_prefetch=1,
            grid=(nblk,),
            in_specs=[pl.BlockSpec(memory_space=pl.ANY),
                      pl.BlockSpec((tm, D), lambda i, s: (blk(i), 0)),
                      pl.BlockSpec((None, 1, D), lambda i, s: (mrow(i), 0, gate_chunk)),
                      pl.BlockSpec((tm, TOP_K), lambda i, s: (i, 0))],
            out_specs=pl.BlockSpec((tm, D), lambda i, s: (i, 0)),
            scratch_shapes=[pltpu.VMEM((2, TOP_K, tm, D), f32), pltpu.SemaphoreType.DMA((2,))]),
        compiler_params=_cp(("arbitrary",)),
        name="moe_combine",
    )(slot_flat, ys, x, mod3, gate)


def _moe_layer(x, g3, mod3, rw, rb3, w1, b1, w2, b2, perm, layer, blk, mrow, nblk):
    t = nblk * RB
    h, idx, gate, rank, cnt = _norm_router(x, g3, mod3, rw, rb3, layer, 3, blk, mrow, nblk)
    n_assign = t * TOP_K
    nb = n_assign // MOE_TM + N_EXPERTS
    nr = nb * MOE_TM
    counts = cnt[0].astype(jnp.int32)
    padded = (counts + MOE_TM - 1) // MOE_TM * MOE_TM
    pad_end = jnp.cumsum(padded)
    pad_start = pad_end - padded
    slot = pad_start[idx] + rank
    slot_flat = slot.reshape(-1)
    tok = jnp.repeat(jnp.arange(t, dtype=jnp.int32), TOP_K)
    slot_tok = (jnp.arange(nr, dtype=jnp.int32) % t).at[slot_flat].set(tok)
    nblk_e = (padded // MOE_TM).astype(jnp.int32)
    xs = _gather_rows(slot_tok, h, nr)
    act = _moe_up(pad_start.astype(jnp.int32), nblk_e, xs, w1, b1, perm, layer, nr)
    ys = _moe_down(pad_start.astype(jnp.int32), nblk_e, act, w2, b2, layer, nr)
    return _combine(slot_flat, ys, x, mod3, gate, 5, blk, mrow, nblk)


def _dot_nt(a, b):
    return lax.dot_general(a, b, (((1,), (1,)), ((), ())), preferred_element_type=f32)


def _dot_tn(a, b):
    return lax.dot_general(a, b, (((0,), (0,)), ((), ())), preferred_element_type=f32)


def _mlstm_kernel(q_ref, k_ref, v_ref, o_ref, gc_ref, gr_ref, ng_ref, y_ref, hf_ref, hb_ref, ct_ref):
    T = A_T
    nch = LA // T
    ct_ref[...] = jnp.zeros_like(ct_ref)
    r_i = lax.broadcasted_iota(jnp.int32, (T, T), 0)
    c_i = lax.broadcasted_iota(jnp.int32, (T, T), 1)
    lane = lax.broadcasted_iota(jnp.int32, (T, 128), 1)
    ones_blk = jnp.where(lane == 0, 1.0, 0.0).astype(bf16)

    def chunk(c, dirn, h_ref):
        r0 = pl.multiple_of(c * T, T)
        q = q_ref[pl.ds(r0, T), :]
        k = k_ref[pl.ds(r0, T), :]
        v = v_ref[pl.ds(r0, T), :]
        gc = gc_ref[pl.ds(r0, T), :]
        o = 5 * dirn
        a_col, m_col, negm = gc[:, o:o + 1], gc[:, o + 1:o + 2], gc[:, o + 2:o + 3]
        m_end, m_prev = gc[:, o + 3:o + 4], gc[:, o + 4:o + 5]
        a_row = gr_ref[dirn:dirn + 1, pl.ds(r0, T)]
        mask = (c_i <= r_i) if dirn == 0 else (c_i >= r_i)
        w = jnp.exp(jnp.where(mask, a_row - m_col, -jnp.inf))
        s = _dot_nt(q, k) * (A_DQK ** -0.5) * w
        v_aug = jnp.concatenate([v, ones_blk], axis=1)
        ct = ct_ref[dirn]
        w_carry = jnp.exp(m_prev - m_col)
        num = (jnp.dot(s.astype(bf16), v_aug, preferred_element_type=f32)
               + w_carry * jnp.dot(q, ct.astype(bf16), preferred_element_type=f32))
        den = num[:, A_DV:A_DV + 1]
        h_ref[pl.ds(r0, T), :] = num[:, :A_DV] / jnp.maximum(jnp.abs(den), jnp.exp(negm))
        w_in = jnp.exp(a_col - m_end)
        decay = jnp.exp(m_prev[0:1, :] - m_end[0:1, :])
        kv = _dot_tn(k, (w_in * v_aug.astype(f32)).astype(bf16)) * (A_DQK ** -0.5)
        ct_ref[dirn] = decay * ct + kv

    def step(it, carry):
        chunk(it, 0, hf_ref)
        chunk(jnp.where(it == 0, 0, nch - it), 1, hb_ref)
        return carry

    lax.fori_loop(0, nch, step, 0)
    h = hf_ref[...] + hb_ref[...]
    hn = h * lax.rsqrt(jnp.mean(h * h, axis=-1, keepdims=True) + EPS) * ng_ref[...]
    y_ref[...] = (hn * jax.nn.sigmoid(o_ref[...].astype(f32))).astype(y_ref.dtype)


def _mlstm(proj, gcol, grow, ng3):
    qb, vb = A_DQK, A_DV
    return pl.pallas_call(
        _mlstm_kernel,
        out_shape=jax.ShapeDtypeStruct((ROWS, HALF), bf16),
        grid=(NB_, A_HEADS),
        in_specs=[pl.BlockSpec((LA, qb), lambda b, h: (b, h)),
                  pl.BlockSpec((LA, qb), lambda b, h: (b, A_HEADS + h)),
                  pl.BlockSpec((LA, vb), lambda b, h: (b, (2 * A_HEADS * qb) // vb + h)),
                  pl.BlockSpec((LA, vb), lambda b, h: (b, (2 * A_HEADS * qb + HALF) // vb + h)),
                  pl.BlockSpec((None, None, LA, 16), lambda b, h: (b, h, 0, 0)),
                  pl.BlockSpec((None, None, 8, LA), lambda b, h: (b, h, 0, 0)),
                  pl.BlockSpec((None, 1, vb), lambda b, h: (h, 0, 0))],
        out_specs=pl.BlockSpec((LA, vb), lambda b, h: (b, h)),
        scratch_shapes=[pltpu.VMEM((LA, vb), f32), pltpu.VMEM((LA, vb), f32),
                        pltpu.VMEM((2, qb, vb + 128), f32)],
        compiler_params=_cp(("arbitrary", "arbitrary")),
        name="mlstm",
    )(proj, proj, proj, proj, gcol, grow, ng3)


def _mlstm_gate_tables(gates):
    i_f, f_f, i_b, f_b = jnp.split(gates, 4, axis=-1)

    def scan_tables(i_pre, f_pre):
        lf = jax.nn.log_sigmoid(f_pre)
        big_f = jnp.cumsum(lf, axis=1)
        a = i_pre - big_f
        m_run = jnp.maximum(lax.cummax(a, axis=1), 0.0)
        m_end = jnp.repeat(m_run[:, A_T - 1::A_T], A_T, axis=1)
        m_prev = jnp.concatenate([jnp.zeros_like(m_end[:, :A_T]), m_end[:, :-A_T]], axis=1)
        return a, m_run, -(big_f + m_run), m_end, m_prev

    def to_bwd(t):
        return jnp.concatenate([jnp.flip(t[:, :CTX], axis=1), jnp.flip(t[:, CTX:], axis=1)], axis=1)

    fw = scan_tables(i_f, f_f)
    bw = [to_bwd(t) for t in scan_tables(to_bwd(i_b), to_bwd(f_b))]
    cols = jnp.stack(list(fw) + bw, axis=-1)
    cols = jnp.pad(cols, ((0, 0), (0, 0), (0, 0), (0, 6))).transpose(0, 2, 1, 3)
    rows = jnp.stack([fw[0], bw[0]], axis=1).transpose(0, 3, 1, 2)
    rows = jnp.pad(rows, ((0, 0), (0, 0), (0, 6), (0, 0)))
    return cols, rows


def _swap_halves(t):
    hd = t.shape[-1] // 2
    return jnp.concatenate([t[..., hd:], t[..., :hd]], axis=-1)


def _attn_kernel(q_ref, k_ref, v_ref, cos_ref, sin_ref, gq_ref, gk_ref, sink_ref, o_ref):
    n = pl.program_id(2)
    scale = B_DH ** -0.5
    nq = B_G * 128

    def norm(t, g_ref):
        return t * lax.rsqrt(jnp.mean(t * t, axis=-1, keepdims=True) + EPS) * g_ref[...]

    def head_cols(ref, rows, h):
        return ref[rows, h * B_DH:(h + 1) * B_DH]

    def finish(j, parts):
        sink = sink_ref[j]
        m = sink
        for s, _ in parts:
            m = jnp.maximum(m, jnp.max(s, axis=-1, keepdims=True))
        den = jnp.exp(sink - m)
        acc = None
        for s, vv in parts:
            p = jnp.exp(s - m)
            den = den + jnp.sum(p, axis=-1, keepdims=True)
            pv = jnp.dot(p.astype(bf16), vv, preferred_element_type=f32)
            acc = pv if acc is None else acc + pv
        o = (acc / den).astype(o_ref.dtype)
        for g in range(B_G):
            h = j * B_G + g
            o_ref[:, h * B_DH:(h + 1) * B_DH] = o[g * 128:(g + 1) * 128]

    def q_heads(j):
        q = jnp.concatenate([head_cols(q_ref, slice(None), j * B_G + g) for g in range(B_G)], axis=0)
        return norm(q.astype(f32), gq_ref)

    def ctx_kv(j):
        kc = norm(head_cols(k_ref, pl.ds(0, CTX), j).astype(f32), gk_ref).astype(bf16)
        return kc, head_cols(v_ref, pl.ds(0, CTX), j)

    @pl.when(n < CTX // 128)
    def _():
        for j in range(B_KP):
            kc, vc = ctx_kv(j)
            s_ctx = _dot_nt(q_heads(j).astype(bf16), kc) * scale
            finish(j, [(s_ctx, vc)])

    @pl.when(n >= CTX // 128)
    def _():
        nl = n - CTX // 128
        pos0 = pl.multiple_of(nl * 128, 128)
        start = pl.multiple_of(jnp.clip((nl - 1) * 128, 0, SEQ - 3 * 128), 128)
        cq = jnp.concatenate([cos_ref[pl.ds(pos0, 128), :]] * B_G, axis=0)
        sq = jnp.concatenate([sin_ref[pl.ds(pos0, 128), :]] * B_G, axis=0)
        q_pos = pos0 + lax.broadcasted_iota(jnp.int32, (nq, 3 * 128), 0) % 128
        k_pos = start + lax.broadcasted_iota(jnp.int32, (nq, 3 * 128), 1)
        in_window = jnp.abs(q_pos - k_pos) <= WINDOW
        for j in range(B_KP):
            kc, vc = ctx_kv(j)
            qn = q_heads(j)
            qr = qn * cq + _swap_halves(qn) * sq
            kb = norm(head_cols(k_ref, pl.ds(CTX + start, 3 * 128), j).astype(f32), gk_ref)
            kb = kb * cos_ref[pl.ds(start, 3 * 128), :] + _swap_halves(kb) * sin_ref[pl.ds(start, 3 * 128), :]
            vb = head_cols(v_ref, pl.ds(CTX + start, 3 * 128), j)
            s_band = jnp.where(in_window, _dot_nt(qr.astype(bf16), kb.astype(bf16)) * scale, -jnp.inf)
            s_ctx = _dot_nt(qn.astype(bf16), kc) * scale
            finish(j, [(s_band, vb), (s_ctx, vc)])


def _attention(proj, cosf, sinf, gq, gk, sink_col):
    nblk = LA // 128
    qw = B_KP * B_G * B_DH
    kw = B_KP * B_DH
    return pl.pallas_call(
        _attn_kernel,
        out_shape=jax.ShapeDtypeStruct((ROWS, HALF), bf16),
        grid=(NB_, B_KV // B_KP, nblk),
        in_specs=[pl.BlockSpec((128, qw), lambda b, kp, n: (b * nblk + n, ATT_COL0 // qw + kp)),
                  pl.BlockSpec((LA, kw), lambda b, kp, n: (b, (ATT_COL0 + HALF) // kw + kp)),
                  pl.BlockSpec((LA, kw), lambda b, kp, n: (b, (ATT_COL0 + HALF + B_KV * B_DH) // kw + kp)),
                  pl.BlockSpec((SEQ, B_DH), lambda b, kp, n: (0, 0)),
                  pl.BlockSpec((SEQ, B_DH), lambda b, kp, n: (0, 0)),
                  pl.BlockSpec((1, B_DH), lambda b, kp, n: (0, 0)),
                  pl.BlockSpec((1, B_DH), lambda b, kp, n: (0, 0)),
                  pl.BlockSpec((B_KP, B_G * 128, 1), lambda b, kp, n: (kp, 0, 0))],
        out_specs=pl.BlockSpec((128, qw), lambda b, kp, n: (b * nblk + n, kp)),
        compiler_params=_cp(("arbitrary", "arbitrary", "arbitrary")),
        name="window_attn",
    )(proj, proj, proj, cosf, sinf, gq, gk, sink_col)


def _rope_tables():
    rows = SEQ // GRID_W
    row = jnp.repeat(jnp.arange(rows), GRID_W).astype(f32)
    col = jnp.tile(jnp.arange(GRID_W), rows).astype(f32)
    n_freq = B_DH // 4
    inv = ROPE_BASE ** (-jnp.arange(n_freq, dtype=f32) / n_freq)
    ang = jnp.concatenate([row[:, None] * inv, col[:, None] * inv], axis=-1)
    cos, sin = jnp.cos(ang), jnp.sin(ang)
    return jnp.concatenate([cos, cos], axis=-1), jnp.concatenate([-sin, sin], axis=-1)


def _s5_kernel(u_ref, cwy_ref, cwi_ref, a_ref, y_ref, wy, wi, u32, ucat, zf, zb):
    T = S5_T
    hs = S5_NS
    ku = T * 128

    def onehot_tile(n, reps):
        r = lax.broadcasted_iota(jnp.int32, (n, n * reps), 0)
        c = lax.broadcasted_iota(jnp.int32, (n, n * reps), 1)
        return jnp.where(c % n == r, 1.0, 0.0).astype(bf16)

    def group_mask(rows, row_group, width):
        r = lax.broadcasted_iota(jnp.int32, (rows, S5_GT * width), 0)
        c = lax.broadcasted_iota(jnp.int32, (rows, S5_GT * width), 1)
        return row_group(r) == c // width

    rep_p = onehot_tile(C_STATE, S5_GT)
    m_i = group_mask(ku, lambda r: (r // C_GROUP) % S5_GT, C_STATE)
    nrp = (ku + 4 * hs) // S5_GT
    sa = lax.broadcasted_iota(jnp.int32, (128, 128), 0)
    sb = lax.broadcasted_iota(jnp.int32, (128, 128), 1)
    rr = lax.broadcasted_iota(jnp.int32, (nrp, 128), 0)
    cg = lax.broadcasted_iota(jnp.int32, (nrp, 128), 1) // C_GROUP
    for q in range(S5_GT):
        sel = jnp.where(sa == q * C_GROUP + sb % C_GROUP, 1.0, 0.0).astype(bf16)
        r = rr + q * nrp
        grp = jnp.where(r < ku, (r // C_GROUP) % S5_GT, ((r - ku) % hs) // C_STATE)
        for t in range(T):
            e = jnp.dot(cwy_ref[t], sel, preferred_element_type=f32)
            wy[t, q * nrp:(q + 1) * nrp, :] = jnp.where(grp == cg, e, 0.0).astype(bf16)
    for d in range(2):
        for q in range(2):
            e = jnp.dot(cwi_ref[d, :, q * C_STATE:(q + 1) * C_STATE], rep_p, preferred_element_type=f32)
            wi[d, :, q * hs:(q + 1) * hs] = jnp.where(m_i, e, 0.0).astype(bf16)

    u32[...] = u_ref[...].astype(f32)
    for s in range(T):
        for b in range(NB_):
            ucat[s, pl.ds(b, S5_CPB, stride=NB_), :] = u32[pl.ds(b * LA + s, S5_CPB, stride=T), :]
    uc = jnp.concatenate([ucat[s].astype(bf16) for s in range(T)], axis=1)
    zf[...] = jnp.dot(uc, wi[0], preferred_element_type=f32)
    zb[...] = jnp.dot(uc, wi[1], preferred_element_type=f32)
    a = a_ref[...]

    def cstep(av, x, z):
        ar, ai = av[:, :hs], av[:, hs:]
        xr, xi = x[:, :hs], x[:, hs:]
        return jnp.concatenate([ar * xr - ai * xi + z[:, :hs], ar * xi + ai * xr + z[:, hs:]], axis=1)

    def step(i, carry):
        x_f, x_b = carry
        rf = pl.multiple_of(i * 2 * NB_, 2 * NB_)
        mb = jnp.where(i < S5_CCTX // 2, S5_CCTX // 2 - 1 - i, S5_CPB // 2 + S5_CCTX // 2 - 1 - i)
        rb = pl.multiple_of(mb * 2 * NB_, 2 * NB_)
        z8 = zf[pl.ds(rf, 2 * NB_), :]
        x1 = cstep(a[0:1], x_f, z8[:NB_])
        zf[pl.ds(rf, 2 * NB_), :] = jnp.concatenate([x_f, x1], axis=0)
        x_f2 = cstep(a[0:1], x1, z8[NB_:])
        w8 = zb[pl.ds(rb, 2 * NB_), :]
        y1 = cstep(a[1:2], x_b, w8[NB_:])
        zb[pl.ds(rb, 2 * NB_), :] = jnp.concatenate([y1, x_b], axis=0)
        x_b2 = cstep(a[1:2], y1, w8[:NB_])
        return x_f2, x_b2

    zero = jnp.zeros((NB_, 2 * hs), f32)
    lax.fori_loop(0, S5_CPB // 2, step, (zero, zero))
    xf = zf[...].astype(bf16)
    xb = zb[...].astype(bf16)
    for t in range(T):
        u32[pl.ds(t * S5_NCH, S5_NCH), :] = (
            jnp.dot(uc, wy[t, 0:ku, :], preferred_element_type=f32)
            + jnp.dot(xf, wy[t, ku:ku + 2 * hs, :], preferred_element_type=f32)
            + jnp.dot(xb, wy[t, ku + 2 * hs:ku + 4 * hs, :], preferred_element_type=f32))
    for t in range(T):
        for b in range(NB_):
            y_ref[pl.ds(b * LA + t, S5_CPB, stride=T), :] = u32[pl.ds(t * S5_NCH + b, S5_CPB, stride=NB_), :]


def _s5_core(u_all, mats):
    cwy, cwi, a = mats
    nt = HALF // 128
    kf = S5_T * 128 + 4 * S5_NS
    return pl.pallas_call(
        _s5_kernel,
        out_shape=jax.ShapeDtypeStruct((ROWS, HALF), f32),
        grid=(nt,),
        in_specs=[pl.BlockSpec((ROWS, 128), lambda j: (0, j)),
                  pl.BlockSpec((None, S5_T, kf // S5_GT, 128), lambda j: (j, 0, 0, 0)),
                  pl.BlockSpec((None, 2, S5_T * 128, 2 * C_STATE), lambda j: (j, 0, 0, 0)),
                  pl.BlockSpec((None, 2, 2 * S5_NS), lambda j: (j, 0, 0))],
        out_specs=pl.BlockSpec((ROWS, 128), lambda j: (0, j)),
        scratch_shapes=[pltpu.VMEM((S5_T, kf, 128), bf16),
                        pltpu.VMEM((2, S5_T * 128, 2 * S5_NS), bf16),
                        pltpu.VMEM((ROWS, 128), f32),
                        pltpu.VMEM((S5_T, S5_NCH, 128), f32),
                        pltpu.VMEM((S5_NCH, 2 * S5_NS), f32),
                        pltpu.VMEM((S5_NCH, 2 * S5_NS), f32)],
        compiler_params=_cp(("arbitrary",), 56),
        name="s5_core",
    )(u_all, cwy, cwi, a)


def _s5_matrices(a_re, a_im, log_dt, b_re, b_im, c_re, c_im):
    T = S5_T
    G, P, C = C_GROUPS, C_STATE, C_GROUP
    nt = G // S5_GT
    lam = lax.complex(a_re.astype(f32), a_im.astype(f32))
    dt = jnp.exp(log_dt.astype(f32))[..., None]
    a_bar = jnp.exp(lam * dt)
    b_bar = ((a_bar - 1) / lam)[..., None] * lax.complex(b_re.astype(f32), b_im.astype(f32))
    tau = jnp.arange(T + 1, dtype=f32)
    apow = jnp.exp((lam * dt)[..., None] * tau)
    cc = lax.complex(c_re.astype(f32), c_im.astype(f32))
    kern = jnp.einsum("gxp,dgpt,dgpc->dgtxc", cc, apow[..., :T], b_bar).real
    t_i = jnp.arange(T)
    lag = t_i[:, None] - t_i[None, :]
    kf = kern[0][:, jnp.clip(lag, 0, T - 1)] * (lag >= 0)[None, :, :, None, None]
    kb = kern[1][:, jnp.clip(-lag, 0, T - 1)] * (lag <= 0)[None, :, :, None, None]
    km = (kf + kb).reshape(nt, S5_GT, T, T, C, C).transpose(0, 2, 3, 1, 5, 4)
    wy_u = km.reshape(nt, T, T * 128, C)
    k_f = cc[:, :, :, None] * apow[0][:, None, :, 1 + t_i]
    k_b = cc[:, :, :, None] * apow[1][:, None, :, T - t_i]

    def wy_state(k):
        k = k.reshape(nt, S5_GT, C, P, T).transpose(0, 4, 1, 3, 2).reshape(nt, T, S5_NS, C)
        return jnp.concatenate([k.real, -k.imag], axis=2)

    cwy = jnp.concatenate([wy_u, wy_state(k_f), wy_state(k_b)], axis=2).astype(bf16)
    nrow = cwy.shape[2]
    cwy = cwy.reshape(nt, T, S5_GT, nrow // S5_GT, C).transpose(0, 1, 3, 2, 4).reshape(nt, T, nrow // S5_GT, 128)
    w_f = apow[0][:, :, T - 1 - t_i][..., None] * b_bar[0][:, :, None, :]
    w_b = apow[1][:, :, t_i][..., None] * b_bar[1][:, :, None, :]

    def wi_dir(w):
        w = w.reshape(nt, S5_GT, P, T, C).transpose(0, 3, 1, 4, 2).reshape(nt, T * 128, P)
        return jnp.concatenate([w.real, w.imag], axis=2)

    cwi = jnp.stack([wi_dir(w_f), wi_dir(w_b)], axis=1).astype(bf16)
    at = apow[..., T].reshape(2, nt, S5_NS)
    a = jnp.concatenate([at.real, at.imag], axis=-1).transpose(1, 0, 2)
    return cwy, cwi, a


def _s5_glu_kernel(y_ref, u_ref, d_ref, w_ref, b_ref, o_ref, wbf_ref):
    @pl.when(pl.program_id(0) == 0)
    def _():
        wbf_ref[...] = w_ref[...].astype(bf16)

    y = y_ref[...] + d_ref[...] * u_ref[...].astype(f32)
    y = jax.nn.gelu(y, approximate=True)
    z = jnp.dot(y.astype(bf16), wbf_ref[...], preferred_element_type=f32) + b_ref[...]
    o_ref[...] = (y * jax.nn.sigmoid(z)).astype(o_ref.dtype)


def _s5_glu(y_ssm, proj, d2, glu_w, glu_b2):
    nblk = NB_ * LBPB
    return pl.pallas_call(
        _s5_glu_kernel,
        out_shape=jax.ShapeDtypeStruct((nblk * RB, HALF), bf16),
        grid=(nblk,),
        in_specs=[pl.BlockSpec((RB, HALF), lambda i: (_lat_blk(i), 0)),
                  pl.BlockSpec((RB, HALF), lambda i: (_lat_blk(i), 0)),
                  pl.BlockSpec((1, HALF), lambda i: (0, 0)),
                  pl.BlockSpec((HALF, HALF), lambda i: (0, 0)),
                  pl.BlockSpec((1, HALF), lambda i: (0, 0))],
        out_specs=pl.BlockSpec((RB, HALF), lambda i: (i, 0)),
        scratch_shapes=[pltpu.VMEM((HALF, HALF), bf16)],
        compiler_params=_cp(("arbitrary",)),
        name="s5_glu",
    )(y_ssm, proj, d2, glu_w, glu_b2)


def _hyena_filter(L, w1, b1, w2, b2, w3, freq):
    t = jnp.linspace(0.0, 1.0, L, dtype=f32)[:, None]
    w = 2 * math.pi * jnp.arange(L, dtype=f32)[:, None] / L
    bands = jnp.linspace(1e-4, HY_BANDS - 1, HY_BANDS, dtype=f32)
    feats = jnp.concatenate([t, jnp.cos(bands * w), -jnp.sin(bands * w)], axis=-1)
    h = jnp.sin(freq * (feats @ w1 + b1))
    h = jnp.sin(freq * (h @ w2 + b2))
    h = h @ w3
    fast = abs(math.log(HY_DECAY_TARGET) / HY_FAST_PCT)
    slow = abs(math.log(HY_DECAY_TARGET) / HY_SLOW_PCT)
    deltas = jnp.tile(jnp.linspace(slow, fast, HALF, dtype=f32), 2)
    h = h * jnp.exp(-t * deltas)
    h_fwd, h_bwd = jnp.split(h, 2, axis=-1)
    h_bwd = h_bwd.at[0].set(0.0)
    norm = jnp.sum(jnp.abs(h_fwd), axis=0, keepdims=True) + jnp.sum(jnp.abs(h_bwd), axis=0, keepdims=True)
    return h_fwd / norm, h_bwd / norm


def _hyena_dft_tables():
    n2 = 2 * SEQ
    n = jnp.arange(SEQ, dtype=jnp.int32)[:, None]
    k = jnp.arange(SEQ, dtype=jnp.int32)[None, :]
    kr = 64
    k1 = jnp.arange(SEQ // kr, dtype=jnp.int32)[None, :]
    k0 = jnp.arange(kr, dtype=jnp.int32)[None, :]
    ang_a = ((n * k1 * kr) % n2).astype(f32) * (2.0 * math.pi / n2)
    ang_b = ((n * k0) % n2).astype(f32) * (2.0 * math.pi / n2)
    ca, sa = jnp.cos(ang_a)[:, :, None], jnp.sin(ang_a)[:, :, None]
    cb, sb = jnp.cos(ang_b)[:, None, :], jnp.sin(ang_b)[:, None, :]
    c = (ca * cb - sa * sb).reshape(SEQ, SEQ)
    s = (sa * cb + ca * sb).reshape(SEQ, SEQ)
    nyq = jnp.where(n % 2 == 0, 1.0, -1.0).astype(f32)
    f_re, f_im = c, jnp.where(k == 0, nyq, -s)
    g_re = jnp.where(k == 0, 1.0, 2.0 * c) / n2
    g_im = jnp.where(k == 0, nyq, -2.0 * s) / n2

    def tiles(re, im):
        nk = SEQ // HY_KT
        return jnp.concatenate([re.reshape(SEQ, nk, 1, HY_KT), im.reshape(SEQ, nk, 1, HY_KT)],
                               axis=2).reshape(SEQ, 2 * SEQ).astype(bf16)

    return tiles(f_re, f_im), tiles(g_re, g_im)


def _hyena_kernel(x0_ref, x1_ref, v_ref, cw0_ref, cw1_ref, cw2_ref, cb0_ref, cb1_ref, cb2_ref, hb_ref,
                  f_ref, g_ref, ht_ref, o_ref, ut_scr, u_scr, y_scr):
    kt = pl.program_id(2)
    L = SEQ
    row = lax.broadcasted_iota(jnp.int32, (L, HY_TC), 0)

    def sconv(z_ref, cw_ref, cb_ref):
        z = z_ref[...].astype(f32)
        zm = jnp.where(row == 0, 0.0, pltpu.roll(z, 1, axis=0))
        zp = jnp.where(row == L - 1, 0.0, pltpu.roll(z, L - 1, axis=0))
        return zm * cw_ref[0:1, :] + z * cw_ref[1:2, :] + zp * cw_ref[2:3, :] + cb_ref[...]

    @pl.when(kt == 0)
    def _():
        u = sconv(x1_ref, cw1_ref, cb1_ref) * sconv(v_ref, cw2_ref, cb2_ref)
        u_scr[...] = u
        ut_scr[...] = u.T.astype(bf16)
        y_scr[...] = jnp.zeros_like(y_scr)

    xt = jnp.dot(ut_scr[...], f_ref[...], preferred_element_type=f32)
    h = ht_ref[...]
    xr, xi, hr, hi = xt[:, :HY_KT], xt[:, HY_KT:], h[:, :HY_KT], h[:, HY_KT:]
    col = lax.broadcasted_iota(jnp.int32, (HY_TC, HY_KT), 1) + kt * HY_KT
    dc = col == 0
    yr = xr * hr - jnp.where(dc, 0.0, xi * hi)
    yi = jnp.where(dc, xi * hi, xr * hi + xi * hr)
    yt = jnp.concatenate([yr, yi], axis=1).astype(bf16)
    y_scr[...] += _dot_nt(g_ref[...], yt)

    @pl.when(kt == pl.num_programs(2) - 1)
    def _():
        o_ref[...] = (sconv(x0_ref, cw0_ref, cb0_ref)
                      * (y_scr[...] + u_scr[...] * hb_ref[...])).astype(o_ref.dtype)


def _hyena(z_lat, conv_w, conv_b2, hy_bias2, ht, ft, gt):
    nct = HALF // HY_TC

    def zspec(k):
        return pl.BlockSpec((SEQ, HY_TC), lambda ct, b, kt: (b, k * nct + ct))

    def wspec(k):
        return pl.BlockSpec((3, HY_TC), lambda ct, b, kt: (0, k * nct + ct))

    def bspec(k):
        return pl.BlockSpec((1, HY_TC), lambda ct, b, kt: (0, k * nct + ct))

    return pl.pallas_call(
        _hyena_kernel,
        out_shape=jax.ShapeDtypeStruct((NB_ * SEQ, HALF), bf16),
        grid=(nct, NB_, SEQ // HY_KT),
        in_specs=[zspec(0), zspec(1), zspec(2), wspec(0), wspec(1), wspec(2), bspec(0), bspec(1), bspec(2),
                  pl.BlockSpec((1, HY_TC), lambda ct, b, kt: (0, ct)),
                  pl.BlockSpec((SEQ, 2 * HY_KT), lambda ct, b, kt: (0, kt)),
                  pl.BlockSpec((SEQ, 2 * HY_KT), lambda ct, b, kt: (0, kt)),
                  pl.BlockSpec((HY_TC, 2 * HY_KT), lambda ct, b, kt: (ct, kt))],
        out_specs=pl.BlockSpec((SEQ, HY_TC), lambda ct, b, kt: (b, ct)),
        scratch_shapes=[pltpu.VMEM((HY_TC, SEQ), bf16), pltpu.VMEM((SEQ, HY_TC), f32),
                        pltpu.VMEM((SEQ, HY_TC), f32)],
        compiler_params=_cp(("arbitrary", "arbitrary", "arbitrary"), 48),
        name="hyena",
    )(z_lat, z_lat, z_lat, conv_w, conv_w, conv_w, conv_b2, conv_b2, conv_b2, hy_bias2, ft, gt, ht)


def _hyena_filter_spectrum(h_fwd, h_bwd, ft):
    c = h_fwd.shape[1]
    ht2 = jnp.concatenate([h_fwd.T, h_bwd.T], axis=0).astype(bf16)
    xt = _matmul([ht2], ft, (), 2 * SEQ, 2 * HY_KT, 2 * c // RB, _ident, f32, name="hyena_filter_dft")
    col = jnp.arange(2 * SEQ)
    sgn = jnp.where((col % (2 * HY_KT) >= HY_KT) & (col != HY_KT), -1.0, 1.0).astype(f32)
    return xt[:c] + xt[c:] * sgn[None, :]


def _swiglu_perm():
    j = np.arange(256)
    src = np.where(j < 128, 2 * j, 2 * (j - 128) + 1)
    p = np.zeros((256, 256), np.float32)
    p[src, j] = 1.0
    return jnp.asarray(p, dtype=bf16)


def kernel(x, c, ctx, c_ctx, ada_w, ada_b, norm_mix_g, norm_ffn_g, ev_w_in, ev_gate_b, ev_h_norm_g,
           ev_q_norm_g, ev_k_norm_g, ev_sink, ev_w_out, od_w_in, od_a_re, od_a_im, od_log_dt, od_b_re,
           od_b_im, od_c_re, od_c_im, od_d, od_glu_w, od_glu_b, od_conv_w, od_conv_b, od_filt_w1,
           od_filt_b1, od_filt_w2, od_filt_b2, od_filt_w3, od_filt_freq, od_hy_bias, od_w_out,
           moe_router_w, moe_router_b, moe_w1, moe_b1, moe_w2, moe_b2):
    depth = ada_w.shape[0]
    xu = jnp.concatenate([ctx, x], axis=1).reshape(ROWS, D)
    cond8 = jnp.concatenate([c_ctx[None], c, jnp.zeros((3, D), f32)], axis=0)
    ada_b3 = ada_b.reshape(depth, 1, 6 * D)
    g_mix3 = norm_mix_g.reshape(depth, 1, D)
    g_ffn3 = norm_ffn_g.reshape(depth, 1, D)
    rb3 = moe_router_b.reshape(depth, 1, N_EXPERTS)
    b1_4 = moe_b1.reshape(depth, N_EXPERTS, 1, 2 * D_FF)
    b2_4 = moe_b2.reshape(depth, N_EXPERTS, 1, D)
    perm = _swiglu_perm()
    nall = ROWS // RB
    nlat = NB_ * LBPB

    mod3 = _ada(cond8, ada_w, ada_b3, 0).reshape(8, 1, 6 * D)
    h = _norm_mod(xu, g_mix3, mod3, 0, 0, _ident, _mod_row, nall, bf16)
    w_in = ev_w_in[0]
    n_a = 2 * A_HEADS * A_DQK + 2 * HALF
    n_g = 4 * A_HEADS
    w_main = jnp.concatenate([w_in[:, :n_a], w_in[:, n_a + n_g:]], axis=1)
    w_gate = jnp.pad(w_in[:, n_a:n_a + n_g], ((0, 0), (0, 128 - n_g)))
    proj = _matmul([h], w_main, (), w_main.shape[1], 1536, nall, _ident, bf16, name="even_in")
    gates = _matmul([h], w_gate, (), 128, 128, nall, _ident, f32, name="even_gates")[:, :n_g]
    gates = (gates + ev_gate_b[0]).reshape(NB_, LA, n_g)
    gcol, grow = _mlstm_gate_tables(gates)
    y_a = _mlstm(proj, gcol, grow, ev_h_norm_g[0].reshape(A_HEADS, 1, A_DV))
    cosf, sinf = _rope_tables()
    sink_col = jnp.repeat(ev_sink[0].astype(f32).reshape(B_KV, B_G), 128, axis=1)[..., None]
    y_b = _attention(proj, cosf, sinf, ev_q_norm_g[0].reshape(1, B_DH), ev_k_norm_g[0].reshape(1, B_DH),
                     sink_col)
    xu = _matmul([y_a, y_b], ev_w_out, (0,), D, 1024, nall, _ident, f32,
                 res=xu, res_blk=_ident, mrow=_mod_row, mod3=mod3, gate_chunk=2, name="even_out")
    xu = _moe_layer(xu, g_ffn3, mod3, moe_router_w, rb3, moe_w1, b1_4, moe_w2, b2_4, perm,
                    0, _ident, _mod_row, nall)

    mod3 = _ada(cond8, ada_w, ada_b3, 1).reshape(8, 1, 6 * D)
    h = _norm_mod(xu, g_mix3, mod3, 1, 0, _ident, _mod_row, nall, bf16)
    proj = _matmul([h], od_w_in, (0,), HALF, 1024, nall, _ident, bf16, name="odd_in_u")
    z_lat = _matmul([h], od_w_in, (0,), 3 * HALF, 1024, nlat, _lat_blk, bf16, w_col0=HALF // 1024,
                    name="odd_in_z")
    mats = _s5_matrices(od_a_re[0], od_a_im[0], od_log_dt[0], od_b_re[0], od_b_im[0], od_c_re[0], od_c_im[0])
    y_ssm = _s5_core(proj, mats)
    s_l = _s5_glu(y_ssm, proj, od_d[0].reshape(1, HALF), od_glu_w[0], od_glu_b[0].reshape(1, HALF))
    h_fwd, h_bwd = _hyena_filter(SEQ, od_filt_w1[0], od_filt_b1[0], od_filt_w2[0], od_filt_b2[0],
                                 od_filt_w3[0], od_filt_freq[0])
    ft, gt = _hyena_dft_tables()
    hy_l = _hyena(z_lat, od_conv_w[0], od_conv_b[0].reshape(1, 3 * HALF), od_hy_bias[0].reshape(1, HALF),
                  _hyena_filter_spectrum(h_fwd, h_bwd, ft), ft, gt)
    x_lat = _matmul([s_l, hy_l], od_w_out, (0,), D, 1024, nlat, _ident, f32,
                    res=xu, res_blk=_lat_blk, mrow=_lat_mod_row, mod3=mod3, gate_chunk=2, name="odd_out")
    out = _moe_layer(x_lat, g_ffn3, mod3, moe_router_w, rb3, moe_w1, b1_4, moe_w2, b2_4, perm,
                     1, _ident, _lat_mod_row, nlat)
    return out.reshape(NB_, SEQ, D)
```

```python
import functools
import math

import jax
import jax.numpy as jnp
import numpy as np
from jax import lax
from jax.experimental import pallas as pl
from jax.experimental.pallas import tpu as pltpu

f32 = jnp.float32
bf16 = jnp.bfloat16

D = 2048
NB_ = 4
SEQ = 2048
CTX = 256
LA = SEQ + CTX
ROWS = NB_ * LA
RB = 256
BPB = LA // RB
LBPB = SEQ // RB
EPS = 1e-6
GRID_W = 64

HALF = D // 2
A_HEADS = 4
A_DV = HALF // A_HEADS
A_DQK = A_DV // 2
A_T = 256
B_HEADS = 16
B_KV = 4
B_G = B_HEADS // B_KV
B_DH = HALF // B_HEADS
B_KP = 2
ATT_COL0 = 2 * A_HEADS * A_DQK + 2 * HALF
WINDOW = 128
ROPE_BASE = 10000.0
C_GROUP = 16
C_GROUPS = HALF // C_GROUP
C_STATE = 64
S5_T = 8
S5_GT = 128 // C_GROUP
S5_NS = S5_GT * C_STATE
S5_NCH = ROWS // S5_T
S5_CPB = LA // S5_T
S5_CCTX = CTX // S5_T
HY_TC = 256
HY_KT = 512
HY_BANDS = 16
HY_DECAY_TARGET = 1e-2
HY_FAST_PCT = 0.3
HY_SLOW_PCT = 1.5
N_EXPERTS = 32
TOP_K = 4
D_FF = D
SWIGLU_LIMIT = 7.0
SWIGLU_ALPHA = 1.702
MOE_TM = 256

EVEN_SPLITS = (A_HEADS * A_DQK, A_HEADS * A_DQK, HALF, HALF, 4 * A_HEADS,
               B_HEADS * B_DH, B_KV * B_DH, B_KV * B_DH)

VMEM_MB = 1024 * 1024


def _cp(sem, vmem_mb=40):
    return pltpu.CompilerParams(dimension_semantics=sem, vmem_limit_bytes=vmem_mb * VMEM_MB)


def _lat_blk(i):
    return (i // LBPB) * BPB + 1 + i % LBPB


def _mod_row(u):
    return jnp.where(u % BPB == 0, 0, 1 + u // BPB)


def _lat_mod_row(i):
    return 1 + i // LBPB


def _ident(i):
    return i


def _ada_kernel(c_ref, w_ref, b_ref, o_ref):
    c = c_ref[...]
    s = c * jax.nn.sigmoid(c)
    o_ref[...] = jnp.dot(s.astype(bf16), w_ref[...].astype(bf16),
                         preferred_element_type=f32) + b_ref[...]


def _ada(cond8, ada_w, ada_b3, layer):
    tn = 1024
    n = 6 * D
    return pl.pallas_call(
        _ada_kernel,
        out_shape=jax.ShapeDtypeStruct((8, n), f32),
        grid=(n // tn,),
        in_specs=[pl.BlockSpec((8, D), lambda j: (0, 0)),
                  pl.BlockSpec((None, D, tn), lambda j: (layer, 0, j)),
                  pl.BlockSpec((None, 1, tn), lambda j: (layer, 0, j))],
        out_specs=pl.BlockSpec((8, tn), lambda j: (0, j)),
        compiler_params=_cp(("arbitrary",)),
        name="ada_params",
    )(cond8, ada_w, ada_b3)


def _norm_mod_kernel(x_ref, g_ref, sh_ref, sc_ref, o_ref):
    x = x_ref[...]
    y = x * lax.rsqrt(jnp.mean(x * x, axis=-1, keepdims=True) + EPS) * g_ref[...]
    o_ref[...] = (y * (1.0 + sc_ref[...]) + sh_ref[...]).astype(o_ref.dtype)


def _norm_mod(x, g3, mod3, layer, chunk, blk, mrow, nblk, out_dtype):
    return pl.pallas_call(
        _norm_mod_kernel,
        out_shape=jax.ShapeDtypeStruct((nblk * RB, D), out_dtype),
        grid=(nblk,),
        in_specs=[pl.BlockSpec((RB, D), lambda i: (blk(i), 0)),
                  pl.BlockSpec((None, 1, D), lambda i: (layer, 0, 0)),
                  pl.BlockSpec((None, 1, D), lambda i: (mrow(i), 0, chunk)),
                  pl.BlockSpec((None, 1, D), lambda i: (mrow(i), 0, chunk + 1))],
        out_specs=pl.BlockSpec((RB, D), lambda i: (i, 0)),
        compiler_params=_cp(("arbitrary",)),
        name="norm_mod",
    )(x, g3, mod3, mod3)


def _split_bf16(x):
    hi = x.astype(bf16)
    lo = (x - hi.astype(f32)).astype(bf16)
    return hi, lo


def _norm_router_kernel(x_ref, g_ref, sh_ref, sc_ref, rw_ref, rb_ref,
                        h_ref, idx_ref, gate_ref, rank_ref, cnt_ref, carry_ref):
    i = pl.program_id(0)

    @pl.when(i == 0)
    def _():
        carry_ref[...] = jnp.zeros_like(carry_ref)

    x = x_ref[...]
    y = x * lax.rsqrt(jnp.mean(x * x, axis=-1, keepdims=True) + EPS) * g_ref[...]
    h = y * (1.0 + sc_ref[...]) + sh_ref[...]
    h_ref[...] = h
    h_hi, h_lo = _split_bf16(h)
    w_hi, w_lo = _split_bf16(rw_ref[...])
    logits = (jnp.dot(h_hi, w_hi, preferred_element_type=f32)
              + jnp.dot(h_lo, w_hi, preferred_element_type=f32)
              + jnp.dot(h_hi, w_lo, preferred_element_type=f32)) + rb_ref[...]
    ne = logits.shape[-1]
    lane = lax.broadcasted_iota(jnp.int32, logits.shape, 1)
    l = logits
    vals, idxs = [], []
    for _ in range(TOP_K):
        m = jnp.max(l, axis=-1, keepdims=True)
        ix = jnp.min(jnp.where(l == m, lane, ne), axis=-1, keepdims=True)
        vals.append(m)
        idxs.append(ix)
        l = jnp.where(lane == ix, -jnp.inf, l)
    es = [jnp.exp(v - vals[0]) for v in vals]
    tot = es[0] + es[1] + es[2] + es[3]
    onehot = jnp.zeros(logits.shape, f32)
    for ix in idxs:
        onehot = onehot + jnp.where(lane == ix, 1.0, 0.0)
    r_i = lax.broadcasted_iota(jnp.int32, (RB, RB), 0)
    c_i = lax.broadcasted_iota(jnp.int32, (RB, RB), 1)
    tri = jnp.where(c_i < r_i, 1.0, 0.0).astype(bf16)
    prefix = jnp.dot(tri, onehot.astype(bf16), preferred_element_type=f32) + carry_ref[...]
    for k in range(TOP_K):
        idx_ref[:, k:k + 1] = idxs[k]
        gate_ref[:, k:k + 1] = es[k] / tot
        rk = jnp.sum(jnp.where(lane == idxs[k], prefix, 0.0), axis=-1, keepdims=True)
        rank_ref[:, k:k + 1] = rk.astype(jnp.int32)
    carry_ref[...] = carry_ref[...] + jnp.sum(onehot, axis=0, keepdims=True)
    cnt_ref[...] = carry_ref[...]


def _norm_router(x, g3, mod3, rw, rb3, layer, chunk, blk, mrow, nblk):
    t = nblk * RB
    return pl.pallas_call(
        _norm_router_kernel,
        out_shape=(jax.ShapeDtypeStruct((t, D), f32),
                   jax.ShapeDtypeStruct((t, TOP_K), jnp.int32),
                   jax.ShapeDtypeStruct((t, TOP_K), f32),
                   jax.ShapeDtypeStruct((t, TOP_K), jnp.int32),
                   jax.ShapeDtypeStruct((1, N_EXPERTS), f32)),
        grid=(nblk,),
        in_specs=[pl.BlockSpec((RB, D), lambda i: (blk(i), 0)),
                  pl.BlockSpec((None, 1, D), lambda i: (layer, 0, 0)),
                  pl.BlockSpec((None, 1, D), lambda i: (mrow(i), 0, chunk)),
                  pl.BlockSpec((None, 1, D), lambda i: (mrow(i), 0, chunk + 1)),
                  pl.BlockSpec((None, D, N_EXPERTS), lambda i: (layer, 0, 0)),
                  pl.BlockSpec((None, 1, N_EXPERTS), lambda i: (layer, 0, 0))],
        out_specs=(pl.BlockSpec((RB, D), lambda i: (i, 0)),
                   pl.BlockSpec((RB, TOP_K), lambda i: (i, 0)),
                   pl.BlockSpec((RB, TOP_K), lambda i: (i, 0)),
                   pl.BlockSpec((RB, TOP_K), lambda i: (i, 0)),
                   pl.BlockSpec((1, N_EXPERTS), lambda i: (0, 0))),
        scratch_shapes=[pltpu.VMEM((1, N_EXPERTS), f32)],
        compiler_params=_cp(("arbitrary",)),
        name="norm_router",
    )(x, g3, mod3, mod3, rw, rb3)


def _mm_kernel(*refs, n_a, k_split, epilogue):
    a_refs = refs[:n_a]
    w_ref = refs[n_a]
    pos = n_a + 1
    if epilogue:
        res_ref, gate_ref = refs[pos], refs[pos + 1]
        pos += 2
    o_ref, wbf_ref = refs[pos], refs[pos + 1]

    @pl.when(pl.program_id(1) == 0)
    def _():
        wbf_ref[...] = w_ref[...].astype(bf16)

    acc = None
    for t, a_ref in enumerate(a_refs):
        part = jnp.dot(a_ref[...], wbf_ref[t * k_split:(t + 1) * k_split, :],
                       preferred_element_type=f32)
        acc = part if acc is None else acc + part
    if epilogue:
        acc = res_ref[...] + gate_ref[...] * acc
    o_ref[...] = acc.astype(o_ref.dtype)


def _matmul(a_list, w, w_idx, n, tn, nblk, a_blk, out_dtype, *, a_col=None, w_col0=0,
            res=None, res_blk=None, mrow=None, mod3=None, gate_chunk=None, name="matmul"):
    n_a = len(a_list)
    k = w.shape[-2]
    k_split = k // n_a
    lead = len(w_idx)
    a_col = a_col or [0] * n_a
    in_specs = [pl.BlockSpec((RB, k_split), functools.partial(lambda j, i, c: (a_blk(i), c), c=c))
                for c in a_col]
    in_specs.append(pl.BlockSpec((None,) * lead + (k, tn), lambda j, i: tuple(w_idx) + (0, j + w_col0)))
    args = list(a_list) + [w]
    epilogue = res is not None
    if epilogue:
        per = D // tn
        in_specs.append(pl.BlockSpec((RB, tn), lambda j, i: (res_blk(i), j)))
        in_specs.append(pl.BlockSpec((None, 1, tn),
                                     lambda j, i: (mrow(i), 0, gate_chunk * per + j)))
        args += [res, mod3]
    return pl.pallas_call(
        functools.partial(_mm_kernel, n_a=n_a, k_split=k_split, epilogue=epilogue),
        out_shape=jax.ShapeDtypeStruct((nblk * RB, n), out_dtype),
        grid=(n // tn, nblk),
        in_specs=in_specs,
        out_specs=pl.BlockSpec((RB, tn), lambda j, i: (i, j)),
        scratch_shapes=[pltpu.VMEM((k, tn), bf16)],
        compiler_params=_cp(("arbitrary", "arbitrary"), 48),
        name=name,
    )(*args)


def _gather_rows_kernel(tok_ref, h_hbm, o_ref, buf, sem, *, rows, nsteps):
    i = pl.program_id(0)

    def issue(blk, slot):
        base = blk * rows
        for r in range(rows):
            t = tok_ref[base + r]
            pltpu.make_async_copy(h_hbm.at[pl.ds(t, 1), :], buf.at[slot, pl.ds(r, 1), :],
                                  sem.at[slot]).start(priority=r % 2)

    @pl.when(i == 0)
    def _():
        issue(0, 0)

    for par in range(2):
        @pl.when(jnp.logical_and(i + 1 < nsteps, (i + 1) % 2 == par))
        def _():
            issue(i + 1, par)

    slot = i % 2
    pltpu.make_async_copy(h_hbm.at[pl.ds(0, rows), :], buf.at[slot], sem.at[slot]).wait()
    o_ref[...] = buf[slot].astype(o_ref.dtype)


def _gather_rows(slot_tok, h, nr):
    rows = 512
    nsteps = nr // rows
    return pl.pallas_call(
        functools.partial(_gather_rows_kernel, rows=rows, nsteps=nsteps),
        out_shape=jax.ShapeDtypeStruct((nr, D), bf16),
        grid_spec=pltpu.PrefetchScalarGridSpec(
            num_scalar_prefetch=1,
            grid=(nsteps,),
            in_specs=[pl.BlockSpec(memory_space=pl.ANY)],
            out_specs=pl.BlockSpec((rows, D), lambda i, tok: (i, 0)),
            scratch_shapes=[pltpu.VMEM((2, rows, D), f32), pltpu.SemaphoreType.DMA((2,))]),
        compiler_params=_cp(("arbitrary",)),
        name="moe_gather",
    )(slot_tok, h)


def _moe_up_kernel(start_ref, nblk_ref, xs_hbm, w_hbm, b_ref, p_ref, act_hbm,
                   wbf_ref, wf32, xbuf, obuf, wsem, xsem, osem, *, tn, layer):
    j = pl.program_id(0)
    e = pl.program_id(1)
    nb = nblk_ref[e]
    row0 = start_ref[e]
    g = j * N_EXPERTS + e

    def w_copy(gg, slot):
        c0 = pl.multiple_of((gg // N_EXPERTS) * tn, tn)
        return pltpu.make_async_copy(w_hbm.at[layer, gg % N_EXPERTS, :, pl.ds(c0, tn)], wf32.at[slot],
                                     wsem.at[slot])

    def x_copy(blk, slot):
        r = pl.multiple_of(row0 + blk * MOE_TM, MOE_TM)
        return pltpu.make_async_copy(xs_hbm.at[pl.ds(r, MOE_TM), :], xbuf.at[slot], xsem.at[slot])

    def o_copy(blk, slot):
        r = pl.multiple_of(row0 + blk * MOE_TM, MOE_TM)
        return pltpu.make_async_copy(obuf.at[slot], act_hbm.at[j, pl.ds(r, MOE_TM), :], osem.at[slot])

    @pl.when(g == 0)
    def _():
        w_copy(0, 0).start(priority=1)

    @pl.when(nb > 0)
    def _():
        x_copy(0, 0).start()

    @pl.when(g + 1 < pl.num_programs(0) * N_EXPERTS)
    def _():
        w_copy(g + 1, (g + 1) % 2).start(priority=1)

    w_copy(g, g % 2).wait()

    @pl.when(nb > 0)
    def _():
        wbf_ref[...] = wf32[g % 2].astype(bf16)

    def body(blk, c):
        slot = blk % 2
        x_copy(blk, slot).wait()

        @pl.when(blk + 1 < nb)
        def _():
            x_copy(blk + 1, 1 - slot).start()

        h = jnp.dot(xbuf[slot], wbf_ref[...], preferred_element_type=f32) + b_ref[...]

        @pl.when(blk >= 2)
        def _():
            o_copy(blk - 2, slot).wait()

        for cc in range(tn // 256):
            d = jnp.dot(h[:, cc * 256:(cc + 1) * 256].astype(bf16), p_ref[...],
                        preferred_element_type=f32)
            gate = jnp.minimum(d[:, :128], SWIGLU_LIMIT)
            lin = jnp.clip(d[:, 128:], -SWIGLU_LIMIT, SWIGLU_LIMIT)
            act = gate * jax.nn.sigmoid(SWIGLU_ALPHA * gate) * (lin + 1.0)
            obuf[slot, :, cc * 128:(cc + 1) * 128] = act.astype(obuf.dtype)
        o_copy(blk, slot).start()
        return c

    lax.fori_loop(0, nb, body, 0)

    @pl.when(nb >= 2)
    def _():
        o_copy(nb - 2, nb % 2).wait()

    @pl.when(nb >= 1)
    def _():
        o_copy(nb - 1, (nb - 1) % 2).wait()

    @pl.when(e == N_EXPERTS - 1)
    def _():
        obuf[0] = jnp.zeros(obuf.shape[1:], obuf.dtype)

        def zero_tail(blk, c):
            r = pl.multiple_of(blk * MOE_TM, MOE_TM)
            cp = pltpu.make_async_copy(obuf.at[0], act_hbm.at[j, pl.ds(r, MOE_TM), :], osem.at[0])
            cp.start()
            cp.wait()
            return c

        lax.fori_loop(row0 // MOE_TM + nb, act_hbm.shape[1] // MOE_TM, zero_tail, 0)


def _moe_up(start, nblk_e, xs, w1, b1, perm, layer, nr):
    tn = 2048
    nj = 2 * D_FF // tn
    return pl.pallas_call(
        functools.partial(_moe_up_kernel, tn=tn, layer=layer),
        out_shape=jax.ShapeDtypeStruct((nj, nr, tn // 2), bf16),
        grid_spec=pltpu.PrefetchScalarGridSpec(
            num_scalar_prefetch=2,
            grid=(nj, N_EXPERTS),
            in_specs=[pl.BlockSpec(memory_space=pl.ANY),
                      pl.BlockSpec(memory_space=pl.ANY),
                      pl.BlockSpec((None, None, 1, tn), lambda j, e, st, nb: (layer, e, 0, j)),
                      pl.BlockSpec((256, 256), lambda j, e, st, nb: (0, 0))],
            out_specs=pl.BlockSpec(memory_space=pl.ANY),
            scratch_shapes=[pltpu.VMEM((D, tn), bf16),
                            pltpu.VMEM((2, D, tn), f32),
                            pltpu.VMEM((2, MOE_TM, D), bf16),
                            pltpu.VMEM((2, MOE_TM, tn // 2), bf16),
                            pltpu.SemaphoreType.DMA((2,)), pltpu.SemaphoreType.DMA((2,)),
                            pltpu.SemaphoreType.DMA((2,))]),
        compiler_params=_cp(("arbitrary", "arbitrary"), 56),
        name="moe_up",
    )(start, nblk_e, xs, w1, b1, perm)


def _moe_down_kernel(start_ref, nblk_ref, act_hbm, w_hbm, b_ref, y_hbm, wbf_ref, wf32, abuf, obuf,
                     wsem, asem, osem, *, layer):
    e = pl.program_id(0)
    nb = nblk_ref[e]
    row0 = start_ref[e]
    nj = abuf.shape[1]
    kh = D_FF // nj

    def w_copy(ee, slot):
        return pltpu.make_async_copy(w_hbm.at[layer, ee], wf32.at[slot], wsem.at[slot])

    def a_copy(blk, slot, jj):
        r = pl.multiple_of(row0 + blk * MOE_TM, MOE_TM)
        return pltpu.make_async_copy(act_hbm.at[jj, pl.ds(r, MOE_TM), :], abuf.at[slot, jj], asem.at[slot])

    def o_copy(blk, slot):
        r = pl.multiple_of(row0 + blk * MOE_TM, MOE_TM)
        return pltpu.make_async_copy(obuf.at[slot], y_hbm.at[pl.ds(r, MOE_TM), :], osem.at[slot])

    @pl.when(e == 0)
    def _():
        w_copy(0, 0).start(priority=1)

    @pl.when(nb > 0)
    def _():
        for jj in range(nj):
            a_copy(0, 0, jj).start()

    @pl.when(e + 1 < N_EXPERTS)
    def _():
        w_copy(e + 1, (e + 1) % 2).start(priority=1)

    w_copy(e, e % 2).wait()

    @pl.when(nb > 0)
    def _():
        wbf_ref[...] = wf32[e % 2].astype(bf16)

    def body(blk, c):
        slot = blk % 2
        for jj in range(nj):
            a_copy(blk, slot, jj).wait()

        @pl.when(blk + 1 < nb)
        def _():
            for jj in range(nj):
                a_copy(blk + 1, 1 - slot, jj).start()

        y = b_ref[...] + jnp.dot(abuf[slot, 0], wbf_ref[0:kh, :], preferred_element_type=f32)
        for jj in range(1, nj):
            y = y + jnp.dot(abuf[slot, jj], wbf_ref[jj * kh:(jj + 1) * kh, :], preferred_element_type=f32)

        @pl.when(blk >= 2)
        def _():
            o_copy(blk - 2, slot).wait()

        obuf[slot] = y
        o_copy(blk, slot).start()
        return c

    lax.fori_loop(0, nb, body, 0)

    @pl.when(nb >= 2)
    def _():
        o_copy(nb - 2, nb % 2).wait()

    @pl.when(nb >= 1)
    def _():
        o_copy(nb - 1, (nb - 1) % 2).wait()

    @pl.when(e == N_EXPERTS - 1)
    def _():
        obuf[0] = jnp.zeros(obuf.shape[1:], obuf.dtype)

        def zero_tail(blk, c):
            r = pl.multiple_of(blk * MOE_TM, MOE_TM)
            cp = pltpu.make_async_copy(obuf.at[0], y_hbm.at[pl.ds(r, MOE_TM), :], osem.at[0])
            cp.start()
            cp.wait()
            return c

        lax.fori_loop(row0 // MOE_TM + nb, y_hbm.shape[0] // MOE_TM, zero_tail, 0)


def _moe_down(start, nblk_e, act, w2, b2, layer, nr):
    nj = act.shape[0]
    return pl.pallas_call(
        functools.partial(_moe_down_kernel, layer=layer),
        out_shape=jax.ShapeDtypeStruct((nr, D), f32),
        grid_spec=pltpu.PrefetchScalarGridSpec(
            num_scalar_prefetch=2,
            grid=(N_EXPERTS,),
            in_specs=[pl.BlockSpec(memory_space=pl.ANY),
                      pl.BlockSpec(memory_space=pl.ANY),
                      pl.BlockSpec((None, None, 1, D), lambda e, st, nb: (layer, e, 0, 0))],
            out_specs=pl.BlockSpec(memory_space=pl.ANY),
            scratch_shapes=[pltpu.VMEM((D_FF, D), bf16),
                            pltpu.VMEM((2, D_FF, D), f32),
                            pltpu.VMEM((2, nj, MOE_TM, D_FF // nj), bf16),
                            pltpu.VMEM((2, MOE_TM, D), f32),
                            pltpu.SemaphoreType.DMA((2,)), pltpu.SemaphoreType.DMA((2,)),
                            pltpu.SemaphoreType.DMA((2,))]),
        compiler_params=_cp(("arbitrary",), 56),
        name="moe_down",
    )(start, nblk_e, act, w2, b2)


def _combine_kernel(slot_ref, y_hbm, x_ref, gmod_ref, w_ref, o_ref, buf, sem, *, tm, nsteps):
    i = pl.program_id(0)

    def issue(blk, slot):
        base = blk * (tm * TOP_K)
        for r in range(tm):
            for k in range(TOP_K):
                s = slot_ref[base + r * TOP_K + k]
                pltpu.make_async_copy(y_hbm.at[pl.ds(s, 1), :], buf.at[slot, k, pl.ds(r, 1), :],
                                      sem.at[slot]).start(priority=k % 2)

    @pl.when(i == 0)
    def _():
        issue(0, 0)

    for par in range(2):
        @pl.when(jnp.logical_and(i + 1 < nsteps, (i + 1) % 2 == par))
        def _():
            issue(i + 1, par)

    slot = i % 2
    for k in range(TOP_K):
        pltpu.make_async_copy(y_hbm.at[pl.ds(0, tm), :], buf.at[slot, k], sem.at[slot]).wait()
    w = w_ref[...]
    acc = ((w[:, 0:1] * buf[slot, 0] + w[:, 1:2] * buf[slot, 1])
           + (w[:, 2:3] * buf[slot, 2] + w[:, 3:4] * buf[slot, 3]))
    o_ref[...] = x_ref[...] + gmod_ref[...] * acc


def _combine(slot_flat, ys, x, mod3, gate, gate_chunk, blk, mrow, nblk):
    tm = RB
    return pl.pallas_call(
        functools.partial(_combine_kernel, tm=tm, nsteps=nblk),
        out_shape=jax.ShapeDtypeStruct((nblk * tm, D), f32),
        grid_spec=pltpu.PrefetchScalarGridSpec(
            num_scalar_prefetch=1,
            grid=(nblk,),
            in_specs=[pl.BlockSpec(memory_space=pl.ANY),
                      pl.BlockSpec((tm, D), lambda i, s: (blk(i), 0)),
                      pl.BlockSpec((None, 1, D), lambda i, s: (mrow(i), 0, gate_chunk)),
                      pl.BlockSpec((tm, TOP_K), lambda i, s: (i, 0))],
            out_specs=pl.BlockSpec((tm, D), lambda i, s: (i, 0)),
            scratch_shapes=[pltpu.VMEM((2, TOP_K, tm, D), f32), pltpu.SemaphoreType.DMA((2,))]),
        compiler_params=_cp(("arbitrary",)),
        name="moe_combine",
    )(slot_flat, ys, x, mod3, gate)


def _moe_layer(x, g3, mod3, rw, rb3, w1, b1, w2, b2, perm, layer, blk, mrow, nblk):
    t = nblk * RB
    h, idx, gate, rank, cnt = _norm_router(x, g3, mod3, rw, rb3, layer, 3, blk, mrow, nblk)
    n_assign = t * TOP_K
    nb = n_assign // MOE_TM + N_EXPERTS
    nr = nb * MOE_TM
    counts = cnt[0].astype(jnp.int32)
    padded = (counts + MOE_TM - 1) // MOE_TM * MOE_TM
    pad_end = jnp.cumsum(padded)
    pad_start = pad_end - padded
    slot = pad_start[idx] + rank
    slot_flat = slot.reshape(-1)
    tok = jnp.repeat(jnp.arange(t, dtype=jnp.int32), TOP_K)
    slot_tok = (jnp.arange(nr, dtype=jnp.int32) % t).at[slot_flat].set(tok)
    nblk_e = (padded // MOE_TM).astype(jnp.int32)
    xs = _gather_rows(slot_tok, h, nr)
    act = _moe_up(pad_start.astype(jnp.int32), nblk_e, xs, w1, b1, perm, layer, nr)
    ys = _moe_down(pad_start.astype(jnp.int32), nblk_e, act, w2, b2, layer, nr)
    return _combine(slot_flat, ys, x, mod3, gate, 5, blk, mrow, nblk)


def _dot_nt(a, b):
    return lax.dot_general(a, b, (((1,), (1,)), ((), ())), preferred_element_type=f32)


def _dot_tn(a, b):
    return lax.dot_general(a, b, (((0,), (0,)), ((), ())), preferred_element_type=f32)


def _mlstm_kernel(q_ref, k_ref, v_ref, o_ref, gc_ref, gr_ref, ng_ref, y_ref, hf_ref, hb_ref, ct_ref):
    T = A_T
    nch = LA // T
    ct_ref[...] = jnp.zeros_like(ct_ref)
    r_i = lax.broadcasted_iota(jnp.int32, (T, T), 0)
    c_i = lax.broadcasted_iota(jnp.int32, (T, T), 1)
    lane = lax.broadcasted_iota(jnp.int32, (T, 128), 1)
    ones_blk = jnp.where(lane == 0, 1.0, 0.0).astype(bf16)

    def chunk(c, dirn, h_ref):
        r0 = pl.multiple_of(c * T, T)
        q = q_ref[pl.ds(r0, T), :]
        k = k_ref[pl.ds(r0, T), :]
        v = v_ref[pl.ds(r0, T), :]
        gc = gc_ref[pl.ds(r0, T), :]
        o = 5 * dirn
        a_col, m_col, negm = gc[:, o:o + 1], gc[:, o + 1:o + 2], gc[:, o + 2:o + 3]
        m_end, m_prev = gc[:, o + 3:o + 4], gc[:, o + 4:o + 5]
        a_row = gr_ref[dirn:dirn + 1, pl.ds(r0, T)]
        mask = (c_i <= r_i) if dirn == 0 else (c_i >= r_i)
        w = jnp.exp(jnp.where(mask, a_row - m_col, -jnp.inf))
        s = _dot_nt(q, k) * (A_DQK ** -0.5) * w
        v_aug = jnp.concatenate([v, ones_blk], axis=1)
        ct = ct_ref[dirn]
        w_carry = jnp.exp(m_prev - m_col)
        num = (jnp.dot(s.astype(bf16), v_aug, preferred_element_type=f32)
               + w_carry * jnp.dot(q, ct.astype(bf16), preferred_element_type=f32))
        den = num[:, A_DV:A_DV + 1]
        h_ref[pl.ds(r0, T), :] = num[:, :A_DV] / jnp.maximum(jnp.abs(den), jnp.exp(negm))
        w_in = jnp.exp(a_col - m_end)
        decay = jnp.exp(m_prev[0:1, :] - m_end[0:1, :])
        kv = _dot_tn(k, (w_in * v_aug.astype(f32)).astype(bf16)) * (A_DQK ** -0.5)
        ct_ref[dirn] = decay * ct + kv

    def step(it, carry):
        chunk(it, 0, hf_ref)
        chunk(jnp.where(it == 0, 0, nch - it), 1, hb_ref)
        return carry

    lax.fori_loop(0, nch, step, 0)
    h = hf_ref[...] + hb_ref[...]
    hn = h * lax.rsqrt(jnp.mean(h * h, axis=-1, keepdims=True) + EPS) * ng_ref[...]
    y_ref[...] = (hn * jax.nn.sigmoid(o_ref[...].astype(f32))).astype(y_ref.dtype)


def _mlstm(proj, gcol, grow, ng3):
    qb, vb = A_DQK, A_DV
    return pl.pallas_call(
        _mlstm_kernel,
        out_shape=jax.ShapeDtypeStruct((ROWS, HALF), bf16),
        grid=(NB_, A_HEADS),
        in_specs=[pl.BlockSpec((LA, qb), lambda b, h: (b, h)),
                  pl.BlockSpec((LA, qb), lambda b, h: (b, A_HEADS + h)),
                  pl.BlockSpec((LA, vb), lambda b, h: (b, (2 * A_HEADS * qb) // vb + h)),
                  pl.BlockSpec((LA, vb), lambda b, h: (b, (2 * A_HEADS * qb + HALF) // vb + h)),
                  pl.BlockSpec((None, None, LA, 16), lambda b, h: (b, h, 0, 0)),
                  pl.BlockSpec((None, None, 8, LA), lambda b, h: (b, h, 0, 0)),
                  pl.BlockSpec((None, 1, vb), lambda b, h: (h, 0, 0))],
        out_specs=pl.BlockSpec((LA, vb), lambda b, h: (b, h)),
        scratch_shapes=[pltpu.VMEM((LA, vb), f32), pltpu.VMEM((LA, vb), f32),
                        pltpu.VMEM((2, qb, vb + 128), f32)],
        compiler_params=_cp(("arbitrary", "arbitrary")),
        name="mlstm",
    )(proj, proj, proj, proj, gcol, grow, ng3)


def _mlstm_gate_tables(gates):
    i_f, f_f, i_b, f_b = jnp.split(gates, 4, axis=-1)

    def scan_tables(i_pre, f_pre):
        lf = jax.nn.log_sigmoid(f_pre)
        big_f = jnp.cumsum(lf, axis=1)
        a = i_pre - big_f
        m_run = jnp.maximum(lax.cummax(a, axis=1), 0.0)
        m_end = jnp.repeat(m_run[:, A_T - 1::A_T], A_T, axis=1)
        m_prev = jnp.concatenate([jnp.zeros_like(m_end[:, :A_T]), m_end[:, :-A_T]], axis=1)
        return a, m_run, -(big_f + m_run), m_end, m_prev

    def to_bwd(t):
        return jnp.concatenate([jnp.flip(t[:, :CTX], axis=1), jnp.flip(t[:, CTX:], axis=1)], axis=1)

    fw = scan_tables(i_f, f_f)
    bw = [to_bwd(t) for t in scan_tables(to_bwd(i_b), to_bwd(f_b))]
    cols = jnp.stack(list(fw) + bw, axis=-1)
    cols = jnp.pad(cols, ((0, 0), (0, 0), (0, 0), (0, 6))).transpose(0, 2, 1, 3)
    rows = jnp.stack([fw[0], bw[0]], axis=1).transpose(0, 3, 1, 2)
    rows = jnp.pad(rows, ((0, 0), (0, 0), (0, 6), (0, 0)))
    return cols, rows


def _swap_halves(t):
    hd = t.shape[-1] // 2
    return jnp.concatenate([t[..., hd:], t[..., :hd]], axis=-1)


def _attn_kernel(q_ref, k_ref, v_ref, cos_ref, sin_ref, gq_ref, gk_ref, sink_ref, o_ref):
    n = pl.program_id(2)
    scale = B_DH ** -0.5
    nq = B_G * 128

    def norm(t, g_ref):
        return t * lax.rsqrt(jnp.mean(t * t, axis=-1, keepdims=True) + EPS) * g_ref[...]

    def head_cols(ref, rows, h):
        return ref[rows, h * B_DH:(h + 1) * B_DH]

    def finish(j, parts):
        sink = sink_ref[j]
        m = sink
        for s, _ in parts:
            m = jnp.maximum(m, jnp.max(s, axis=-1, keepdims=True))
        den = jnp.exp(sink - m)
        acc = None
        for s, vv in parts:
            p = jnp.exp(s - m)
            den = den + jnp.sum(p, axis=-1, keepdims=True)
            pv = jnp.dot(p.astype(bf16), vv, preferred_element_type=f32)
            acc = pv if acc is None else acc + pv
        o = (acc / den).astype(o_ref.dtype)
        for g in range(B_G):
            h = j * B_G + g
            o_ref[:, h * B_DH:(h + 1) * B_DH] = o[g * 128:(g + 1) * 128]

    def q_heads(j):
        q = jnp.concatenate([head_cols(q_ref, slice(None), j * B_G + g) for g in range(B_G)], axis=0)
        return norm(q.astype(f32), gq_ref)

    def ctx_kv(j):
        kc = norm(head_cols(k_ref, pl.ds(0, CTX), j).astype(f32), gk_ref).astype(bf16)
        return kc, head_cols(v_ref, pl.ds(0, CTX), j)

    @pl.when(n < CTX // 128)
    def _():
        for j in range(B_KP):
            kc, vc = ctx_kv(j)
            s_ctx = _dot_nt(q_heads(j).astype(bf16), kc) * scale
            finish(j, [(s_ctx, vc)])

    @pl.when(n >= CTX // 128)
    def _():
        nl = n - CTX // 128
        pos0 = pl.multiple_of(nl * 128, 128)
        start = pl.multiple_of(jnp.clip((nl - 1) * 128, 0, SEQ - 3 * 128), 128)
        cq = jnp.concatenate([cos_ref[pl.ds(pos0, 128), :]] * B_G, axis=0)
        sq = jnp.concatenate([sin_ref[pl.ds(pos0, 128), :]] * B_G, axis=0)
        q_pos = pos0 + lax.broadcasted_iota(jnp.int32, (nq, 3 * 128), 0) % 128
        k_pos = start + lax.broadcasted_iota(jnp.int32, (nq, 3 * 128), 1)
        in_window = jnp.abs(q_pos - k_pos) <= WINDOW
        for j in range(B_KP):
            kc, vc = ctx_kv(j)
            qn = q_heads(j)
            qr = qn * cq + _swap_halves(qn) * sq
            kb = norm(head_cols(k_ref, pl.ds(CTX + start, 3 * 128), j).astype(f32), gk_ref)
            kb = kb * cos_ref[pl.ds(start, 3 * 128), :] + _swap_halves(kb) * sin_ref[pl.ds(start, 3 * 128), :]
            vb = head_cols(v_ref, pl.ds(CTX + start, 3 * 128), j)
            s_band = jnp.where(in_window, _dot_nt(qr.astype(bf16), kb.astype(bf16)) * scale, -jnp.inf)
            s_ctx = _dot_nt(qn.astype(bf16), kc) * scale
            finish(j, [(s_band, vb), (s_ctx, vc)])


def _attention(proj, cosf, sinf, gq, gk, sink_col):
    nblk = LA // 128
    qw = B_KP * B_G * B_DH
    kw = B_KP * B_DH
    return pl.pallas_call(
        _attn_kernel,
        out_shape=jax.ShapeDtypeStruct((ROWS, HALF), bf16),
        grid=(NB_, B_KV // B_KP, nblk),
        in_specs=[pl.BlockSpec((128, qw), lambda b, kp, n: (b * nblk + n, ATT_COL0 // qw + kp)),
                  pl.BlockSpec((LA, kw), lambda b, kp, n: (b, (ATT_COL0 + HALF) // kw + kp)),
                  pl.BlockSpec((LA, kw), lambda b, kp, n: (b, (ATT_COL0 + HALF + B_KV * B_DH) // kw + kp)),
                  pl.BlockSpec((SEQ, B_DH), lambda b, kp, n: (0, 0)),
                  pl.BlockSpec((SEQ, B_DH), lambda b, kp, n: (0, 0)),
                  pl.BlockSpec((1, B_DH), lambda b, kp, n: (0, 0)),
                  pl.BlockSpec((1, B_DH), lambda b, kp, n: (0, 0)),
                  pl.BlockSpec((B_KP, B_G * 128, 1), lambda b, kp, n: (kp, 0, 0))],
        out_specs=pl.BlockSpec((128, qw), lambda b, kp, n: (b * nblk + n, kp)),
        compiler_params=_cp(("arbitrary", "arbitrary", "arbitrary")),
        name="window_attn",
    )(proj, proj, proj, cosf, sinf, gq, gk, sink_col)


def _rope_tables():
    rows = SEQ // GRID_W
    row = jnp.repeat(jnp.arange(rows), GRID_W).astype(f32)
    col = jnp.tile(jnp.arange(GRID_W), rows).astype(f32)
    n_freq = B_DH // 4
    inv = ROPE_BASE ** (-jnp.arange(n_freq, dtype=f32) / n_freq)
    ang = jnp.concatenate([row[:, None] * inv, col[:, None] * inv], axis=-1)
    cos, sin = jnp.cos(ang), jnp.sin(ang)
    return jnp.concatenate([cos, cos], axis=-1), jnp.concatenate([-sin, sin], axis=-1)


def _s5_kernel(u_ref, cwy_ref, cwi_ref, a_ref, y_ref, wy, wi, u32, ucat, zf, zb):
    T = S5_T
    hs = S5_NS
    ku = T * 128

    def onehot_tile(n, reps):
        r = lax.broadcasted_iota(jnp.int32, (n, n * reps), 0)
        c = lax.broadcasted_iota(jnp.int32, (n, n * reps), 1)
        return jnp.where(c % n == r, 1.0, 0.0).astype(bf16)

    def group_mask(rows, row_group, width):
        r = lax.broadcasted_iota(jnp.int32, (rows, S5_GT * width), 0)
        c = lax.broadcasted_iota(jnp.int32, (rows, S5_GT * width), 1)
        return row_group(r) == c // width

    rep_p = onehot_tile(C_STATE, S5_GT)
    m_i = group_mask(ku, lambda r: (r // C_GROUP) % S5_GT, C_STATE)
    nrp = (ku + 4 * hs) // S5_GT
    sa = lax.broadcasted_iota(jnp.int32, (128, 128), 0)
    sb = lax.broadcasted_iota(jnp.int32, (128, 128), 1)
    rr = lax.broadcasted_iota(jnp.int32, (nrp, 128), 0)
    cg = lax.broadcasted_iota(jnp.int32, (nrp, 128), 1) // C_GROUP
    for q in range(S5_GT):
        sel = jnp.where(sa == q * C_GROUP + sb % C_GROUP, 1.0, 0.0).astype(bf16)
        r = rr + q * nrp
        grp = jnp.where(r < ku, (r // C_GROUP) % S5_GT, ((r - ku) % hs) // C_STATE)
        for t in range(T):
            e = jnp.dot(cwy_ref[t], sel, preferred_element_type=f32)
            wy[t, q * nrp:(q + 1) * nrp, :] = jnp.where(grp == cg, e, 0.0).astype(bf16)
    for d in range(2):
        for q in range(2):
            e = jnp.dot(cwi_ref[d, :, q * C_STATE:(q + 1) * C_STATE], rep_p, preferred_element_type=f32)
            wi[d, :, q * hs:(q + 1) * hs] = jnp.where(m_i, e, 0.0).astype(bf16)

    u32[...] = u_ref[...].astype(f32)
    for s in range(T):
        for b in range(NB_):
            ucat[s, pl.ds(b, S5_CPB, stride=NB_), :] = u32[pl.ds(b * LA + s, S5_CPB, stride=T), :]
    uc = jnp.concatenate([ucat[s].astype(bf16) for s in range(T)], axis=1)
    zf[...] = jnp.dot(uc, wi[0], preferred_element_type=f32)
    zb[...] = jnp.dot(uc, wi[1], preferred_element_type=f32)
    a = a_ref[...]

    def cstep(av, x, z):
        ar, ai = av[:, :hs], av[:, hs:]
        xr, xi = x[:, :hs], x[:, hs:]
        return jnp.concatenate([ar * xr - ai * xi + z[:, :hs], ar * xi + ai * xr + z[:, hs:]], axis=1)

    def step(i, carry):
        x_f, x_b = carry
        rf = pl.multiple_of(i * 2 * NB_, 2 * NB_)
        mb = jnp.where(i < S5_CCTX // 2, S5_CCTX // 2 - 1 - i, S5_CPB // 2 + S5_CCTX // 2 - 1 - i)
        rb = pl.multiple_of(mb * 2 * NB_, 2 * NB_)
        z8 = zf[pl.ds(rf, 2 * NB_), :]
        x1 = cstep(a[0:1], x_f, z8[:NB_])
        zf[pl.ds(rf, 2 * NB_), :] = jnp.concatenate([x_f, x1], axis=0)
        x_f2 = cstep(a[0:1], x1, z8[NB_:])
        w8 = zb[pl.ds(rb, 2 * NB_), :]
        y1 = cstep(a[1:2], x_b, w8[NB_:])
        zb[pl.ds(rb, 2 * NB_), :] = jnp.concatenate([y1, x_b], axis=0)
        x_b2 = cstep(a[1:2], y1, w8[:NB_])
        return x_f2, x_b2

    zero = jnp.zeros((NB_, 2 * hs), f32)
    lax.fori_loop(0, S5_CPB // 2, step, (zero, zero))
    xf = zf[...].astype(bf16)
    xb = zb[...].astype(bf16)
    for t in range(T):
        u32[pl.ds(t * S5_NCH, S5_NCH), :] = (
            jnp.dot(uc, wy[t, 0:ku, :], preferred_element_type=f32)
            + jnp.dot(xf, wy[t, ku:ku + 2 * hs, :], preferred_element_type=f32)
            + jnp.dot(xb, wy[t, ku + 2 * hs:ku + 4 * hs, :], preferred_element_type=f32))
    for t in range(T):
        for b in range(NB_):
            y_ref[pl.ds(b * LA + t, S5_CPB, stride=T), :] = u32[pl.ds(t * S5_NCH + b, S5_CPB, stride=NB_), :]


def _s5_core(u_all, mats):
    cwy, cwi, a = mats
    nt = HALF // 128
    kf = S5_T * 128 + 4 * S5_NS
    return pl.pallas_call(
        _s5_kernel,
        out_shape=jax.ShapeDtypeStruct((ROWS, HALF), f32),
        grid=(nt,),
        in_specs=[pl.BlockSpec((ROWS, 128), lambda j: (0, j)),
                  pl.BlockSpec((None, S5_T, kf // S5_GT, 128), lambda j: (j, 0, 0, 0)),
                  pl.BlockSpec((None, 2, S5_T * 128, 2 * C_STATE), lambda j: (j, 0, 0, 0)),
                  pl.BlockSpec((None, 2, 2 * S5_NS), lambda j: (j, 0, 0))],
        out_specs=pl.BlockSpec((ROWS, 128), lambda j: (0, j)),
        scratch_shapes=[pltpu.VMEM((S5_T, kf, 128), bf16),
                        pltpu.VMEM((2, S5_T * 128, 2 * S5_NS), bf16),
                        pltpu.VMEM((ROWS, 128), f32),
                        pltpu.VMEM((S5_T, S5_NCH, 128), f32),
                        pltpu.VMEM((S5_NCH, 2 * S5_NS), f32),
                        pltpu.VMEM((S5_NCH, 2 * S5_NS), f32)],
        compiler_params=_cp(("arbitrary",), 56),
        name="s5_core",
    )(u_all, cwy, cwi, a)


def _s5_matrices(a_re, a_im, log_dt, b_re, b_im, c_re, c_im):
    T = S5_T
    G, P, C = C_GROUPS, C_STATE, C_GROUP
    nt = G // S5_GT
    lam = lax.complex(a_re.astype(f32), a_im.astype(f32))
    dt = jnp.exp(log_dt.astype(f32))[..., None]
    a_bar = jnp.exp(lam * dt)
    b_bar = ((a_bar - 1) / lam)[..., None] * lax.complex(b_re.astype(f32), b_im.astype(f32))
    tau = jnp.arange(T + 1, dtype=f32)
    apow = jnp.exp((lam * dt)[..., None] * tau)
    cc = lax.complex(c_re.astype(f32), c_im.astype(f32))
    kern = jnp.einsum("gxp,dgpt,dgpc->dgtxc", cc, apow[..., :T], b_bar).real
    t_i = jnp.arange(T)
    lag = t_i[:, None] - t_i[None, :]
    kf = kern[0][:, jnp.clip(lag, 0, T - 1)] * (lag >= 0)[None, :, :, None, None]
    kb = kern[1][:, jnp.clip(-lag, 0, T - 1)] * (lag <= 0)[None, :, :, None, None]
    km = (kf + kb).reshape(nt, S5_GT, T, T, C, C).transpose(0, 2, 3, 1, 5, 4)
    wy_u = km.reshape(nt, T, T * 128, C)
    k_f = cc[:, :, :, None] * apow[0][:, None, :, 1 + t_i]
    k_b = cc[:, :, :, None] * apow[1][:, None, :, T - t_i]

    def wy_state(k):
        k = k.reshape(nt, S5_GT, C, P, T).transpose(0, 4, 1, 3, 2).reshape(nt, T, S5_NS, C)
        return jnp.concatenate([k.real, -k.imag], axis=2)

    cwy = jnp.concatenate([wy_u, wy_state(k_f), wy_state(k_b)], axis=2).astype(bf16)
    nrow = cwy.shape[2]
    cwy = cwy.reshape(nt, T, S5_GT, nrow // S5_GT, C).transpose(0, 1, 3, 2, 4).reshape(nt, T, nrow // S5_GT, 128)
    w_f = apow[0][:, :, T - 1 - t_i][..., None] * b_bar[0][:, :, None, :]
    w_b = apow[1][:, :, t_i][..., None] * b_bar[1][:, :, None, :]

    def wi_dir(w):
        w = w.reshape(nt, S5_GT, P, T, C).transpose(0, 3, 1, 4, 2).reshape(nt, T * 128, P)
        return jnp.concatenate([w.real, w.imag], axis=2)

    cwi = jnp.stack([wi_dir(w_f), wi_dir(w_b)], axis=1).astype(bf16)
    at = apow[..., T].reshape(2, nt, S5_NS)
    a = jnp.concatenate([at.real, at.imag], axis=-1).transpose(1, 0, 2)
    return cwy, cwi, a


def _s5_glu_kernel(y_ref, u_ref, d_ref, w_ref, b_ref, o_ref, wbf_ref):
    @pl.when(pl.program_id(0) == 0)
    def _():
        wbf_ref[...] = w_ref[...].astype(bf16)

    y = y_ref[...] + d_ref[...] * u_ref[...].astype(f32)
    y = jax.nn.gelu(y, approximate=True)
    z = jnp.dot(y.astype(bf16), wbf_ref[...], preferred_element_type=f32) + b_ref[...]
    o_ref[...] = (y * jax.nn.sigmoid(z)).astype(o_ref.dtype)


def _s5_glu(y_ssm, proj, d2, glu_w, glu_b2):
    nblk = NB_ * LBPB
    return pl.pallas_call(
        _s5_glu_kernel,
        out_shape=jax.ShapeDtypeStruct((nblk * RB, HALF), bf16),
        grid=(nblk,),
        in_specs=[pl.BlockSpec((RB, HALF), lambda i: (_lat_blk(i), 0)),
                  pl.BlockSpec((RB, HALF), lambda i: (_lat_blk(i), 0)),
                  pl.BlockSpec((1, HALF), lambda i: (0, 0)),
                  pl.BlockSpec((HALF, HALF), lambda i: (0, 0)),
                  pl.BlockSpec((1, HALF), lambda i: (0, 0))],
        out_specs=pl.BlockSpec((RB, HALF), lambda i: (i, 0)),
        scratch_shapes=[pltpu.VMEM((HALF, HALF), bf16)],
        compiler_params=_cp(("arbitrary",)),
        name="s5_glu",
    )(y_ssm, proj, d2, glu_w, glu_b2)


def _hyena_filter(L, w1, b1, w2, b2, w3, freq):
    t = jnp.linspace(0.0, 1.0, L, dtype=f32)[:, None]
    w = 2 * math.pi * jnp.arange(L, dtype=f32)[:, None] / L
    bands = jnp.linspace(1e-4, HY_BANDS - 1, HY_BANDS, dtype=f32)
    feats = jnp.concatenate([t, jnp.cos(bands * w), -jnp.sin(bands * w)], axis=-1)
    h = jnp.sin(freq * (feats @ w1 + b1))
    h = jnp.sin(freq * (h @ w2 + b2))
    h = h @ w3
    fast = abs(math.log(HY_DECAY_TARGET) / HY_FAST_PCT)
    slow = abs(math.log(HY_DECAY_TARGET) / HY_SLOW_PCT)
    deltas = jnp.tile(jnp.linspace(slow, fast, HALF, dtype=f32), 2)
    h = h * jnp.exp(-t * deltas)
    h_fwd, h_bwd = jnp.split(h, 2, axis=-1)
    h_bwd = h_bwd.at[0].set(0.0)
    norm = jnp.sum(jnp.abs(h_fwd), axis=0, keepdims=True) + jnp.sum(jnp.abs(h_bwd), axis=0, keepdims=True)
    return h_fwd / norm, h_bwd / norm


def _hyena_dft_tables():
    n2 = 2 * SEQ
    n = jnp.arange(SEQ, dtype=jnp.int32)[:, None]
    k = jnp.arange(SEQ, dtype=jnp.int32)[None, :]
    kr = 64
    k1 = jnp.arange(SEQ // kr, dtype=jnp.int32)[None, :]
    k0 = jnp.arange(kr, dtype=jnp.int32)[None, :]
    ang_a = ((n * k1 * kr) % n2).astype(f32) * (2.0 * math.pi / n2)
    ang_b = ((n * k0) % n2).astype(f32) * (2.0 * math.pi / n2)
    ca, sa = jnp.cos(ang_a)[:, :, None], jnp.sin(ang_a)[:, :, None]
    cb, sb = jnp.cos(ang_b)[:, None, :], jnp.sin(ang_b)[:, None, :]
    c = (ca * cb - sa * sb).reshape(SEQ, SEQ)
    s = (sa * cb + ca * sb).reshape(SEQ, SEQ)
    nyq = jnp.where(n % 2 == 0, 1.0, -1.0).astype(f32)
    f_re, f_im = c, jnp.where(k == 0, nyq, -s)
    g_re = jnp.where(k == 0, 1.0, 2.0 * c) / n2
    g_im = jnp.where(k == 0, nyq, -2.0 * s) / n2

    def tiles(re, im):
        nk = SEQ // HY_KT
        return jnp.concatenate([re.reshape(SEQ, nk, 1, HY_KT), im.reshape(SEQ, nk, 1, HY_KT)],
                               axis=2).reshape(SEQ, 2 * SEQ).astype(bf16)

    return tiles(f_re, f_im), tiles(g_re, g_im)


def _hyena_kernel(x0_ref, x1_ref, v_ref, cw0_ref, cw1_ref, cw2_ref, cb0_ref, cb1_ref, cb2_ref, hb_ref,
                  f_ref, g_ref, ht_ref, o_ref, ut_scr, u_scr, y_scr):
    kt = pl.program_id(2)
    L = SEQ
    row = lax.broadcasted_iota(jnp.int32, (L, HY_TC), 0)

    def sconv(z_ref, cw_ref, cb_ref):
        z = z_ref[...].astype(f32)
        zm = jnp.where(row == 0, 0.0, pltpu.roll(z, 1, axis=0))
        zp = jnp.where(row == L - 1, 0.0, pltpu.roll(z, L - 1, axis=0))
        return zm * cw_ref[0:1, :] + z * cw_ref[1:2, :] + zp * cw_ref[2:3, :] + cb_ref[...]

    @pl.when(kt == 0)
    def _():
        u = sconv(x1_ref, cw1_ref, cb1_ref) * sconv(v_ref, cw2_ref, cb2_ref)
        u_scr[...] = u
        ut_scr[...] = u.T.astype(bf16)
        y_scr[...] = jnp.zeros_like(y_scr)

    xt = jnp.dot(ut_scr[...], f_ref[...], preferred_element_type=f32)
    h = ht_ref[...]
    xr, xi, hr, hi = xt[:, :HY_KT], xt[:, HY_KT:], h[:, :HY_KT], h[:, HY_KT:]
    col = lax.broadcasted_iota(jnp.int32, (HY_TC, HY_KT), 1) + kt * HY_KT
    dc = col == 0
    yr = xr * hr - jnp.where(dc, 0.0, xi * hi)
    yi = jnp.where(dc, xi * hi, xr * hi + xi * hr)
    yt = jnp.concatenate([yr, yi], axis=1).astype(bf16)
    y_scr[...] += _dot_nt(g_ref[...], yt)

    @pl.when(kt == pl.num_programs(2) - 1)
    def _():
        o_ref[...] = (sconv(x0_ref, cw0_ref, cb0_ref)
                      * (y_scr[...] + u_scr[...] * hb_ref[...])).astype(o_ref.dtype)


def _hyena(z_lat, conv_w, conv_b2, hy_bias2, ht, ft, gt):
    nct = HALF // HY_TC

    def zspec(k):
        return pl.BlockSpec((SEQ, HY_TC), lambda ct, b, kt: (b, k * nct + ct))

    def wspec(k):
        return pl.BlockSpec((3, HY_TC), lambda ct, b, kt: (0, k * nct + ct))

    def bspec(k):
        return pl.BlockSpec((1, HY_TC), lambda ct, b, kt: (0, k * nct + ct))

    return pl.pallas_call(
        _hyena_kernel,
        out_shape=jax.ShapeDtypeStruct((NB_ * SEQ, HALF), bf16),
        grid=(nct, NB_, SEQ // HY_KT),
        in_specs=[zspec(0), zspec(1), zspec(2), wspec(0), wspec(1), wspec(2), bspec(0), bspec(1), bspec(2),
                  pl.BlockSpec((1, HY_TC), lambda ct, b, kt: (0, ct)),
                  pl.BlockSpec((SEQ, 2 * HY_KT), lambda ct, b, kt: (0, kt)),
                  pl.BlockSpec((SEQ, 2 * HY_KT), lambda ct, b, kt: (0, kt)),
                  pl.BlockSpec((HY_TC, 2 * HY_KT), lambda ct, b, kt: (ct, kt))],
        out_specs=pl.BlockSpec((SEQ, HY_TC), lambda ct, b, kt: (b, ct)),
        scratch_shapes=[pltpu.VMEM((HY_TC, SEQ), bf16), pltpu.VMEM((SEQ, HY_TC), f32),
                        pltpu.VMEM((SEQ, HY_TC), f32)],
        compiler_params=_cp(("arbitrary", "arbitrary", "arbitrary"), 48),
        name="hyena",
    )(z_lat, z_lat, z_lat, conv_w, conv_w, conv_w, conv_b2, conv_b2, conv_b2, hy_bias2, ft, gt, ht)


def _hyena_filter_spectrum(h_fwd, h_bwd, ft):
    c = h_fwd.shape[1]
    ht2 = jnp.concatenate([h_fwd.T, h_bwd.T], axis=0).astype(bf16)
    xt = _matmul([ht2], ft, (), 2 * SEQ, 2 * HY_KT, 2 * c // RB, _ident, f32, name="hyena_filter_dft")
    col = jnp.arange(2 * SEQ)
    sgn = jnp.where((col % (2 * HY_KT) >= HY_KT) & (col != HY_KT), -1.0, 1.0).astype(f32)
    return xt[:c] + xt[c:] * sgn[None, :]


def _swiglu_perm():
    j = np.arange(256)
    src = np.where(j < 128, 2 * j, 2 * (j - 128) + 1)
    p = np.zeros((256, 256), np.float32)
    p[src, j] = 1.0
    return jnp.asarray(p, dtype=bf16)


def kernel(x, c, ctx, c_ctx, ada_w, ada_b, norm_mix_g, norm_ffn_g, ev_w_in, ev_gate_b, ev_h_norm_g,
           ev_q_norm_g, ev_k_norm_g, ev_sink, ev_w_out, od_w_in, od_a_re, od_a_im, od_log_dt, od_b_re,
           od_b_im, od_c_re, od_c_im, od_d, od_glu_w, od_glu_b, od_conv_w, od_conv_b, od_filt_w1,
           od_filt_b1, od_filt_w2, od_filt_b2, od_filt_w3, od_filt_freq, od_hy_bias, od_w_out,
           moe_router_w, moe_router_b, moe_w1, moe_b1, moe_w2, moe_b2):
    depth = ada_w.shape[0]
    xu = jnp.concatenate([ctx, x], axis=1).reshape(ROWS, D)
    cond8 = jnp.concatenate([c_ctx[None], c, jnp.zeros((3, D), f32)], axis=0)
    ada_b3 = ada_b.reshape(depth, 1, 6 * D)
    g_mix3 = norm_mix_g.reshape(depth, 1, D)
    g_ffn3 = norm_ffn_g.reshape(depth, 1, D)
    rb3 = moe_router_b.reshape(depth, 1, N_EXPERTS)
    b1_4 = moe_b1.reshape(depth, N_EXPERTS, 1, 2 * D_FF)
    b2_4 = moe_b2.reshape(depth, N_EXPERTS, 1, D)
    perm = _swiglu_perm()
    nall = ROWS // RB
    nlat = NB_ * LBPB

    mod3 = _ada(cond8, ada_w, ada_b3, 0).reshape(8, 1, 6 * D)
    h = _norm_mod(xu, g_mix3, mod3, 0, 0, _ident, _mod_row, nall, bf16)
    w_in = ev_w_in[0]
    n_a = 2 * A_HEADS * A_DQK + 2 * HALF
    n_g = 4 * A_HEADS
    w_main = jnp.concatenate([w_in[:, :n_a], w_in[:, n_a + n_g:]], axis=1)
    w_gate = jnp.pad(w_in[:, n_a:n_a + n_g], ((0, 0), (0, 128 - n_g)))
    proj = _matmul([h], w_main, (), w_main.shape[1], 1536, nall, _ident, bf16, name="even_in")
    gates = _matmul([h], w_gate, (), 128, 128, nall, _ident, f32, name="even_gates")[:, :n_g]
    gates = (gates + ev_gate_b[0]).reshape(NB_, LA, n_g)
    gcol, grow = _mlstm_gate_tables(gates)
    y_a = _mlstm(proj, gcol, grow, ev_h_norm_g[0].reshape(A_HEADS, 1, A_DV))
    cosf, sinf = _rope_tables()
    sink_col = jnp.repeat(ev_sink[0].astype(f32).reshape(B_KV, B_G), 128, axis=1)[..., None]
    y_b = _attention(proj, cosf, sinf, ev_q_norm_g[0].reshape(1, B_DH), ev_k_norm_g[0].reshape(1, B_DH),
                     sink_col)
    xu = _matmul([y_a, y_b], ev_w_out, (0,), D, 1024, nall, _ident, f32,
                 res=xu, res_blk=_ident, mrow=_mod_row, mod3=mod3, gate_chunk=2, name="even_out")
    xu = _moe_layer(xu, g_ffn3, mod3, moe_router_w, rb3, moe_w1, b1_4, moe_w2, b2_4, perm,
                    0, _ident, _mod_row, nall)

    mod3 = _ada(cond8, ada_w, ada_b3, 1).reshape(8, 1, 6 * D)
    h = _norm_mod(xu, g_mix3, mod3, 1, 0, _ident, _mod_row, nall, bf16)
    proj = _matmul([h], od_w_in, (0,), HALF, 1024, nall, _ident, bf16, name="odd_in_u")
    z_lat = _matmul([h], od_w_in, (0,), 3 * HALF, 1024, nlat, _lat_blk, bf16, w_col0=HALF // 1024,
                    name="odd_in_z")
    mats = _s5_matrices(od_a_re[0], od_a_im[0], od_log_dt[0], od_b_re[0], od_b_im[0], od_c_re[0], od_c_im[0])
    y_ssm = _s5_core(proj, mats)
    s_l = _s5_glu(y_ssm, proj, od_d[0].reshape(1, HALF), od_glu_w[0], od_glu_b[0].reshape(1, HALF))
    h_fwd, h_bwd = _hyena_filter(SEQ, od_filt_w1[0], od_filt_b1[0], od_filt_w2[0], od_filt_b2[0],
                                 od_filt_w3[0], od_filt_freq[0])
    ft, gt = _hyena_dft_tables()
    hy_l = _hyena(z_lat, od_conv_w[0], od_conv_b[0].reshape(1, 3 * HALF), od_hy_bias[0].reshape(1, HALF),
                  _hyena_filter_spectrum(h_fwd, h_bwd, ft), ft, gt)
    x_lat = _matmul([s_l, hy_l], od_w_out, (0,), D, 1024, nlat, _ident, f32,
                    res=xu, res_blk=_lat_blk, mrow=_lat_mod_row, mod3=mod3, gate_chunk=2, name="odd_out")
    out = _moe_layer(x_lat, g_ffn3, mod3, moe_router_w, rb3, moe_w1, b1_4, moe_w2, b2_4, perm,
                     1, _ident, _lat_mod_row, nlat)
    return out.reshape(NB_, SEQ, D)
```

```python
import functools
import math

import jax
import jax.numpy as jnp
import numpy as np
from jax import lax
from jax.experimental import pallas as pl
from jax.experimental.pallas import tpu as pltpu

f32 = jnp.float32
bf16 = jnp.bfloat16

D = 2048
NB_ = 4
SEQ = 2048
CTX = 256
LA = SEQ + CTX
ROWS = NB_ * LA
RB = 256
BPB = LA // RB
LBPB = SEQ // RB
EPS = 1e-6
GRID_W = 64

HALF = D // 2
A_HEADS = 4
A_DV = HALF // A_HEADS
A_DQK = A_DV // 2
A_T = 256
B_HEADS = 16
B_KV = 4
B_G = B_HEADS // B_KV
B_DH = HALF // B_HEADS
B_KP = 2
ATT_COL0 = 2 * A_HEADS * A_DQK + 2 * HALF
WINDOW = 128
ROPE_BASE = 10000.0
C_GROUP = 16
C_GROUPS = HALF // C_GROUP
C_STATE = 64
S5_T = 8
S5_GT = 128 // C_GROUP
S5_NS = S5_GT * C_STATE
S5_NCH = ROWS // S5_T
S5_CPB = LA // S5_T
S5_CCTX = CTX // S5_T
HY_TC = 256
HY_KT = 512
HY_BANDS = 16
HY_DECAY_TARGET = 1e-2
HY_FAST_PCT = 0.3
HY_SLOW_PCT = 1.5
N_EXPERTS = 32
TOP_K = 4
D_FF = D
SWIGLU_LIMIT = 7.0
SWIGLU_ALPHA = 1.702
MOE_TM = 256

EVEN_SPLITS = (A_HEADS * A_DQK, A_HEADS * A_DQK, HALF, HALF, 4 * A_HEADS,
               B_HEADS * B_DH, B_KV * B_DH, B_KV * B_DH)

VMEM_MB = 1024 * 1024


def _cp(sem, vmem_mb=40):
    return pltpu.CompilerParams(dimension_semantics=sem, vmem_limit_bytes=vmem_mb * VMEM_MB)


def _lat_blk(i):
    return (i // LBPB) * BPB + 1 + i % LBPB


def _mod_row(u):
    return jnp.where(u % BPB == 0, 0, 1 + u // BPB)


def _lat_mod_row(i):
    return 1 + i // LBPB


def _ident(i):
    return i


def _ada_kernel(c_ref, w_ref, b_ref, o_ref):
    c = c_ref[...]
    s = c * jax.nn.sigmoid(c)
    o_ref[...] = jnp.dot(s.astype(bf16), w_ref[...].astype(bf16),
                         preferred_element_type=f32) + b_ref[...]


def _ada(cond8, ada_w, ada_b3, layer):
    tn = 1024
    n = 6 * D
    return pl.pallas_call(
        _ada_kernel,
        out_shape=jax.ShapeDtypeStruct((8, n), f32),
        grid=(n // tn,),
        in_specs=[pl.BlockSpec((8, D), lambda j: (0, 0)),
                  pl.BlockSpec((None, D, tn), lambda j: (layer, 0, j)),
                  pl.BlockSpec((None, 1, tn), lambda j: (layer, 0, j))],
        out_specs=pl.BlockSpec((8, tn), lambda j: (0, j)),
        compiler_params=_cp(("arbitrary",)),
        name="ada_params",
    )(cond8, ada_w, ada_b3)


def _norm_mod_kernel(x_ref, g_ref, sh_ref, sc_ref, o_ref):
    x = x_ref[...]
    y = x * lax.rsqrt(jnp.mean(x * x, axis=-1, keepdims=True) + EPS) * g_ref[...]
    o_ref[...] = (y * (1.0 + sc_ref[...]) + sh_ref[...]).astype(o_ref.dtype)


def _norm_mod(x, g3, mod3, layer, chunk, blk, mrow, nblk, out_dtype):
    return pl.pallas_call(
        _norm_mod_kernel,
        out_shape=jax.ShapeDtypeStruct((nblk * RB, D), out_dtype),
        grid=(nblk,),
        in_specs=[pl.BlockSpec((RB, D), lambda i: (blk(i), 0)),
                  pl.BlockSpec((None, 1, D), lambda i: (layer, 0, 0)),
                  pl.BlockSpec((None, 1, D), lambda i: (mrow(i), 0, chunk)),
                  pl.BlockSpec((None, 1, D), lambda i: (mrow(i), 0, chunk + 1))],
        out_specs=pl.BlockSpec((RB, D), lambda i: (i, 0)),
        compiler_params=_cp(("arbitrary",)),
        name="norm_mod",
    )(x, g3, mod3, mod3)


def _split_bf16(x):
    hi = x.astype(bf16)
    lo = (x - hi.astype(f32)).astype(bf16)
    return hi, lo


def _norm_router_kernel(x_ref, g_ref, sh_ref, sc_ref, rw_ref, rb_ref,
                        h_ref, idx_ref, gate_ref, rank_ref, cnt_ref, carry_ref):
    i = pl.program_id(0)

    @pl.when(i == 0)
    def _():
        carry_ref[...] = jnp.zeros_like(carry_ref)

    x = x_ref[...]
    y = x * lax.rsqrt(jnp.mean(x * x, axis=-1, keepdims=True) + EPS) * g_ref[...]
    h = y * (1.0 + sc_ref[...]) + sh_ref[...]
    h_ref[...] = h
    h_hi, h_lo = _split_bf16(h)
    w_hi, w_lo = _split_bf16(rw_ref[...])
    logits = (jnp.dot(h_hi, w_hi, preferred_element_type=f32)
              + jnp.dot(h_lo, w_hi, preferred_element_type=f32)
              + jnp.dot(h_hi, w_lo, preferred_element_type=f32)) + rb_ref[...]
    ne = logits.shape[-1]
    lane = lax.broadcasted_iota(jnp.int32, logits.shape, 1)
    l = logits
    vals, idxs = [], []
    for _ in range(TOP_K):
        m = jnp.max(l, axis=-1, keepdims=True)
        ix = jnp.min(jnp.where(l == m, lane, ne), axis=-1, keepdims=True)
        vals.append(m)
        idxs.append(ix)
        l = jnp.where(lane == ix, -jnp.inf, l)
    es = [jnp.exp(v - vals[0]) for v in vals]
    tot = es[0] + es[1] + es[2] + es[3]
    onehot = jnp.zeros(logits.shape, f32)
    for ix in idxs:
        onehot = onehot + jnp.where(lane == ix, 1.0, 0.0)
    r_i = lax.broadcasted_iota(jnp.int32, (RB, RB), 0)
    c_i = lax.broadcasted_iota(jnp.int32, (RB, RB), 1)
    tri = jnp.where(c_i < r_i, 1.0, 0.0).astype(bf16)
    prefix = jnp.dot(tri, onehot.astype(bf16), preferred_element_type=f32) + carry_ref[...]
    for k in range(TOP_K):
        idx_ref[:, k:k + 1] = idxs[k]
        gate_ref[:, k:k + 1] = es[k] / tot
        rk = jnp.sum(jnp.where(lane == idxs[k], prefix, 0.0), axis=-1, keepdims=True)
        rank_ref[:, k:k + 1] = rk.astype(jnp.int32)
    carry_ref[...] = carry_ref[...] + jnp.sum(onehot, axis=0, keepdims=True)
    cnt_ref[...] = carry_ref[...]


def _norm_router(x, g3, mod3, rw, rb3, layer, chunk, blk, mrow, nblk):
    t = nblk * RB
    return pl.pallas_call(
        _norm_router_kernel,
        out_shape=(jax.ShapeDtypeStruct((t, D), f32),
                   jax.ShapeDtypeStruct((t, TOP_K), jnp.int32),
                   jax.ShapeDtypeStruct((t, TOP_K), f32),
                   jax.ShapeDtypeStruct((t, TOP_K), jnp.int32),
                   jax.ShapeDtypeStruct((1, N_EXPERTS), f32)),
        grid=(nblk,),
        in_specs=[pl.BlockSpec((RB, D), lambda i: (blk(i), 0)),
                  pl.BlockSpec((None, 1, D), lambda i: (layer, 0, 0)),
                  pl.BlockSpec((None, 1, D), lambda i: (mrow(i), 0, chunk)),
                  pl.BlockSpec((None, 1, D), lambda i: (mrow(i), 0, chunk + 1)),
                  pl.BlockSpec((None, D, N_EXPERTS), lambda i: (layer, 0, 0)),
                  pl.BlockSpec((None, 1, N_EXPERTS), lambda i: (layer, 0, 0))],
        out_specs=(pl.BlockSpec((RB, D), lambda i: (i, 0)),
                   pl.BlockSpec((RB, TOP_K), lambda i: (i, 0)),
                   pl.BlockSpec((RB, TOP_K), lambda i: (i, 0)),
                   pl.BlockSpec((RB, TOP_K), lambda i: (i, 0)),
                   pl.BlockSpec((1, N_EXPERTS), lambda i: (0, 0))),
        scratch_shapes=[pltpu.VMEM((1, N_EXPERTS), f32)],
        compiler_params=_cp(("arbitrary",)),
        name="norm_router",
    )(x, g3, mod3, mod3, rw, rb3)


def _mm_kernel(*refs, n_a, k_split, epilogue):
    a_refs = refs[:n_a]
    w_ref = refs[n_a]
    pos = n_a + 1
    if epilogue:
        res_ref, gate_ref = refs[pos], refs[pos + 1]
        pos += 2
    o_ref, wbf_ref = refs[pos], refs[pos + 1]

    @pl.when(pl.program_id(1) == 0)
    def _():
        wbf_ref[...] = w_ref[...].astype(bf16)

    acc = None
    for t, a_ref in enumerate(a_refs):
        part = jnp.dot(a_ref[...], wbf_ref[t * k_split:(t + 1) * k_split, :],
                       preferred_element_type=f32)
        acc = part if acc is None else acc + part
    if epilogue:
        acc = res_ref[...] + gate_ref[...] * acc
    o_ref[...] = acc.astype(o_ref.dtype)


def _matmul(a_list, w, w_idx, n, tn, nblk, a_blk, out_dtype, *, a_col=None, w_col0=0,
            res=None, res_blk=None, mrow=None, mod3=None, gate_chunk=None, name="matmul"):
    n_a = len(a_list)
    k = w.shape[-2]
    k_split = k // n_a
    lead = len(w_idx)
    a_col = a_col or [0] * n_a
    in_specs = [pl.BlockSpec((RB, k_split), functools.partial(lambda j, i, c: (a_blk(i), c), c=c))
                for c in a_col]
    in_specs.append(pl.BlockSpec((None,) * lead + (k, tn), lambda j, i: tuple(w_idx) + (0, j + w_col0)))
    args = list(a_list) + [w]
    epilogue = res is not None
    if epilogue:
        per = D // tn
        in_specs.append(pl.BlockSpec((RB, tn), lambda j, i: (res_blk(i), j)))
        in_specs.append(pl.BlockSpec((None, 1, tn),
                                     lambda j, i: (mrow(i), 0, gate_chunk * per + j)))
        args += [res, mod3]
    return pl.pallas_call(
        functools.partial(_mm_kernel, n_a=n_a, k_split=k_split, epilogue=epilogue),
        out_shape=jax.ShapeDtypeStruct((nblk * RB, n), out_dtype),
        grid=(n // tn, nblk),
        in_specs=in_specs,
        out_specs=pl.BlockSpec((RB, tn), lambda j, i: (i, j)),
        scratch_shapes=[pltpu.VMEM((k, tn), bf16)],
        compiler_params=_cp(("arbitrary", "arbitrary"), 48),
        name=name,
    )(*args)


def _gather_rows_kernel(tok_ref, h_hbm, o_ref, buf, sem, *, rows, nsteps):
    i = pl.program_id(0)

    def issue(blk, slot):
        base = blk * rows
        for r in range(rows):
            t = tok_ref[base + r]
            pltpu.make_async_copy(h_hbm.at[pl.ds(t, 1), :], buf.at[slot, pl.ds(r, 1), :],
                                  sem.at[slot]).start(priority=r % 2)

    @pl.when(i == 0)
    def _():
        issue(0, 0)

    for par in range(2):
        @pl.when(jnp.logical_and(i + 1 < nsteps, (i + 1) % 2 == par))
        def _():
            issue(i + 1, par)

    slot = i % 2
    pltpu.make_async_copy(h_hbm.at[pl.ds(0, rows), :], buf.at[slot], sem.at[slot]).wait()
    o_ref[...] = buf[slot].astype(o_ref.dtype)


def _gather_rows(slot_tok, h, nr):
    rows = 512
    nsteps = nr // rows
    return pl.pallas_call(
        functools.partial(_gather_rows_kernel, rows=rows, nsteps=nsteps),
        out_shape=jax.ShapeDtypeStruct((nr, D), bf16),
        grid_spec=pltpu.PrefetchScalarGridSpec(
            num_scalar_prefetch=1,
            grid=(nsteps,),
            in_specs=[pl.BlockSpec(memory_space=pl.ANY)],
            out_specs=pl.BlockSpec((rows, D), lambda i, tok: (i, 0)),
            scratch_shapes=[pltpu.VMEM((2, rows, D), f32), pltpu.SemaphoreType.DMA((2,))]),
        compiler_params=_cp(("arbitrary",)),
        name="moe_gather",
    )(slot_tok, h)


def _moe_up_kernel(start_ref, nblk_ref, xs_hbm, w_hbm, b_ref, act_hbm,
                   wbf_ref, wf32, xbuf, obuf, wsem, xsem, osem, *, tn, layer):
    j = pl.program_id(0)
    e = pl.program_id(1)
    nb = nblk_ref[e]
    row0 = start_ref[e]
    g = j * N_EXPERTS + e

    def w_copy(gg, slot):
        c0 = pl.multiple_of((gg // N_EXPERTS) * tn, tn)
        return pltpu.make_async_copy(w_hbm.at[layer, gg % N_EXPERTS, :, pl.ds(c0, tn)], wf32.at[slot],
                                     wsem.at[slot])

    def x_copy(blk, slot):
        r = pl.multiple_of(row0 + blk * MOE_TM, MOE_TM)
        return pltpu.make_async_copy(xs_hbm.at[pl.ds(r, MOE_TM), :], xbuf.at[slot], xsem.at[slot])

    def o_copy(blk, slot):
        r = pl.multiple_of(row0 + blk * MOE_TM, MOE_TM)
        return pltpu.make_async_copy(obuf.at[slot], act_hbm.at[j, pl.ds(r, MOE_TM), :], osem.at[slot])

    @pl.when(g == 0)
    def _():
        w_copy(0, 0).start(priority=1)

    @pl.when(nb > 0)
    def _():
        x_copy(0, 0).start()

    @pl.when(g + 1 < pl.num_programs(0) * N_EXPERTS)
    def _():
        w_copy(g + 1, (g + 1) % 2).start(priority=1)

    w_copy(g, g % 2).wait()

    @pl.when(nb > 0)
    def _():
        wbf_ref[...] = wf32[g % 2].astype(bf16)

    lane = lax.broadcasted_iota(jnp.int32, (MOE_TM, 128), 1)
    even = (2 * lane) % 128
    first_half = lane < 64

    def pick(a, b, idx):
        return jnp.where(first_half, jnp.take_along_axis(a, idx, axis=1), jnp.take_along_axis(b, idx, axis=1))

    def body(blk, c):
        slot = blk % 2
        x_copy(blk, slot).wait()

        @pl.when(blk + 1 < nb)
        def _():
            x_copy(blk + 1, 1 - slot).start()

        @pl.when(blk >= 2)
        def _():
            o_copy(blk - 2, slot).wait()

        x = xbuf[slot]
        for cc in range(tn // 256):
            cs = slice(cc * 256, (cc + 1) * 256)
            h = jnp.dot(x, wbf_ref[:, cs], preferred_element_type=f32) + b_ref[:, cs]
            a, b = h[:, :128], h[:, 128:]
            gate = jnp.minimum(pick(a, b, even), SWIGLU_LIMIT)
            lin = jnp.clip(pick(a, b, even + 1), -SWIGLU_LIMIT, SWIGLU_LIMIT)
            act = gate * jax.nn.sigmoid(SWIGLU_ALPHA * gate) * (lin + 1.0)
            obuf[slot, :, cc * 128:(cc + 1) * 128] = act.astype(obuf.dtype)
        o_copy(blk, slot).start()
        return c

    lax.fori_loop(0, nb, body, 0)

    @pl.when(nb >= 2)
    def _():
        o_copy(nb - 2, nb % 2).wait()

    @pl.when(nb >= 1)
    def _():
        o_copy(nb - 1, (nb - 1) % 2).wait()

    @pl.when(e == N_EXPERTS - 1)
    def _():
        obuf[0] = jnp.zeros(obuf.shape[1:], obuf.dtype)

        def zero_tail(blk, c):
            r = pl.multiple_of(blk * MOE_TM, MOE_TM)
            cp = pltpu.make_async_copy(obuf.at[0], act_hbm.at[j, pl.ds(r, MOE_TM), :], osem.at[0])
            cp.start()
            cp.wait()
            return c

        lax.fori_loop(row0 // MOE_TM + nb, act_hbm.shape[1] // MOE_TM, zero_tail, 0)


def _moe_up(start, nblk_e, xs, w1, b1, layer, nr):
    tn = 2048
    nj = 2 * D_FF // tn
    return pl.pallas_call(
        functools.partial(_moe_up_kernel, tn=tn, layer=layer),
        out_shape=jax.ShapeDtypeStruct((nj, nr, tn // 2), bf16),
        grid_spec=pltpu.PrefetchScalarGridSpec(
            num_scalar_prefetch=2,
            grid=(nj, N_EXPERTS),
            in_specs=[pl.BlockSpec(memory_space=pl.ANY),
                      pl.BlockSpec(memory_space=pl.ANY),
                      pl.BlockSpec((None, None, 1, tn), lambda j, e, st, nb: (layer, e, 0, j))],
            out_specs=pl.BlockSpec(memory_space=pl.ANY),
            scratch_shapes=[pltpu.VMEM((D, tn), bf16),
                            pltpu.VMEM((2, D, tn), f32),
                            pltpu.VMEM((2, MOE_TM, D), bf16),
                            pltpu.VMEM((2, MOE_TM, tn // 2), bf16),
                            pltpu.SemaphoreType.DMA((2,)), pltpu.SemaphoreType.DMA((2,)),
                            pltpu.SemaphoreType.DMA((2,))]),
        compiler_params=_cp(("arbitrary", "arbitrary"), 56),
        name="moe_up",
    )(start, nblk_e, xs, w1, b1)


def _moe_down_kernel(start_ref, nblk_ref, act_hbm, w_hbm, b_ref, y_hbm, wbf_ref, wf32, abuf, obuf,
                     wsem, asem, osem, *, layer):
    e = pl.program_id(0)
    nb = nblk_ref[e]
    row0 = start_ref[e]
    nj = abuf.shape[1]
    kh = D_FF // nj

    def w_copy(ee, slot):
        return pltpu.make_async_copy(w_hbm.at[layer, ee], wf32.at[slot], wsem.at[slot])

    def a_copy(blk, slot, jj):
        r = pl.multiple_of(row0 + blk * MOE_TM, MOE_TM)
        return pltpu.make_async_copy(act_hbm.at[jj, pl.ds(r, MOE_TM), :], abuf.at[slot, jj], asem.at[slot])

    def o_copy(blk, slot):
        r = pl.multiple_of(row0 + blk * MOE_TM, MOE_TM)
        return pltpu.make_async_copy(obuf.at[slot], y_hbm.at[pl.ds(r, MOE_TM), :], osem.at[slot])

    @pl.when(e == 0)
    def _():
        w_copy(0, 0).start(priority=1)

    @pl.when(nb > 0)
    def _():
        for jj in range(nj):
            a_copy(0, 0, jj).start()

    @pl.when(e + 1 < N_EXPERTS)
    def _():
        w_copy(e + 1, (e + 1) % 2).start(priority=1)

    w_copy(e, e % 2).wait()

    @pl.when(nb > 0)
    def _():
        wbf_ref[...] = wf32[e % 2].astype(bf16)

    def body(blk, c):
        slot = blk % 2
        for jj in range(nj):
            a_copy(blk, slot, jj).wait()

        @pl.when(blk + 1 < nb)
        def _():
            for jj in range(nj):
                a_copy(blk + 1, 1 - slot, jj).start()

        y = b_ref[...] + jnp.dot(abuf[slot, 0], wbf_ref[0:kh, :], preferred_element_type=f32)
        for jj in range(1, nj):
            y = y + jnp.dot(abuf[slot, jj], wbf_ref[jj * kh:(jj + 1) * kh, :], preferred_element_type=f32)

        @pl.when(blk >= 2)
        def _():
            o_copy(blk - 2, slot).wait()

        obuf[slot] = y
        o_copy(blk, slot).start()
        return c

    lax.fori_loop(0, nb, body, 0)

    @pl.when(nb >= 2)
    def _():
        o_copy(nb - 2, nb % 2).wait()

    @pl.when(nb >= 1)
    def _():
        o_copy(nb - 1, (nb - 1) % 2).wait()

    @pl.when(e == N_EXPERTS - 1)
    def _():
        obuf[0] = jnp.zeros(obuf.shape[1:], obuf.dtype)

        def zero_tail(blk, c):
            r = pl.multiple_of(blk * MOE_TM, MOE_TM)
            cp = pltpu.make_async_copy(obuf.at[0], y_hbm.at[pl.ds(r, MOE_TM), :], osem.at[0])
            cp.start()
            cp.wait()
            return c

        lax.fori_loop(row0 // MOE_TM + nb, y_hbm.shape[0] // MOE_TM, zero_tail, 0)


def _moe_down(start, nblk_e, act, w2, b2, layer, nr):
    nj = act.shape[0]
    return pl.pallas_call(
        functools.partial(_moe_down_kernel, layer=layer),
        out_shape=jax.ShapeDtypeStruct((nr, D), f32),
        grid_spec=pltpu.PrefetchScalarGridSpec(
            num_scalar_prefetch=2,
            grid=(N_EXPERTS,),
            in_specs=[pl.BlockSpec(memory_space=pl.ANY),
                      pl.BlockSpec(memory_space=pl.ANY),
                      pl.BlockSpec((None, None, 1, D), lambda e, st, nb: (layer, e, 0, 0))],
            out_specs=pl.BlockSpec(memory_space=pl.ANY),
            scratch_shapes=[pltpu.VMEM((D_FF, D), bf16),
                            pltpu.VMEM((2, D_FF, D), f32),
                            pltpu.VMEM((2, nj, MOE_TM, D_FF // nj), bf16),
                            pltpu.VMEM((2, MOE_TM, D), f32),
                            pltpu.SemaphoreType.DMA((2,)), pltpu.SemaphoreType.DMA((2,)),
                            pltpu.SemaphoreType.DMA((2,))]),
        compiler_params=_cp(("arbitrary",), 56),
        name="moe_down",
    )(start, nblk_e, act, w2, b2)


def _combine_kernel(slot_ref, y_hbm, x_ref, gmod_ref, w_ref, o_ref, buf, sem, *, tm, nsteps):
    i = pl.program_id(0)

    def issue(blk, slot):
        base = blk * (tm * TOP_K)
        for r in range(tm):
            for k in range(TOP_K):
                s = slot_ref[base + r * TOP_K + k]
                pltpu.make_async_copy(y_hbm.at[pl.ds(s, 1), :], buf.at[slot, k, pl.ds(r, 1), :],
                                      sem.at[slot]).start(priority=k % 2)

    @pl.when(i == 0)
    def _():
        issue(0, 0)

    for par in range(2):
        @pl.when(jnp.logical_and(i + 1 < nsteps, (i + 1) % 2 == par))
        def _():
            issue(i + 1, par)

    slot = i % 2
    for k in range(TOP_K):
        pltpu.make_async_copy(y_hbm.at[pl.ds(0, tm), :], buf.at[slot, k], sem.at[slot]).wait()
    w = w_ref[...]
    acc = ((w[:, 0:1] * buf[slot, 0] + w[:, 1:2] * buf[slot, 1])
           + (w[:, 2:3] * buf[slot, 2] + w[:, 3:4] * buf[slot, 3]))
    o_ref[...] = x_ref[...] + gmod_ref[...] * acc


def _combine(slot_flat, ys, x, mod3, gate, gate_chunk, blk, mrow, nblk):
    tm = RB
    return pl.pallas_call(
        functools.partial(_combine_kernel, tm=tm, nsteps=nblk),
        out_shape=jax.ShapeDtypeStruct((nblk * tm, D), f32),
        grid_spec=pltpu.PrefetchScalarGridSpec(
            num_scalar_prefetch=1,
            grid=(nblk,),
            in_specs=[pl.BlockSpec(memory_space=pl.ANY),
                      pl.BlockSpec((tm, D), lambda i, s: (blk(i), 0)),
                      pl.BlockSpec((None, 1, D), lambda i, s: (mrow(i), 0, gate_chunk)),
                      pl.BlockSpec((tm, TOP_K), lambda i, s: (i, 0))],
            out_specs=pl.BlockSpec((tm, D), lambda i, s: (i, 0)),
            scratch_shapes=[pltpu.VMEM((2, TOP_K, tm, D), f32), pltpu.SemaphoreType.DMA((2,))]),
        compiler_params=_cp(("arbitrary",)),
        name="moe_combine",
    )(slot_flat, ys, x, mod3, gate)


def _moe_layer(x, g3, mod3, rw, rb3, w1, b1, w2, b2, layer, blk, mrow, nblk):
    t = nblk * RB
    h, idx, gate, rank, cnt = _norm_router(x, g3, mod3, rw, rb3, layer, 3, blk, mrow, nblk)
    n_assign = t * TOP_K
    nb = n_assign // MOE_TM + N_EXPERTS
    nr = nb * MOE_TM
    counts = cnt[0].astype(jnp.int32)
    padded = (counts + MOE_TM - 1) // MOE_TM * MOE_TM
    pad_end = jnp.cumsum(padded)
    pad_start = pad_end - padded
    slot = pad_start[idx] + rank
    slot_flat = slot.reshape(-1)
    tok = jnp.repeat(jnp.arange(t, dtype=jnp.int32), TOP_K)
    slot_tok = (jnp.arange(nr, dtype=jnp.int32) % t).at[slot_flat].set(tok)
    nblk_e = (padded // MOE_TM).astype(jnp.int32)
    xs = _gather_rows(slot_tok, h, nr)
    act = _moe_up(pad_start.astype(jnp.int32), nblk_e, xs, w1, b1, layer, nr)
    ys = _moe_down(pad_start.astype(jnp.int32), nblk_e, act, w2, b2, layer, nr)
    return _combine(slot_flat, ys, x, mod3, gate, 5, blk, mrow, nblk)


def _dot_nt(a, b):
    return lax.dot_general(a, b, (((1,), (1,)), ((), ())), preferred_element_type=f32)


def _dot_tn(a, b):
    return lax.dot_general(a, b, (((0,), (0,)), ((), ())), preferred_element_type=f32)


def _mlstm_kernel(q_ref, k_ref, v_ref, o_ref, gc_ref, gr_ref, ng_ref, y_ref, hf_ref, hb_ref, ct_ref):
    T = A_T
    nch = LA // T
    ct_ref[...] = jnp.zeros_like(ct_ref)
    r_i = lax.broadcasted_iota(jnp.int32, (T, T), 0)
    c_i = lax.broadcasted_iota(jnp.int32, (T, T), 1)
    lane = lax.broadcasted_iota(jnp.int32, (T, 128), 1)
    ones_blk = jnp.where(lane == 0, 1.0, 0.0).astype(bf16)

    def chunk(c, dirn, h_ref):
        r0 = pl.multiple_of(c * T, T)
        q = q_ref[pl.ds(r0, T), :]
        k = k_ref[pl.ds(r0, T), :]
        v = v_ref[pl.ds(r0, T), :]
        gc = gc_ref[pl.ds(r0, T), :]
        o = 5 * dirn
        a_col, m_col, negm = gc[:, o:o + 1], gc[:, o + 1:o + 2], gc[:, o + 2:o + 3]
        m_end, m_prev = gc[:, o + 3:o + 4], gc[:, o + 4:o + 5]
        a_row = gr_ref[dirn:dirn + 1, pl.ds(r0, T)]
        mask = (c_i <= r_i) if dirn == 0 else (c_i >= r_i)
        w = jnp.exp(jnp.where(mask, a_row - m_col, -jnp.inf))
        s = _dot_nt(q, k) * (A_DQK ** -0.5) * w
        v_aug = jnp.concatenate([v, ones_blk], axis=1)
        ct = ct_ref[dirn]
        w_carry = jnp.exp(m_prev - m_col)
        num = (jnp.dot(s.astype(bf16), v_aug, preferred_element_type=f32)
               + w_carry * jnp.dot(q, ct.astype(bf16), preferred_element_type=f32))
        den = num[:, A_DV:A_DV + 1]
        h_ref[pl.ds(r0, T), :] = num[:, :A_DV] / jnp.maximum(jnp.abs(den), jnp.exp(negm))
        w_in = jnp.exp(a_col - m_end)
        decay = jnp.exp(m_prev[0:1, :] - m_end[0:1, :])
        kv = _dot_tn(k, (w_in * v_aug.astype(f32)).astype(bf16)) * (A_DQK ** -0.5)
        ct_ref[dirn] = decay * ct + kv

    def step(it, carry):
        chunk(it, 0, hf_ref)
        chunk(jnp.where(it == 0, 0, nch - it), 1, hb_ref)
        return carry

    lax.fori_loop(0, nch, step, 0)
    h = hf_ref[...] + hb_ref[...]
    hn = h * lax.rsqrt(jnp.mean(h * h, axis=-1, keepdims=True) + EPS) * ng_ref[...]
    y_ref[...] = (hn * jax.nn.sigmoid(o_ref[...].astype(f32))).astype(y_ref.dtype)


def _mlstm(proj, gcol, grow, ng3):
    qb, vb = A_DQK, A_DV
    return pl.pallas_call(
        _mlstm_kernel,
        out_shape=jax.ShapeDtypeStruct((ROWS, HALF), bf16),
        grid=(NB_, A_HEADS),
        in_specs=[pl.BlockSpec((LA, qb), lambda b, h: (b, h)),
                  pl.BlockSpec((LA, qb), lambda b, h: (b, A_HEADS + h)),
                  pl.BlockSpec((LA, vb), lambda b, h: (b, (2 * A_HEADS * qb) // vb + h)),
                  pl.BlockSpec((LA, vb), lambda b, h: (b, (2 * A_HEADS * qb + HALF) // vb + h)),
                  pl.BlockSpec((None, None, LA, 16), lambda b, h: (b, h, 0, 0)),
                  pl.BlockSpec((None, None, 8, LA), lambda b, h: (b, h, 0, 0)),
                  pl.BlockSpec((None, 1, vb), lambda b, h: (h, 0, 0))],
        out_specs=pl.BlockSpec((LA, vb), lambda b, h: (b, h)),
        scratch_shapes=[pltpu.VMEM((LA, vb), f32), pltpu.VMEM((LA, vb), f32),
                        pltpu.VMEM((2, qb, vb + 128), f32)],
        compiler_params=_cp(("arbitrary", "arbitrary")),
        name="mlstm",
    )(proj, proj, proj, proj, gcol, grow, ng3)


def _mlstm_gate_tables(gates):
    i_f, f_f, i_b, f_b = jnp.split(gates, 4, axis=-1)

    def scan_tables(i_pre, f_pre):
        lf = jax.nn.log_sigmoid(f_pre)
        big_f = jnp.cumsum(lf, axis=1)
        a = i_pre - big_f
        m_run = jnp.maximum(lax.cummax(a, axis=1), 0.0)
        m_end = jnp.repeat(m_run[:, A_T - 1::A_T], A_T, axis=1)
        m_prev = jnp.concatenate([jnp.zeros_like(m_end[:, :A_T]), m_end[:, :-A_T]], axis=1)
        return a, m_run, -(big_f + m_run), m_end, m_prev

    def to_bwd(t):
        return jnp.concatenate([jnp.flip(t[:, :CTX], axis=1), jnp.flip(t[:, CTX:], axis=1)], axis=1)

    fw = scan_tables(i_f, f_f)
    bw = [to_bwd(t) for t in scan_tables(to_bwd(i_b), to_bwd(f_b))]
    cols = jnp.stack(list(fw) + bw, axis=-1)
    cols = jnp.pad(cols, ((0, 0), (0, 0), (0, 0), (0, 6))).transpose(0, 2, 1, 3)
    rows = jnp.stack([fw[0], bw[0]], axis=1).transpose(0, 3, 1, 2)
    rows = jnp.pad(rows, ((0, 0), (0, 0), (0, 6), (0, 0)))
    return cols, rows


def _swap_halves(t):
    hd = t.shape[-1] // 2
    return jnp.concatenate([t[..., hd:], t[..., :hd]], axis=-1)


def _attn_kernel(q_ref, k_ref, v_ref, cos_ref, sin_ref, gq_ref, gk_ref, sink_ref, o_ref):
    n = pl.program_id(2)
    scale = B_DH ** -0.5
    nq = B_G * 128

    def norm(t, g_ref):
        return t * lax.rsqrt(jnp.mean(t * t, axis=-1, keepdims=True) + EPS) * g_ref[...]

    def head_cols(ref, rows, h):
        return ref[rows, h * B_DH:(h + 1) * B_DH]

    def finish(j, parts):
        sink = sink_ref[j]
        m = sink
        for s, _ in parts:
            m = jnp.maximum(m, jnp.max(s, axis=-1, keepdims=True))
        den = jnp.exp(sink - m)
        acc = None
        for s, vv in parts:
            p = jnp.exp(s - m)
            den = den + jnp.sum(p, axis=-1, keepdims=True)
            pv = jnp.dot(p.astype(bf16), vv, preferred_element_type=f32)
            acc = pv if acc is None else acc + pv
        o = (acc / den).astype(o_ref.dtype)
        for g in range(B_G):
            h = j * B_G + g
            o_ref[:, h * B_DH:(h + 1) * B_DH] = o[g * 128:(g + 1) * 128]

    def q_heads(j):
        q = jnp.concatenate([head_cols(q_ref, slice(None), j * B_G + g) for g in range(B_G)], axis=0)
        return norm(q.astype(f32), gq_ref)

    def ctx_kv(j):
        kc = norm(head_cols(k_ref, pl.ds(0, CTX), j).astype(f32), gk_ref).astype(bf16)
        return kc, head_cols(v_ref, pl.ds(0, CTX), j)

    @pl.when(n < CTX // 128)
    def _():
        for j in range(B_KP):
            kc, vc = ctx_kv(j)
            s_ctx = _dot_nt(q_heads(j).astype(bf16), kc) * scale
            finish(j, [(s_ctx, vc)])

    @pl.when(n >= CTX // 128)
    def _():
        nl = n - CTX // 128
        pos0 = pl.multiple_of(nl * 128, 128)
        start = pl.multiple_of(jnp.clip((nl - 1) * 128, 0, SEQ - 3 * 128), 128)
        cq = jnp.concatenate([cos_ref[pl.ds(pos0, 128), :]] * B_G, axis=0)
        sq = jnp.concatenate([sin_ref[pl.ds(pos0, 128), :]] * B_G, axis=0)
        q_pos = pos0 + lax.broadcasted_iota(jnp.int32, (nq, 3 * 128), 0) % 128
        k_pos = start + lax.broadcasted_iota(jnp.int32, (nq, 3 * 128), 1)
        in_window = jnp.abs(q_pos - k_pos) <= WINDOW
        for j in range(B_KP):
            kc, vc = ctx_kv(j)
            qn = q_heads(j)
            qr = qn * cq + _swap_halves(qn) * sq
            kb = norm(head_cols(k_ref, pl.ds(CTX + start, 3 * 128), j).astype(f32), gk_ref)
            kb = kb * cos_ref[pl.ds(start, 3 * 128), :] + _swap_halves(kb) * sin_ref[pl.ds(start, 3 * 128), :]
            vb = head_cols(v_ref, pl.ds(CTX + start, 3 * 128), j)
            s_band = jnp.where(in_window, _dot_nt(qr.astype(bf16), kb.astype(bf16)) * scale, -jnp.inf)
            s_ctx = _dot_nt(qn.astype(bf16), kc) * scale
            finish(j, [(s_band, vb), (s_ctx, vc)])


def _attention(proj, cosf, sinf, gq, gk, sink_col):
    nblk = LA // 128
    qw = B_KP * B_G * B_DH
    kw = B_KP * B_DH
    return pl.pallas_call(
        _attn_kernel,
        out_shape=jax.ShapeDtypeStruct((ROWS, HALF), bf16),
        grid=(NB_, B_KV // B_KP, nblk),
        in_specs=[pl.BlockSpec((128, qw), lambda b, kp, n: (b * nblk + n, ATT_COL0 // qw + kp)),
                  pl.BlockSpec((LA, kw), lambda b, kp, n: (b, (ATT_COL0 + HALF) // kw + kp)),
                  pl.BlockSpec((LA, kw), lambda b, kp, n: (b, (ATT_COL0 + HALF + B_KV * B_DH) // kw + kp)),
                  pl.BlockSpec((SEQ, B_DH), lambda b, kp, n: (0, 0)),
                  pl.BlockSpec((SEQ, B_DH), lambda b, kp, n: (0, 0)),
                  pl.BlockSpec((1, B_DH), lambda b, kp, n: (0, 0)),
                  pl.BlockSpec((1, B_DH), lambda b, kp, n: (0, 0)),
                  pl.BlockSpec((B_KP, B_G * 128, 1), lambda b, kp, n: (kp, 0, 0))],
        out_specs=pl.BlockSpec((128, qw), lambda b, kp, n: (b * nblk + n, kp)),
        compiler_params=_cp(("arbitrary", "arbitrary", "arbitrary")),
        name="window_attn",
    )(proj, proj, proj, cosf, sinf, gq, gk, sink_col)


def _rope_tables():
    rows = SEQ // GRID_W
    row = jnp.repeat(jnp.arange(rows), GRID_W).astype(f32)
    col = jnp.tile(jnp.arange(GRID_W), rows).astype(f32)
    n_freq = B_DH // 4
    inv = ROPE_BASE ** (-jnp.arange(n_freq, dtype=f32) / n_freq)
    ang = jnp.concatenate([row[:, None] * inv, col[:, None] * inv], axis=-1)
    cos, sin = jnp.cos(ang), jnp.sin(ang)
    return jnp.concatenate([cos, cos], axis=-1), jnp.concatenate([-sin, sin], axis=-1)


def _s5_kernel(u_ref, cwy_ref, cwi_ref, a_ref, y_ref, wy, wi, u32, ucat, zf, zb):
    T = S5_T
    hs = S5_NS
    ku = T * 128

    def onehot_tile(n, reps):
        r = lax.broadcasted_iota(jnp.int32, (n, n * reps), 0)
        c = lax.broadcasted_iota(jnp.int32, (n, n * reps), 1)
        return jnp.where(c % n == r, 1.0, 0.0).astype(bf16)

    def group_mask(rows, row_group, width):
        r = lax.broadcasted_iota(jnp.int32, (rows, S5_GT * width), 0)
        c = lax.broadcasted_iota(jnp.int32, (rows, S5_GT * width), 1)
        return row_group(r) == c // width

    rep_p = onehot_tile(C_STATE, S5_GT)
    m_i = group_mask(ku, lambda r: (r // C_GROUP) % S5_GT, C_STATE)
    nrp = (ku + 4 * hs) // S5_GT
    sa = lax.broadcasted_iota(jnp.int32, (128, 128), 0)
    sb = lax.broadcasted_iota(jnp.int32, (128, 128), 1)
    rr = lax.broadcasted_iota(jnp.int32, (nrp, 128), 0)
    cg = lax.broadcasted_iota(jnp.int32, (nrp, 128), 1) // C_GROUP
    for q in range(S5_GT):
        sel = jnp.where(sa == q * C_GROUP + sb % C_GROUP, 1.0, 0.0).astype(bf16)
        r = rr + q * nrp
        grp = jnp.where(r < ku, (r // C_GROUP) % S5_GT, ((r - ku) % hs) // C_STATE)
        for t in range(T):
            e = jnp.dot(cwy_ref[t], sel, preferred_element_type=f32)
            wy[t, q * nrp:(q + 1) * nrp, :] = jnp.where(grp == cg, e, 0.0).astype(bf16)
    for d in range(2):
        for q in range(2):
            e = jnp.dot(cwi_ref[d, :, q * C_STATE:(q + 1) * C_STATE], rep_p, preferred_element_type=f32)
            wi[d, :, q * hs:(q + 1) * hs] = jnp.where(m_i, e, 0.0).astype(bf16)

    u32[...] = u_ref[...].astype(f32)
    for s in range(T):
        for b in range(NB_):
            ucat[s, pl.ds(b, S5_CPB, stride=NB_), :] = u32[pl.ds(b * LA + s, S5_CPB, stride=T), :]
    uc = jnp.concatenate([ucat[s].astype(bf16) for s in range(T)], axis=1)
    zf[...] = jnp.dot(uc, wi[0], preferred_element_type=f32)
    zb[...] = jnp.dot(uc, wi[1], preferred_element_type=f32)
    a = a_ref[...]

    def cstep(av, x, z):
        ar, ai = av[:, :hs], av[:, hs:]
        xr, xi = x[:, :hs], x[:, hs:]
        return jnp.concatenate([ar * xr - ai * xi + z[:, :hs], ar * xi + ai * xr + z[:, hs:]], axis=1)

    def step(i, carry):
        x_f, x_b = carry
        rf = pl.multiple_of(i * 2 * NB_, 2 * NB_)
        mb = jnp.where(i < S5_CCTX // 2, S5_CCTX // 2 - 1 - i, S5_CPB // 2 + S5_CCTX // 2 - 1 - i)
        rb = pl.multiple_of(mb * 2 * NB_, 2 * NB_)
        z8 = zf[pl.ds(rf, 2 * NB_), :]
        x1 = cstep(a[0:1], x_f, z8[:NB_])
        zf[pl.ds(rf, 2 * NB_), :] = jnp.concatenate([x_f, x1], axis=0)
        x_f2 = cstep(a[0:1], x1, z8[NB_:])
        w8 = zb[pl.ds(rb, 2 * NB_), :]
        y1 = cstep(a[1:2], x_b, w8[NB_:])
        zb[pl.ds(rb, 2 * NB_), :] = jnp.concatenate([y1, x_b], axis=0)
        x_b2 = cstep(a[1:2], y1, w8[:NB_])
        return x_f2, x_b2

    zero = jnp.zeros((NB_, 2 * hs), f32)
    lax.fori_loop(0, S5_CPB // 2, step, (zero, zero))
    xf = zf[...].astype(bf16)
    xb = zb[...].astype(bf16)
    for t in range(T):
        u32[pl.ds(t * S5_NCH, S5_NCH), :] = (
            jnp.dot(uc, wy[t, 0:ku, :], preferred_element_type=f32)
            + jnp.dot(xf, wy[t, ku:ku + 2 * hs, :], preferred_element_type=f32)
            + jnp.dot(xb, wy[t, ku + 2 * hs:ku + 4 * hs, :], preferred_element_type=f32))
    for t in range(T):
        for b in range(NB_):
            y_ref[pl.ds(b * LA + t, S5_CPB, stride=T), :] = u32[pl.ds(t * S5_NCH + b, S5_CPB, stride=NB_), :]


def _s5_core(u_all, mats):
    cwy, cwi, a = mats
    nt = HALF // 128
    kf = S5_T * 128 + 4 * S5_NS
    return pl.pallas_call(
        _s5_kernel,
        out_shape=jax.ShapeDtypeStruct((ROWS, HALF), f32),
        grid=(nt,),
        in_specs=[pl.BlockSpec((ROWS, 128), lambda j: (0, j)),
                  pl.BlockSpec((None, S5_T, kf // S5_GT, 128), lambda j: (j, 0, 0, 0)),
                  pl.BlockSpec((None, 2, S5_T * 128, 2 * C_STATE), lambda j: (j, 0, 0, 0)),
                  pl.BlockSpec((None, 2, 2 * S5_NS), lambda j: (j, 0, 0))],
        out_specs=pl.BlockSpec((ROWS, 128), lambda j: (0, j)),
        scratch_shapes=[pltpu.VMEM((S5_T, kf, 128), bf16),
                        pltpu.VMEM((2, S5_T * 128, 2 * S5_NS), bf16),
                        pltpu.VMEM((ROWS, 128), f32),
                        pltpu.VMEM((S5_T, S5_NCH, 128), f32),
                        pltpu.VMEM((S5_NCH, 2 * S5_NS), f32),
                        pltpu.VMEM((S5_NCH, 2 * S5_NS), f32)],
        compiler_params=_cp(("arbitrary",), 56),
        name="s5_core",
    )(u_all, cwy, cwi, a)


def _s5_matrices(a_re, a_im, log_dt, b_re, b_im, c_re, c_im):
    T = S5_T
    G, P, C = C_GROUPS, C_STATE, C_GROUP
    nt = G // S5_GT
    lam = lax.complex(a_re.astype(f32), a_im.astype(f32))
    dt = jnp.exp(log_dt.astype(f32))[..., None]
    a_bar = jnp.exp(lam * dt)
    b_bar = ((a_bar - 1) / lam)[..., None] * lax.complex(b_re.astype(f32), b_im.astype(f32))
    tau = jnp.arange(T + 1, dtype=f32)
    apow = jnp.exp((lam * dt)[..., None] * tau)
    cc = lax.complex(c_re.astype(f32), c_im.astype(f32))
    kern = jnp.einsum("gxp,dgpt,dgpc->dgtxc", cc, apow[..., :T], b_bar).real
    t_i = jnp.arange(T)
    lag = t_i[:, None] - t_i[None, :]
    kf = kern[0][:, jnp.clip(lag, 0, T - 1)] * (lag >= 0)[None, :, :, None, None]
    kb = kern[1][:, jnp.clip(-lag, 0, T - 1)] * (lag <= 0)[None, :, :, None, None]
    km = (kf + kb).reshape(nt, S5_GT, T, T, C, C).transpose(0, 2, 3, 1, 5, 4)
    wy_u = km.reshape(nt, T, T * 128, C)
    k_f = cc[:, :, :, None] * apow[0][:, None, :, 1 + t_i]
    k_b = cc[:, :, :, None] * apow[1][:, None, :, T - t_i]

    def wy_state(k):
        k = k.reshape(nt, S5_GT, C, P, T).transpose(0, 4, 1, 3, 2).reshape(nt, T, S5_NS, C)
        return jnp.concatenate([k.real, -k.imag], axis=2)

    cwy = jnp.concatenate([wy_u, wy_state(k_f), wy_state(k_b)], axis=2).astype(bf16)
    nrow = cwy.shape[2]
    cwy = cwy.reshape(nt, T, S5_GT, nrow // S5_GT, C).transpose(0, 1, 3, 2, 4).reshape(nt, T, nrow // S5_GT, 128)
    w_f = apow[0][:, :, T - 1 - t_i][..., None] * b_bar[0][:, :, None, :]
    w_b = apow[1][:, :, t_i][..., None] * b_bar[1][:, :, None, :]

    def wi_dir(w):
        w = w.reshape(nt, S5_GT, P, T, C).transpose(0, 3, 1, 4, 2).reshape(nt, T * 128, P)
        return jnp.concatenate([w.real, w.imag], axis=2)

    cwi = jnp.stack([wi_dir(w_f), wi_dir(w_b)], axis=1).astype(bf16)
    at = apow[..., T].reshape(2, nt, S5_NS)
    a = jnp.concatenate([at.real, at.imag], axis=-1).transpose(1, 0, 2)
    return cwy, cwi, a


def _s5_glu_kernel(y_ref, u_ref, d_ref, w_ref, b_ref, o_ref, wbf_ref):
    @pl.when(pl.program_id(0) == 0)
    def _():
        wbf_ref[...] = w_ref[...].astype(bf16)

    y = y_ref[...] + d_ref[...] * u_ref[...].astype(f32)
    y = jax.nn.gelu(y, approximate=True)
    z = jnp.dot(y.astype(bf16), wbf_ref[...], preferred_element_type=f32) + b_ref[...]
    o_ref[...] = (y * jax.nn.sigmoid(z)).astype(o_ref.dtype)


def _s5_glu(y_ssm, proj, d2, glu_w, glu_b2):
    nblk = NB_ * LBPB
    return pl.pallas_call(
        _s5_glu_kernel,
        out_shape=jax.ShapeDtypeStruct((nblk * RB, HALF), bf16),
        grid=(nblk,),
        in_specs=[pl.BlockSpec((RB, HALF), lambda i: (_lat_blk(i), 0)),
                  pl.BlockSpec((RB, HALF), lambda i: (_lat_blk(i), 0)),
                  pl.BlockSpec((1, HALF), lambda i: (0, 0)),
                  pl.BlockSpec((HALF, HALF), lambda i: (0, 0)),
                  pl.BlockSpec((1, HALF), lambda i: (0, 0))],
        out_specs=pl.BlockSpec((RB, HALF), lambda i: (i, 0)),
        scratch_shapes=[pltpu.VMEM((HALF, HALF), bf16)],
        compiler_params=_cp(("arbitrary",)),
        name="s5_glu",
    )(y_ssm, proj, d2, glu_w, glu_b2)


def _hyena_filter(L, w1, b1, w2, b2, w3, freq):
    t = jnp.linspace(0.0, 1.0, L, dtype=f32)[:, None]
    w = 2 * math.pi * jnp.arange(L, dtype=f32)[:, None] / L
    bands = jnp.linspace(1e-4, HY_BANDS - 1, HY_BANDS, dtype=f32)
    feats = jnp.concatenate([t, jnp.cos(bands * w), -jnp.sin(bands * w)], axis=-1)
    h = jnp.sin(freq * (feats @ w1 + b1))
    h = jnp.sin(freq * (h @ w2 + b2))
    h = h @ w3
    fast = abs(math.log(HY_DECAY_TARGET) / HY_FAST_PCT)
    slow = abs(math.log(HY_DECAY_TARGET) / HY_SLOW_PCT)
    deltas = jnp.tile(jnp.linspace(slow, fast, HALF, dtype=f32), 2)
    h = h * jnp.exp(-t * deltas)
    h_fwd, h_bwd = jnp.split(h, 2, axis=-1)
    h_bwd = h_bwd.at[0].set(0.0)
    norm = jnp.sum(jnp.abs(h_fwd), axis=0, keepdims=True) + jnp.sum(jnp.abs(h_bwd), axis=0, keepdims=True)
    return h_fwd / norm, h_bwd / norm


def _hyena_dft_tables():
    n2 = 2 * SEQ
    n = jnp.arange(SEQ, dtype=jnp.int32)[:, None]
    k = jnp.arange(SEQ, dtype=jnp.int32)[None, :]
    kr = 64
    k1 = jnp.arange(SEQ // kr, dtype=jnp.int32)[None, :]
    k0 = jnp.arange(kr, dtype=jnp.int32)[None, :]
    ang_a = ((n * k1 * kr) % n2).astype(f32) * (2.0 * math.pi / n2)
    ang_b = ((n * k0) % n2).astype(f32) * (2.0 * math.pi / n2)
    ca, sa = jnp.cos(ang_a)[:, :, None], jnp.sin(ang_a)[:, :, None]
    cb, sb = jnp.cos(ang_b)[:, None, :], jnp.sin(ang_b)[:, None, :]
    c = (ca * cb - sa * sb).reshape(SEQ, SEQ)
    s = (sa * cb + ca * sb).reshape(SEQ, SEQ)
    nyq = jnp.where(n % 2 == 0, 1.0, -1.0).astype(f32)
    f_re, f_im = c, jnp.where(k == 0, nyq, -s)
    g_re = jnp.where(k == 0, 1.0, 2.0 * c) / n2
    g_im = jnp.where(k == 0, nyq, -2.0 * s) / n2

    def tiles(re, im):
        nk = SEQ // HY_KT
        return jnp.concatenate([re.reshape(SEQ, nk, 1, HY_KT), im.reshape(SEQ, nk, 1, HY_KT)],
                               axis=2).reshape(SEQ, 2 * SEQ).astype(bf16)

    return tiles(f_re, f_im), tiles(g_re, g_im)


def _hyena_kernel(x0_ref, x1_ref, v_ref, cw0_ref, cw1_ref, cw2_ref, cb0_ref, cb1_ref, cb2_ref, hb_ref,
                  f_ref, g_ref, ht_ref, o_ref, ut_scr, u_scr, y_scr):
    kt = pl.program_id(2)
    L = SEQ
    row = lax.broadcasted_iota(jnp.int32, (L, HY_TC), 0)

    def sconv(z_ref, cw_ref, cb_ref):
        z = z_ref[...].astype(f32)
        zm = jnp.where(row == 0, 0.0, pltpu.roll(z, 1, axis=0))
        zp = jnp.where(row == L - 1, 0.0, pltpu.roll(z, L - 1, axis=0))
        return zm * cw_ref[0:1, :] + z * cw_ref[1:2, :] + zp * cw_ref[2:3, :] + cb_ref[...]

    @pl.when(kt == 0)
    def _():
        u = sconv(x1_ref, cw1_ref, cb1_ref) * sconv(v_ref, cw2_ref, cb2_ref)
        u_scr[...] = u
        ut_scr[...] = u.T.astype(bf16)
        y_scr[...] = jnp.zeros_like(y_scr)

    xt = jnp.dot(ut_scr[...], f_ref[...], preferred_element_type=f32)
    h = ht_ref[...]
    xr, xi, hr, hi = xt[:, :HY_KT], xt[:, HY_KT:], h[:, :HY_KT], h[:, HY_KT:]
    col = lax.broadcasted_iota(jnp.int32, (HY_TC, HY_KT), 1) + kt * HY_KT
    dc = col == 0
    yr = xr * hr - jnp.where(dc, 0.0, xi * hi)
    yi = jnp.where(dc, xi * hi, xr * hi + xi * hr)
    yt = jnp.concatenate([yr, yi], axis=1).astype(bf16)
    y_scr[...] += _dot_nt(g_ref[...], yt)

    @pl.when(kt == pl.num_programs(2) - 1)
    def _():
        o_ref[...] = (sconv(x0_ref, cw0_ref, cb0_ref)
                      * (y_scr[...] + u_scr[...] * hb_ref[...])).astype(o_ref.dtype)


def _hyena(z_lat, conv_w, conv_b2, hy_bias2, ht, ft, gt):
    nct = HALF // HY_TC

    def zspec(k):
        return pl.BlockSpec((SEQ, HY_TC), lambda ct, b, kt: (b, k * nct + ct))

    def wspec(k):
        return pl.BlockSpec((3, HY_TC), lambda ct, b, kt: (0, k * nct + ct))

    def bspec(k):
        return pl.BlockSpec((1, HY_TC), lambda ct, b, kt: (0, k * nct + ct))

    return pl.pallas_call(
        _hyena_kernel,
        out_shape=jax.ShapeDtypeStruct((NB_ * SEQ, HALF), bf16),
        grid=(nct, NB_, SEQ // HY_KT),
        in_specs=[zspec(0), zspec(1), zspec(2), wspec(0), wspec(1), wspec(2), bspec(0), bspec(1), bspec(2),
                  pl.BlockSpec((1, HY_TC), lambda ct, b, kt: (0, ct)),
                  pl.BlockSpec((SEQ, 2 * HY_KT), lambda ct, b, kt: (0, kt)),
                  pl.BlockSpec((SEQ, 2 * HY_KT), lambda ct, b, kt: (0, kt)),
                  pl.BlockSpec((HY_TC, 2 * HY_KT), lambda ct, b, kt: (ct, kt))],
        out_specs=pl.BlockSpec((SEQ, HY_TC), lambda ct, b, kt: (b, ct)),
        scratch_shapes=[pltpu.VMEM((HY_TC, SEQ), bf16), pltpu.VMEM((SEQ, HY_TC), f32),
                        pltpu.VMEM((SEQ, HY_TC), f32)],
        compiler_params=_cp(("arbitrary", "arbitrary", "arbitrary"), 48),
        name="hyena",
    )(z_lat, z_lat, z_lat, conv_w, conv_w, conv_w, conv_b2, conv_b2, conv_b2, hy_bias2, ft, gt, ht)


def _hyena_filter_spectrum(h_fwd, h_bwd, ft):
    c = h_fwd.shape[1]
    ht2 = jnp.concatenate([h_fwd.T, h_bwd.T], axis=0).astype(bf16)
    xt = _matmul([ht2], ft, (), 2 * SEQ, 2 * HY_KT, 2 * c // RB, _ident, f32, name="hyena_filter_dft")
    col = jnp.arange(2 * SEQ)
    sgn = jnp.where((col % (2 * HY_KT) >= HY_KT) & (col != HY_KT), -1.0, 1.0).astype(f32)
    return xt[:c] + xt[c:] * sgn[None, :]


def kernel(x, c, ctx, c_ctx, ada_w, ada_b, norm_mix_g, norm_ffn_g, ev_w_in, ev_gate_b, ev_h_norm_g,
           ev_q_norm_g, ev_k_norm_g, ev_sink, ev_w_out, od_w_in, od_a_re, od_a_im, od_log_dt, od_b_re,
           od_b_im, od_c_re, od_c_im, od_d, od_glu_w, od_glu_b, od_conv_w, od_conv_b, od_filt_w1,
           od_filt_b1, od_filt_w2, od_filt_b2, od_filt_w3, od_filt_freq, od_hy_bias, od_w_out,
           moe_router_w, moe_router_b, moe_w1, moe_b1, moe_w2, moe_b2):
    depth = ada_w.shape[0]
    xu = jnp.concatenate([ctx, x], axis=1).reshape(ROWS, D)
    cond8 = jnp.concatenate([c_ctx[None], c, jnp.zeros((3, D), f32)], axis=0)
    ada_b3 = ada_b.reshape(depth, 1, 6 * D)
    g_mix3 = norm_mix_g.reshape(depth, 1, D)
    g_ffn3 = norm_ffn_g.reshape(depth, 1, D)
    rb3 = moe_router_b.reshape(depth, 1, N_EXPERTS)
    b1_4 = moe_b1.reshape(depth, N_EXPERTS, 1, 2 * D_FF)
    b2_4 = moe_b2.reshape(depth, N_EXPERTS, 1, D)
    nall = ROWS // RB
    nlat = NB_ * LBPB

    mod3 = _ada(cond8, ada_w, ada_b3, 0).reshape(8, 1, 6 * D)
    h = _norm_mod(xu, g_mix3, mod3, 0, 0, _ident, _mod_row, nall, bf16)
    w_in = ev_w_in[0]
    n_a = 2 * A_HEADS * A_DQK + 2 * HALF
    n_g = 4 * A_HEADS
    w_main = jnp.concatenate([w_in[:, :n_a], w_in[:, n_a + n_g:]], axis=1)
    w_gate = jnp.pad(w_in[:, n_a:n_a + n_g], ((0, 0), (0, 128 - n_g)))
    proj = _matmul([h], w_main, (), w_main.shape[1], 1536, nall, _ident, bf16, name="even_in")
    gates = _matmul([h], w_gate, (), 128, 128, nall, _ident, f32, name="even_gates")[:, :n_g]
    gates = (gates + ev_gate_b[0]).reshape(NB_, LA, n_g)
    gcol, grow = _mlstm_gate_tables(gates)
    y_a = _mlstm(proj, gcol, grow, ev_h_norm_g[0].reshape(A_HEADS, 1, A_DV))
    cosf, sinf = _rope_tables()
    sink_col = jnp.repeat(ev_sink[0].astype(f32).reshape(B_KV, B_G), 128, axis=1)[..., None]
    y_b = _attention(proj, cosf, sinf, ev_q_norm_g[0].reshape(1, B_DH), ev_k_norm_g[0].reshape(1, B_DH),
                     sink_col)
    xu = _matmul([y_a, y_b], ev_w_out, (0,), D, 1024, nall, _ident, f32,
                 res=xu, res_blk=_ident, mrow=_mod_row, mod3=mod3, gate_chunk=2, name="even_out")
    xu = _moe_layer(xu, g_ffn3, mod3, moe_router_w, rb3, moe_w1, b1_4, moe_w2, b2_4,
                    0, _ident, _mod_row, nall)

    mod3 = _ada(cond8, ada_w, ada_b3, 1).reshape(8, 1, 6 * D)
    h = _norm_mod(xu, g_mix3, mod3, 1, 0, _ident, _mod_row, nall, bf16)
    proj = _matmul([h], od_w_in, (0,), HALF, 1024, nall, _ident, bf16, name="odd_in_u")
    z_lat = _matmul([h], od_w_in, (0,), 3 * HALF, 1024, nlat, _lat_blk, bf16, w_col0=HALF // 1024,
                    name="odd_in_z")
    mats = _s5_matrices(od_a_re[0], od_a_im[0], od_log_dt[0], od_b_re[0], od_b_im[0], od_c_re[0], od_c_im[0])
    y_ssm = _s5_core(proj, mats)
    s_l = _s5_glu(y_ssm, proj, od_d[0].reshape(1, HALF), od_glu_w[0], od_glu_b[0].reshape(1, HALF))
    h_fwd, h_bwd = _hyena_filter(SEQ, od_filt_w1[0], od_filt_b1[0], od_filt_w2[0], od_filt_b2[0],
                                 od_filt_w3[0], od_filt_freq[0])
    ft, gt = _hyena_dft_tables()
    hy_l = _hyena(z_lat, od_conv_w[0], od_conv_b[0].reshape(1, 3 * HALF), od_hy_bias[0].reshape(1, HALF),
                  _hyena_filter_spectrum(h_fwd, h_bwd, ft), ft, gt)
    x_lat = _matmul([s_l, hy_l], od_w_out, (0,), D, 1024, nlat, _ident, f32,
                    res=xu, res_blk=_lat_blk, mrow=_lat_mod_row, mod3=mod3, gate_chunk=2, name="odd_out")
    out = _moe_layer(x_lat, g_ffn3, mod3, moe_router_w, rb3, moe_w1, b1_4, moe_w2, b2_4,
                     1, _ident, _lat_mod_row, nlat)
    return out.reshape(NB_, SEQ, D)
```

```python
import functools
import math

import jax
import jax.numpy as jnp
import numpy as np
from jax import lax
from jax.experimental import pallas as pl
from jax.experimental.pallas import tpu as pltpu

f32 = jnp.float32
bf16 = jnp.bfloat16

D = 2048
NB_ = 4
SEQ = 2048
CTX = 256
LA = SEQ + CTX
ROWS = NB_ * LA
RB = 256
BPB = LA // RB
LBPB = SEQ // RB
EPS = 1e-6
GRID_W = 64

HALF = D // 2
A_HEADS = 4
A_DV = HALF // A_HEADS
A_DQK = A_DV // 2
A_T = 256
B_HEADS = 16
B_KV = 4
B_G = B_HEADS // B_KV
B_DH = HALF // B_HEADS
B_KP = 2
ATT_COL0 = 2 * A_HEADS * A_DQK + 2 * HALF
WINDOW = 128
ROPE_BASE = 10000.0
C_GROUP = 16
C_GROUPS = HALF // C_GROUP
C_STATE = 64
S5_T = 8
S5_GT = 128 // C_GROUP
S5_NS = S5_GT * C_STATE
S5_NCH = ROWS // S5_T
S5_CPB = LA // S5_T
S5_CCTX = CTX // S5_T
HY_TC = 256
HY_KT = 512
HY_BANDS = 16
HY_DECAY_TARGET = 1e-2
HY_FAST_PCT = 0.3
HY_SLOW_PCT = 1.5
N_EXPERTS = 32
TOP_K = 4
D_FF = D
SWIGLU_LIMIT = 7.0
SWIGLU_ALPHA = 1.702
MOE_TM = 256

EVEN_SPLITS = (A_HEADS * A_DQK, A_HEADS * A_DQK, HALF, HALF, 4 * A_HEADS,
               B_HEADS * B_DH, B_KV * B_DH, B_KV * B_DH)

VMEM_MB = 1024 * 1024


def _cp(sem, vmem_mb=40):
    return pltpu.CompilerParams(dimension_semantics=sem, vmem_limit_bytes=vmem_mb * VMEM_MB)


def _lat_blk(i):
    return (i // LBPB) * BPB + 1 + i % LBPB


def _mod_row(u):
    return jnp.where(u % BPB == 0, 0, 1 + u // BPB)


def _lat_mod_row(i):
    return 1 + i // LBPB


def _ident(i):
    return i


def _ada_kernel(c_ref, w_ref, b_ref, o_ref):
    c = c_ref[...]
    s = c * jax.nn.sigmoid(c)
    o_ref[...] = jnp.dot(s.astype(bf16), w_ref[...].astype(bf16),
                         preferred_element_type=f32) + b_ref[...]


def _ada(cond8, ada_w, ada_b3, layer):
    tn = 1024
    n = 6 * D
    return pl.pallas_call(
        _ada_kernel,
        out_shape=jax.ShapeDtypeStruct((8, n), f32),
        grid=(n // tn,),
        in_specs=[pl.BlockSpec((8, D), lambda j: (0, 0)),
                  pl.BlockSpec((None, D, tn), lambda j: (layer, 0, j)),
                  pl.BlockSpec((None, 1, tn), lambda j: (layer, 0, j))],
        out_specs=pl.BlockSpec((8, tn), lambda j: (0, j)),
        compiler_params=_cp(("arbitrary",)),
        name="ada_params",
    )(cond8, ada_w, ada_b3)


def _norm_mod_kernel(x_ref, g_ref, sh_ref, sc_ref, o_ref):
    x = x_ref[...]
    y = x * lax.rsqrt(jnp.mean(x * x, axis=-1, keepdims=True) + EPS) * g_ref[...]
    o_ref[...] = (y * (1.0 + sc_ref[...]) + sh_ref[...]).astype(o_ref.dtype)


def _norm_mod(x, g3, mod3, layer, chunk, blk, mrow, nblk, out_dtype):
    return pl.pallas_call(
        _norm_mod_kernel,
        out_shape=jax.ShapeDtypeStruct((nblk * RB, D), out_dtype),
        grid=(nblk,),
        in_specs=[pl.BlockSpec((RB, D), lambda i: (blk(i), 0)),
                  pl.BlockSpec((None, 1, D), lambda i: (layer, 0, 0)),
                  pl.BlockSpec((None, 1, D), lambda i: (mrow(i), 0, chunk)),
                  pl.BlockSpec((None, 1, D), lambda i: (mrow(i), 0, chunk + 1))],
        out_specs=pl.BlockSpec((RB, D), lambda i: (i, 0)),
        compiler_params=_cp(("arbitrary",)),
        name="norm_mod",
    )(x, g3, mod3, mod3)


def _split_bf16(x):
    hi = x.astype(bf16)
    lo = (x - hi.astype(f32)).astype(bf16)
    return hi, lo


def _norm_router_kernel(x_ref, g_ref, sh_ref, sc_ref, rw_ref, rb_ref,
                        h_ref, idx_ref, gate_ref, rank_ref, cnt_ref, carry_ref):
    i = pl.program_id(0)

    @pl.when(i == 0)
    def _():
        carry_ref[...] = jnp.zeros_like(carry_ref)

    x = x_ref[...]
    y = x * lax.rsqrt(jnp.mean(x * x, axis=-1, keepdims=True) + EPS) * g_ref[...]
    h = y * (1.0 + sc_ref[...]) + sh_ref[...]
    h_ref[...] = h
    h_hi, h_lo = _split_bf16(h)
    w_hi, w_lo = _split_bf16(rw_ref[...])
    logits = (jnp.dot(h_hi, w_hi, preferred_element_type=f32)
              + jnp.dot(h_lo, w_hi, preferred_element_type=f32)
              + jnp.dot(h_hi, w_lo, preferred_element_type=f32)) + rb_ref[...]
    ne = logits.shape[-1]
    lane = lax.broadcasted_iota(jnp.int32, logits.shape, 1)
    l = logits
    vals, idxs = [], []
    for _ in range(TOP_K):
        m = jnp.max(l, axis=-1, keepdims=True)
        ix = jnp.min(jnp.where(l == m, lane, ne), axis=-1, keepdims=True)
        vals.append(m)
        idxs.append(ix)
        l = jnp.where(lane == ix, -jnp.inf, l)
    es = [jnp.exp(v - vals[0]) for v in vals]
    tot = es[0] + es[1] + es[2] + es[3]
    onehot = jnp.zeros(logits.shape, f32)
    for ix in idxs:
        onehot = onehot + jnp.where(lane == ix, 1.0, 0.0)
    r_i = lax.broadcasted_iota(jnp.int32, (RB, RB), 0)
    c_i = lax.broadcasted_iota(jnp.int32, (RB, RB), 1)
    tri = jnp.where(c_i < r_i, 1.0, 0.0).astype(bf16)
    prefix = jnp.dot(tri, onehot.astype(bf16), preferred_element_type=f32) + carry_ref[...]
    for k in range(TOP_K):
        idx_ref[:, k:k + 1] = idxs[k]
        gate_ref[:, k:k + 1] = es[k] / tot
        rk = jnp.sum(jnp.where(lane == idxs[k], prefix, 0.0), axis=-1, keepdims=True)
        rank_ref[:, k:k + 1] = rk.astype(jnp.int32)
    carry_ref[...] = carry_ref[...] + jnp.sum(onehot, axis=0, keepdims=True)
    cnt_ref[...] = carry_ref[...]


def _norm_router(x, g3, mod3, rw, rb3, layer, chunk, blk, mrow, nblk):
    t = nblk * RB
    return pl.pallas_call(
        _norm_router_kernel,
        out_shape=(jax.ShapeDtypeStruct((t, D), f32),
                   jax.ShapeDtypeStruct((t, TOP_K), jnp.int32),
                   jax.ShapeDtypeStruct((t, TOP_K), f32),
                   jax.ShapeDtypeStruct((t, TOP_K), jnp.int32),
                   jax.ShapeDtypeStruct((1, N_EXPERTS), f32)),
        grid=(nblk,),
        in_specs=[pl.BlockSpec((RB, D), lambda i: (blk(i), 0)),
                  pl.BlockSpec((None, 1, D), lambda i: (layer, 0, 0)),
                  pl.BlockSpec((None, 1, D), lambda i: (mrow(i), 0, chunk)),
                  pl.BlockSpec((None, 1, D), lambda i: (mrow(i), 0, chunk + 1)),
                  pl.BlockSpec((None, D, N_EXPERTS), lambda i: (layer, 0, 0)),
                  pl.BlockSpec((None, 1, N_EXPERTS), lambda i: (layer, 0, 0))],
        out_specs=(pl.BlockSpec((RB, D), lambda i: (i, 0)),
                   pl.BlockSpec((RB, TOP_K), lambda i: (i, 0)),
                   pl.BlockSpec((RB, TOP_K), lambda i: (i, 0)),
                   pl.BlockSpec((RB, TOP_K), lambda i: (i, 0)),
                   pl.BlockSpec((1, N_EXPERTS), lambda i: (0, 0))),
        scratch_shapes=[pltpu.VMEM((1, N_EXPERTS), f32)],
        compiler_params=_cp(("arbitrary",)),
        name="norm_router",
    )(x, g3, mod3, mod3, rw, rb3)


def _mm_kernel(*refs, n_a, k_split, epilogue):
    a_refs = refs[:n_a]
    w_ref = refs[n_a]
    pos = n_a + 1
    if epilogue:
        res_ref, gate_ref = refs[pos], refs[pos + 1]
        pos += 2
    o_ref, wbf_ref = refs[pos], refs[pos + 1]

    @pl.when(pl.program_id(1) == 0)
    def _():
        wbf_ref[...] = w_ref[...].astype(bf16)

    acc = None
    for t, a_ref in enumerate(a_refs):
        part = jnp.dot(a_ref[...], wbf_ref[t * k_split:(t + 1) * k_split, :],
                       preferred_element_type=f32)
        acc = part if acc is None else acc + part
    if epilogue:
        acc = res_ref[...] + gate_ref[...] * acc
    o_ref[...] = acc.astype(o_ref.dtype)


def _matmul(a_list, w, w_idx, n, tn, nblk, a_blk, out_dtype, *, a_col=None, w_col0=0,
            res=None, res_blk=None, mrow=None, mod3=None, gate_chunk=None, name="matmul"):
    n_a = len(a_list)
    k = w.shape[-2]
    k_split = k // n_a
    lead = len(w_idx)
    a_col = a_col or [0] * n_a
    in_specs = [pl.BlockSpec((RB, k_split), functools.partial(lambda j, i, c: (a_blk(i), c), c=c))
                for c in a_col]
    in_specs.append(pl.BlockSpec((None,) * lead + (k, tn), lambda j, i: tuple(w_idx) + (0, j + w_col0)))
    args = list(a_list) + [w]
    epilogue = res is not None
    if epilogue:
        per = D // tn
        in_specs.append(pl.BlockSpec((RB, tn), lambda j, i: (res_blk(i), j)))
        in_specs.append(pl.BlockSpec((None, 1, tn),
                                     lambda j, i: (mrow(i), 0, gate_chunk * per + j)))
        args += [res, mod3]
    return pl.pallas_call(
        functools.partial(_mm_kernel, n_a=n_a, k_split=k_split, epilogue=epilogue),
        out_shape=jax.ShapeDtypeStruct((nblk * RB, n), out_dtype),
        grid=(n // tn, nblk),
        in_specs=in_specs,
        out_specs=pl.BlockSpec((RB, tn), lambda j, i: (i, j)),
        scratch_shapes=[pltpu.VMEM((k, tn), bf16)],
        compiler_params=_cp(("arbitrary", "arbitrary"), 48),
        name=name,
    )(*args)


def _gather_rows_kernel(tok_ref, h_hbm, o_ref, buf, sem, *, rows, nsteps):
    i = pl.program_id(0)

    def issue(blk, slot):
        base = blk * rows
        for r in range(rows):
            t = tok_ref[base + r]
            pltpu.make_async_copy(h_hbm.at[pl.ds(t, 1), :], buf.at[slot, pl.ds(r, 1), :],
                                  sem.at[slot]).start(priority=r % 2)

    @pl.when(i == 0)
    def _():
        issue(0, 0)

    for par in range(2):
        @pl.when(jnp.logical_and(i + 1 < nsteps, (i + 1) % 2 == par))
        def _():
            issue(i + 1, par)

    slot = i % 2
    pltpu.make_async_copy(h_hbm.at[pl.ds(0, rows), :], buf.at[slot], sem.at[slot]).wait()
    o_ref[...] = buf[slot].astype(o_ref.dtype)


def _gather_rows(slot_tok, h, nr):
    rows = 512
    nsteps = nr // rows
    return pl.pallas_call(
        functools.partial(_gather_rows_kernel, rows=rows, nsteps=nsteps),
        out_shape=jax.ShapeDtypeStruct((nr, D), bf16),
        grid_spec=pltpu.PrefetchScalarGridSpec(
            num_scalar_prefetch=1,
            grid=(nsteps,),
            in_specs=[pl.BlockSpec(memory_space=pl.ANY)],
            out_specs=pl.BlockSpec((rows, D), lambda i, tok: (i, 0)),
            scratch_shapes=[pltpu.VMEM((2, rows, D), f32), pltpu.SemaphoreType.DMA((2,))]),
        compiler_params=_cp(("arbitrary",)),
        name="moe_gather",
    )(slot_tok, h)


def _moe_up_kernel(start_ref, nblk_ref, xs_hbm, w_hbm, b_ref, act_hbm,
                   wbf_ref, wf32, xbuf, obuf, wsem, xsem, osem, *, tn, layer):
    j = pl.program_id(0)
    e = pl.program_id(1)
    nb = nblk_ref[e]
    row0 = start_ref[e]
    g = j * N_EXPERTS + e

    def w_copy(gg, slot):
        c0 = pl.multiple_of((gg // N_EXPERTS) * tn, tn)
        return pltpu.make_async_copy(w_hbm.at[layer, gg % N_EXPERTS, :, pl.ds(c0, tn)], wf32.at[slot],
                                     wsem.at[slot])

    def x_copy(blk, slot):
        r = pl.multiple_of(row0 + blk * MOE_TM, MOE_TM)
        return pltpu.make_async_copy(xs_hbm.at[pl.ds(r, MOE_TM), :], xbuf.at[slot], xsem.at[slot])

    def o_copy(blk, slot):
        r = pl.multiple_of(row0 + blk * MOE_TM, MOE_TM)
        return pltpu.make_async_copy(obuf.at[slot], act_hbm.at[j, pl.ds(r, MOE_TM), :], osem.at[slot])

    @pl.when(g == 0)
    def _():
        w_copy(0, 0).start(priority=1)

    @pl.when(nb > 0)
    def _():
        x_copy(0, 0).start()

    @pl.when(g + 1 < pl.num_programs(0) * N_EXPERTS)
    def _():
        w_copy(g + 1, (g + 1) % 2).start(priority=1)

    w_copy(g, g % 2).wait()

    @pl.when(nb > 0)
    def _():
        wbf_ref[...] = wf32[g % 2].astype(bf16)

    lane = lax.broadcasted_iota(jnp.int32, (MOE_TM, 128), 1)
    even = (2 * lane) % 128
    first_half = lane < 64

    def pick(a, b, idx):
        return jnp.where(first_half, jnp.take_along_axis(a, idx, axis=1), jnp.take_along_axis(b, idx, axis=1))

    def body(blk, c):
        slot = blk % 2
        x_copy(blk, slot).wait()

        @pl.when(blk + 1 < nb)
        def _():
            x_copy(blk + 1, 1 - slot).start()

        @pl.when(blk >= 2)
        def _():
            o_copy(blk - 2, slot).wait()

        x = xbuf[slot]
        for cc in range(tn // 256):
            cs = slice(cc * 256, (cc + 1) * 256)
            h = jnp.dot(x, wbf_ref[:, cs], preferred_element_type=f32) + b_ref[:, cs]
            a, b = h[:, :128], h[:, 128:]
            gate = jnp.minimum(pick(a, b, even), SWIGLU_LIMIT)
            lin = jnp.clip(pick(a, b, even + 1), -SWIGLU_LIMIT, SWIGLU_LIMIT)
            act = gate * jax.nn.sigmoid(SWIGLU_ALPHA * gate) * (lin + 1.0)
            obuf[slot, :, cc * 128:(cc + 1) * 128] = act.astype(obuf.dtype)
        o_copy(blk, slot).start()
        return c

    lax.fori_loop(0, nb, body, 0)

    @pl.when(nb >= 2)
    def _():
        o_copy(nb - 2, nb % 2).wait()

    @pl.when(nb >= 1)
    def _():
        o_copy(nb - 1, (nb - 1) % 2).wait()

    @pl.when(e == N_EXPERTS - 1)
    def _():
        obuf[0] = jnp.zeros(obuf.shape[1:], obuf.dtype)

        def zero_tail(blk, c):
            r = pl.multiple_of(blk * MOE_TM, MOE_TM)
            cp = pltpu.make_async_copy(obuf.at[0], act_hbm.at[j, pl.ds(r, MOE_TM), :], osem.at[0])
            cp.start()
            cp.wait()
            return c

        lax.fori_loop(row0 // MOE_TM + nb, act_hbm.shape[1] // MOE_TM, zero_tail, 0)


def _moe_up(start, nblk_e, xs, w1, b1, layer, nr):
    tn = 2048
    nj = 2 * D_FF // tn
    return pl.pallas_call(
        functools.partial(_moe_up_kernel, tn=tn, layer=layer),
        out_shape=jax.ShapeDtypeStruct((nj, nr, tn // 2), bf16),
        grid_spec=pltpu.PrefetchScalarGridSpec(
            num_scalar_prefetch=2,
            grid=(nj, N_EXPERTS),
            in_specs=[pl.BlockSpec(memory_space=pl.ANY),
                      pl.BlockSpec(memory_space=pl.ANY),
                      pl.BlockSpec((None, None, 1, tn), lambda j, e, st, nb: (layer, e, 0, j))],
            out_specs=pl.BlockSpec(memory_space=pl.ANY),
            scratch_shapes=[pltpu.VMEM((D, tn), bf16),
                            pltpu.VMEM((2, D, tn), f32),
                            pltpu.VMEM((2, MOE_TM, D), bf16),
                            pltpu.VMEM((2, MOE_TM, tn // 2), bf16),
                            pltpu.SemaphoreType.DMA((2,)), pltpu.SemaphoreType.DMA((2,)),
                            pltpu.SemaphoreType.DMA((2,))]),
        compiler_params=_cp(("arbitrary", "arbitrary"), 56),
        name="moe_up",
    )(start, nblk_e, xs, w1, b1)


def _moe_down_kernel(start_ref, nblk_ref, act_hbm, w_hbm, b_ref, y_hbm, wbf_ref, wf32, abuf, obuf,
                     wsem, asem, osem, *, layer):
    e = pl.program_id(0)
    nb = nblk_ref[e]
    row0 = start_ref[e]
    nj = abuf.shape[1]
    kh = D_FF // nj

    def w_copy(ee, slot):
        return pltpu.make_async_copy(w_hbm.at[layer, ee], wf32.at[slot], wsem.at[slot])

    def a_copy(blk, slot, jj):
        r = pl.multiple_of(row0 + blk * MOE_TM, MOE_TM)
        return pltpu.make_async_copy(act_hbm.at[jj, pl.ds(r, MOE_TM), :], abuf.at[slot, jj], asem.at[slot])

    def o_copy(blk, slot):
        r = pl.multiple_of(row0 + blk * MOE_TM, MOE_TM)
        return pltpu.make_async_copy(obuf.at[slot], y_hbm.at[pl.ds(r, MOE_TM), :], osem.at[slot])

    @pl.when(e == 0)
    def _():
        w_copy(0, 0).start(priority=1)

    @pl.when(nb > 0)
    def _():
        for jj in range(nj):
            a_copy(0, 0, jj).start()

    @pl.when(e + 1 < N_EXPERTS)
    def _():
        w_copy(e + 1, (e + 1) % 2).start(priority=1)

    w_copy(e, e % 2).wait()

    @pl.when(nb > 0)
    def _():
        wbf_ref[...] = wf32[e % 2].astype(bf16)

    def body(blk, c):
        slot = blk % 2
        for jj in range(nj):
            a_copy(blk, slot, jj).wait()

        @pl.when(blk + 1 < nb)
        def _():
            for jj in range(nj):
                a_copy(blk + 1, 1 - slot, jj).start()

        y = b_ref[...] + jnp.dot(abuf[slot, 0], wbf_ref[0:kh, :], preferred_element_type=f32)
        for jj in range(1, nj):
            y = y + jnp.dot(abuf[slot, jj], wbf_ref[jj * kh:(jj + 1) * kh, :], preferred_element_type=f32)

        @pl.when(blk >= 2)
        def _():
            o_copy(blk - 2, slot).wait()

        obuf[slot] = y
        o_copy(blk, slot).start()
        return c

    lax.fori_loop(0, nb, body, 0)

    @pl.when(nb >= 2)
    def _():
        o_copy(nb - 2, nb % 2).wait()

    @pl.when(nb >= 1)
    def _():
        o_copy(nb - 1, (nb - 1) % 2).wait()

    @pl.when(e == N_EXPERTS - 1)
    def _():
        obuf[0] = jnp.zeros(obuf.shape[1:], obuf.dtype)

        def zero_tail(blk, c):
            r = pl.multiple_of(blk * MOE_TM, MOE_TM)
            cp = pltpu.make_async_copy(obuf.at[0], y_hbm.at[pl.ds(r, MOE_TM), :], osem.at[0])
            cp.start()
            cp.wait()
            return c

        lax.fori_loop(row0 // MOE_TM + nb, y_hbm.shape[0] // MOE_TM, zero_tail, 0)


def _moe_down(start, nblk_e, act, w2, b2, layer, nr):
    nj = act.shape[0]
    return pl.pallas_call(
        functools.partial(_moe_down_kernel, layer=layer),
        out_shape=jax.ShapeDtypeStruct((nr, D), f32),
        grid_spec=pltpu.PrefetchScalarGridSpec(
            num_scalar_prefetch=2,
            grid=(N_EXPERTS,),
            in_specs=[pl.BlockSpec(memory_space=pl.ANY),
                      pl.BlockSpec(memory_space=pl.ANY),
                      pl.BlockSpec((None, None, 1, D), lambda e, st, nb: (layer, e, 0, 0))],
            out_specs=pl.BlockSpec(memory_space=pl.ANY),
            scratch_shapes=[pltpu.VMEM((D_FF, D), bf16),
                            pltpu.VMEM((2, D_FF, D), f32),
                            pltpu.VMEM((2, nj, MOE_TM, D_FF // nj), bf16),
                            pltpu.VMEM((2, MOE_TM, D), f32),
                            pltpu.SemaphoreType.DMA((2,)), pltpu.SemaphoreType.DMA((2,)),
                            pltpu.SemaphoreType.DMA((2,))]),
        compiler_params=_cp(("arbitrary",), 56),
        name="moe_down",
    )(start, nblk_e, act, w2, b2)


def _combine_kernel(slot_ref, y_hbm, x_ref, gmod_ref, w_ref, o_ref, buf, sem, *, tm, nsteps):
    i = pl.program_id(0)

    def issue(blk, slot):
        base = blk * (tm * TOP_K)
        for r in range(tm):
            for k in range(TOP_K):
                s = slot_ref[base + r * TOP_K + k]
                pltpu.make_async_copy(y_hbm.at[pl.ds(s, 1), :], buf.at[slot, k, pl.ds(r, 1), :],
                                      sem.at[slot]).start(priority=k % 2)

    @pl.when(i == 0)
    def _():
        issue(0, 0)

    for par in range(2):
        @pl.when(jnp.logical_and(i + 1 < nsteps, (i + 1) % 2 == par))
        def _():
            issue(i + 1, par)

    slot = i % 2
    for k in range(TOP_K):
        pltpu.make_async_copy(y_hbm.at[pl.ds(0, tm), :], buf.at[slot, k], sem.at[slot]).wait()
    w = w_ref[...]
    acc = ((w[:, 0:1] * buf[slot, 0] + w[:, 1:2] * buf[slot, 1])
           + (w[:, 2:3] * buf[slot, 2] + w[:, 3:4] * buf[slot, 3]))
    o_ref[...] = x_ref[...] + gmod_ref[...] * acc


def _combine(slot_flat, ys, x, mod3, gate, gate_chunk, blk, mrow, nblk):
    tm = RB
    return pl.pallas_call(
        functools.partial(_combine_kernel, tm=tm, nsteps=nblk),
        out_shape=jax.ShapeDtypeStruct((nblk * tm, D), f32),
        grid_spec=pltpu.PrefetchScalarGridSpec(
            num_scalar_prefetch=1,
            grid=(nblk,),
            in_specs=[pl.BlockSpec(memory_space=pl.ANY),
                      pl.BlockSpec((tm, D), lambda i, s: (blk(i), 0)),
                      pl.BlockSpec((None, 1, D), lambda i, s: (mrow(i), 0, gate_chunk)),
                      pl.BlockSpec((tm, TOP_K), lambda i, s: (i, 0))],
            out_specs=pl.BlockSpec((tm, D), lambda i, s: (i, 0)),
            scratch_shapes=[pltpu.VMEM((2, TOP_K, tm, D), f32), pltpu.SemaphoreType.DMA((2,))]),
        compiler_params=_cp(("arbitrary",)),
        name="moe_combine",
    )(slot_flat, ys, x, mod3, gate)


def _moe_layer(x, g3, mod3, rw, rb3, w1, b1, w2, b2, layer, blk, mrow, nblk):
    t = nblk * RB
    h, idx, gate, rank, cnt = _norm_router(x, g3, mod3, rw, rb3, layer, 3, blk, mrow, nblk)
    n_assign = t * TOP_K
    nb = n_assign // MOE_TM + N_EXPERTS
    nr = nb * MOE_TM
    counts = cnt[0].astype(jnp.int32)
    padded = (counts + MOE_TM - 1) // MOE_TM * MOE_TM
    pad_end = jnp.cumsum(padded)
    pad_start = pad_end - padded
    slot = pad_start[idx] + rank
    slot_flat = slot.reshape(-1)
    tok = jnp.repeat(jnp.arange(t, dtype=jnp.int32), TOP_K)
    slot_tok = (jnp.arange(nr, dtype=jnp.int32) % t).at[slot_flat].set(
        tok, unique_indices=True, mode="promise_in_bounds")
    nblk_e = (padded // MOE_TM).astype(jnp.int32)
    xs = _gather_rows(slot_tok, h, nr)
    act = _moe_up(pad_start.astype(jnp.int32), nblk_e, xs, w1, b1, layer, nr)
    ys = _moe_down(pad_start.astype(jnp.int32), nblk_e, act, w2, b2, layer, nr)
    return _combine(slot_flat, ys, x, mod3, gate, 5, blk, mrow, nblk)


def _dot_nt(a, b):
    return lax.dot_general(a, b, (((1,), (1,)), ((), ())), preferred_element_type=f32)


def _dot_tn(a, b):
    return lax.dot_general(a, b, (((0,), (0,)), ((), ())), preferred_element_type=f32)


def _mlstm_kernel(q_ref, k_ref, v_ref, o_ref, gc_ref, gr_ref, ng_ref, y_ref, hf_ref, hb_ref, ct_ref):
    T = A_T
    nch = LA // T
    ct_ref[...] = jnp.zeros_like(ct_ref)
    r_i = lax.broadcasted_iota(jnp.int32, (T, T), 0)
    c_i = lax.broadcasted_iota(jnp.int32, (T, T), 1)
    lane = lax.broadcasted_iota(jnp.int32, (T, 128), 1)
    ones_blk = jnp.where(lane == 0, 1.0, 0.0).astype(bf16)

    def chunk(c, dirn, h_ref):
        r0 = pl.multiple_of(c * T, T)
        q = q_ref[pl.ds(r0, T), :]
        k = k_ref[pl.ds(r0, T), :]
        v = v_ref[pl.ds(r0, T), :]
        gc = gc_ref[pl.ds(r0, T), :]
        o = 5 * dirn
        a_col, m_col, negm = gc[:, o:o + 1], gc[:, o + 1:o + 2], gc[:, o + 2:o + 3]
        m_end, m_prev = gc[:, o + 3:o + 4], gc[:, o + 4:o + 5]
        a_row = gr_ref[dirn:dirn + 1, pl.ds(r0, T)]
        mask = (c_i <= r_i) if dirn == 0 else (c_i >= r_i)
        w = jnp.exp(jnp.where(mask, a_row - m_col, -jnp.inf))
        s = _dot_nt(q, k) * (A_DQK ** -0.5) * w
        v_aug = jnp.concatenate([v, ones_blk], axis=1)
        ct = ct_ref[dirn]
        w_carry = jnp.exp(m_prev - m_col)
        num = (jnp.dot(s.astype(bf16), v_aug, preferred_element_type=f32)
               + w_carry * jnp.dot(q, ct.astype(bf16), preferred_element_type=f32))
        den = num[:, A_DV:A_DV + 1]
        h_ref[pl.ds(r0, T), :] = num[:, :A_DV] / jnp.maximum(jnp.abs(den), jnp.exp(negm))
        w_in = jnp.exp(a_col - m_end)
        decay = jnp.exp(m_prev[0:1, :] - m_end[0:1, :])
        kv = _dot_tn(k, (w_in * v_aug.astype(f32)).astype(bf16)) * (A_DQK ** -0.5)
        ct_ref[dirn] = decay * ct + kv

    def step(it, carry):
        chunk(it, 0, hf_ref)
        chunk(jnp.where(it == 0, 0, nch - it), 1, hb_ref)
        return carry

    lax.fori_loop(0, nch, step, 0)
    h = hf_ref[...] + hb_ref[...]
    hn = h * lax.rsqrt(jnp.mean(h * h, axis=-1, keepdims=True) + EPS) * ng_ref[...]
    y_ref[...] = (hn * jax.nn.sigmoid(o_ref[...].astype(f32))).astype(y_ref.dtype)


def _mlstm(proj, gcol, grow, ng3):
    qb, vb = A_DQK, A_DV
    return pl.pallas_call(
        _mlstm_kernel,
        out_shape=jax.ShapeDtypeStruct((ROWS, HALF), bf16),
        grid=(NB_, A_HEADS),
        in_specs=[pl.BlockSpec((LA, qb), lambda b, h: (b, h)),
                  pl.BlockSpec((LA, qb), lambda b, h: (b, A_HEADS + h)),
                  pl.BlockSpec((LA, vb), lambda b, h: (b, (2 * A_HEADS * qb) // vb + h)),
                  pl.BlockSpec((LA, vb), lambda b, h: (b, (2 * A_HEADS * qb + HALF) // vb + h)),
                  pl.BlockSpec((None, None, LA, 16), lambda b, h: (b, h, 0, 0)),
                  pl.BlockSpec((None, None, 8, LA), lambda b, h: (b, h, 0, 0)),
                  pl.BlockSpec((None, 1, vb), lambda b, h: (h, 0, 0))],
        out_specs=pl.BlockSpec((LA, vb), lambda b, h: (b, h)),
        scratch_shapes=[pltpu.VMEM((LA, vb), f32), pltpu.VMEM((LA, vb), f32),
                        pltpu.VMEM((2, qb, vb + 128), f32)],
        compiler_params=_cp(("arbitrary", "arbitrary")),
        name="mlstm",
    )(proj, proj, proj, proj, gcol, grow, ng3)


def _mlstm_gate_tables(gates):
    i_f, f_f, i_b, f_b = jnp.split(gates, 4, axis=-1)

    def scan_tables(i_pre, f_pre):
        lf = jax.nn.log_sigmoid(f_pre)
        big_f = jnp.cumsum(lf, axis=1)
        a = i_pre - big_f
        m_run = jnp.maximum(lax.cummax(a, axis=1), 0.0)
        m_end = jnp.repeat(m_run[:, A_T - 1::A_T], A_T, axis=1)
        m_prev = jnp.concatenate([jnp.zeros_like(m_end[:, :A_T]), m_end[:, :-A_T]], axis=1)
        return a, m_run, -(big_f + m_run), m_end, m_prev

    def to_bwd(t):
        return jnp.concatenate([jnp.flip(t[:, :CTX], axis=1), jnp.flip(t[:, CTX:], axis=1)], axis=1)

    fw = scan_tables(i_f, f_f)
    bw = [to_bwd(t) for t in scan_tables(to_bwd(i_b), to_bwd(f_b))]
    cols = jnp.stack(list(fw) + bw, axis=-1)
    cols = jnp.pad(cols, ((0, 0), (0, 0), (0, 0), (0, 6))).transpose(0, 2, 1, 3)
    rows = jnp.stack([fw[0], bw[0]], axis=1).transpose(0, 3, 1, 2)
    rows = jnp.pad(rows, ((0, 0), (0, 0), (0, 6), (0, 0)))
    return cols, rows


def _swap_halves(t):
    hd = t.shape[-1] // 2
    return jnp.concatenate([t[..., hd:], t[..., :hd]], axis=-1)


def _attn_kernel(q_ref, k_ref, v_ref, cos_ref, sin_ref, gq_ref, gk_ref, sink_ref, o_ref):
    n = pl.program_id(2)
    scale = B_DH ** -0.5
    nq = B_G * 128

    def norm(t, g_ref):
        return t * lax.rsqrt(jnp.mean(t * t, axis=-1, keepdims=True) + EPS) * g_ref[...]

    def head_cols(ref, rows, h):
        return ref[rows, h * B_DH:(h + 1) * B_DH]

    def finish(j, parts):
        sink = sink_ref[j]
        m = sink
        for s, _ in parts:
            m = jnp.maximum(m, jnp.max(s, axis=-1, keepdims=True))
        den = jnp.exp(sink - m)
        acc = None
        for s, vv in parts:
            p = jnp.exp(s - m)
            den = den + jnp.sum(p, axis=-1, keepdims=True)
            pv = jnp.dot(p.astype(bf16), vv, preferred_element_type=f32)
            acc = pv if acc is None else acc + pv
        o = (acc / den).astype(o_ref.dtype)
        for g in range(B_G):
            h = j * B_G + g
            o_ref[:, h * B_DH:(h + 1) * B_DH] = o[g * 128:(g + 1) * 128]

    def q_heads(j):
        q = jnp.concatenate([head_cols(q_ref, slice(None), j * B_G + g) for g in range(B_G)], axis=0)
        return norm(q.astype(f32), gq_ref)

    def ctx_kv(j):
        kc = norm(head_cols(k_ref, pl.ds(0, CTX), j).astype(f32), gk_ref).astype(bf16)
        return kc, head_cols(v_ref, pl.ds(0, CTX), j)

    @pl.when(n < CTX // 128)
    def _():
        for j in range(B_KP):
            kc, vc = ctx_kv(j)
            s_ctx = _dot_nt(q_heads(j).astype(bf16), kc) * scale
            finish(j, [(s_ctx, vc)])

    @pl.when(n >= CTX // 128)
    def _():
        nl = n - CTX // 128
        pos0 = pl.multiple_of(nl * 128, 128)
        start = pl.multiple_of(jnp.clip((nl - 1) * 128, 0, SEQ - 3 * 128), 128)
        cq = jnp.concatenate([cos_ref[pl.ds(pos0, 128), :]] * B_G, axis=0)
        sq = jnp.concatenate([sin_ref[pl.ds(pos0, 128), :]] * B_G, axis=0)
        q_pos = pos0 + lax.broadcasted_iota(jnp.int32, (nq, 3 * 128), 0) % 128
        k_pos = start + lax.broadcasted_iota(jnp.int32, (nq, 3 * 128), 1)
        in_window = jnp.abs(q_pos - k_pos) <= WINDOW
        for j in range(B_KP):
            kc, vc = ctx_kv(j)
            qn = q_heads(j)
            qr = qn * cq + _swap_halves(qn) * sq
            kb = norm(head_cols(k_ref, pl.ds(CTX + start, 3 * 128), j).astype(f32), gk_ref)
            kb = kb * cos_ref[pl.ds(start, 3 * 128), :] + _swap_halves(kb) * sin_ref[pl.ds(start, 3 * 128), :]
            vb = head_cols(v_ref, pl.ds(CTX + start, 3 * 128), j)
            s_band = jnp.where(in_window, _dot_nt(qr.astype(bf16), kb.astype(bf16)) * scale, -jnp.inf)
            s_ctx = _dot_nt(qn.astype(bf16), kc) * scale
            finish(j, [(s_band, vb), (s_ctx, vc)])


def _attention(proj, cosf, sinf, gq, gk, sink_col):
    nblk = LA // 128
    qw = B_KP * B_G * B_DH
    kw = B_KP * B_DH
    return pl.pallas_call(
        _attn_kernel,
        out_shape=jax.ShapeDtypeStruct((ROWS, HALF), bf16),
        grid=(NB_, B_KV // B_KP, nblk),
        in_specs=[pl.BlockSpec((128, qw), lambda b, kp, n: (b * nblk + n, ATT_COL0 // qw + kp)),
                  pl.BlockSpec((LA, kw), lambda b, kp, n: (b, (ATT_COL0 + HALF) // kw + kp)),
                  pl.BlockSpec((LA, kw), lambda b, kp, n: (b, (ATT_COL0 + HALF + B_KV * B_DH) // kw + kp)),
                  pl.BlockSpec((SEQ, B_DH), lambda b, kp, n: (0, 0)),
                  pl.BlockSpec((SEQ, B_DH), lambda b, kp, n: (0, 0)),
                  pl.BlockSpec((1, B_DH), lambda b, kp, n: (0, 0)),
                  pl.BlockSpec((1, B_DH), lambda b, kp, n: (0, 0)),
                  pl.BlockSpec((B_KP, B_G * 128, 1), lambda b, kp, n: (kp, 0, 0))],
        out_specs=pl.BlockSpec((128, qw), lambda b, kp, n: (b * nblk + n, kp)),
        compiler_params=_cp(("arbitrary", "arbitrary", "arbitrary")),
        name="window_attn",
    )(proj, proj, proj, cosf, sinf, gq, gk, sink_col)


def _rope_tables():
    rows = SEQ // GRID_W
    row = jnp.repeat(jnp.arange(rows), GRID_W).astype(f32)
    col = jnp.tile(jnp.arange(GRID_W), rows).astype(f32)
    n_freq = B_DH // 4
    inv = ROPE_BASE ** (-jnp.arange(n_freq, dtype=f32) / n_freq)
    ang = jnp.concatenate([row[:, None] * inv, col[:, None] * inv], axis=-1)
    cos, sin = jnp.cos(ang), jnp.sin(ang)
    return jnp.concatenate([cos, cos], axis=-1), jnp.concatenate([-sin, sin], axis=-1)


def _s5_kernel(u_ref, cwy_ref, cwi_ref, a_ref, y_ref, wy, wi, u32, ucat, zf, zb):
    T = S5_T
    hs = S5_NS
    ku = T * 128

    def onehot_tile(n, reps):
        r = lax.broadcasted_iota(jnp.int32, (n, n * reps), 0)
        c = lax.broadcasted_iota(jnp.int32, (n, n * reps), 1)
        return jnp.where(c % n == r, 1.0, 0.0).astype(bf16)

    def group_mask(rows, row_group, width):
        r = lax.broadcasted_iota(jnp.int32, (rows, S5_GT * width), 0)
        c = lax.broadcasted_iota(jnp.int32, (rows, S5_GT * width), 1)
        return row_group(r) == c // width

    rep_p = onehot_tile(C_STATE, S5_GT)
    m_i = group_mask(ku, lambda r: (r // C_GROUP) % S5_GT, C_STATE)
    nrp = (ku + 4 * hs) // S5_GT
    sa = lax.broadcasted_iota(jnp.int32, (128, 128), 0)
    sb = lax.broadcasted_iota(jnp.int32, (128, 128), 1)
    rr = lax.broadcasted_iota(jnp.int32, (nrp, 128), 0)
    cg = lax.broadcasted_iota(jnp.int32, (nrp, 128), 1) // C_GROUP
    for q in range(S5_GT):
        sel = jnp.where(sa == q * C_GROUP + sb % C_GROUP, 1.0, 0.0).astype(bf16)
        r = rr + q * nrp
        grp = jnp.where(r < ku, (r // C_GROUP) % S5_GT, ((r - ku) % hs) // C_STATE)
        for t in range(T):
            e = jnp.dot(cwy_ref[t], sel, preferred_element_type=f32)
            wy[t, q * nrp:(q + 1) * nrp, :] = jnp.where(grp == cg, e, 0.0).astype(bf16)
    for d in range(2):
        for q in range(2):
            e = jnp.dot(cwi_ref[d, :, q * C_STATE:(q + 1) * C_STATE], rep_p, preferred_element_type=f32)
            wi[d, :, q * hs:(q + 1) * hs] = jnp.where(m_i, e, 0.0).astype(bf16)

    u32[...] = u_ref[...].astype(f32)
    for s in range(T):
        for b in range(NB_):
            ucat[s, pl.ds(b, S5_CPB, stride=NB_), :] = u32[pl.ds(b * LA + s, S5_CPB, stride=T), :]
    uc = jnp.concatenate([ucat[s].astype(bf16) for s in range(T)], axis=1)
    zf[...] = jnp.dot(uc, wi[0], preferred_element_type=f32)
    zb[...] = jnp.dot(uc, wi[1], preferred_element_type=f32)
    a = a_ref[...]

    def cstep(av, x, z):
        ar, ai = av[:, :hs], av[:, hs:]
        xr, xi = x[:, :hs], x[:, hs:]
        return jnp.concatenate([ar * xr - ai * xi + z[:, :hs], ar * xi + ai * xr + z[:, hs:]], axis=1)

    def step(i, carry):
        x_f, x_b = carry
        rf = pl.multiple_of(i * 2 * NB_, 2 * NB_)
        mb = jnp.where(i < S5_CCTX // 2, S5_CCTX // 2 - 1 - i, S5_CPB // 2 + S5_CCTX // 2 - 1 - i)
        rb = pl.multiple_of(mb * 2 * NB_, 2 * NB_)
        z8 = zf[pl.ds(rf, 2 * NB_), :]
        x1 = cstep(a[0:1], x_f, z8[:NB_])
        zf[pl.ds(rf, 2 * NB_), :] = jnp.concatenate([x_f, x1], axis=0)
        x_f2 = cstep(a[0:1], x1, z8[NB_:])
        w8 = zb[pl.ds(rb, 2 * NB_), :]
        y1 = cstep(a[1:2], x_b, w8[NB_:])
        zb[pl.ds(rb, 2 * NB_), :] = jnp.concatenate([y1, x_b], axis=0)
        x_b2 = cstep(a[1:2], y1, w8[:NB_])
        return x_f2, x_b2

    zero = jnp.zeros((NB_, 2 * hs), f32)
    lax.fori_loop(0, S5_CPB // 2, step, (zero, zero))
    xf = zf[...].astype(bf16)
    xb = zb[...].astype(bf16)
    for t in range(T):
        u32[pl.ds(t * S5_NCH, S5_NCH), :] = (
            jnp.dot(uc, wy[t, 0:ku, :], preferred_element_type=f32)
            + jnp.dot(xf, wy[t, ku:ku + 2 * hs, :], preferred_element_type=f32)
            + jnp.dot(xb, wy[t, ku + 2 * hs:ku + 4 * hs, :], preferred_element_type=f32))
    for t in range(T):
        for b in range(NB_):
            y_ref[pl.ds(b * LA + t, S5_CPB, stride=T), :] = u32[pl.ds(t * S5_NCH + b, S5_CPB, stride=NB_), :]


def _s5_core(u_all, mats):
    cwy, cwi, a = mats
    nt = HALF // 128
    kf = S5_T * 128 + 4 * S5_NS
    return pl.pallas_call(
        _s5_kernel,
        out_shape=jax.ShapeDtypeStruct((ROWS, HALF), f32),
        grid=(nt,),
        in_specs=[pl.BlockSpec((ROWS, 128), lambda j: (0, j)),
                  pl.BlockSpec((None, S5_T, kf // S5_GT, 128), lambda j: (j, 0, 0, 0)),
                  pl.BlockSpec((None, 2, S5_T * 128, 2 * C_STATE), lambda j: (j, 0, 0, 0)),
                  pl.BlockSpec((None, 2, 2 * S5_NS), lambda j: (j, 0, 0))],
        out_specs=pl.BlockSpec((ROWS, 128), lambda j: (0, j)),
        scratch_shapes=[pltpu.VMEM((S5_T, kf, 128), bf16),
                        pltpu.VMEM((2, S5_T * 128, 2 * S5_NS), bf16),
                        pltpu.VMEM((ROWS, 128), f32),
                        pltpu.VMEM((S5_T, S5_NCH, 128), f32),
                        pltpu.VMEM((S5_NCH, 2 * S5_NS), f32),
                        pltpu.VMEM((S5_NCH, 2 * S5_NS), f32)],
        compiler_params=_cp(("arbitrary",), 56),
        name="s5_core",
    )(u_all, cwy, cwi, a)


def _s5_matrices(a_re, a_im, log_dt, b_re, b_im, c_re, c_im):
    T = S5_T
    G, P, C = C_GROUPS, C_STATE, C_GROUP
    nt = G // S5_GT
    lam = lax.complex(a_re.astype(f32), a_im.astype(f32))
    dt = jnp.exp(log_dt.astype(f32))[..., None]
    a_bar = jnp.exp(lam * dt)
    b_bar = ((a_bar - 1) / lam)[..., None] * lax.complex(b_re.astype(f32), b_im.astype(f32))
    tau = jnp.arange(T + 1, dtype=f32)
    apow = jnp.exp((lam * dt)[..., None] * tau)
    cc = lax.complex(c_re.astype(f32), c_im.astype(f32))
    kern = jnp.einsum("gxp,dgpt,dgpc->dgtxc", cc, apow[..., :T], b_bar).real
    t_i = jnp.arange(T)
    lag = t_i[:, None] - t_i[None, :]
    kf = kern[0][:, jnp.clip(lag, 0, T - 1)] * (lag >= 0)[None, :, :, None, None]
    kb = kern[1][:, jnp.clip(-lag, 0, T - 1)] * (lag <= 0)[None, :, :, None, None]
    km = (kf + kb).reshape(nt, S5_GT, T, T, C, C).transpose(0, 2, 3, 1, 5, 4)
    wy_u = km.reshape(nt, T, T * 128, C)
    k_f = cc[:, :, :, None] * apow[0][:, None, :, 1 + t_i]
    k_b = cc[:, :, :, None] * apow[1][:, None, :, T - t_i]

    def wy_state(k):
        k = k.reshape(nt, S5_GT, C, P, T).transpose(0, 4, 1, 3, 2).reshape(nt, T, S5_NS, C)
        return jnp.concatenate([k.real, -k.imag], axis=2)

    cwy = jnp.concatenate([wy_u, wy_state(k_f), wy_state(k_b)], axis=2).astype(bf16)
    nrow = cwy.shape[2]
    cwy = cwy.reshape(nt, T, S5_GT, nrow // S5_GT, C).transpose(0, 1, 3, 2, 4).reshape(nt, T, nrow // S5_GT, 128)
    w_f = apow[0][:, :, T - 1 - t_i][..., None] * b_bar[0][:, :, None, :]
    w_b = apow[1][:, :, t_i][..., None] * b_bar[1][:, :, None, :]

    def wi_dir(w):
        w = w.reshape(nt, S5_GT, P, T, C).transpose(0, 3, 1, 4, 2).reshape(nt, T * 128, P)
        return jnp.concatenate([w.real, w.imag], axis=2)

    cwi = jnp.stack([wi_dir(w_f), wi_dir(w_b)], axis=1).astype(bf16)
    at = apow[..., T].reshape(2, nt, S5_NS)
    a = jnp.concatenate([at.real, at.imag], axis=-1).transpose(1, 0, 2)
    return cwy, cwi, a


def _s5_glu_kernel(y_ref, u_ref, d_ref, w_ref, b_ref, o_ref, wbf_ref):
    @pl.when(pl.program_id(0) == 0)
    def _():
        wbf_ref[...] = w_ref[...].astype(bf16)

    y = y_ref[...] + d_ref[...] * u_ref[...].astype(f32)
    y = jax.nn.gelu(y, approximate=True)
    z = jnp.dot(y.astype(bf16), wbf_ref[...], preferred_element_type=f32) + b_ref[...]
    o_ref[...] = (y * jax.nn.sigmoid(z)).astype(o_ref.dtype)


def _s5_glu(y_ssm, proj, d2, glu_w, glu_b2):
    nblk = NB_ * LBPB
    return pl.pallas_call(
        _s5_glu_kernel,
        out_shape=jax.ShapeDtypeStruct((nblk * RB, HALF), bf16),
        grid=(nblk,),
        in_specs=[pl.BlockSpec((RB, HALF), lambda i: (_lat_blk(i), 0)),
                  pl.BlockSpec((RB, HALF), lambda i: (_lat_blk(i), 0)),
                  pl.BlockSpec((1, HALF), lambda i: (0, 0)),
                  pl.BlockSpec((HALF, HALF), lambda i: (0, 0)),
                  pl.BlockSpec((1, HALF), lambda i: (0, 0))],
        out_specs=pl.BlockSpec((RB, HALF), lambda i: (i, 0)),
        scratch_shapes=[pltpu.VMEM((HALF, HALF), bf16)],
        compiler_params=_cp(("arbitrary",)),
        name="s5_glu",
    )(y_ssm, proj, d2, glu_w, glu_b2)


def _hyena_filter(L, w1, b1, w2, b2, w3, freq):
    t = jnp.linspace(0.0, 1.0, L, dtype=f32)[:, None]
    w = 2 * math.pi * jnp.arange(L, dtype=f32)[:, None] / L
    bands = jnp.linspace(1e-4, HY_BANDS - 1, HY_BANDS, dtype=f32)
    feats = jnp.concatenate([t, jnp.cos(bands * w), -jnp.sin(bands * w)], axis=-1)
    h = jnp.sin(freq * (feats @ w1 + b1))
    h = jnp.sin(freq * (h @ w2 + b2))
    h = h @ w3
    fast = abs(math.log(HY_DECAY_TARGET) / HY_FAST_PCT)
    slow = abs(math.log(HY_DECAY_TARGET) / HY_SLOW_PCT)
    deltas = jnp.tile(jnp.linspace(slow, fast, HALF, dtype=f32), 2)
    h = h * jnp.exp(-t * deltas)
    h_fwd, h_bwd = jnp.split(h, 2, axis=-1)
    h_bwd = h_bwd.at[0].set(0.0)
    norm = jnp.sum(jnp.abs(h_fwd), axis=0, keepdims=True) + jnp.sum(jnp.abs(h_bwd), axis=0, keepdims=True)
    return h_fwd / norm, h_bwd / norm


def _hyena_dft_tables():
    n2 = 2 * SEQ
    n = jnp.arange(SEQ, dtype=jnp.int32)[:, None]
    k = jnp.arange(SEQ, dtype=jnp.int32)[None, :]
    kr = 64
    k1 = jnp.arange(SEQ // kr, dtype=jnp.int32)[None, :]
    k0 = jnp.arange(kr, dtype=jnp.int32)[None, :]
    ang_a = ((n * k1 * kr) % n2).astype(f32) * (2.0 * math.pi / n2)
    ang_b = ((n * k0) % n2).astype(f32) * (2.0 * math.pi / n2)
    ca, sa = jnp.cos(ang_a)[:, :, None], jnp.sin(ang_a)[:, :, None]
    cb, sb = jnp.cos(ang_b)[:, None, :], jnp.sin(ang_b)[:, None, :]
    c = (ca * cb - sa * sb).reshape(SEQ, SEQ)
    s = (sa * cb + ca * sb).reshape(SEQ, SEQ)
    nyq = jnp.where(n % 2 == 0, 1.0, -1.0).astype(f32)
    f_re, f_im = c, jnp.where(k == 0, nyq, -s)
    g_re = jnp.where(k == 0, 1.0, 2.0 * c) / n2
    g_im = jnp.where(k == 0, nyq, -2.0 * s) / n2

    def tiles(re, im):
        nk = SEQ // HY_KT
        return jnp.concatenate([re.reshape(SEQ, nk, 1, HY_KT), im.reshape(SEQ, nk, 1, HY_KT)],
                               axis=2).reshape(SEQ, 2 * SEQ).astype(bf16)

    return tiles(f_re, f_im), tiles(g_re, g_im)


def _hyena_kernel(x0_ref, x1_ref, v_ref, cw0_ref, cw1_ref, cw2_ref, cb0_ref, cb1_ref, cb2_ref, hb_ref,
                  f_ref, g_ref, ht_ref, o_ref, ut_scr, u_scr, y_scr):
    kt = pl.program_id(2)
    L = SEQ
    row = lax.broadcasted_iota(jnp.int32, (L, HY_TC), 0)

    def sconv(z_ref, cw_ref, cb_ref):
        z = z_ref[...].astype(f32)
        zm = jnp.where(row == 0, 0.0, pltpu.roll(z, 1, axis=0))
        zp = jnp.where(row == L - 1, 0.0, pltpu.roll(z, L - 1, axis=0))
        return zm * cw_ref[0:1, :] + z * cw_ref[1:2, :] + zp * cw_ref[2:3, :] + cb_ref[...]

    @pl.when(kt == 0)
    def _():
        u = sconv(x1_ref, cw1_ref, cb1_ref) * sconv(v_ref, cw2_ref, cb2_ref)
        u_scr[...] = u
        ut_scr[...] = u.T.astype(bf16)
        y_scr[...] = jnp.zeros_like(y_scr)

    xt = jnp.dot(ut_scr[...], f_ref[...], preferred_element_type=f32)
    h = ht_ref[...]
    xr, xi, hr, hi = xt[:, :HY_KT], xt[:, HY_KT:], h[:, :HY_KT], h[:, HY_KT:]
    col = lax.broadcasted_iota(jnp.int32, (HY_TC, HY_KT), 1) + kt * HY_KT
    dc = col == 0
    yr = xr * hr - jnp.where(dc, 0.0, xi * hi)
    yi = jnp.where(dc, xi * hi, xr * hi + xi * hr)
    yt = jnp.concatenate([yr, yi], axis=1).astype(bf16)
    y_scr[...] += _dot_nt(g_ref[...], yt)

    @pl.when(kt == pl.num_programs(2) - 1)
    def _():
        o_ref[...] = (sconv(x0_ref, cw0_ref, cb0_ref)
                      * (y_scr[...] + u_scr[...] * hb_ref[...])).astype(o_ref.dtype)


def _hyena(z_lat, conv_w, conv_b2, hy_bias2, ht, ft, gt):
    nct = HALF // HY_TC

    def zspec(k):
        return pl.BlockSpec((SEQ, HY_TC), lambda ct, b, kt: (b, k * nct + ct))

    def wspec(k):
        return pl.BlockSpec((3, HY_TC), lambda ct, b, kt: (0, k * nct + ct))

    def bspec(k):
        return pl.BlockSpec((1, HY_TC), lambda ct, b, kt: (0, k * nct + ct))

    return pl.pallas_call(
        _hyena_kernel,
        out_shape=jax.ShapeDtypeStruct((NB_ * SEQ, HALF), bf16),
        grid=(nct, NB_, SEQ // HY_KT),
        in_specs=[zspec(0), zspec(1), zspec(2), wspec(0), wspec(1), wspec(2), bspec(0), bspec(1), bspec(2),
                  pl.BlockSpec((1, HY_TC), lambda ct, b, kt: (0, ct)),
                  pl.BlockSpec((SEQ, 2 * HY_KT), lambda ct, b, kt: (0, kt)),
                  pl.BlockSpec((SEQ, 2 * HY_KT), lambda ct, b, kt: (0, kt)),
                  pl.BlockSpec((HY_TC, 2 * HY_KT), lambda ct, b, kt: (ct, kt))],
        out_specs=pl.BlockSpec((SEQ, HY_TC), lambda ct, b, kt: (b, ct)),
        scratch_shapes=[pltpu.VMEM((HY_TC, SEQ), bf16), pltpu.VMEM((SEQ, HY_TC), f32),
                        pltpu.VMEM((SEQ, HY_TC), f32)],
        compiler_params=_cp(("arbitrary", "arbitrary", "arbitrary"), 48),
        name="hyena",
    )(z_lat, z_lat, z_lat, conv_w, conv_w, conv_w, conv_b2, conv_b2, conv_b2, hy_bias2, ft, gt, ht)


def _hyena_filter_spectrum(h_fwd, h_bwd, ft):
    c = h_fwd.shape[1]
    ht2 = jnp.concatenate([h_fwd.T, h_bwd.T], axis=0).astype(bf16)
    xt = _matmul([ht2], ft, (), 2 * SEQ, 2 * HY_KT, 2 * c // RB, _ident, f32, name="hyena_filter_dft")
    col = jnp.arange(2 * SEQ)
    sgn = jnp.where((col % (2 * HY_KT) >= HY_KT) & (col != HY_KT), -1.0, 1.0).astype(f32)
    return xt[:c] + xt[c:] * sgn[None, :]


def kernel(x, c, ctx, c_ctx, ada_w, ada_b, norm_mix_g, norm_ffn_g, ev_w_in, ev_gate_b, ev_h_norm_g,
           ev_q_norm_g, ev_k_norm_g, ev_sink, ev_w_out, od_w_in, od_a_re, od_a_im, od_log_dt, od_b_re,
           od_b_im, od_c_re, od_c_im, od_d, od_glu_w, od_glu_b, od_conv_w, od_conv_b, od_filt_w1,
           od_filt_b1, od_filt_w2, od_filt_b2, od_filt_w3, od_filt_freq, od_hy_bias, od_w_out,
           moe_router_w, moe_router_b, moe_w1, moe_b1, moe_w2, moe_b2):
    depth = ada_w.shape[0]
    xu = jnp.concatenate([ctx, x], axis=1).reshape(ROWS, D)
    cond8 = jnp.concatenate([c_ctx[None], c, jnp.zeros((3, D), f32)], axis=0)
    ada_b3 = ada_b.reshape(depth, 1, 6 * D)
    g_mix3 = norm_mix_g.reshape(depth, 1, D)
    g_ffn3 = norm_ffn_g.reshape(depth, 1, D)
    rb3 = moe_router_b.reshape(depth, 1, N_EXPERTS)
    b1_4 = moe_b1.reshape(depth, N_EXPERTS, 1, 2 * D_FF)
    b2_4 = moe_b2.reshape(depth, N_EXPERTS, 1, D)
    nall = ROWS // RB
    nlat = NB_ * LBPB

    mod3 = _ada(cond8, ada_w, ada_b3, 0).reshape(8, 1, 6 * D)
    h = _norm_mod(xu, g_mix3, mod3, 0, 0, _ident, _mod_row, nall, bf16)
    w_in = ev_w_in[0]
    n_a = 2 * A_HEADS * A_DQK + 2 * HALF
    n_g = 4 * A_HEADS
    w_main = jnp.concatenate([w_in[:, :n_a], w_in[:, n_a + n_g:]], axis=1)
    w_gate = jnp.pad(w_in[:, n_a:n_a + n_g], ((0, 0), (0, 128 - n_g)))
    proj = _matmul([h], w_main, (), w_main.shape[1], 1536, nall, _ident, bf16, name="even_in")
    gates = _matmul([h], w_gate, (), 128, 128, nall, _ident, f32, name="even_gates")[:, :n_g]
    gates = (gates + ev_gate_b[0]).reshape(NB_, LA, n_g)
    gcol, grow = _mlstm_gate_tables(gates)
    y_a = _mlstm(proj, gcol, grow, ev_h_norm_g[0].reshape(A_HEADS, 1, A_DV))
    cosf, sinf = _rope_tables()
    sink_col = jnp.repeat(ev_sink[0].astype(f32).reshape(B_KV, B_G), 128, axis=1)[..., None]
    y_b = _attention(proj, cosf, sinf, ev_q_norm_g[0].reshape(1, B_DH), ev_k_norm_g[0].reshape(1, B_DH),
                     sink_col)
    xu = _matmul([y_a, y_b], ev_w_out, (0,), D, 1024, nall, _ident, f32,
                 res=xu, res_blk=_ident, mrow=_mod_row, mod3=mod3, gate_chunk=2, name="even_out")
    xu = _moe_layer(xu, g_ffn3, mod3, moe_router_w, rb3, moe_w1, b1_4, moe_w2, b2_4,
                    0, _ident, _mod_row, nall)

    mod3 = _ada(cond8, ada_w, ada_b3, 1).reshape(8, 1, 6 * D)
    h = _norm_mod(xu, g_mix3, mod3, 1, 0, _ident, _mod_row, nall, bf16)
    proj = _matmul([h], od_w_in, (0,), HALF, 1024, nall, _ident, bf16, name="odd_in_u")
    z_lat = _matmul([h], od_w_in, (0,), 3 * HALF, 1024, nlat, _lat_blk, bf16, w_col0=HALF // 1024,
                    name="odd_in_z")
    mats = _s5_matrices(od_a_re[0], od_a_im[0], od_log_dt[0], od_b_re[0], od_b_im[0], od_c_re[0], od_c_im[0])
    y_ssm = _s5_core(proj, mats)
    s_l = _s5_glu(y_ssm, proj, od_d[0].reshape(1, HALF), od_glu_w[0], od_glu_b[0].reshape(1, HALF))
    h_fwd, h_bwd = _hyena_filter(SEQ, od_filt_w1[0], od_filt_b1[0], od_filt_w2[0], od_filt_b2[0],
                                 od_filt_w3[0], od_filt_freq[0])
    ft, gt = _hyena_dft_tables()
    hy_l = _hyena(z_lat, od_conv_w[0], od_conv_b[0].reshape(1, 3 * HALF), od_hy_bias[0].reshape(1, HALF),
                  _hyena_filter_spectrum(h_fwd, h_bwd, ft), ft, gt)
    x_lat = _matmul([s_l, hy_l], od_w_out, (0,), D, 1024, nlat, _ident, f32,
                    res=xu, res_blk=_lat_blk, mrow=_lat_mod_row, mod3=mod3, gate_chunk=2, name="odd_out")
    out = _moe_layer(x_lat, g_ffn3, mod3, moe_router_w, rb3, moe_w1, b1_4, moe_w2, b2_4,
                     1, _ident, _lat_mod_row, nlat)
    return out.reshape(NB_, SEQ, D)
```

```python
import functools
import math

import jax
import jax.numpy as jnp
import numpy as np
from jax import lax
from jax.experimental import pallas as pl
from jax.experimental.pallas import tpu as pltpu

f32 = jnp.float32
bf16 = jnp.bfloat16

D = 2048
NB_ = 4
SEQ = 2048
CTX = 256
LA = SEQ + CTX
ROWS = NB_ * LA
RB = 256
BPB = LA // RB
LBPB = SEQ // RB
EPS = 1e-6
GRID_W = 64

HALF = D // 2
A_HEADS = 4
A_DV = HALF // A_HEADS
A_DQK = A_DV // 2
A_T = 256
B_HEADS = 16
B_KV = 4
B_G = B_HEADS // B_KV
B_DH = HALF // B_HEADS
B_KP = 2
ATT_COL0 = 2 * A_HEADS * A_DQK + 2 * HALF
WINDOW = 128
ROPE_BASE = 10000.0
C_GROUP = 16
C_GROUPS = HALF // C_GROUP
C_STATE = 64
S5_T = 8
S5_GT = 128 // C_GROUP
S5_NS = S5_GT * C_STATE
S5_NCH = ROWS // S5_T
S5_CPB = LA // S5_T
S5_CCTX = CTX // S5_T
HY_TC = 256
HY_KT = 512
HY_BANDS = 16
HY_DECAY_TARGET = 1e-2
HY_FAST_PCT = 0.3
HY_SLOW_PCT = 1.5
N_EXPERTS = 32
TOP_K = 4
D_FF = D
SWIGLU_LIMIT = 7.0
SWIGLU_ALPHA = 1.702
MOE_TM = 256
MOE_TAIL = 128

EVEN_SPLITS = (A_HEADS * A_DQK, A_HEADS * A_DQK, HALF, HALF, 4 * A_HEADS,
               B_HEADS * B_DH, B_KV * B_DH, B_KV * B_DH)

VMEM_MB = 1024 * 1024


def _cp(sem, vmem_mb=40):
    return pltpu.CompilerParams(dimension_semantics=sem, vmem_limit_bytes=vmem_mb * VMEM_MB)


def _lat_blk(i):
    return (i // LBPB) * BPB + 1 + i % LBPB


def _mod_row(u):
    return jnp.where(u % BPB == 0, 0, 1 + u // BPB)


def _lat_mod_row(i):
    return 1 + i // LBPB


def _ident(i):
    return i


def _ada_kernel(c_ref, w_ref, b_ref, o_ref):
    c = c_ref[...]
    s = c * jax.nn.sigmoid(c)
    o_ref[...] = jnp.dot(s.astype(bf16), w_ref[...].astype(bf16),
                         preferred_element_type=f32) + b_ref[...]


def _ada(cond8, ada_w, ada_b3, layer):
    tn = 1024
    n = 6 * D
    return pl.pallas_call(
        _ada_kernel,
        out_shape=jax.ShapeDtypeStruct((8, n), f32),
        grid=(n // tn,),
        in_specs=[pl.BlockSpec((8, D), lambda j: (0, 0)),
                  pl.BlockSpec((None, D, tn), lambda j: (layer, 0, j)),
                  pl.BlockSpec((None, 1, tn), lambda j: (layer, 0, j))],
        out_specs=pl.BlockSpec((8, tn), lambda j: (0, j)),
        compiler_params=_cp(("arbitrary",)),
        name="ada_params",
    )(cond8, ada_w, ada_b3)


def _norm_mod_kernel(x_ref, g_ref, sh_ref, sc_ref, o_ref):
    x = x_ref[...]
    y = x * lax.rsqrt(jnp.mean(x * x, axis=-1, keepdims=True) + EPS) * g_ref[...]
    o_ref[...] = (y * (1.0 + sc_ref[...]) + sh_ref[...]).astype(o_ref.dtype)


def _norm_mod(x, g3, mod3, layer, chunk, blk, mrow, nblk, out_dtype):
    return pl.pallas_call(
        _norm_mod_kernel,
        out_shape=jax.ShapeDtypeStruct((nblk * RB, D), out_dtype),
        grid=(nblk,),
        in_specs=[pl.BlockSpec((RB, D), lambda i: (blk(i), 0)),
                  pl.BlockSpec((None, 1, D), lambda i: (layer, 0, 0)),
                  pl.BlockSpec((None, 1, D), lambda i: (mrow(i), 0, chunk)),
                  pl.BlockSpec((None, 1, D), lambda i: (mrow(i), 0, chunk + 1))],
        out_specs=pl.BlockSpec((RB, D), lambda i: (i, 0)),
        compiler_params=_cp(("arbitrary",)),
        name="norm_mod",
    )(x, g3, mod3, mod3)


def _split_bf16(x):
    hi = x.astype(bf16)
    lo = (x - hi.astype(f32)).astype(bf16)
    return hi, lo


def _norm_router_kernel(x_ref, g_ref, sh_ref, sc_ref, rw_ref, rb_ref,
                        h_ref, idx_ref, gate_ref, rank_ref, cnt_ref, carry_ref):
    i = pl.program_id(0)

    @pl.when(i == 0)
    def _():
        carry_ref[...] = jnp.zeros_like(carry_ref)

    x = x_ref[...]
    y = x * lax.rsqrt(jnp.mean(x * x, axis=-1, keepdims=True) + EPS) * g_ref[...]
    h = y * (1.0 + sc_ref[...]) + sh_ref[...]
    h_ref[...] = h
    h_hi, h_lo = _split_bf16(h)
    w_hi, w_lo = _split_bf16(rw_ref[...])
    logits = (jnp.dot(h_hi, w_hi, preferred_element_type=f32)
              + jnp.dot(h_lo, w_hi, preferred_element_type=f32)
              + jnp.dot(h_hi, w_lo, preferred_element_type=f32)) + rb_ref[...]
    ne = logits.shape[-1]
    lane = lax.broadcasted_iota(jnp.int32, logits.shape, 1)
    l = logits
    vals, idxs = [], []
    for _ in range(TOP_K):
        m = jnp.max(l, axis=-1, keepdims=True)
        ix = jnp.min(jnp.where(l == m, lane, ne), axis=-1, keepdims=True)
        vals.append(m)
        idxs.append(ix)
        l = jnp.where(lane == ix, -jnp.inf, l)
    es = [jnp.exp(v - vals[0]) for v in vals]
    tot = es[0] + es[1] + es[2] + es[3]
    onehot = jnp.zeros(logits.shape, f32)
    for ix in idxs:
        onehot = onehot + jnp.where(lane == ix, 1.0, 0.0)
    r_i = lax.broadcasted_iota(jnp.int32, (RB, RB), 0)
    c_i = lax.broadcasted_iota(jnp.int32, (RB, RB), 1)
    tri = jnp.where(c_i < r_i, 1.0, 0.0).astype(bf16)
    prefix = jnp.dot(tri, onehot.astype(bf16), preferred_element_type=f32) + carry_ref[...]
    for k in range(TOP_K):
        idx_ref[:, k:k + 1] = idxs[k]
        gate_ref[:, k:k + 1] = es[k] / tot
        rk = jnp.sum(jnp.where(lane == idxs[k], prefix, 0.0), axis=-1, keepdims=True)
        rank_ref[:, k:k + 1] = rk.astype(jnp.int32)
    carry_ref[...] = carry_ref[...] + jnp.sum(onehot, axis=0, keepdims=True)
    cnt_ref[...] = carry_ref[...]


def _norm_router(x, g3, mod3, rw, rb3, layer, chunk, blk, mrow, nblk):
    t = nblk * RB
    return pl.pallas_call(
        _norm_router_kernel,
        out_shape=(jax.ShapeDtypeStruct((t, D), f32),
                   jax.ShapeDtypeStruct((t, TOP_K), jnp.int32),
                   jax.ShapeDtypeStruct((t, TOP_K), f32),
                   jax.ShapeDtypeStruct((t, TOP_K), jnp.int32),
                   jax.ShapeDtypeStruct((1, N_EXPERTS), f32)),
        grid=(nblk,),
        in_specs=[pl.BlockSpec((RB, D), lambda i: (blk(i), 0)),
                  pl.BlockSpec((None, 1, D), lambda i: (layer, 0, 0)),
                  pl.BlockSpec((None, 1, D), lambda i: (mrow(i), 0, chunk)),
                  pl.BlockSpec((None, 1, D), lambda i: (mrow(i), 0, chunk + 1)),
                  pl.BlockSpec((None, D, N_EXPERTS), lambda i: (layer, 0, 0)),
                  pl.BlockSpec((None, 1, N_EXPERTS), lambda i: (layer, 0, 0))],
        out_specs=(pl.BlockSpec((RB, D), lambda i: (i, 0)),
                   pl.BlockSpec((RB, TOP_K), lambda i: (i, 0)),
                   pl.BlockSpec((RB, TOP_K), lambda i: (i, 0)),
                   pl.BlockSpec((RB, TOP_K), lambda i: (i, 0)),
                   pl.BlockSpec((1, N_EXPERTS), lambda i: (0, 0))),
        scratch_shapes=[pltpu.VMEM((1, N_EXPERTS), f32)],
        compiler_params=_cp(("arbitrary",)),
        name="norm_router",
    )(x, g3, mod3, mod3, rw, rb3)


def _mm_kernel(*refs, n_a, k_split, epilogue):
    a_refs = refs[:n_a]
    w_ref = refs[n_a]
    pos = n_a + 1
    if epilogue:
        res_ref, gate_ref = refs[pos], refs[pos + 1]
        pos += 2
    o_ref, wbf_ref = refs[pos], refs[pos + 1]

    @pl.when(pl.program_id(1) == 0)
    def _():
        wbf_ref[...] = w_ref[...].astype(bf16)

    acc = None
    for t, a_ref in enumerate(a_refs):
        part = jnp.dot(a_ref[...], wbf_ref[t * k_split:(t + 1) * k_split, :],
                       preferred_element_type=f32)
        acc = part if acc is None else acc + part
    if epilogue:
        acc = res_ref[...] + gate_ref[...] * acc
    o_ref[...] = acc.astype(o_ref.dtype)


def _matmul(a_list, w, w_idx, n, tn, nblk, a_blk, out_dtype, *, a_col=None, w_col0=0,
            res=None, res_blk=None, mrow=None, mod3=None, gate_chunk=None, name="matmul"):
    n_a = len(a_list)
    k = w.shape[-2]
    k_split = k // n_a
    lead = len(w_idx)
    a_col = a_col or [0] * n_a
    in_specs = [pl.BlockSpec((RB, k_split), functools.partial(lambda j, i, c: (a_blk(i), c), c=c))
                for c in a_col]
    in_specs.append(pl.BlockSpec((None,) * lead + (k, tn), lambda j, i: tuple(w_idx) + (0, j + w_col0)))
    args = list(a_list) + [w]
    epilogue = res is not None
    if epilogue:
        per = D // tn
        in_specs.append(pl.BlockSpec((RB, tn), lambda j, i: (res_blk(i), j)))
        in_specs.append(pl.BlockSpec((None, 1, tn),
                                     lambda j, i: (mrow(i), 0, gate_chunk * per + j)))
        args += [res, mod3]
    return pl.pallas_call(
        functools.partial(_mm_kernel, n_a=n_a, k_split=k_split, epilogue=epilogue),
        out_shape=jax.ShapeDtypeStruct((nblk * RB, n), out_dtype),
        grid=(n // tn, nblk),
        in_specs=in_specs,
        out_specs=pl.BlockSpec((RB, tn), lambda j, i: (i, j)),
        scratch_shapes=[pltpu.VMEM((k, tn), bf16)],
        compiler_params=_cp(("arbitrary", "arbitrary"), 48),
        name=name,
    )(*args)


def _gather_rows_kernel(tok_ref, h_hbm, o_ref, buf, sem, *, rows, nsteps):
    i = pl.program_id(0)

    def issue(blk, slot):
        base = blk * rows
        for r in range(rows):
            t = tok_ref[base + r]
            pltpu.make_async_copy(h_hbm.at[pl.ds(t, 1), :], buf.at[slot, pl.ds(r, 1), :],
                                  sem.at[slot]).start(priority=r % 2)

    @pl.when(i == 0)
    def _():
        issue(0, 0)

    for par in range(2):
        @pl.when(jnp.logical_and(i + 1 < nsteps, (i + 1) % 2 == par))
        def _():
            issue(i + 1, par)

    slot = i % 2
    pltpu.make_async_copy(h_hbm.at[pl.ds(0, rows), :], buf.at[slot], sem.at[slot]).wait()
    o_ref[...] = buf[slot].astype(o_ref.dtype)


def _gather_rows(slot_tok, h, nr):
    rows = 512
    nsteps = nr // rows
    return pl.pallas_call(
        functools.partial(_gather_rows_kernel, rows=rows, nsteps=nsteps),
        out_shape=jax.ShapeDtypeStruct((nr, D), bf16),
        grid_spec=pltpu.PrefetchScalarGridSpec(
            num_scalar_prefetch=1,
            grid=(nsteps,),
            in_specs=[pl.BlockSpec(memory_space=pl.ANY)],
            out_specs=pl.BlockSpec((rows, D), lambda i, tok: (i, 0)),
            scratch_shapes=[pltpu.VMEM((2, rows, D), f32), pltpu.SemaphoreType.DMA((2,))]),
        compiler_params=_cp(("arbitrary",)),
        name="moe_gather",
    )(slot_tok, h)


def _moe_up_kernel(start_ref, nblk_ref, tail_ref, xs_hbm, w_hbm, b_ref, act_hbm,
                   wbf_ref, wf32, xbuf, obuf, xtail, otail, wsem, xsem, osem, tsem, *, tn, layer):
    j = pl.program_id(0)
    e = pl.program_id(1)
    nb = nblk_ref[e]
    has_tail = tail_ref[e] > 0
    row0 = start_ref[e]
    g = j * N_EXPERTS + e
    rt = pl.multiple_of(row0 + nb * MOE_TM, MOE_TAIL)

    def xt_copy():
        return pltpu.make_async_copy(xs_hbm.at[pl.ds(rt, MOE_TAIL), :], xtail, tsem.at[0])

    def ot_copy(r):
        return pltpu.make_async_copy(otail, act_hbm.at[j, pl.ds(r, MOE_TAIL), :], tsem.at[1])

    def w_copy(gg, slot):
        c0 = pl.multiple_of((gg // N_EXPERTS) * tn, tn)
        return pltpu.make_async_copy(w_hbm.at[layer, gg % N_EXPERTS, :, pl.ds(c0, tn)], wf32.at[slot],
                                     wsem.at[slot])

    def x_copy(blk, slot):
        r = pl.multiple_of(row0 + blk * MOE_TM, MOE_TAIL)
        return pltpu.make_async_copy(xs_hbm.at[pl.ds(r, MOE_TM), :], xbuf.at[slot], xsem.at[slot])

    def o_copy(blk, slot):
        r = pl.multiple_of(row0 + blk * MOE_TM, MOE_TAIL)
        return pltpu.make_async_copy(obuf.at[slot], act_hbm.at[j, pl.ds(r, MOE_TM), :], osem.at[slot])

    @pl.when(g == 0)
    def _():
        w_copy(0, 0).start(priority=1)

    @pl.when(nb > 0)
    def _():
        x_copy(0, 0).start()

    @pl.when(has_tail)
    def _():
        xt_copy().start()

    @pl.when(g + 1 < pl.num_programs(0) * N_EXPERTS)
    def _():
        w_copy(g + 1, (g + 1) % 2).start(priority=1)

    w_copy(g, g % 2).wait()

    @pl.when(jnp.logical_or(nb > 0, has_tail))
    def _():
        wbf_ref[...] = wf32[g % 2].astype(bf16)

    def swiglu_block(x, store):
        lane = lax.broadcasted_iota(jnp.int32, (x.shape[0], 128), 1)
        even = (2 * lane) % 128
        first_half = lane < 64

        def pick(a, b, idx):
            return jnp.where(first_half, jnp.take_along_axis(a, idx, axis=1),
                             jnp.take_along_axis(b, idx, axis=1))

        for cc in range(tn // 256):
            cs = slice(cc * 256, (cc + 1) * 256)
            h = jnp.dot(x, wbf_ref[:, cs], preferred_element_type=f32) + b_ref[:, cs]
            a, b = h[:, :128], h[:, 128:]
            gate = jnp.minimum(pick(a, b, even), SWIGLU_LIMIT)
            lin = jnp.clip(pick(a, b, even + 1), -SWIGLU_LIMIT, SWIGLU_LIMIT)
            store(cc, (gate * jax.nn.sigmoid(SWIGLU_ALPHA * gate) * (lin + 1.0)).astype(bf16))

    def body(blk, c):
        slot = blk % 2
        x_copy(blk, slot).wait()

        @pl.when(blk + 1 < nb)
        def _():
            x_copy(blk + 1, 1 - slot).start()

        @pl.when(blk >= 2)
        def _():
            o_copy(blk - 2, slot).wait()

        def store(cc, act):
            obuf[slot, :, cc * 128:(cc + 1) * 128] = act

        swiglu_block(xbuf[slot], store)
        o_copy(blk, slot).start()
        return c

    lax.fori_loop(0, nb, body, 0)

    @pl.when(has_tail)
    def _():
        xt_copy().wait()

        def store(cc, act):
            otail[:, cc * 128:(cc + 1) * 128] = act

        swiglu_block(xtail[...], store)
        ot_copy(rt).start()
        ot_copy(rt).wait()

    @pl.when(nb >= 2)
    def _():
        o_copy(nb - 2, nb % 2).wait()

    @pl.when(nb >= 1)
    def _():
        o_copy(nb - 1, (nb - 1) % 2).wait()

    @pl.when(e == N_EXPERTS - 1)
    def _():
        otail[...] = jnp.zeros(otail.shape, otail.dtype)

        def zero_rest(blk, c):
            r = pl.multiple_of(blk * MOE_TAIL, MOE_TAIL)
            ot_copy(r).start()
            ot_copy(r).wait()
            return c

        first = (row0 + nb * MOE_TM) // MOE_TAIL + tail_ref[e]
        lax.fori_loop(first, act_hbm.shape[1] // MOE_TAIL, zero_rest, 0)


def _moe_up(start, nblk_e, tail_e, xs, w1, b1, layer, nr):
    tn = 2048
    nj = 2 * D_FF // tn
    return pl.pallas_call(
        functools.partial(_moe_up_kernel, tn=tn, layer=layer),
        out_shape=jax.ShapeDtypeStruct((nj, nr, tn // 2), bf16),
        grid_spec=pltpu.PrefetchScalarGridSpec(
            num_scalar_prefetch=3,
            grid=(nj, N_EXPERTS),
            in_specs=[pl.BlockSpec(memory_space=pl.ANY),
                      pl.BlockSpec(memory_space=pl.ANY),
                      pl.BlockSpec((None, None, 1, tn), lambda j, e, st, nb, tl: (layer, e, 0, j))],
            out_specs=pl.BlockSpec(memory_space=pl.ANY),
            scratch_shapes=[pltpu.VMEM((D, tn), bf16),
                            pltpu.VMEM((2, D, tn), f32),
                            pltpu.VMEM((2, MOE_TM, D), bf16),
                            pltpu.VMEM((2, MOE_TM, tn // 2), bf16),
                            pltpu.VMEM((MOE_TAIL, D), bf16),
                            pltpu.VMEM((MOE_TAIL, tn // 2), bf16),
                            pltpu.SemaphoreType.DMA((2,)), pltpu.SemaphoreType.DMA((2,)),
                            pltpu.SemaphoreType.DMA((2,)), pltpu.SemaphoreType.DMA((2,))]),
        compiler_params=_cp(("arbitrary", "arbitrary"), 56),
        name="moe_up",
    )(start, nblk_e, tail_e, xs, w1, b1)


def _moe_down_kernel(start_ref, nblk_ref, tail_ref, act_hbm, w_hbm, b_ref, y_hbm, wbf_ref, wf32, abuf, obuf,
                     atail, otail, wsem, asem, osem, tsem, *, layer):
    e = pl.program_id(0)
    nb = nblk_ref[e]
    has_tail = tail_ref[e] > 0
    row0 = start_ref[e]
    nj = abuf.shape[1]
    kh = D_FF // nj
    rt = pl.multiple_of(row0 + nb * MOE_TM, MOE_TAIL)

    def at_copy(jj):
        return pltpu.make_async_copy(act_hbm.at[jj, pl.ds(rt, MOE_TAIL), :], atail.at[jj], tsem.at[0])

    def ot_copy(r):
        return pltpu.make_async_copy(otail, y_hbm.at[pl.ds(r, MOE_TAIL), :], tsem.at[1])

    def down(a_of):
        y = b_ref[...] + jnp.dot(a_of(0), wbf_ref[0:kh, :], preferred_element_type=f32)
        for jj in range(1, nj):
            y = y + jnp.dot(a_of(jj), wbf_ref[jj * kh:(jj + 1) * kh, :], preferred_element_type=f32)
        return y

    def w_copy(ee, slot):
        return pltpu.make_async_copy(w_hbm.at[layer, ee], wf32.at[slot], wsem.at[slot])

    def a_copy(blk, slot, jj):
        r = pl.multiple_of(row0 + blk * MOE_TM, MOE_TAIL)
        return pltpu.make_async_copy(act_hbm.at[jj, pl.ds(r, MOE_TM), :], abuf.at[slot, jj], asem.at[slot])

    def o_copy(blk, slot):
        r = pl.multiple_of(row0 + blk * MOE_TM, MOE_TAIL)
        return pltpu.make_async_copy(obuf.at[slot], y_hbm.at[pl.ds(r, MOE_TM), :], osem.at[slot])

    @pl.when(e == 0)
    def _():
        w_copy(0, 0).start(priority=1)

    @pl.when(nb > 0)
    def _():
        for jj in range(nj):
            a_copy(0, 0, jj).start()

    @pl.when(has_tail)
    def _():
        for jj in range(nj):
            at_copy(jj).start()

    @pl.when(e + 1 < N_EXPERTS)
    def _():
        w_copy(e + 1, (e + 1) % 2).start(priority=1)

    w_copy(e, e % 2).wait()

    @pl.when(jnp.logical_or(nb > 0, has_tail))
    def _():
        wbf_ref[...] = wf32[e % 2].astype(bf16)

    def body(blk, c):
        slot = blk % 2
        for jj in range(nj):
            a_copy(blk, slot, jj).wait()

        @pl.when(blk + 1 < nb)
        def _():
            for jj in range(nj):
                a_copy(blk + 1, 1 - slot, jj).start()

        y = down(lambda jj: abuf[slot, jj])

        @pl.when(blk >= 2)
        def _():
            o_copy(blk - 2, slot).wait()

        obuf[slot] = y
        o_copy(blk, slot).start()
        return c

    lax.fori_loop(0, nb, body, 0)

    @pl.when(has_tail)
    def _():
        for jj in range(nj):
            at_copy(jj).wait()
        otail[...] = down(lambda jj: atail[jj])
        ot_copy(rt).start()
        ot_copy(rt).wait()

    @pl.when(nb >= 2)
    def _():
        o_copy(nb - 2, nb % 2).wait()

    @pl.when(nb >= 1)
    def _():
        o_copy(nb - 1, (nb - 1) % 2).wait()

    @pl.when(e == N_EXPERTS - 1)
    def _():
        otail[...] = jnp.zeros(otail.shape, otail.dtype)

        def zero_rest(blk, c):
            r = pl.multiple_of(blk * MOE_TAIL, MOE_TAIL)
            ot_copy(r).start()
            ot_copy(r).wait()
            return c

        first = (row0 + nb * MOE_TM) // MOE_TAIL + tail_ref[e]
        lax.fori_loop(first, y_hbm.shape[0] // MOE_TAIL, zero_rest, 0)


def _moe_down(start, nblk_e, tail_e, act, w2, b2, layer, nr):
    nj = act.shape[0]
    return pl.pallas_call(
        functools.partial(_moe_down_kernel, layer=layer),
        out_shape=jax.ShapeDtypeStruct((nr, D), f32),
        grid_spec=pltpu.PrefetchScalarGridSpec(
            num_scalar_prefetch=3,
            grid=(N_EXPERTS,),
            in_specs=[pl.BlockSpec(memory_space=pl.ANY),
                      pl.BlockSpec(memory_space=pl.ANY),
                      pl.BlockSpec((None, None, 1, D), lambda e, st, nb, tl: (layer, e, 0, 0))],
            out_specs=pl.BlockSpec(memory_space=pl.ANY),
            scratch_shapes=[pltpu.VMEM((D_FF, D), bf16),
                            pltpu.VMEM((2, D_FF, D), f32),
                            pltpu.VMEM((2, nj, MOE_TM, D_FF // nj), bf16),
                            pltpu.VMEM((2, MOE_TM, D), f32),
                            pltpu.VMEM((nj, MOE_TAIL, D_FF // nj), bf16),
                            pltpu.VMEM((MOE_TAIL, D), f32),
                            pltpu.SemaphoreType.DMA((2,)), pltpu.SemaphoreType.DMA((2,)),
                            pltpu.SemaphoreType.DMA((2,)), pltpu.SemaphoreType.DMA((2,))]),
        compiler_params=_cp(("arbitrary",), 56),
        name="moe_down",
    )(start, nblk_e, tail_e, act, w2, b2)


def _combine_kernel(slot_ref, y_hbm, x_ref, gmod_ref, w_ref, o_ref, buf, sem, *, tm, nsteps):
    i = pl.program_id(0)

    def issue(blk, slot):
        base = blk * (tm * TOP_K)
        for r in range(tm):
            for k in range(TOP_K):
                s = slot_ref[base + r * TOP_K + k]
                pltpu.make_async_copy(y_hbm.at[pl.ds(s, 1), :], buf.at[slot, k, pl.ds(r, 1), :],
                                      sem.at[slot]).start(priority=k % 2)

    @pl.when(i == 0)
    def _():
        issue(0, 0)

    for par in range(2):
        @pl.when(jnp.logical_and(i + 1 < nsteps, (i + 1) % 2 == par))
        def _():
            issue(i + 1, par)

    slot = i % 2
    for k in range(TOP_K):
        pltpu.make_async_copy(y_hbm.at[pl.ds(0, tm), :], buf.at[slot, k], sem.at[slot]).wait()
    w = w_ref[...]
    acc = ((w[:, 0:1] * buf[slot, 0] + w[:, 1:2] * buf[slot, 1])
           + (w[:, 2:3] * buf[slot, 2] + w[:, 3:4] * buf[slot, 3]))
    o_ref[...] = x_ref[...] + gmod_ref[...] * acc


def _combine(slot_flat, ys, x, mod3, gate, gate_chunk, blk, mrow, nblk):
    tm = RB
    return pl.pallas_call(
        functools.partial(_combine_kernel, tm=tm, nsteps=nblk),
        out_shape=jax.ShapeDtypeStruct((nblk * tm, D), f32),
        grid_spec=pltpu.PrefetchScalarGridSpec(
            num_scalar_prefetch=1,
            grid=(nblk,),
            in_specs=[pl.BlockSpec(memory_space=pl.ANY),
                      pl.BlockSpec((tm, D), lambda i, s: (blk(i), 0)),
                      pl.BlockSpec((None, 1, D), lambda i, s: (mrow(i), 0, gate_chunk)),
                      pl.BlockSpec((tm, TOP_K), lambda i, s: (i, 0))],
            out_specs=pl.BlockSpec((tm, D), lambda i, s: (i, 0)),
            scratch_shapes=[pltpu.VMEM((2, TOP_K, tm, D), f32), pltpu.SemaphoreType.DMA((2,))]),
        compiler_params=_cp(("arbitrary",)),
        name="moe_combine",
    )(slot_flat, ys, x, mod3, gate)


def _moe_layer(x, g3, mod3, rw, rb3, w1, b1, w2, b2, layer, blk, mrow, nblk):
    t = nblk * RB
    h, idx, gate, rank, cnt = _norm_router(x, g3, mod3, rw, rb3, layer, 3, blk, mrow, nblk)
    n_assign = t * TOP_K
    nr = n_assign + N_EXPERTS * MOE_TAIL
    counts = cnt[0].astype(jnp.int32)
    padded = (counts + MOE_TAIL - 1) // MOE_TAIL * MOE_TAIL
    pad_end = jnp.cumsum(padded)
    pad_start = pad_end - padded
    slot = pad_start[idx] + rank
    slot_flat = slot.reshape(-1)
    tok = jnp.repeat(jnp.arange(t, dtype=jnp.int32), TOP_K)
    slot_tok = (jnp.arange(nr, dtype=jnp.int32) % t).at[slot_flat].set(tok)
    nblk_e = (padded // MOE_TM).astype(jnp.int32)
    tail_e = ((padded % MOE_TM) // MOE_TAIL).astype(jnp.int32)
    xs = _gather_rows(slot_tok, h, nr)
    act = _moe_up(pad_start.astype(jnp.int32), nblk_e, tail_e, xs, w1, b1, layer, nr)
    ys = _moe_down(pad_start.astype(jnp.int32), nblk_e, tail_e, act, w2, b2, layer, nr)
    return _combine(slot_flat, ys, x, mod3, gate, 5, blk, mrow, nblk)


def _dot_nt(a, b):
    return lax.dot_general(a, b, (((1,), (1,)), ((), ())), preferred_element_type=f32)


def _dot_tn(a, b):
    return lax.dot_general(a, b, (((0,), (0,)), ((), ())), preferred_element_type=f32)


def _mlstm_kernel(q_ref, k_ref, v_ref, o_ref, gc_ref, gr_ref, ng_ref, y_ref, hf_ref, hb_ref, ct_ref):
    T = A_T
    nch = LA // T
    ct_ref[...] = jnp.zeros_like(ct_ref)
    r_i = lax.broadcasted_iota(jnp.int32, (T, T), 0)
    c_i = lax.broadcasted_iota(jnp.int32, (T, T), 1)
    lane = lax.broadcasted_iota(jnp.int32, (T, 128), 1)
    ones_blk = jnp.where(lane == 0, 1.0, 0.0).astype(bf16)

    def chunk(c, dirn, h_ref):
        r0 = pl.multiple_of(c * T, T)
        q = q_ref[pl.ds(r0, T), :]
        k = k_ref[pl.ds(r0, T), :]
        v = v_ref[pl.ds(r0, T), :]
        gc = gc_ref[pl.ds(r0, T), :]
        o = 5 * dirn
        a_col, m_col, negm = gc[:, o:o + 1], gc[:, o + 1:o + 2], gc[:, o + 2:o + 3]
        m_end, m_prev = gc[:, o + 3:o + 4], gc[:, o + 4:o + 5]
        a_row = gr_ref[dirn:dirn + 1, pl.ds(r0, T)]
        mask = (c_i <= r_i) if dirn == 0 else (c_i >= r_i)
        w = jnp.exp(jnp.where(mask, a_row - m_col, -jnp.inf))
        s = _dot_nt(q, k) * (A_DQK ** -0.5) * w
        v_aug = jnp.concatenate([v, ones_blk], axis=1)
        ct = ct_ref[dirn]
        w_carry = jnp.exp(m_prev - m_col)
        num = (jnp.dot(s.astype(bf16), v_aug, preferred_element_type=f32)
               + w_carry * jnp.dot(q, ct.astype(bf16), preferred_element_type=f32))
        den = num[:, A_DV:A_DV + 1]
        h_ref[pl.ds(r0, T), :] = num[:, :A_DV] / jnp.maximum(jnp.abs(den), jnp.exp(negm))
        w_in = jnp.exp(a_col - m_end)
        decay = jnp.exp(m_prev[0:1, :] - m_end[0:1, :])
        kv = _dot_tn(k, (w_in * v_aug.astype(f32)).astype(bf16)) * (A_DQK ** -0.5)
        ct_ref[dirn] = decay * ct + kv

    def step(it, carry):
        chunk(it, 0, hf_ref)
        chunk(jnp.where(it == 0, 0, nch - it), 1, hb_ref)
        return carry

    lax.fori_loop(0, nch, step, 0)
    h = hf_ref[...] + hb_ref[...]
    hn = h * lax.rsqrt(jnp.mean(h * h, axis=-1, keepdims=True) + EPS) * ng_ref[...]
    y_ref[...] = (hn * jax.nn.sigmoid(o_ref[...].astype(f32))).astype(y_ref.dtype)


def _mlstm(proj, gcol, grow, ng3):
    qb, vb = A_DQK, A_DV
    return pl.pallas_call(
        _mlstm_kernel,
        out_shape=jax.ShapeDtypeStruct((ROWS, HALF), bf16),
        grid=(NB_, A_HEADS),
        in_specs=[pl.BlockSpec((LA, qb), lambda b, h: (b, h)),
                  pl.BlockSpec((LA, qb), lambda b, h: (b, A_HEADS + h)),
                  pl.BlockSpec((LA, vb), lambda b, h: (b, (2 * A_HEADS * qb) // vb + h)),
                  pl.BlockSpec((LA, vb), lambda b, h: (b, (2 * A_HEADS * qb + HALF) // vb + h)),
                  pl.BlockSpec((None, None, LA, 16), lambda b, h: (b, h, 0, 0)),
                  pl.BlockSpec((None, None, 8, LA), lambda b, h: (b, h, 0, 0)),
                  pl.BlockSpec((None, 1, vb), lambda b, h: (h, 0, 0))],
        out_specs=pl.BlockSpec((LA, vb), lambda b, h: (b, h)),
        scratch_shapes=[pltpu.VMEM((LA, vb), f32), pltpu.VMEM((LA, vb), f32),
                        pltpu.VMEM((2, qb, vb + 128), f32)],
        compiler_params=_cp(("arbitrary", "arbitrary")),
        name="mlstm",
    )(proj, proj, proj, proj, gcol, grow, ng3)


def _mlstm_gate_tables(gates):
    i_f, f_f, i_b, f_b = jnp.split(gates, 4, axis=-1)

    def scan_tables(i_pre, f_pre):
        lf = jax.nn.log_sigmoid(f_pre)
        big_f = jnp.cumsum(lf, axis=1)
        a = i_pre - big_f
        m_run = jnp.maximum(lax.cummax(a, axis=1), 0.0)
        m_end = jnp.repeat(m_run[:, A_T - 1::A_T], A_T, axis=1)
        m_prev = jnp.concatenate([jnp.zeros_like(m_end[:, :A_T]), m_end[:, :-A_T]], axis=1)
        return a, m_run, -(big_f + m_run), m_end, m_prev

    def to_bwd(t):
        return jnp.concatenate([jnp.flip(t[:, :CTX], axis=1), jnp.flip(t[:, CTX:], axis=1)], axis=1)

    fw = scan_tables(i_f, f_f)
    bw = [to_bwd(t) for t in scan_tables(to_bwd(i_b), to_bwd(f_b))]
    cols = jnp.stack(list(fw) + bw, axis=-1)
    cols = jnp.pad(cols, ((0, 0), (0, 0), (0, 0), (0, 6))).transpose(0, 2, 1, 3)
    rows = jnp.stack([fw[0], bw[0]], axis=1).transpose(0, 3, 1, 2)
    rows = jnp.pad(rows, ((0, 0), (0, 0), (0, 6), (0, 0)))
    return cols, rows


def _swap_halves(t):
    hd = t.shape[-1] // 2
    return jnp.concatenate([t[..., hd:], t[..., :hd]], axis=-1)


def _attn_kernel(q_ref, k_ref, v_ref, cos_ref, sin_ref, gq_ref, gk_ref, sink_ref, o_ref):
    n = pl.program_id(2)
    scale = B_DH ** -0.5
    nq = B_G * 128

    def norm(t, g_ref):
        return t * lax.rsqrt(jnp.mean(t * t, axis=-1, keepdims=True) + EPS) * g_ref[...]

    def head_cols(ref, rows, h):
        return ref[rows, h * B_DH:(h + 1) * B_DH]

    def finish(j, parts):
        sink = sink_ref[j]
        m = sink
        for s, _ in parts:
            m = jnp.maximum(m, jnp.max(s, axis=-1, keepdims=True))
        den = jnp.exp(sink - m)
        acc = None
        for s, vv in parts:
            p = jnp.exp(s - m)
            den = den + jnp.sum(p, axis=-1, keepdims=True)
            pv = jnp.dot(p.astype(bf16), vv, preferred_element_type=f32)
            acc = pv if acc is None else acc + pv
        o = (acc / den).astype(o_ref.dtype)
        for g in range(B_G):
            h = j * B_G + g
            o_ref[:, h * B_DH:(h + 1) * B_DH] = o[g * 128:(g + 1) * 128]

    def q_heads(j):
        q = jnp.concatenate([head_cols(q_ref, slice(None), j * B_G + g) for g in range(B_G)], axis=0)
        return norm(q.astype(f32), gq_ref)

    def ctx_kv(j):
        kc = norm(head_cols(k_ref, pl.ds(0, CTX), j).astype(f32), gk_ref).astype(bf16)
        return kc, head_cols(v_ref, pl.ds(0, CTX), j)

    @pl.when(n < CTX // 128)
    def _():
        for j in range(B_KP):
            kc, vc = ctx_kv(j)
            s_ctx = _dot_nt(q_heads(j).astype(bf16), kc) * scale
            finish(j, [(s_ctx, vc)])

    @pl.when(n >= CTX // 128)
    def _():
        nl = n - CTX // 128
        pos0 = pl.multiple_of(nl * 128, 128)
        start = pl.multiple_of(jnp.clip((nl - 1) * 128, 0, SEQ - 3 * 128), 128)
        cq = jnp.concatenate([cos_ref[pl.ds(pos0, 128), :]] * B_G, axis=0)
        sq = jnp.concatenate([sin_ref[pl.ds(pos0, 128), :]] * B_G, axis=0)
        q_pos = pos0 + lax.broadcasted_iota(jnp.int32, (nq, 3 * 128), 0) % 128
        k_pos = start + lax.broadcasted_iota(jnp.int32, (nq, 3 * 128), 1)
        in_window = jnp.abs(q_pos - k_pos) <= WINDOW
        for j in range(B_KP):
            kc, vc = ctx_kv(j)
            qn = q_heads(j)
            qr = qn * cq + _swap_halves(qn) * sq
            kb = norm(head_cols(k_ref, pl.ds(CTX + start, 3 * 128), j).astype(f32), gk_ref)
            kb = kb * cos_ref[pl.ds(start, 3 * 128), :] + _swap_halves(kb) * sin_ref[pl.ds(start, 3 * 128), :]
            vb = head_cols(v_ref, pl.ds(CTX + start, 3 * 128), j)
            s_band = jnp.where(in_window, _dot_nt(qr.astype(bf16), kb.astype(bf16)) * scale, -jnp.inf)
            s_ctx = _dot_nt(qn.astype(bf16), kc) * scale
            finish(j, [(s_band, vb), (s_ctx, vc)])


def _attention(proj, cosf, sinf, gq, gk, sink_col):
    nblk = LA // 128
    qw = B_KP * B_G * B_DH
    kw = B_KP * B_DH
    return pl.pallas_call(
        _attn_kernel,
        out_shape=jax.ShapeDtypeStruct((ROWS, HALF), bf16),
        grid=(NB_, B_KV // B_KP, nblk),
        in_specs=[pl.BlockSpec((128, qw), lambda b, kp, n: (b * nblk + n, ATT_COL0 // qw + kp)),
                  pl.BlockSpec((LA, kw), lambda b, kp, n: (b, (ATT_COL0 + HALF) // kw + kp)),
                  pl.BlockSpec((LA, kw), lambda b, kp, n: (b, (ATT_COL0 + HALF + B_KV * B_DH) // kw + kp)),
                  pl.BlockSpec((SEQ, B_DH), lambda b, kp, n: (0, 0)),
                  pl.BlockSpec((SEQ, B_DH), lambda b, kp, n: (0, 0)),
                  pl.BlockSpec((1, B_DH), lambda b, kp, n: (0, 0)),
                  pl.BlockSpec((1, B_DH), lambda b, kp, n: (0, 0)),
                  pl.BlockSpec((B_KP, B_G * 128, 1), lambda b, kp, n: (kp, 0, 0))],
        out_specs=pl.BlockSpec((128, qw), lambda b, kp, n: (b * nblk + n, kp)),
        compiler_params=_cp(("arbitrary", "arbitrary", "arbitrary")),
        name="window_attn",
    )(proj, proj, proj, cosf, sinf, gq, gk, sink_col)


def _rope_tables():
    rows = SEQ // GRID_W
    row = jnp.repeat(jnp.arange(rows), GRID_W).astype(f32)
    col = jnp.tile(jnp.arange(GRID_W), rows).astype(f32)
    n_freq = B_DH // 4
    inv = ROPE_BASE ** (-jnp.arange(n_freq, dtype=f32) / n_freq)
    ang = jnp.concatenate([row[:, None] * inv, col[:, None] * inv], axis=-1)
    cos, sin = jnp.cos(ang), jnp.sin(ang)
    return jnp.concatenate([cos, cos], axis=-1), jnp.concatenate([-sin, sin], axis=-1)


def _s5_kernel(u_ref, cwy_ref, cwi_ref, a_ref, y_ref, wy, wi, u32, ucat, zf, zb):
    T = S5_T
    hs = S5_NS
    ku = T * 128

    def onehot_tile(n, reps):
        r = lax.broadcasted_iota(jnp.int32, (n, n * reps), 0)
        c = lax.broadcasted_iota(jnp.int32, (n, n * reps), 1)
        return jnp.where(c % n == r, 1.0, 0.0).astype(bf16)

    def group_mask(rows, row_group, width):
        r = lax.broadcasted_iota(jnp.int32, (rows, S5_GT * width), 0)
        c = lax.broadcasted_iota(jnp.int32, (rows, S5_GT * width), 1)
        return row_group(r) == c // width

    rep_p = onehot_tile(C_STATE, S5_GT)
    m_i = group_mask(ku, lambda r: (r // C_GROUP) % S5_GT, C_STATE)
    nrp = (ku + 4 * hs) // S5_GT
    sa = lax.broadcasted_iota(jnp.int32, (128, 128), 0)
    sb = lax.broadcasted_iota(jnp.int32, (128, 128), 1)
    rr = lax.broadcasted_iota(jnp.int32, (nrp, 128), 0)
    cg = lax.broadcasted_iota(jnp.int32, (nrp, 128), 1) // C_GROUP
    for q in range(S5_GT):
        sel = jnp.where(sa == q * C_GROUP + sb % C_GROUP, 1.0, 0.0).astype(bf16)
        r = rr + q * nrp
        grp = jnp.where(r < ku, (r // C_GROUP) % S5_GT, ((r - ku) % hs) // C_STATE)
        for t in range(T):
            e = jnp.dot(cwy_ref[t], sel, preferred_element_type=f32)
            wy[t, q * nrp:(q + 1) * nrp, :] = jnp.where(grp == cg, e, 0.0).astype(bf16)
    for d in range(2):
        for q in range(2):
            e = jnp.dot(cwi_ref[d, :, q * C_STATE:(q + 1) * C_STATE], rep_p, preferred_element_type=f32)
            wi[d, :, q * hs:(q + 1) * hs] = jnp.where(m_i, e, 0.0).astype(bf16)

    u32[...] = u_ref[...].astype(f32)
    for s in range(T):
        for b in range(NB_):
            ucat[s, pl.ds(b, S5_CPB, stride=NB_), :] = u32[pl.ds(b * LA + s, S5_CPB, stride=T), :]
    uc = jnp.concatenate([ucat[s].astype(bf16) for s in range(T)], axis=1)
    zf[...] = jnp.dot(uc, wi[0], preferred_element_type=f32)
    zb[...] = jnp.dot(uc, wi[1], preferred_element_type=f32)
    a = a_ref[...]

    def cstep(av, x, z):
        ar, ai = av[:, :hs], av[:, hs:]
        xr, xi = x[:, :hs], x[:, hs:]
        return jnp.concatenate([ar * xr - ai * xi + z[:, :hs], ar * xi + ai * xr + z[:, hs:]], axis=1)

    def step(i, carry):
        x_f, x_b = carry
        rf = pl.multiple_of(i * 2 * NB_, 2 * NB_)
        mb = jnp.where(i < S5_CCTX // 2, S5_CCTX // 2 - 1 - i, S5_CPB // 2 + S5_CCTX // 2 - 1 - i)
        rb = pl.multiple_of(mb * 2 * NB_, 2 * NB_)
        z8 = zf[pl.ds(rf, 2 * NB_), :]
        x1 = cstep(a[0:1], x_f, z8[:NB_])
        zf[pl.ds(rf, 2 * NB_), :] = jnp.concatenate([x_f, x1], axis=0)
        x_f2 = cstep(a[0:1], x1, z8[NB_:])
        w8 = zb[pl.ds(rb, 2 * NB_), :]
        y1 = cstep(a[1:2], x_b, w8[NB_:])
        zb[pl.ds(rb, 2 * NB_), :] = jnp.concatenate([y1, x_b], axis=0)
        x_b2 = cstep(a[1:2], y1, w8[:NB_])
        return x_f2, x_b2

    zero = jnp.zeros((NB_, 2 * hs), f32)
    lax.fori_loop(0, S5_CPB // 2, step, (zero, zero))
    xf = zf[...].astype(bf16)
    xb = zb[...].astype(bf16)
    for t in range(T):
        u32[pl.ds(t * S5_NCH, S5_NCH), :] = (
            jnp.dot(uc, wy[t, 0:ku, :], preferred_element_type=f32)
            + jnp.dot(xf, wy[t, ku:ku + 2 * hs, :], preferred_element_type=f32)
            + jnp.dot(xb, wy[t, ku + 2 * hs:ku + 4 * hs, :], preferred_element_type=f32))
    for t in range(T):
        for b in range(NB_):
            y_ref[pl.ds(b * LA + t, S5_CPB, stride=T), :] = u32[pl.ds(t * S5_NCH + b, S5_CPB, stride=NB_), :]


def _s5_core(u_all, mats):
    cwy, cwi, a = mats
    nt = HALF // 128
    kf = S5_T * 128 + 4 * S5_NS
    return pl.pallas_call(
        _s5_kernel,
        out_shape=jax.ShapeDtypeStruct((ROWS, HALF), f32),
        grid=(nt,),
        in_specs=[pl.BlockSpec((ROWS, 128), lambda j: (0, j)),
                  pl.BlockSpec((None, S5_T, kf // S5_GT, 128), lambda j: (j, 0, 0, 0)),
                  pl.BlockSpec((None, 2, S5_T * 128, 2 * C_STATE), lambda j: (j, 0, 0, 0)),
                  pl.BlockSpec((None, 2, 2 * S5_NS), lambda j: (j, 0, 0))],
        out_specs=pl.BlockSpec((ROWS, 128), lambda j: (0, j)),
        scratch_shapes=[pltpu.VMEM((S5_T, kf, 128), bf16),
                        pltpu.VMEM((2, S5_T * 128, 2 * S5_NS), bf16),
                        pltpu.VMEM((ROWS, 128), f32),
                        pltpu.VMEM((S5_T, S5_NCH, 128), f32),
                        pltpu.VMEM((S5_NCH, 2 * S5_NS), f32),
                        pltpu.VMEM((S5_NCH, 2 * S5_NS), f32)],
        compiler_params=_cp(("arbitrary",), 56),
        name="s5_core",
    )(u_all, cwy, cwi, a)


def _s5_matrices(a_re, a_im, log_dt, b_re, b_im, c_re, c_im):
    T = S5_T
    G, P, C = C_GROUPS, C_STATE, C_GROUP
    nt = G // S5_GT
    lam = lax.complex(a_re.astype(f32), a_im.astype(f32))
    dt = jnp.exp(log_dt.astype(f32))[..., None]
    a_bar = jnp.exp(lam * dt)
    b_bar = ((a_bar - 1) / lam)[..., None] * lax.complex(b_re.astype(f32), b_im.astype(f32))
    tau = jnp.arange(T + 1, dtype=f32)
    apow = jnp.exp((lam * dt)[..., None] * tau)
    cc = lax.complex(c_re.astype(f32), c_im.astype(f32))
    kern = jnp.einsum("gxp,dgpt,dgpc->dgtxc", cc, apow[..., :T], b_bar).real
    t_i = jnp.arange(T)
    lag = t_i[:, None] - t_i[None, :]
    kf = kern[0][:, jnp.clip(lag, 0, T - 1)] * (lag >= 0)[None, :, :, None, None]
    kb = kern[1][:, jnp.clip(-lag, 0, T - 1)] * (lag <= 0)[None, :, :, None, None]
    km = (kf + kb).reshape(nt, S5_GT, T, T, C, C).transpose(0, 2, 3, 1, 5, 4)
    wy_u = km.reshape(nt, T, T * 128, C)
    k_f = cc[:, :, :, None] * apow[0][:, None, :, 1 + t_i]
    k_b = cc[:, :, :, None] * apow[1][:, None, :, T - t_i]

    def wy_state(k):
        k = k.reshape(nt, S5_GT, C, P, T).transpose(0, 4, 1, 3, 2).reshape(nt, T, S5_NS, C)
        return jnp.concatenate([k.real, -k.imag], axis=2)

    cwy = jnp.concatenate([wy_u, wy_state(k_f), wy_state(k_b)], axis=2).astype(bf16)
    nrow = cwy.shape[2]
    cwy = cwy.reshape(nt, T, S5_GT, nrow // S5_GT, C).transpose(0, 1, 3, 2, 4).reshape(nt, T, nrow // S5_GT, 128)
    w_f = apow[0][:, :, T - 1 - t_i][..., None] * b_bar[0][:, :, None, :]
    w_b = apow[1][:, :, t_i][..., None] * b_bar[1][:, :, None, :]

    def wi_dir(w):
        w = w.reshape(nt, S5_GT, P, T, C).transpose(0, 3, 1, 4, 2).reshape(nt, T * 128, P)
        return jnp.concatenate([w.real, w.imag], axis=2)

    cwi = jnp.stack([wi_dir(w_f), wi_dir(w_b)], axis=1).astype(bf16)
    at = apow[..., T].reshape(2, nt, S5_NS)
    a = jnp.concatenate([at.real, at.imag], axis=-1).transpose(1, 0, 2)
    return cwy, cwi, a


def _s5_glu_kernel(y_ref, u_ref, d_ref, w_ref, b_ref, o_ref, wbf_ref):
    @pl.when(pl.program_id(0) == 0)
    def _():
        wbf_ref[...] = w_ref[...].astype(bf16)

    y = y_ref[...] + d_ref[...] * u_ref[...].astype(f32)
    y = jax.nn.gelu(y, approximate=True)
    z = jnp.dot(y.astype(bf16), wbf_ref[...], preferred_element_type=f32) + b_ref[...]
    o_ref[...] = (y * jax.nn.sigmoid(z)).astype(o_ref.dtype)


def _s5_glu(y_ssm, proj, d2, glu_w, glu_b2):
    nblk = NB_ * LBPB
    return pl.pallas_call(
        _s5_glu_kernel,
        out_shape=jax.ShapeDtypeStruct((nblk * RB, HALF), bf16),
        grid=(nblk,),
        in_specs=[pl.BlockSpec((RB, HALF), lambda i: (_lat_blk(i), 0)),
                  pl.BlockSpec((RB, HALF), lambda i: (_lat_blk(i), 0)),
                  pl.BlockSpec((1, HALF), lambda i: (0, 0)),
                  pl.BlockSpec((HALF, HALF), lambda i: (0, 0)),
                  pl.BlockSpec((1, HALF), lambda i: (0, 0))],
        out_specs=pl.BlockSpec((RB, HALF), lambda i: (i, 0)),
        scratch_shapes=[pltpu.VMEM((HALF, HALF), bf16)],
        compiler_params=_cp(("arbitrary",)),
        name="s5_glu",
    )(y_ssm, proj, d2, glu_w, glu_b2)


def _hyena_filter(L, w1, b1, w2, b2, w3, freq):
    t = jnp.linspace(0.0, 1.0, L, dtype=f32)[:, None]
    w = 2 * math.pi * jnp.arange(L, dtype=f32)[:, None] / L
    bands = jnp.linspace(1e-4, HY_BANDS - 1, HY_BANDS, dtype=f32)
    feats = jnp.concatenate([t, jnp.cos(bands * w), -jnp.sin(bands * w)], axis=-1)
    h = jnp.sin(freq * (feats @ w1 + b1))
    h = jnp.sin(freq * (h @ w2 + b2))
    h = h @ w3
    fast = abs(math.log(HY_DECAY_TARGET) / HY_FAST_PCT)
    slow = abs(math.log(HY_DECAY_TARGET) / HY_SLOW_PCT)
    deltas = jnp.tile(jnp.linspace(slow, fast, HALF, dtype=f32), 2)
    h = h * jnp.exp(-t * deltas)
    h_fwd, h_bwd = jnp.split(h, 2, axis=-1)
    h_bwd = h_bwd.at[0].set(0.0)
    norm = jnp.sum(jnp.abs(h_fwd), axis=0, keepdims=True) + jnp.sum(jnp.abs(h_bwd), axis=0, keepdims=True)
    return h_fwd / norm, h_bwd / norm


def _hyena_dft_tables():
    n2 = 2 * SEQ
    n = jnp.arange(SEQ, dtype=jnp.int32)[:, None]
    k = jnp.arange(SEQ, dtype=jnp.int32)[None, :]
    kr = 64
    k1 = jnp.arange(SEQ // kr, dtype=jnp.int32)[None, :]
    k0 = jnp.arange(kr, dtype=jnp.int32)[None, :]
    ang_a = ((n * k1 * kr) % n2).astype(f32) * (2.0 * math.pi / n2)
    ang_b = ((n * k0) % n2).astype(f32) * (2.0 * math.pi / n2)
    ca, sa = jnp.cos(ang_a)[:, :, None], jnp.sin(ang_a)[:, :, None]
    cb, sb = jnp.cos(ang_b)[:, None, :], jnp.sin(ang_b)[:, None, :]
    c = (ca * cb - sa * sb).reshape(SEQ, SEQ)
    s = (sa * cb + ca * sb).reshape(SEQ, SEQ)
    nyq = jnp.where(n % 2 == 0, 1.0, -1.0).astype(f32)
    f_re, f_im = c, jnp.where(k == 0, nyq, -s)
    g_re = jnp.where(k == 0, 1.0, 2.0 * c) / n2
    g_im = jnp.where(k == 0, nyq, -2.0 * s) / n2

    def tiles(re, im):
        nk = SEQ // HY_KT
        return jnp.concatenate([re.reshape(SEQ, nk, 1, HY_KT), im.reshape(SEQ, nk, 1, HY_KT)],
                               axis=2).reshape(SEQ, 2 * SEQ).astype(bf16)

    return tiles(f_re, f_im), tiles(g_re, g_im)


def _hyena_kernel(x0_ref, x1_ref, v_ref, cw0_ref, cw1_ref, cw2_ref, cb0_ref, cb1_ref, cb2_ref, hb_ref,
                  f_ref, g_ref, ht_ref, o_ref, ut_scr, u_scr, y_scr):
    kt = pl.program_id(2)
    L = SEQ
    row = lax.broadcasted_iota(jnp.int32, (L, HY_TC), 0)

    def sconv(z_ref, cw_ref, cb_ref):
        z = z_ref[...].astype(f32)
        zm = jnp.where(row == 0, 0.0, pltpu.roll(z, 1, axis=0))
        zp = jnp.where(row == L - 1, 0.0, pltpu.roll(z, L - 1, axis=0))
        return zm * cw_ref[0:1, :] + z * cw_ref[1:2, :] + zp * cw_ref[2:3, :] + cb_ref[...]

    @pl.when(kt == 0)
    def _():
        u = sconv(x1_ref, cw1_ref, cb1_ref) * sconv(v_ref, cw2_ref, cb2_ref)
        u_scr[...] = u
        ut_scr[...] = u.T.astype(bf16)
        y_scr[...] = jnp.zeros_like(y_scr)

    xt = jnp.dot(ut_scr[...], f_ref[...], preferred_element_type=f32)
    h = ht_ref[...]
    xr, xi, hr, hi = xt[:, :HY_KT], xt[:, HY_KT:], h[:, :HY_KT], h[:, HY_KT:]
    col = lax.broadcasted_iota(jnp.int32, (HY_TC, HY_KT), 1) + kt * HY_KT
    dc = col == 0
    yr = xr * hr - jnp.where(dc, 0.0, xi * hi)
    yi = jnp.where(dc, xi * hi, xr * hi + xi * hr)
    yt = jnp.concatenate([yr, yi], axis=1).astype(bf16)
    y_scr[...] += _dot_nt(g_ref[...], yt)

    @pl.when(kt == pl.num_programs(2) - 1)
    def _():
        o_ref[...] = (sconv(x0_ref, cw0_ref, cb0_ref)
                      * (y_scr[...] + u_scr[...] * hb_ref[...])).astype(o_ref.dtype)


def _hyena(z_lat, conv_w, conv_b2, hy_bias2, ht, ft, gt):
    nct = HALF // HY_TC

    def zspec(k):
        return pl.BlockSpec((SEQ, HY_TC), lambda ct, b, kt: (b, k * nct + ct))

    def wspec(k):
        return pl.BlockSpec((3, HY_TC), lambda ct, b, kt: (0, k * nct + ct))

    def bspec(k):
        return pl.BlockSpec((1, HY_TC), lambda ct, b, kt: (0, k * nct + ct))

    return pl.pallas_call(
        _hyena_kernel,
        out_shape=jax.ShapeDtypeStruct((NB_ * SEQ, HALF), bf16),
        grid=(nct, NB_, SEQ // HY_KT),
        in_specs=[zspec(0), zspec(1), zspec(2), wspec(0), wspec(1), wspec(2), bspec(0), bspec(1), bspec(2),
                  pl.BlockSpec((1, HY_TC), lambda ct, b, kt: (0, ct)),
                  pl.BlockSpec((SEQ, 2 * HY_KT), lambda ct, b, kt: (0, kt)),
                  pl.BlockSpec((SEQ, 2 * HY_KT), lambda ct, b, kt: (0, kt)),
                  pl.BlockSpec((HY_TC, 2 * HY_KT), lambda ct, b, kt: (ct, kt))],
        out_specs=pl.BlockSpec((SEQ, HY_TC), lambda ct, b, kt: (b, ct)),
        scratch_shapes=[pltpu.VMEM((HY_TC, SEQ), bf16), pltpu.VMEM((SEQ, HY_TC), f32),
                        pltpu.VMEM((SEQ, HY_TC), f32)],
        compiler_params=_cp(("arbitrary", "arbitrary", "arbitrary"), 48),
        name="hyena",
    )(z_lat, z_lat, z_lat, conv_w, conv_w, conv_w, conv_b2, conv_b2, conv_b2, hy_bias2, ft, gt, ht)


def _hyena_filter_spectrum(h_fwd, h_bwd, ft):
    c = h_fwd.shape[1]
    ht2 = jnp.concatenate([h_fwd.T, h_bwd.T], axis=0).astype(bf16)
    xt = _matmul([ht2], ft, (), 2 * SEQ, 2 * HY_KT, 2 * c // RB, _ident, f32, name="hyena_filter_dft")
    col = jnp.arange(2 * SEQ)
    sgn = jnp.where((col % (2 * HY_KT) >= HY_KT) & (col != HY_KT), -1.0, 1.0).astype(f32)
    return xt[:c] + xt[c:] * sgn[None, :]


def kernel(x, c, ctx, c_ctx, ada_w, ada_b, norm_mix_g, norm_ffn_g, ev_w_in, ev_gate_b, ev_h_norm_g,
           ev_q_norm_g, ev_k_norm_g, ev_sink, ev_w_out, od_w_in, od_a_re, od_a_im, od_log_dt, od_b_re,
           od_b_im, od_c_re, od_c_im, od_d, od_glu_w, od_glu_b, od_conv_w, od_conv_b, od_filt_w1,
           od_filt_b1, od_filt_w2, od_filt_b2, od_filt_w3, od_filt_freq, od_hy_bias, od_w_out,
           moe_router_w, moe_router_b, moe_w1, moe_b1, moe_w2, moe_b2):
    depth = ada_w.shape[0]
    xu = jnp.concatenate([ctx, x], axis=1).reshape(ROWS, D)
    cond8 = jnp.concatenate([c_ctx[None], c, jnp.zeros((3, D), f32)], axis=0)
    ada_b3 = ada_b.reshape(depth, 1, 6 * D)
    g_mix3 = norm_mix_g.reshape(depth, 1, D)
    g_ffn3 = norm_ffn_g.reshape(depth, 1, D)
    rb3 = moe_router_b.reshape(depth, 1, N_EXPERTS)
    b1_4 = moe_b1.reshape(depth, N_EXPERTS, 1, 2 * D_FF)
    b2_4 = moe_b2.reshape(depth, N_EXPERTS, 1, D)
    nall = ROWS // RB
    nlat = NB_ * LBPB

    mod3 = _ada(cond8, ada_w, ada_b3, 0).reshape(8, 1, 6 * D)
    h = _norm_mod(xu, g_mix3, mod3, 0, 0, _ident, _mod_row, nall, bf16)
    w_in = ev_w_in[0]
    n_a = 2 * A_HEADS * A_DQK + 2 * HALF
    n_g = 4 * A_HEADS
    w_main = jnp.concatenate([w_in[:, :n_a], w_in[:, n_a + n_g:]], axis=1)
    w_gate = jnp.pad(w_in[:, n_a:n_a + n_g], ((0, 0), (0, 128 - n_g)))
    proj = _matmul([h], w_main, (), w_main.shape[1], 1536, nall, _ident, bf16, name="even_in")
    gates = _matmul([h], w_gate, (), 128, 128, nall, _ident, f32, name="even_gates")[:, :n_g]
    gates = (gates + ev_gate_b[0]).reshape(NB_, LA, n_g)
    gcol, grow = _mlstm_gate_tables(gates)
    y_a = _mlstm(proj, gcol, grow, ev_h_norm_g[0].reshape(A_HEADS, 1, A_DV))
    cosf, sinf = _rope_tables()
    sink_col = jnp.repeat(ev_sink[0].astype(f32).reshape(B_KV, B_G), 128, axis=1)[..., None]
    y_b = _attention(proj, cosf, sinf, ev_q_norm_g[0].reshape(1, B_DH), ev_k_norm_g[0].reshape(1, B_DH),
                     sink_col)
    xu = _matmul([y_a, y_b], ev_w_out, (0,), D, 1024, nall, _ident, f32,
                 res=xu, res_blk=_ident, mrow=_mod_row, mod3=mod3, gate_chunk=2, name="even_out")
    xu = _moe_layer(xu, g_ffn3, mod3, moe_router_w, rb3, moe_w1, b1_4, moe_w2, b2_4,
                    0, _ident, _mod_row, nall)

    mod3 = _ada(cond8, ada_w, ada_b3, 1).reshape(8, 1, 6 * D)
    h = _norm_mod(xu, g_mix3, mod3, 1, 0, _ident, _mod_row, nall, bf16)
    proj = _matmul([h], od_w_in, (0,), HALF, 1024, nall, _ident, bf16, name="odd_in_u")
    z_lat = _matmul([h], od_w_in, (0,), 3 * HALF, 1024, nlat, _lat_blk, bf16, w_col0=HALF // 1024,
                    name="odd_in_z")
    mats = _s5_matrices(od_a_re[0], od_a_im[0], od_log_dt[0], od_b_re[0], od_b_im[0], od_c_re[0], od_c_im[0])
    y_ssm = _s5_core(proj, mats)
    s_l = _s5_glu(y_ssm, proj, od_d[0].reshape(1, HALF), od_glu_w[0], od_glu_b[0].reshape(1, HALF))
    h_fwd, h_bwd = _hyena_filter(SEQ, od_filt_w1[0], od_filt_b1[0], od_filt_w2[0], od_filt_b2[0],
                                 od_filt_w3[0], od_filt_freq[0])
    ft, gt = _hyena_dft_tables()
    hy_l = _hyena(z_lat, od_conv_w[0], od_conv_b[0].reshape(1, 3 * HALF), od_hy_bias[0].reshape(1, HALF),
                  _hyena_filter_spectrum(h_fwd, h_bwd, ft), ft, gt)
    x_lat = _matmul([s_l, hy_l], od_w_out, (0,), D, 1024, nlat, _ident, f32,
                    res=xu, res_blk=_lat_blk, mrow=_lat_mod_row, mod3=mod3, gate_chunk=2, name="odd_out")
    out = _moe_layer(x_lat, g_ffn3, mod3, moe_router_w, rb3, moe_w1, b1_4, moe_w2, b2_4,
                     1, _ident, _lat_mod_row, nlat)
    return out.reshape(NB_, SEQ, D)
```
